```python
import math
import jax
import jax.numpy as jnp
from jax import lax
import numpy as np

D_MODEL = 1024
BATCH = 1
SEQ = 16384
DEPTH = 1
DEC_BATCH = 32
DEC_SEQ = 8
PAST_LEN = 16384
PAGE_SIZE = 128

N_HEADS = 8
HEAD_DIM = 64
ATTN_WIDTH = N_HEADS * HEAD_DIM
DILATED_BRANCHES = ((128, 1), (512, 4), (2048, 16))
MAX_WINDOW = 2048
Q_BLOCK = 128
SSM_WIDTH = D_MODEL - ATTN_WIDTH
SSM_CH = 16
SSM_GROUPS = SSM_WIDTH // SSM_CH
SSM_STATE = 64
DT_MIN = 1e-3
DT_MAX = 1e-1
N_EXPERTS = 32
TOP_K = 4
D_EXPERT = D_MODEL
SWIGLU_LIMIT = 7.0
SWIGLU_ALPHA = 1.702
N_MOD = 6
EPS = 1e-6

kernel_name = 'hymba_dilated_s5_moe_adaln_step'


def rmsnorm(x, g):
    xf = x.astype(jnp.float32)
    y = xf * lax.rsqrt(jnp.mean(xf * xf, axis=-1, keepdims=True) + EPS)
    return (y * g.astype(jnp.float32)).astype(x.dtype)


def adaln_modulation(c, w_ada, b_ada):
    m = jax.nn.silu(c) @ w_ada + b_ada
    return jnp.split(m[:, None, :], N_MOD, axis=-1)


def modulate(h, shift, scale):
    return h * (1 + scale) + shift


def alibi_slopes():
    return jnp.exp2(-8.0 * jnp.arange(1, N_HEADS + 1, dtype=jnp.float32) / N_HEADS)


def dilated_mixture_attention(q, k, v, q_idx):
    slopes = alibi_slopes()
    qf = q.astype(jnp.float32) * (HEAD_DIM ** -0.5)
    nums, dens, maxs = [], [], []
    for window, dil in DILATED_BRANCHES:
        dist = jnp.arange(0, window + 1, dil, dtype=jnp.int32)
        idx = q_idx[:, None] - dist[None, :]
        valid = idx >= 0
        idx = jnp.maximum(idx, 0)
        kg = jnp.take(k, idx, axis=1).astype(jnp.float32)
        vg = jnp.take(v, idx, axis=1).astype(jnp.float32)
        s = jnp.einsum('bqhd,bqkhd->bhqk', qf, kg) - slopes[:, None, None] * dist.astype(jnp.float32)
        s = jnp.where(valid, s, -jnp.inf)
        m = jnp.max(s, axis=-1)
        p = jnp.exp(s - m[..., None])
        dens.append(jnp.sum(p, axis=-1))
        maxs.append(m)
        nums.append(jnp.einsum('bhqk,bqkhd->bqhd', p, vg))
    m_all = jnp.max(jnp.stack(maxs), axis=0)
    num = 0.0
    den = 0.0
    for n_b, d_b, m_b in zip(nums, dens, maxs):
        w_b = jnp.exp(m_b - m_all)
        num = num + n_b * jnp.transpose(w_b, (0, 2, 1))[..., None]
        den = den + d_b * w_b
    out = num / jnp.transpose(den, (0, 2, 1))[..., None]
    return out.astype(q.dtype)


def dilated_attention(q, k_all, v_all, q_idx):
    B, Q = q.shape[0], q.shape[1]
    blk = Q_BLOCK if Q % Q_BLOCK == 0 else Q
    nblk = Q // blk
    if nblk == 1:
        return dilated_mixture_attention(q, k_all, v_all, q_idx)
    qb = q.reshape(B, nblk, blk, N_HEADS, HEAD_DIM).swapaxes(0, 1)
    ib = q_idx.reshape(nblk, blk)
    ob = lax.map(lambda a: dilated_mixture_attention(a[0], k_all, v_all, a[1]), (qb, ib))
    return ob.swapaxes(0, 1).reshape(B, Q, N_HEADS, HEAD_DIM)


def _complex_affine_combine(e1, e2):
    a1r, a1i, b1r, b1i = e1
    a2r, a2i, b2r, b2i = e2
    return (a2r * a1r - a2i * a1i,
            a2r * a1i + a2i * a1r,
            a2r * b1r - a2i * b1i + b2r,
            a2r * b1i + a2i * b1r + b2i)


def ssm_scan(u, h0_re, h0_im, a_re, a_im, log_dt, b_re, b_im, c_re, c_im, d_skip):
    f32 = jnp.float32
    a_re = a_re.astype(f32)
    a_im = a_im.astype(f32)
    dt = jnp.exp(log_dt.astype(f32))[:, None]
    mag = jnp.exp(dt * a_re)
    lam_re = mag * jnp.cos(dt * a_im)
    lam_im = mag * jnp.sin(dt * a_im)
    nr = lam_re - 1.0
    ni = lam_im
    inv = 1.0 / (a_re * a_re + a_im * a_im)
    coef_re = (nr * a_re + ni * a_im) * inv
    coef_im = (ni * a_re - nr * a_im) * inv
    br = b_re.astype(f32)
    bi = b_im.astype(f32)
    bb_re = coef_re[..., None] * br - coef_im[..., None] * bi
    bb_im = coef_re[..., None] * bi + coef_im[..., None] * br
    uf = u.astype(f32)
    bu_re = jnp.einsum('btgc,gnc->btgn', uf, bb_re)
    bu_im = jnp.einsum('btgc,gnc->btgn', uf, bb_im)
    h0r = h0_re.astype(f32)
    h0i = h0_im.astype(f32)
    bu_re = bu_re.at[:, 0].add(lam_re * h0r - lam_im * h0i)
    bu_im = bu_im.at[:, 0].add(lam_re * h0i + lam_im * h0r)
    ar = jnp.broadcast_to(lam_re, bu_re.shape)
    ai = jnp.broadcast_to(lam_im, bu_im.shape)
    _, _, h_re, h_im = lax.associative_scan(_complex_affine_combine, (ar, ai, bu_re, bu_im), axis=1)
    y = (jnp.einsum('btgn,gcn->btgc', h_re, c_re.astype(f32))
         - jnp.einsum('btgn,gcn->btgc', h_im, c_im.astype(f32))
         + d_skip.astype(f32).reshape(SSM_GROUPS, SSM_CH) * uf)
    return y, h_re[:, -1], h_im[:, -1]


def moe_ffn(h, w_router, b_router, w_gate_up, b_gate_up, w_down, b_down):
    f32 = jnp.float32
    B, T, D = h.shape
    hf = h.reshape(B * T, D)
    logits = (hf @ w_router + b_router).astype(f32)
    top_v, top_i = lax.top_k(logits, TOP_K)
    gates = jax.nn.softmax(top_v, axis=-1)
    combine = jnp.einsum('mk,mke->me', gates, jax.nn.one_hot(top_i, N_EXPERTS, dtype=f32))
    out = jnp.zeros((B * T, D), f32)
    for e in range(N_EXPERTS):
        gu = (hf @ w_gate_up[e] + b_gate_up[e]).astype(f32)
        gate = jnp.minimum(gu[:, :D_EXPERT], SWIGLU_LIMIT)
        up = jnp.clip(gu[:, D_EXPERT:], -SWIGLU_LIMIT, SWIGLU_LIMIT)
        act = (up + 1.0) * gate * jax.nn.sigmoid(SWIGLU_ALPHA * gate)
        out = out + combine[:, e:e + 1] * (act.astype(h.dtype) @ w_down[e] + b_down[e]).astype(f32)
    return out.reshape(B, T, D)


def trunk_layer(x, c, past_k, past_v, h0_re, h0_im, p):
    B, T, _ = x.shape
    sh1, sc1, gt1, sh2, sc2, gt2 = adaln_modulation(c, p['w_ada'], p['b_ada'])
    h = modulate(rmsnorm(x, p['g_norm1']), sh1, sc1)
    proj = h @ p['w_in']
    q, k, v, u = jnp.split(proj, [ATTN_WIDTH, 2 * ATTN_WIDTH, 3 * ATTN_WIDTH], axis=-1)
    q = q.reshape(B, T, N_HEADS, HEAD_DIM)
    k = k.reshape(B, T, N_HEADS, HEAD_DIM)
    v = v.reshape(B, T, N_HEADS, HEAD_DIM)
    if past_k is None:
        k_all, v_all, offset = k, v, 0
    else:
        k_all = jnp.concatenate([past_k.astype(k.dtype), k], axis=1)
        v_all = jnp.concatenate([past_v.astype(v.dtype), v], axis=1)
        offset = past_k.shape[1]
    q_idx = offset + jnp.arange(T, dtype=jnp.int32)
    o_attn = dilated_attention(q, k_all, v_all, q_idx).reshape(B, T, ATTN_WIDTH)
    y_ssm, hT_re, hT_im = ssm_scan(u.reshape(B, T, SSM_GROUPS, SSM_CH), h0_re, h0_im,
                                   p['ssm_a_re'], p['ssm_a_im'], p['ssm_log_dt'],
                                   p['ssm_b_re'], p['ssm_b_im'], p['ssm_c_re'], p['ssm_c_im'], p['ssm_d'])
    y_ssm = jax.nn.gelu(y_ssm.reshape(B, T, SSM_WIDTH))
    y_ssm = y_ssm * jax.nn.sigmoid(y_ssm @ p['w_glu'].astype(jnp.float32) + p['b_glu'].astype(jnp.float32))
    mixed = jnp.concatenate([rmsnorm(o_attn, p['g_out_attn']),
                             rmsnorm(y_ssm.astype(x.dtype), p['g_out_ssm'])], axis=-1) @ p['w_out']
    x = x + (gt1 * mixed).astype(x.dtype)
    h2 = modulate(rmsnorm(x, p['g_norm2']), sh2, sc2)
    ff = moe_ffn(h2, p['w_router'], p['b_router'], p['w_gate_up'], p['b_gate_up'], p['w_down'], p['b_down'])
    x = x + (gt2 * ff).astype(x.dtype)
    return x, k, v, hT_re, hT_im


def setup_inputs(seed: int = 0) -> dict:
    key = jax.random.key(seed)
    ks = jax.random.split(key, 40)
    f32 = jnp.float32
    L, D, A, S, G, N, CH, E, F = DEPTH, D_MODEL, ATTN_WIDTH, SSM_WIDTH, SSM_GROUPS, SSM_STATE, SSM_CH, N_EXPERTS, D_EXPERT
    win = min(MAX_WINDOW, PAST_LEN)

    def nrm(k, shape, s):
        return s * jax.random.normal(k, shape, f32)

    n_idx = jnp.arange(N, dtype=f32)
    gate_pattern = jnp.repeat(jnp.array([0.0, 0.0, 1.0, 0.0, 0.0, 1.0], f32), D)
    return {
        'x_prompt': nrm(ks[0], (BATCH, SEQ, D), 1.0),
        'x_sample': nrm(ks[1], (DEC_BATCH, DEC_SEQ, D), 1.0),
        'cache_k_win': nrm(ks[2], (L, DEC_BATCH, win, N_HEADS, HEAD_DIM), 1.0),
        'cache_v_win': nrm(ks[3], (L, DEC_BATCH, win, N_HEADS, HEAD_DIM), 1.0),
        'state_ssm_re': nrm(ks[4], (L, DEC_BATCH, G, N), 0.1),
        'state_ssm_im': nrm(ks[5], (L, DEC_BATCH, G, N), 0.1),
        'c_prompt': nrm(ks[6], (BATCH, D), 1.0),
        'c_sample': nrm(ks[7], (DEC_BATCH, D), 1.0),
        'w_ada': nrm(ks[8], (L, D, N_MOD * D), 0.1 * D ** -0.5),
        'b_ada': gate_pattern + nrm(ks[9], (L, N_MOD * D), 0.02),
        'g_norm1': 1.0 + nrm(ks[10], (L, D), 0.01),
        'w_in': nrm(ks[11], (L, D, 3 * A + S), D ** -0.5),
        'ssm_a_re': -0.5 + nrm(ks[12], (L, G, N), 0.01),
        'ssm_a_im': math.pi * n_idx + nrm(ks[13], (L, G, N), 0.01),
        'ssm_log_dt': jax.random.uniform(ks[14], (L, G), f32, math.log(DT_MIN), math.log(DT_MAX)),
        'ssm_b_re': nrm(ks[15], (L, G, N, CH), (2 * CH) ** -0.5),
        'ssm_b_im': nrm(ks[16], (L, G, N, CH), (2 * CH) ** -0.5),
        'ssm_c_re': nrm(ks[17], (L, G, CH, N), (2 * N) ** -0.5),
        'ssm_c_im': nrm(ks[18], (L, G, CH, N), (2 * N) ** -0.5),
        'ssm_d': nrm(ks[19], (L, S), 1.0),
        'w_glu': nrm(ks[20], (L, S, S), S ** -0.5),
        'b_glu': nrm(ks[21], (L, S), 0.02),
        'g_out_attn': 1.0 + nrm(ks[22], (L, A), 0.01),
        'g_out_ssm': 1.0 + nrm(ks[23], (L, S), 0.01),
        'w_out': nrm(ks[24], (L, D, D), D ** -0.5),
        'g_norm2': 1.0 + nrm(ks[25], (L, D), 0.01),
        'w_router': nrm(ks[26], (L, D, E), D ** -0.5),
        'b_router': nrm(ks[27], (L, E), 0.01),
        'w_gate_up': nrm(ks[28], (L, E, D, 2 * F), D ** -0.5),
        'b_gate_up': nrm(ks[29], (L, E, 2 * F), 0.01),
        'w_down': nrm(ks[30], (L, E, F, D), F ** -0.5),
        'b_down': nrm(ks[31], (L, E, D), 0.01),
        'g_final': 1.0 + nrm(ks[32], (D,), 0.01),
    }


def reference(x_prompt, x_sample, cache_k_win, cache_v_win, state_ssm_re, state_ssm_im, c_prompt, c_sample,
              w_ada, b_ada, g_norm1, w_in, ssm_a_re, ssm_a_im, ssm_log_dt, ssm_b_re, ssm_b_im, ssm_c_re, ssm_c_im,
              ssm_d, w_glu, b_glu, g_out_attn, g_out_ssm, w_out, g_norm2, w_router, b_router, w_gate_up, b_gate_up,
              w_down, b_down, g_final):
    xp = x_prompt
    xs = x_sample
    kp, vp, rp, ipr, kn, vn, rs, isr = [], [], [], [], [], [], [], []
    for l in range(DEPTH):
        p = {
            'w_ada': w_ada[l], 'b_ada': b_ada[l], 'g_norm1': g_norm1[l], 'w_in': w_in[l],
            'ssm_a_re': ssm_a_re[l], 'ssm_a_im': ssm_a_im[l], 'ssm_log_dt': ssm_log_dt[l],
            'ssm_b_re': ssm_b_re[l], 'ssm_b_im': ssm_b_im[l], 'ssm_c_re': ssm_c_re[l], 'ssm_c_im': ssm_c_im[l],
            'ssm_d': ssm_d[l], 'w_glu': w_glu[l], 'b_glu': b_glu[l], 'g_out_attn': g_out_attn[l],
            'g_out_ssm': g_out_ssm[l], 'w_out': w_out[l], 'g_norm2': g_norm2[l], 'w_router': w_router[l],
            'b_router': b_router[l], 'w_gate_up': w_gate_up[l], 'b_gate_up': b_gate_up[l],
            'w_down': w_down[l], 'b_down': b_down[l],
        }
        h0 = jnp.zeros((xp.shape[0], SSM_GROUPS, SSM_STATE), jnp.float32)
        xp, k_l, v_l, r_l, i_l = trunk_layer(xp, c_prompt, None, None, h0, h0, p)
        keep = min(MAX_WINDOW, xp.shape[1])
        kp.append(k_l[:, xp.shape[1] - keep:])
        vp.append(v_l[:, xp.shape[1] - keep:])
        rp.append(r_l)
        ipr.append(i_l)
        xs, k_l, v_l, r_l, i_l = trunk_layer(xs, c_sample, cache_k_win[l], cache_v_win[l],
                                             state_ssm_re[l], state_ssm_im[l], p)
        kn.append(k_l)
        vn.append(v_l)
        rs.append(r_l)
        isr.append(i_l)
    y_prompt = rmsnorm(xp, g_final)
    y_sample = rmsnorm(xs, g_final)
    return (y_prompt, y_sample, jnp.stack(kp), jnp.stack(vp), jnp.stack(rp), jnp.stack(ipr),
            jnp.stack(kn), jnp.stack(vn), jnp.stack(rs), jnp.stack(isr))
```

```python
import functools

import numpy as np
import jax
import jax.numpy as jnp
from jax import lax
from jax.experimental import pallas as pl
from jax.experimental.pallas import tpu as pltpu

F32 = jnp.float32
BF16 = jnp.bfloat16

D_MODEL = 1024
N_HEADS = 8
HEAD_DIM = 64
ATTN_WIDTH = N_HEADS * HEAD_DIM
DILATED_BRANCHES = ((128, 1), (512, 4), (2048, 16))
KEYS_PER_BRANCH = 129
MAX_WINDOW = 2048
SSM_WIDTH = D_MODEL - ATTN_WIDTH
SSM_CH = 16
SSM_GROUPS = SSM_WIDTH // SSM_CH
SSM_STATE = 64
N_STATE = SSM_GROUPS * SSM_STATE
N_EXPERTS = 32
TOP_K = 4
D_EXPERT = D_MODEL
SWIGLU_LIMIT = 7.0
SWIGLU_ALPHA = 1.702
N_MOD = 6
EPS = 1e-6
MASKED = -1e30

V7X_LANES = 128
V7X_SUBLANES = 8
V7X_VMEM_LIMIT_BYTES = 56 * 1024 * 1024

TOKEN_TILE = 512
Q_TILE = 128
SSM_ROWS_PROMPT = 8
MOE_TILE = 512


def _cparams(n_axes=1):
    return pltpu.CompilerParams(
        dimension_semantics=("arbitrary",) * n_axes,
        vmem_limit_bytes=V7X_VMEM_LIMIT_BYTES,
    )


def _full(shape):
    n = len(shape)
    return pl.BlockSpec(shape, lambda *_: (0,) * n)


def _rms(x, g):
    return x * lax.rsqrt(jnp.mean(x * x, axis=-1, keepdims=True) + EPS) * g


def _sigmoid(x):
    return 1.0 / (1.0 + jnp.exp(-x))


def _ada_kernel(c_ref, w_ref, b_ref, o_ref):
    c = c_ref[...]
    s = (c * _sigmoid(c)).astype(BF16)
    o_ref[...] = jnp.dot(s, w_ref[...].astype(BF16), preferred_element_type=F32) + b_ref[...]


def _ada_modulation(c_rows, w_ada, b_ada):
    m, d = c_rows.shape
    n = w_ada.shape[1]
    tn = n // 4
    return pl.pallas_call(
        _ada_kernel,
        out_shape=jax.ShapeDtypeStruct((m, n), F32),
        grid=(n // tn,),
        in_specs=[_full((m, d)),
                  pl.BlockSpec((d, tn), lambda j: (0, j)),
                  pl.BlockSpec((1, tn), lambda j: (0, j))],
        out_specs=pl.BlockSpec((m, tn), lambda j: (0, j)),
        compiler_params=_cparams(),
        name="ada_modulation",
    )(c_rows, w_ada, b_ada.reshape(1, n))


def _inproj_kernel(x_ref, sh_ref, sc_ref, g_ref, w_ref, perm_ref,
                   q_ref, k_ref, v_ref, kf_ref, vf_ref, u_ref, wbf_ref):
    @pl.when(pl.program_id(0) == 0)
    def _():
        wbf_ref[...] = w_ref[...].astype(BF16)

    h = _rms(x_ref[...], g_ref[...]) * (1.0 + sc_ref[...]) + sh_ref[...]
    hb = h.astype(BF16)
    a = ATTN_WIDTH
    proj = jnp.dot(hb, wbf_ref[:, :3 * a], preferred_element_type=F32)
    q_ref[...] = (proj[:, :a] * (HEAD_DIM ** -0.5)).astype(BF16)
    k = proj[:, a:2 * a]
    v = proj[:, 2 * a:]
    k_ref[...] = k.astype(BF16)
    v_ref[...] = v.astype(BF16)
    kf_ref[...] = k
    vf_ref[...] = v
    hp = jnp.dot(perm_ref[...], hb, preferred_element_type=F32).astype(BF16)
    u_ref[...] = jnp.dot(hp, wbf_ref[:, 3 * a:], preferred_element_type=F32)


def _inproj(x, sh, sc, g, w_in, perm, tm):
    n, d = x.shape
    a = ATTN_WIDTH
    mod_rows = sh.shape[0]
    mod_spec = (pl.BlockSpec((1, d), lambda i: (0, 0)) if mod_rows == 1
                else pl.BlockSpec((tm, d), lambda i: (i, 0)))
    row = lambda w: pl.BlockSpec((tm, w), lambda i: (i, 0))
    return pl.pallas_call(
        _inproj_kernel,
        out_shape=(jax.ShapeDtypeStruct((n, a), BF16),) * 3
        + (jax.ShapeDtypeStruct((n, a), F32),) * 2
        + (jax.ShapeDtypeStruct((n, SSM_WIDTH), F32),),
        grid=(n // tm,),
        in_specs=[row(d), mod_spec, mod_spec, _full((1, d)), _full(w_in.shape), _full((tm, tm))],
        out_specs=(row(a),) * 5 + (row(SSM_WIDTH),),
        scratch_shapes=[pltpu.VMEM(w_in.shape, BF16)],
        compiler_params=_cparams(),
        name="inproj",
    )(x, sh, sc, g.reshape(1, d), w_in, perm)


def _chunk_perm(rows, steps):
    n = rows * steps
    p = np.zeros((n, n), np.float32)
    c, t = np.meshgrid(np.arange(rows), np.arange(steps), indexing="ij")
    p[(t * rows + c).ravel(), (c * steps + t).ravel()] = 1.0
    return p


def _alibi_slopes():
    return np.exp2(-8.0 * np.arange(1, N_HEADS + 1, dtype=np.float64) / N_HEADS).astype(np.float32)


def _branch_bias(dil):
    qi = np.arange(Q_TILE)[:, None]
    col = np.arange(2 * Q_TILE)[None, :]
    j = Q_TILE + qi - col
    valid = (j >= 0) & (j <= Q_TILE)
    dist = (j * dil).astype(np.float32)
    tabs = []
    for first in (True, False):
        ok = valid & (col >= Q_TILE) if first else valid
        per_head = [np.where(ok, -s * dist, np.float32(MASKED)) for s in _alibi_slopes()]
        tabs.append(np.concatenate(per_head, axis=0))
    return np.stack(tabs).astype(np.float32)


def _attn_branch_kernel(q_ref, kp_ref, kc_ref, vp_ref, vc_ref, bias_ref, o_ref, lse_ref):
    sel = jnp.minimum(pl.program_id(0), 1)
    lane = lax.broadcasted_iota(jnp.int32, (Q_TILE, V7X_LANES), 1)
    lo = lane < HEAD_DIM
    lse_acc = jnp.zeros((Q_TILE, V7X_LANES), F32)
    for p in range(N_HEADS // 2):
        cs = slice(V7X_LANES * p, V7X_LANES * (p + 1))
        q2 = q_ref[:, cs]
        zero = jnp.zeros_like(q2)
        qq = jnp.concatenate([jnp.where(lo, q2, zero), jnp.where(lo, zero, q2)], axis=0)
        kk = jnp.concatenate([kp_ref[:, cs], kc_ref[:, cs]], axis=0)
        vv = jnp.concatenate([vp_ref[:, cs], vc_ref[:, cs]], axis=0)
        s = lax.dot_general(qq, kk, (((1,), (1,)), ((), ())), preferred_element_type=F32)
        s = s + bias_ref[sel, 2 * Q_TILE * p:2 * Q_TILE * (p + 1), :]
        m = jnp.max(s, axis=1, keepdims=True)
        e = jnp.exp(s - m)
        l = jnp.sum(e, axis=1, keepdims=True)
        eb = e.astype(BF16)
        o0 = jnp.dot(eb[:Q_TILE], vv, preferred_element_type=F32) * (1.0 / l[:Q_TILE])
        o1 = jnp.dot(eb[Q_TILE:], vv, preferred_element_type=F32) * (1.0 / l[Q_TILE:])
        o_ref[:, cs] = jnp.where(lo, o0, o1).astype(o_ref.dtype)
        lse = m + jnp.log(l)
        lse_acc = jnp.where(lane == 2 * p, lse[:Q_TILE], lse_acc)
        lse_acc = jnp.where(lane == 2 * p + 1, lse[Q_TILE:], lse_acc)
    lse_ref[...] = lse_acc[:, :N_HEADS]


def _attn_branch(q, k, v, dil):
    t = q.shape[0]
    a = ATTN_WIDTH
    rows = t // dil
    qv, kv, vv = (z.reshape(rows, dil * a) for z in (q, k, v))
    cur = pl.BlockSpec((Q_TILE, a), lambda i, r: (i, r))
    prev = pl.BlockSpec((Q_TILE, a), lambda i, r: (jnp.maximum(i - 1, 0), r))
    bias = jnp.asarray(_branch_bias(dil))
    o, lse = pl.pallas_call(
        _attn_branch_kernel,
        out_shape=(jax.ShapeDtypeStruct((rows, dil * a), BF16),
                   jax.ShapeDtypeStruct((dil, rows, N_HEADS), F32)),
        grid=(rows // Q_TILE, dil),
        in_specs=[cur, prev, cur, prev, cur, _full(bias.shape)],
        out_specs=(cur, pl.BlockSpec((None, Q_TILE, N_HEADS), lambda i, r: (r, i, 0))),
        compiler_params=_cparams(2),
        name=f"attn_branch_d{dil}",
    )(qv, kv, kv, vv, vv, bias)
    return o.reshape(t, a), lse.transpose(1, 0, 2).reshape(t, N_HEADS)


def _attn_combine_kernel(o1_ref, o2_ref, o3_ref, l1_ref, l2_ref, l3_ref, g_ref, a_ref):
    ls = [l1_ref[...], l2_ref[...], l3_ref[...]]
    top = jnp.maximum(jnp.maximum(ls[0], ls[1]), ls[2])
    ws = [jnp.exp(l - top) for l in ls]
    inv = 1.0 / (ws[0] + ws[1] + ws[2])
    cs = [w * inv for w in ws]
    tq = a_ref.shape[0]
    lane = lax.broadcasted_iota(jnp.int32, (tq, V7X_LANES), 1)
    lo = lane < HEAD_DIM
    cols = []
    for p in range(N_HEADS // 2):
        sl = slice(V7X_LANES * p, V7X_LANES * (p + 1))
        acc = jnp.zeros((tq, V7X_LANES), F32)
        for c, o_ref in zip(cs, (o1_ref, o2_ref, o3_ref)):
            cexp = jnp.where(lo,
                             jnp.broadcast_to(c[:, 2 * p:2 * p + 1], (tq, V7X_LANES)),
                             jnp.broadcast_to(c[:, 2 * p + 1:2 * p + 2], (tq, V7X_LANES)))
            acc = acc + cexp * o_ref[:, sl].astype(F32)
        cols.append(acc)
    o = jnp.concatenate(cols, axis=1)
    a_ref[...] = _rms(o, g_ref[...]).astype(a_ref.dtype)


def _attn_combine(os_, lses, g):
    t, a = os_[0].shape
    tq = 1024
    row = lambda w: pl.BlockSpec((tq, w), lambda i: (i, 0))
    return pl.pallas_call(
        _attn_combine_kernel,
        out_shape=jax.ShapeDtypeStruct((t, a), BF16),
        grid=(t // tq,),
        in_specs=[row(a)] * 3 + [row(N_HEADS)] * 3 + [_full((1, a))],
        out_specs=row(a),
        compiler_params=_cparams(),
        name="attn_combine",
    )(*os_, *lses, g.reshape(1, a))


NEW_KEY_PAD = V7X_LANES


def _sample_bias(win, steps):
    ncol = win + NEW_KEY_PAD
    col = np.arange(ncol)[None, :]
    t = np.arange(steps)[:, None]
    is_new = col >= win
    dist = np.where(is_new, t - (col - win), win + t - col)
    ok_new = (col - win) < steps
    tabs = []
    for window, dil in DILATED_BRANCHES:
        ok = (dist >= 0) & (dist <= window) & (dist % dil == 0) & (~is_new | ok_new)
        per_head = [np.where(ok, -s * dist.astype(np.float32), np.float32(MASKED)) for s in _alibi_slopes()]
        tabs.append(np.concatenate(per_head, axis=0))
    return np.stack(tabs).astype(np.float32)


def _attn_sample_kernel(q_ref, kn_ref, vn_ref, ck_ref, cv_ref, bias_ref, g_ref, o_ref):
    steps = q_ref.shape[0]
    nq = N_HEADS * steps
    a = ATTN_WIDTH
    row = lax.broadcasted_iota(jnp.int32, (nq, a), 0)
    col = lax.broadcasted_iota(jnp.int32, (nq, a), 1)
    own = (row // steps) == (col // HEAD_DIM)
    qf = q_ref[...].astype(F32)
    qbd = jnp.where(own, jnp.concatenate([qf] * N_HEADS, axis=0), 0.0).astype(BF16)
    pad = jnp.zeros((NEW_KEY_PAD - steps, a), F32)
    k_all = jnp.concatenate([ck_ref[...], kn_ref[...].astype(F32), pad], axis=0).astype(BF16)
    v_all = jnp.concatenate([cv_ref[...], vn_ref[...].astype(F32), pad], axis=0).astype(BF16)
    s = lax.dot_general(qbd, k_all, (((1,), (1,)), ((), ())), preferred_element_type=F32)
    ms, ls, es = [], [], []
    for b in range(len(DILATED_BRANCHES)):
        sb = s + bias_ref[b]
        m = jnp.max(sb, axis=1, keepdims=True)
        e = jnp.exp(sb - m)
        ms.append(m)
        ls.append(jnp.sum(e, axis=1, keepdims=True))
        es.append(e)
    o_all = jnp.dot(jnp.concatenate(es, axis=0).astype(BF16), v_all, preferred_element_type=F32)
    top = jnp.maximum(jnp.maximum(ms[0], ms[1]), ms[2])
    num = jnp.zeros((nq, a), F32)
    den = jnp.zeros((nq, 1), F32)
    for b in range(len(DILATED_BRANCHES)):
        w = jnp.exp(ms[b] - top)
        num = num + o_all[b * nq:(b + 1) * nq] * w
        den = den + ls[b] * w
    full = jnp.where(own, num * (1.0 / den), 0.0)
    o = full[0:steps]
    for h in range(1, N_HEADS):
        o = o + full[h * steps:(h + 1) * steps]
    o_ref[...] = _rms(o, g_ref[...])


def _attn_sample(q, kn, vn, cache_k, cache_v, g):
    b, steps, a = q.shape
    win = cache_k.shape[1]
    bias = jnp.asarray(_sample_bias(win, steps))
    new = pl.BlockSpec((None, steps, a), lambda i: (i, 0, 0))
    cache = pl.BlockSpec((None, win, a), lambda i: (i, 0, 0))
    return pl.pallas_call(
        _attn_sample_kernel,
        out_shape=jax.ShapeDtypeStruct((b, steps, a), F32),
        grid=(b,),
        in_specs=[new, new, new, cache, cache, _full(bias.shape), _full((1, a))],
        out_specs=new,
        compiler_params=_cparams(),
        name="attn_sample",
    )(q, kn, vn, cache_k, cache_v, bias, g.reshape(1, a))


def _gelu_tanh(x):
    return 0.5 * x * (1.0 + jnp.tanh(np.sqrt(2.0 / np.pi).astype(np.float32) * (x + 0.044715 * (x * x * x))))


def _ssm_kernel(u_ref, bb_ref, lam_ref, pow_ref, cm_ref, dsk_ref, wglu_ref, bglu_ref, g_ref, pt_ref, h0_ref,
                z_ref, ht_ref, h_s, hin_s, carry_s, *, rows, steps, chain):
    ns = N_STATE
    cw = (4 * V7X_SUBLANES * V7X_LANES) // rows
    last = (steps - 1) * rows

    @pl.when(pl.program_id(0) == 0)
    def _():
        carry_s[...] = h0_ref[0:1, :]

    h_s[...] = jnp.dot(u_ref[...].astype(BF16), bb_ref[...], preferred_element_type=F32)

    for cc in range(ns // cw):
        cr = slice(cc * cw, (cc + 1) * cw)
        ci = slice(ns + cc * cw, ns + (cc + 1) * cw)
        lr = jnp.broadcast_to(lam_ref[0:1, cr], (rows, cw))
        li = jnp.broadcast_to(lam_ref[1:2, cr], (rows, cw))

        def scan_body(t, carry, cr=cr, ci=ci, lr=lr, li=li):
            hr, hi = carry
            rs = pl.ds(pl.multiple_of(t * rows, rows), rows)
            nr = lr * hr - li * hi + h_s[rs, cr]
            ni = lr * hi + li * hr + h_s[rs, ci]
            h_s[rs, cr] = nr
            h_s[rs, ci] = ni
            return nr, ni

        zero = jnp.zeros((rows, cw), F32)
        lax.fori_loop(0, steps, scan_body, (zero, zero))

    if chain:
        ptr = pow_ref[steps - 1, 0:1, :ns]
        pti = pow_ref[steps - 1, 0:1, ns:]
        carry = carry_s[...]
        for c in range(rows):
            hin_s[c:c + 1, :] = carry
            cr_, ci_ = carry[:, :ns], carry[:, ns:]
            e = h_s[last + c:last + c + 1, :]
            carry = jnp.concatenate([ptr * cr_ - pti * ci_ + e[:, :ns],
                                     ptr * ci_ + pti * cr_ + e[:, ns:]], axis=1)
        carry_s[...] = carry
    else:
        hin_s[...] = h0_ref[...]

    for cc in range(ns // cw):
        cr = slice(cc * cw, (cc + 1) * cw)
        ci = slice(ns + cc * cw, ns + (cc + 1) * cw)
        hr0 = hin_s[:, cr]
        hi0 = hin_s[:, ci]

        def fix_body(t, _, cr=cr, ci=ci, hr0=hr0, hi0=hi0):
            rs = pl.ds(pl.multiple_of(t * rows, rows), rows)
            pr = pow_ref[t, :, cr]
            pi_ = pow_ref[t, :, ci]
            h_s[rs, cr] = h_s[rs, cr] + (pr * hr0 - pi_ * hi0)
            h_s[rs, ci] = h_s[rs, ci] + (pr * hi0 + pi_ * hr0)
            return 0

        lax.fori_loop(0, steps, fix_body, 0)

    ht_ref[...] = h_s[last:last + rows, :]
    y = jnp.dot(h_s[...].astype(BF16), cm_ref[...], preferred_element_type=F32) + dsk_ref[...] * u_ref[...]
    y = _gelu_tanh(y)
    gl = jnp.dot(y.astype(BF16), wglu_ref[...], preferred_element_type=F32) + bglu_ref[...]
    z = _rms(y * _sigmoid(gl), g_ref[...]).astype(BF16)
    z_ref[...] = jnp.dot(pt_ref[...], z, preferred_element_type=F32).astype(z_ref.dtype)


def _ssm_tables(ssm_a_re, ssm_a_im, ssm_log_dt, ssm_b_re, ssm_b_im, ssm_c_re, ssm_c_im, max_steps):
    g, n, ch = SSM_GROUPS, SSM_STATE, SSM_CH
    a_re = ssm_a_re.astype(F32)
    a_im = ssm_a_im.astype(F32)
    dt = jnp.exp(ssm_log_dt.astype(F32))[:, None]
    mag = jnp.exp(dt * a_re)
    lam_re = mag * jnp.cos(dt * a_im)
    lam_im = mag * jnp.sin(dt * a_im)
    nr = lam_re - 1.0
    ni = lam_im
    inv = 1.0 / (a_re * a_re + a_im * a_im)
    coef_re = (nr * a_re + ni * a_im) * inv
    coef_im = (ni * a_re - nr * a_im) * inv
    br = ssm_b_re.astype(F32)
    bi = ssm_b_im.astype(F32)
    bb_re = coef_re[..., None] * br - coef_im[..., None] * bi
    bb_im = coef_re[..., None] * bi + coef_im[..., None] * br
    eye = jnp.eye(g, dtype=F32)
    bmat = lambda b: jnp.einsum("gnc,gh->gchn", b, eye).reshape(g * ch, g * n)
    cmat = lambda c: jnp.einsum("gcn,gh->gnhc", c.astype(F32), eye).reshape(g * n, g * ch)
    bb = jnp.concatenate([bmat(bb_re), bmat(bb_im)], axis=1).astype(BF16)
    cm = jnp.concatenate([cmat(ssm_c_re), -cmat(ssm_c_im)], axis=0).astype(BF16)
    lam = jnp.stack([lam_re.reshape(-1), lam_im.reshape(-1)])

    def pow_step(carry, _):
        pr, pi_ = carry
        nxt = (pr * lam[0] - pi_ * lam[1], pr * lam[1] + pi_ * lam[0])
        return nxt, jnp.concatenate(carry)

    _, pows = lax.scan(pow_step, (lam[0], lam[1]), None, length=max_steps)
    return bb, cm, lam, pows


def _ssm(u_perm, h0, tabs, dsk, w_glu, b_glu, g, perm_t, rows, steps, chain):
    n, w = u_perm.shape
    blk = rows * steps
    bb, cm, lam, pows = tabs
    row = pl.BlockSpec((blk, w), lambda i: (i, 0))
    kern = functools.partial(_ssm_kernel, rows=rows, steps=steps, chain=chain)
    return pl.pallas_call(
        kern,
        out_shape=(jax.ShapeDtypeStruct((n, w), BF16), jax.ShapeDtypeStruct((rows, 2 * N_STATE), F32)),
        grid=(n // blk,),
        in_specs=[row, _full(bb.shape), _full(lam.shape), _full((steps, rows, 2 * N_STATE)), _full(cm.shape),
                  _full((1, w)), _full((w, w)), _full((1, w)), _full((1, w)), _full((blk, blk)),
                  _full((rows, 2 * N_STATE))],
        out_specs=(row, _full((rows, 2 * N_STATE))),
        scratch_shapes=[pltpu.VMEM((blk, 2 * N_STATE), F32),
                        pltpu.VMEM((rows, 2 * N_STATE), F32),
                        pltpu.VMEM((1, 2 * N_STATE), F32)],
        compiler_params=_cparams(),
        name=f"ssm_r{rows}",
    )(u_perm, bb, lam, jnp.broadcast_to(pows[:steps, None, :], (steps, rows, 2 * N_STATE)), cm,
      dsk.reshape(1, w), w_glu.astype(BF16), b_glu.reshape(1, w),
      g.reshape(1, w), perm_t, h0)


def _outproj_kernel(x_ref, a_ref, z_ref, gt_ref, sh_ref, sc_ref, g_ref, wo_ref, wr_ref, br_ref,
                    x1_ref, h2_ref, ids_ref, gates_ref, wbf_ref):
    @pl.when(pl.program_id(0) == 0)
    def _():
        wbf_ref[...] = wo_ref[...].astype(BF16)

    a = ATTN_WIDTH
    mixed = (jnp.dot(a_ref[...].astype(BF16), wbf_ref[:a, :], preferred_element_type=F32)
             + jnp.dot(z_ref[...], wbf_ref[a:, :], preferred_element_type=F32))
    x1 = x_ref[...] + gt_ref[...] * mixed
    x1_ref[...] = x1
    h2 = (_rms(x1, g_ref[...]) * (1.0 + sc_ref[...]) + sh_ref[...]).astype(BF16)
    h2_ref[...] = h2
    lg = jnp.dot(h2, wr_ref[...], preferred_element_type=F32) + br_ref[...]
    tm = lg.shape[0]
    lane = lax.broadcasted_iota(jnp.int32, (tm, V7X_LANES), 1).astype(F32)
    vals, idxs = [], []
    for _ in range(TOP_K):
        m = jnp.max(lg, axis=1, keepdims=True)
        idx = jnp.min(jnp.where(lg == m, lane, float(V7X_LANES)), axis=1, keepdims=True)
        vals.append(m)
        idxs.append(idx)
        lg = jnp.where(lane == idx, MASKED * 2, lg)
    es = [jnp.exp(v - vals[0]) for v in vals]
    inv = 1.0 / (es[0] + es[1] + es[2] + es[3])
    ids = jnp.zeros((tm, V7X_LANES), F32)
    gates = jnp.zeros((tm, V7X_LANES), F32)
    for k in range(TOP_K):
        ids = jnp.where(lane == float(k), idxs[k], ids)
        gates = jnp.where(lane == float(k), es[k] * inv, gates)
    ids_ref[...] = ids.astype(jnp.int32)
    gates_ref[...] = gates


def _outproj(x, a, z, gt, sh, sc, g, w_out, wr_pad, br_pad, tm):
    n, d = x.shape
    aw = ATTN_WIDTH
    mod_spec = (pl.BlockSpec((1, d), lambda i: (0, 0)) if gt.shape[0] == 1
                else pl.BlockSpec((tm, d), lambda i: (i, 0)))
    row = lambda w: pl.BlockSpec((tm, w), lambda i: (i, 0))
    return pl.pallas_call(
        _outproj_kernel,
        out_shape=(jax.ShapeDtypeStruct((n, d), F32), jax.ShapeDtypeStruct((n, d), BF16),
                   jax.ShapeDtypeStruct((n, V7X_LANES), jnp.int32), jax.ShapeDtypeStruct((n, V7X_LANES), F32)),
        grid=(n // tm,),
        in_specs=[row(d), row(aw), row(SSM_WIDTH), mod_spec, mod_spec, mod_spec, _full((1, d)),
                  _full(w_out.shape), _full(wr_pad.shape), _full(br_pad.shape)],
        out_specs=(row(d), row(d), row(V7X_LANES), row(V7X_LANES)),
        scratch_shapes=[pltpu.VMEM(w_out.shape, BF16)],
        compiler_params=_cparams(),
        name="outproj_router",
    )(x, a, z, gt, sh, sc, g.reshape(1, d), w_out, wr_pad, br_pad)


def _moe_kernel(te_ref, nu_ref, x_ref, wgu_ref, bgu_ref, wd_ref, bd_ref, y_ref, wgu_bf, wd_bf):
    t = pl.program_id(0)

    @pl.when(t < nu_ref[0])
    def _():
        prev = te_ref[jnp.maximum(t - 1, 0)]
        fresh = jnp.logical_or(t == 0, te_ref[t] != prev)

        @pl.when(fresh)
        def _():
            wgu_bf[...] = wgu_ref[...].astype(BF16)
            wd_bf[...] = wd_ref[...].astype(BF16)

        f = D_EXPERT
        gu = jnp.dot(x_ref[...], wgu_bf[...], preferred_element_type=F32) + bgu_ref[...]
        gate = jnp.minimum(gu[:, :f], SWIGLU_LIMIT)
        up = jnp.clip(gu[:, f:], -SWIGLU_LIMIT, SWIGLU_LIMIT)
        act = (up + 1.0) * gate * _sigmoid(SWIGLU_ALPHA * gate)
        y_ref[...] = jnp.dot(act.astype(BF16), wd_bf[...], preferred_element_type=F32) + bd_ref[...]


def _moe(tile_expert, n_used, x_sorted, w_gate_up, b_gate_up, w_down, b_down, tm):
    npad, d = x_sorted.shape
    e, _, f2 = w_gate_up.shape
    nt = npad // tm
    rowmap = lambda t, te, nu: (jnp.minimum(t, nu[0] - 1), 0)
    wmap = lambda t, te, nu: (te[t], 0, 0)
    grid_spec = pltpu.PrefetchScalarGridSpec(
        num_scalar_prefetch=2,
        grid=(nt,),
        in_specs=[pl.BlockSpec((tm, d), rowmap),
                  pl.BlockSpec((None, d, f2), wmap),
                  pl.BlockSpec((None, 1, f2), wmap),
                  pl.BlockSpec((None, f2 // 2, d), wmap),
                  pl.BlockSpec((None, 1, d), wmap)],
        out_specs=pl.BlockSpec((tm, d), rowmap),
        scratch_shapes=[pltpu.VMEM((d, f2), BF16), pltpu.VMEM((f2 // 2, d), BF16)],
    )
    return pl.pallas_call(
        _moe_kernel,
        out_shape=jax.ShapeDtypeStruct((npad, d), F32),
        grid_spec=grid_spec,
        compiler_params=_cparams(),
        name="moe_experts",
    )(tile_expert, n_used, x_sorted, w_gate_up, b_gate_up.reshape(e, 1, f2), w_down, b_down.reshape(e, 1, d))


def _final_kernel(x_ref, ff_ref, gt_ref, g_ref, y_ref):
    y_ref[...] = _rms(x_ref[...] + gt_ref[...] * ff_ref[...], g_ref[...])


def _final(x1, ff, gt, g, tm):
    n, d = x1.shape
    mod_spec = (pl.BlockSpec((1, d), lambda i: (0, 0)) if gt.shape[0] == 1
                else pl.BlockSpec((tm, d), lambda i: (i, 0)))
    row = pl.BlockSpec((tm, d), lambda i: (i, 0))
    return pl.pallas_call(
        _final_kernel,
        out_shape=jax.ShapeDtypeStruct((n, d), F32),
        grid=(n // tm,),
        in_specs=[row, row, mod_spec, _full((1, d))],
        out_specs=row,
        compiler_params=_cparams(),
        name="final_norm",
    )(x1, ff, gt, g.reshape(1, d))


def _route(ids, tm):
    n_tok = ids.shape[0]
    n_asg = n_tok * TOP_K
    flat = ids.reshape(-1)
    order = jnp.argsort(flat, stable=True).astype(jnp.int32)
    counts = jnp.zeros((N_EXPERTS,), jnp.int32).at[flat].add(1)
    tiles = (counts + tm - 1) // tm
    tile_end = jnp.cumsum(tiles)
    pad_start = (tile_end - tiles) * tm
    start = jnp.cumsum(counts) - counts
    n_tiles = n_asg // tm + N_EXPERTS
    n_used = tile_end[-1:].astype(jnp.int32)
    tile_expert = jnp.searchsorted(tile_end, jnp.arange(n_tiles, dtype=jnp.int32), side="right")
    tile_expert = jnp.minimum(tile_expert, N_EXPERTS - 1).astype(jnp.int32)
    last_expert = tile_expert[jnp.maximum(n_used[0] - 1, 0)]
    tile_expert = jnp.where(jnp.arange(n_tiles) < n_used[0], tile_expert, last_expert)
    p = jnp.arange(n_tiles * tm, dtype=jnp.int32)
    pe = tile_expert[p // tm]
    local = p - pad_start[pe]
    valid = local < counts[pe]
    src = jnp.where(valid, start[pe] + local, 0)
    token_of_row = jnp.where(valid, order[src] // TOP_K, 0)
    rank = jnp.zeros((n_asg,), jnp.int32).at[order].set(jnp.arange(n_asg, dtype=jnp.int32))
    pos = pad_start[flat] + rank - start[flat]
    return tile_expert, n_used, token_of_row, pos.reshape(n_tok, TOP_K)


def kernel(x_prompt, x_sample, cache_k_win, cache_v_win, state_ssm_re, state_ssm_im, c_prompt, c_sample,
           w_ada, b_ada, g_norm1, w_in, ssm_a_re, ssm_a_im, ssm_log_dt, ssm_b_re, ssm_b_im, ssm_c_re, ssm_c_im,
           ssm_d, w_glu, b_glu, g_out_attn, g_out_ssm, w_out, g_norm2, w_router, b_router, w_gate_up, b_gate_up,
           w_down, b_down, g_final):
    depth = w_ada.shape[0]
    assert depth == 1 and x_prompt.shape[0] == 1
    bp, t, d = x_prompt.shape
    bs, ts, _ = x_sample.shape
    ns = bs * ts
    l = 0
    a = ATTN_WIDTH

    n_c = bp + bs
    c_pad = -(-n_c // V7X_SUBLANES) * V7X_SUBLANES
    c_rows = jnp.concatenate([c_prompt, c_sample, jnp.zeros((c_pad - n_c, d), F32)], axis=0)
    mod = _ada_modulation(c_rows, w_ada[l], b_ada[l])
    mod_p = [mod[0:1, i * d:(i + 1) * d] for i in range(N_MOD)]
    mod_s = [jnp.repeat(mod[bp:bp + bs, i * d:(i + 1) * d], ts, axis=0) for i in range(N_MOD)]

    steps_p = TOKEN_TILE // SSM_ROWS_PROMPT
    perm_p = _chunk_perm(SSM_ROWS_PROMPT, steps_p)
    perm_s = _chunk_perm(bs, ts)
    tabs = _ssm_tables(ssm_a_re[l], ssm_a_im[l], ssm_log_dt[l], ssm_b_re[l], ssm_b_im[l],
                       ssm_c_re[l], ssm_c_im[l], max(steps_p, ts))
    wr_pad = jnp.zeros((d, V7X_LANES), F32).at[:, :N_EXPERTS].set(w_router[l]).astype(BF16)
    br_pad = jnp.full((1, V7X_LANES), MASKED, F32).at[0, :N_EXPERTS].set(b_router[l])

    xp = x_prompt.reshape(t, d)
    qp, kp, vp, kpf, vpf, up = _inproj(xp, mod_p[0], mod_p[1], g_norm1[l], w_in[l],
                                       jnp.asarray(perm_p, BF16), TOKEN_TILE)
    outs = [_attn_branch(qp, kp, vp, dil) for _, dil in DILATED_BRANCHES]
    ap = _attn_combine([o for o, _ in outs], [s for _, s in outs], g_out_attn[l])
    zeros_h = jnp.zeros((SSM_ROWS_PROMPT, 2 * N_STATE), F32)
    zp, hp = _ssm(up, zeros_h, tabs, ssm_d[l], w_glu[l], b_glu[l], g_out_ssm[l],
                  jnp.asarray(perm_p.T, BF16), SSM_ROWS_PROMPT, steps_p, True)
    x1p, h2p, idp, gp = _outproj(xp, ap, zp, mod_p[2], mod_p[3], mod_p[4], g_norm2[l], w_out[l],
                                 wr_pad, br_pad, TOKEN_TILE)

    xs = x_sample.reshape(ns, d)
    qs, ks, vs, ksf, vsf, us = _inproj(xs, mod_s[0], mod_s[1], g_norm1[l], w_in[l],
                                       jnp.asarray(perm_s, BF16), ns)
    as_ = _attn_sample(qs.reshape(bs, ts, a), ks.reshape(bs, ts, a), vs.reshape(bs, ts, a),
                       cache_k_win[l].reshape(bs, -1, a), cache_v_win[l].reshape(bs, -1, a), g_out_attn[l])
    h0s = jnp.concatenate([state_ssm_re[l].reshape(bs, N_STATE), state_ssm_im[l].reshape(bs, N_STATE)], axis=1)
    zs, hs = _ssm(us, h0s, tabs, ssm_d[l], w_glu[l], b_glu[l], g_out_ssm[l],
                  jnp.asarray(perm_s.T, BF16), bs, ts, False)
    x1s, h2s, ids_, gs = _outproj(xs, as_.reshape(ns, a), zs, mod_s[2], mod_s[3], mod_s[4], g_norm2[l], w_out[l],
                                  wr_pad, br_pad, ns)

    h2 = jnp.concatenate([h2p, h2s], axis=0)
    ids = jnp.concatenate([idp[:, :TOP_K], ids_[:, :TOP_K]], axis=0)
    gates = jnp.concatenate([gp[:, :TOP_K], gs[:, :TOP_K]], axis=0)
    tile_expert, n_used, token_of_row, pos = _route(ids, MOE_TILE)
    x_sorted = jnp.take(h2, token_of_row, axis=0)
    y_sorted = _moe(tile_expert, n_used, x_sorted, w_gate_up[l], b_gate_up[l], w_down[l], b_down[l], MOE_TILE)
    ff = jnp.sum(jnp.take(y_sorted, pos, axis=0) * gates[:, :, None], axis=1)

    y_prompt = _final(x1p, ff[:t], mod_p[5], g_final, TOKEN_TILE).reshape(bp, t, d)
    y_sample = _final(x1s, ff[t:], mod_s[5], g_final, ns).reshape(bs, ts, d)

    keep = min(MAX_WINDOW, t)
    shp = (1, bp, keep, N_HEADS, HEAD_DIM)
    k_win = kpf[t - keep:].reshape(shp)
    v_win = vpf[t - keep:].reshape(shp)
    st = (1, bp, SSM_GROUPS, SSM_STATE)
    hp_last = hp[SSM_ROWS_PROMPT - 1]
    ss = (1, bs, SSM_GROUPS, SSM_STATE)
    return (y_prompt, y_sample, k_win, v_win,
            hp_last[:N_STATE].reshape(st), hp_last[N_STATE:].reshape(st),
            ksf.reshape(1, bs, ts, N_HEADS, HEAD_DIM), vsf.reshape(1, bs, ts, N_HEADS, HEAD_DIM),
            hs[:, :N_STATE].reshape(ss), hs[:, N_STATE:].reshape(ss))
```

```python
import functools

import numpy as np
import jax
import jax.numpy as jnp
from jax import lax
from jax.experimental import pallas as pl
from jax.experimental.pallas import tpu as pltpu

F32 = jnp.float32
BF16 = jnp.bfloat16

D_MODEL = 1024
N_HEADS = 8
HEAD_DIM = 64
ATTN_WIDTH = N_HEADS * HEAD_DIM
DILATED_BRANCHES = ((128, 1), (512, 4), (2048, 16))
KEYS_PER_BRANCH = 129
MAX_WINDOW = 2048
SSM_WIDTH = D_MODEL - ATTN_WIDTH
SSM_CH = 16
SSM_GROUPS = SSM_WIDTH // SSM_CH
SSM_STATE = 64
N_STATE = SSM_GROUPS * SSM_STATE
N_EXPERTS = 32
TOP_K = 4
D_EXPERT = D_MODEL
SWIGLU_LIMIT = 7.0
SWIGLU_ALPHA = 1.702
N_MOD = 6
EPS = 1e-6
MASKED = -1e30

V7X_LANES = 128
V7X_SUBLANES = 8
V7X_VMEM_LIMIT_BYTES = 56 * 1024 * 1024

TOKEN_TILE = 512
Q_TILE = 128
SSM_ROWS_PROMPT = 8
MOE_TILE = 512
ROW_ALIGN = 16
GROUP_CAP = -(-(TOKEN_TILE * TOP_K + N_EXPERTS * (ROW_ALIGN - 1)) // (2 * V7X_LANES)) * (2 * V7X_LANES)


def _cparams(n_axes=1):
    return pltpu.CompilerParams(
        dimension_semantics=("arbitrary",) * n_axes,
        vmem_limit_bytes=V7X_VMEM_LIMIT_BYTES,
    )


def _full(shape):
    n = len(shape)
    return pl.BlockSpec(shape, lambda *_: (0,) * n)


def _rms(x, g):
    return x * lax.rsqrt(jnp.mean(x * x, axis=-1, keepdims=True) + EPS) * g


def _sigmoid(x):
    return 1.0 / (1.0 + jnp.exp(-x))


def _ada_kernel(c_ref, w_ref, b_ref, o_ref):
    c = c_ref[...]
    s = (c * _sigmoid(c)).astype(BF16)
    o_ref[...] = jnp.dot(s, w_ref[...].astype(BF16), preferred_element_type=F32) + b_ref[...]


def _ada_modulation(c_rows, w_ada, b_ada):
    m, d = c_rows.shape
    n = w_ada.shape[1]
    tn = n // 4
    return pl.pallas_call(
        _ada_kernel,
        out_shape=jax.ShapeDtypeStruct((m, n), F32),
        grid=(n // tn,),
        in_specs=[_full((m, d)),
                  pl.BlockSpec((d, tn), lambda j: (0, j)),
                  pl.BlockSpec((1, tn), lambda j: (0, j))],
        out_specs=pl.BlockSpec((m, tn), lambda j: (0, j)),
        compiler_params=_cparams(),
        name="ada_modulation",
    )(c_rows, w_ada, b_ada.reshape(1, n))


def _inproj_kernel(x_ref, sh_ref, sc_ref, g_ref, w_ref, perm_ref,
                   q_ref, k_ref, v_ref, kf_ref, vf_ref, u_ref, wbf_ref):
    @pl.when(pl.program_id(0) == 0)
    def _():
        wbf_ref[...] = w_ref[...].astype(BF16)

    h = _rms(x_ref[...], g_ref[...]) * (1.0 + sc_ref[...]) + sh_ref[...]
    hb = h.astype(BF16)
    a = ATTN_WIDTH
    proj = jnp.dot(hb, wbf_ref[:, :3 * a], preferred_element_type=F32)
    q_ref[...] = (proj[:, :a] * (HEAD_DIM ** -0.5)).astype(BF16)
    k = proj[:, a:2 * a]
    v = proj[:, 2 * a:]
    k_ref[...] = k.astype(BF16)
    v_ref[...] = v.astype(BF16)
    kf_ref[...] = k
    vf_ref[...] = v
    hp = jnp.dot(perm_ref[...], hb, preferred_element_type=F32).astype(BF16)
    u_ref[...] = jnp.dot(hp, wbf_ref[:, 3 * a:], preferred_element_type=F32)


def _inproj(x, sh, sc, g, w_in, perm, tm):
    n, d = x.shape
    a = ATTN_WIDTH
    mod_rows = sh.shape[0]
    mod_spec = (pl.BlockSpec((1, d), lambda i: (0, 0)) if mod_rows == 1
                else pl.BlockSpec((tm, d), lambda i: (i, 0)))
    row = lambda w: pl.BlockSpec((tm, w), lambda i: (i, 0))
    return pl.pallas_call(
        _inproj_kernel,
        out_shape=(jax.ShapeDtypeStruct((n, a), BF16),) * 3
        + (jax.ShapeDtypeStruct((n, a), F32),) * 2
        + (jax.ShapeDtypeStruct((n, SSM_WIDTH), F32),),
        grid=(n // tm,),
        in_specs=[row(d), mod_spec, mod_spec, _full((1, d)), _full(w_in.shape), _full((tm, tm))],
        out_specs=(row(a),) * 5 + (row(SSM_WIDTH),),
        scratch_shapes=[pltpu.VMEM(w_in.shape, BF16)],
        compiler_params=_cparams(),
        name="inproj",
    )(x, sh, sc, g.reshape(1, d), w_in, perm)


def _chunk_perm(rows, steps):
    n = rows * steps
    p = np.zeros((n, n), np.float32)
    c, t = np.meshgrid(np.arange(rows), np.arange(steps), indexing="ij")
    p[(t * rows + c).ravel(), (c * steps + t).ravel()] = 1.0
    return p


def _alibi_slopes():
    return np.exp2(-8.0 * np.arange(1, N_HEADS + 1, dtype=np.float64) / N_HEADS).astype(np.float32)


def _branch_bias(dil):
    qi = np.arange(Q_TILE)[:, None]
    col = np.arange(2 * Q_TILE)[None, :]
    j = Q_TILE + qi - col
    valid = (j >= 0) & (j <= Q_TILE)
    dist = (j * dil).astype(np.float32)
    tabs = []
    for first in (True, False):
        ok = valid & (col >= Q_TILE) if first else valid
        per_head = [np.where(ok, -s * dist, np.float32(MASKED)) for s in _alibi_slopes()]
        tabs.append(np.concatenate(per_head, axis=0))
    return np.stack(tabs).astype(np.float32)


def _attn_branch_kernel(q_ref, kp_ref, kc_ref, vp_ref, vc_ref, bias_ref, o_ref, lse_ref):
    sel = jnp.minimum(pl.program_id(0), 1)
    lane = lax.broadcasted_iota(jnp.int32, (Q_TILE, V7X_LANES), 1)
    lo = lane < HEAD_DIM
    lse_acc = jnp.zeros((Q_TILE, V7X_LANES), F32)
    for p in range(N_HEADS // 2):
        cs = slice(V7X_LANES * p, V7X_LANES * (p + 1))
        q2 = q_ref[:, cs]
        zero = jnp.zeros_like(q2)
        qq = jnp.concatenate([jnp.where(lo, q2, zero), jnp.where(lo, zero, q2)], axis=0)
        kk = jnp.concatenate([kp_ref[:, cs], kc_ref[:, cs]], axis=0)
        vv = jnp.concatenate([vp_ref[:, cs], vc_ref[:, cs]], axis=0)
        s = lax.dot_general(qq, kk, (((1,), (1,)), ((), ())), preferred_element_type=F32)
        s = s + bias_ref[sel, 2 * Q_TILE * p:2 * Q_TILE * (p + 1), :]
        m = jnp.max(s, axis=1, keepdims=True)
        e = jnp.exp(s - m)
        l = jnp.sum(e, axis=1, keepdims=True)
        eb = e.astype(BF16)
        o0 = jnp.dot(eb[:Q_TILE], vv, preferred_element_type=F32) * (1.0 / l[:Q_TILE])
        o1 = jnp.dot(eb[Q_TILE:], vv, preferred_element_type=F32) * (1.0 / l[Q_TILE:])
        o_ref[:, cs] = jnp.where(lo, o0, o1).astype(o_ref.dtype)
        lse = m + jnp.log(l)
        lse_acc = jnp.where(lane == 2 * p, lse[:Q_TILE], lse_acc)
        lse_acc = jnp.where(lane == 2 * p + 1, lse[Q_TILE:], lse_acc)
    lse_ref[...] = lse_acc[:, :N_HEADS]


def _attn_branch(q, k, v, dil):
    t = q.shape[0]
    a = ATTN_WIDTH
    rows = t // dil
    qv, kv, vv = (z.reshape(rows, dil * a) for z in (q, k, v))
    cur = pl.BlockSpec((Q_TILE, a), lambda i, r: (i, r))
    prev = pl.BlockSpec((Q_TILE, a), lambda i, r: (jnp.maximum(i - 1, 0), r))
    bias = jnp.asarray(_branch_bias(dil))
    o, lse = pl.pallas_call(
        _attn_branch_kernel,
        out_shape=(jax.ShapeDtypeStruct((rows, dil * a), BF16),
                   jax.ShapeDtypeStruct((dil, rows, N_HEADS), F32)),
        grid=(rows // Q_TILE, dil),
        in_specs=[cur, prev, cur, prev, cur, _full(bias.shape)],
        out_specs=(cur, pl.BlockSpec((None, Q_TILE, N_HEADS), lambda i, r: (r, i, 0))),
        compiler_params=_cparams(2),
        name=f"attn_branch_d{dil}",
    )(qv, kv, kv, vv, vv, bias)
    return o.reshape(t, a), lse.transpose(1, 0, 2).reshape(t, N_HEADS)


def _attn_combine_kernel(o1_ref, o2_ref, o3_ref, l1_ref, l2_ref, l3_ref, g_ref, a_ref):
    ls = [l1_ref[...], l2_ref[...], l3_ref[...]]
    top = jnp.maximum(jnp.maximum(ls[0], ls[1]), ls[2])
    ws = [jnp.exp(l - top) for l in ls]
    inv = 1.0 / (ws[0] + ws[1] + ws[2])
    cs = [w * inv for w in ws]
    tq = a_ref.shape[0]
    lane = lax.broadcasted_iota(jnp.int32, (tq, V7X_LANES), 1)
    lo = lane < HEAD_DIM
    cols = []
    for p in range(N_HEADS // 2):
        sl = slice(V7X_LANES * p, V7X_LANES * (p + 1))
        acc = jnp.zeros((tq, V7X_LANES), F32)
        for c, o_ref in zip(cs, (o1_ref, o2_ref, o3_ref)):
            cexp = jnp.where(lo,
                             jnp.broadcast_to(c[:, 2 * p:2 * p + 1], (tq, V7X_LANES)),
                             jnp.broadcast_to(c[:, 2 * p + 1:2 * p + 2], (tq, V7X_LANES)))
            acc = acc + cexp * o_ref[:, sl].astype(F32)
        cols.append(acc)
    o = jnp.concatenate(cols, axis=1)
    a_ref[...] = _rms(o, g_ref[...]).astype(a_ref.dtype)


def _attn_combine(os_, lses, g):
    t, a = os_[0].shape
    tq = 1024
    row = lambda w: pl.BlockSpec((tq, w), lambda i: (i, 0))
    return pl.pallas_call(
        _attn_combine_kernel,
        out_shape=jax.ShapeDtypeStruct((t, a), BF16),
        grid=(t // tq,),
        in_specs=[row(a)] * 3 + [row(N_HEADS)] * 3 + [_full((1, a))],
        out_specs=row(a),
        compiler_params=_cparams(),
        name="attn_combine",
    )(*os_, *lses, g.reshape(1, a))


NEW_KEY_PAD = V7X_LANES


def _sample_bias(win, steps):
    ncol = win + NEW_KEY_PAD
    col = np.arange(ncol)[None, :]
    t = np.arange(steps)[:, None]
    is_new = col >= win
    dist = np.where(is_new, t - (col - win), win + t - col)
    ok_new = (col - win) < steps
    tabs = []
    for window, dil in DILATED_BRANCHES:
        ok = (dist >= 0) & (dist <= window) & (dist % dil == 0) & (~is_new | ok_new)
        per_head = [np.where(ok, -s * dist.astype(np.float32), np.float32(MASKED)) for s in _alibi_slopes()]
        tabs.append(np.concatenate(per_head, axis=0))
    return np.stack(tabs).astype(np.float32)


def _attn_sample_kernel(q_ref, kn_ref, vn_ref, ck_ref, cv_ref, bias_ref, g_ref, o_ref):
    steps = q_ref.shape[0]
    nq = N_HEADS * steps
    a = ATTN_WIDTH
    row = lax.broadcasted_iota(jnp.int32, (nq, a), 0)
    col = lax.broadcasted_iota(jnp.int32, (nq, a), 1)
    own = (row // steps) == (col // HEAD_DIM)
    qf = q_ref[...].astype(F32)
    qbd = jnp.where(own, jnp.concatenate([qf] * N_HEADS, axis=0), 0.0).astype(BF16)
    pad = jnp.zeros((NEW_KEY_PAD - steps, a), F32)
    k_all = jnp.concatenate([ck_ref[...], kn_ref[...].astype(F32), pad], axis=0).astype(BF16)
    v_all = jnp.concatenate([cv_ref[...], vn_ref[...].astype(F32), pad], axis=0).astype(BF16)
    s = lax.dot_general(qbd, k_all, (((1,), (1,)), ((), ())), preferred_element_type=F32)
    ms, ls, es = [], [], []
    for b in range(len(DILATED_BRANCHES)):
        sb = s + bias_ref[b]
        m = jnp.max(sb, axis=1, keepdims=True)
        e = jnp.exp(sb - m)
        ms.append(m)
        ls.append(jnp.sum(e, axis=1, keepdims=True))
        es.append(e)
    o_all = jnp.dot(jnp.concatenate(es, axis=0).astype(BF16), v_all, preferred_element_type=F32)
    top = jnp.maximum(jnp.maximum(ms[0], ms[1]), ms[2])
    num = jnp.zeros((nq, a), F32)
    den = jnp.zeros((nq, 1), F32)
    for b in range(len(DILATED_BRANCHES)):
        w = jnp.exp(ms[b] - top)
        num = num + o_all[b * nq:(b + 1) * nq] * w
        den = den + ls[b] * w
    full = jnp.where(own, num * (1.0 / den), 0.0)
    o = full[0:steps]
    for h in range(1, N_HEADS):
        o = o + full[h * steps:(h + 1) * steps]
    o_ref[...] = _rms(o, g_ref[...])


def _attn_sample(q, kn, vn, cache_k, cache_v, g):
    b, steps, a = q.shape
    win = cache_k.shape[1]
    bias = jnp.asarray(_sample_bias(win, steps))
    new = pl.BlockSpec((None, steps, a), lambda i: (i, 0, 0))
    cache = pl.BlockSpec((None, win, a), lambda i: (i, 0, 0))
    return pl.pallas_call(
        _attn_sample_kernel,
        out_shape=jax.ShapeDtypeStruct((b, steps, a), F32),
        grid=(b,),
        in_specs=[new, new, new, cache, cache, _full(bias.shape), _full((1, a))],
        out_specs=new,
        compiler_params=_cparams(),
        name="attn_sample",
    )(q, kn, vn, cache_k, cache_v, bias, g.reshape(1, a))


def _gelu_tanh(x):
    return 0.5 * x * (1.0 + jnp.tanh(np.sqrt(2.0 / np.pi).astype(np.float32) * (x + 0.044715 * (x * x * x))))


def _ssm_kernel(u_ref, bb_ref, lam_ref, pow_ref, cm_ref, dsk_ref, wglu_ref, bglu_ref, g_ref, pt_ref, h0_ref,
                z_ref, ht_ref, h_s, hin_s, carry_s, *, rows, steps, chain):
    ns = N_STATE
    cw = (4 * V7X_SUBLANES * V7X_LANES) // rows
    last = (steps - 1) * rows

    @pl.when(pl.program_id(0) == 0)
    def _():
        carry_s[...] = h0_ref[0:1, :]

    h_s[...] = jnp.dot(u_ref[...].astype(BF16), bb_ref[...], preferred_element_type=F32)

    for cc in range(ns // cw):
        cr = slice(cc * cw, (cc + 1) * cw)
        ci = slice(ns + cc * cw, ns + (cc + 1) * cw)
        lr = jnp.broadcast_to(lam_ref[0:1, cr], (rows, cw))
        li = jnp.broadcast_to(lam_ref[1:2, cr], (rows, cw))

        def scan_body(t, carry, cr=cr, ci=ci, lr=lr, li=li):
            hr, hi = carry
            rs = pl.ds(pl.multiple_of(t * rows, rows), rows)
            nr = lr * hr - li * hi + h_s[rs, cr]
            ni = lr * hi + li * hr + h_s[rs, ci]
            h_s[rs, cr] = nr
            h_s[rs, ci] = ni
            return nr, ni

        zero = jnp.zeros((rows, cw), F32)
        lax.fori_loop(0, steps, scan_body, (zero, zero))

    if chain:
        ptr = pow_ref[steps - 1, 0:1, :ns]
        pti = pow_ref[steps - 1, 0:1, ns:]
        carry = carry_s[...]
        for c in range(rows):
            hin_s[c:c + 1, :] = carry
            cr_, ci_ = carry[:, :ns], carry[:, ns:]
            e = h_s[last + c:last + c + 1, :]
            carry = jnp.concatenate([ptr * cr_ - pti * ci_ + e[:, :ns],
                                     ptr * ci_ + pti * cr_ + e[:, ns:]], axis=1)
        carry_s[...] = carry
    else:
        hin_s[...] = h0_ref[...]

    for cc in range(ns // cw):
        cr = slice(cc * cw, (cc + 1) * cw)
        ci = slice(ns + cc * cw, ns + (cc + 1) * cw)
        hr0 = hin_s[:, cr]
        hi0 = hin_s[:, ci]

        def fix_body(t, _, cr=cr, ci=ci, hr0=hr0, hi0=hi0):
            rs = pl.ds(pl.multiple_of(t * rows, rows), rows)
            pr = pow_ref[t, :, cr]
            pi_ = pow_ref[t, :, ci]
            h_s[rs, cr] = h_s[rs, cr] + (pr * hr0 - pi_ * hi0)
            h_s[rs, ci] = h_s[rs, ci] + (pr * hi0 + pi_ * hr0)
            return 0

        lax.fori_loop(0, steps, fix_body, 0)

    ht_ref[...] = h_s[last:last + rows, :]
    y = jnp.dot(h_s[...].astype(BF16), cm_ref[...], preferred_element_type=F32) + dsk_ref[...] * u_ref[...]
    y = _gelu_tanh(y)
    gl = jnp.dot(y.astype(BF16), wglu_ref[...], preferred_element_type=F32) + bglu_ref[...]
    z = _rms(y * _sigmoid(gl), g_ref[...]).astype(BF16)
    z_ref[...] = jnp.dot(pt_ref[...], z, preferred_element_type=F32).astype(z_ref.dtype)


def _ssm_tables(ssm_a_re, ssm_a_im, ssm_log_dt, ssm_b_re, ssm_b_im, ssm_c_re, ssm_c_im, max_steps):
    g, n, ch = SSM_GROUPS, SSM_STATE, SSM_CH
    a_re = ssm_a_re.astype(F32)
    a_im = ssm_a_im.astype(F32)
    dt = jnp.exp(ssm_log_dt.astype(F32))[:, None]
    mag = jnp.exp(dt * a_re)
    lam_re = mag * jnp.cos(dt * a_im)
    lam_im = mag * jnp.sin(dt * a_im)
    nr = lam_re - 1.0
    ni = lam_im
    inv = 1.0 / (a_re * a_re + a_im * a_im)
    coef_re = (nr * a_re + ni * a_im) * inv
    coef_im = (ni * a_re - nr * a_im) * inv
    br = ssm_b_re.astype(F32)
    bi = ssm_b_im.astype(F32)
    bb_re = coef_re[..., None] * br - coef_im[..., None] * bi
    bb_im = coef_re[..., None] * bi + coef_im[..., None] * br
    eye = jnp.eye(g, dtype=F32)
    bmat = lambda b: jnp.einsum("gnc,gh->gchn", b, eye).reshape(g * ch, g * n)
    cmat = lambda c: jnp.einsum("gcn,gh->gnhc", c.astype(F32), eye).reshape(g * n, g * ch)
    bb = jnp.concatenate([bmat(bb_re), bmat(bb_im)], axis=1).astype(BF16)
    cm = jnp.concatenate([cmat(ssm_c_re), -cmat(ssm_c_im)], axis=0).astype(BF16)
    lam = jnp.stack([lam_re.reshape(-1), lam_im.reshape(-1)])

    k = jnp.arange(1, max_steps + 1, dtype=F32)[:, None]
    kdt = k * dt.reshape(1, -1).repeat(n, axis=1)
    pmag = jnp.exp(kdt * a_re.reshape(1, -1))
    parg = kdt * a_im.reshape(1, -1)
    pows = jnp.concatenate([pmag * jnp.cos(parg), pmag * jnp.sin(parg)], axis=1)
    return bb, cm, lam, pows


def _ssm(u_perm, h0, tabs, dsk, w_glu, b_glu, g, perm_t, rows, steps, chain):
    n, w = u_perm.shape
    blk = rows * steps
    bb, cm, lam, pows = tabs
    row = pl.BlockSpec((blk, w), lambda i: (i, 0))
    kern = functools.partial(_ssm_kernel, rows=rows, steps=steps, chain=chain)
    return pl.pallas_call(
        kern,
        out_shape=(jax.ShapeDtypeStruct((n, w), BF16), jax.ShapeDtypeStruct((rows, 2 * N_STATE), F32)),
        grid=(n // blk,),
        in_specs=[row, _full(bb.shape), _full(lam.shape), _full((steps, rows, 2 * N_STATE)), _full(cm.shape),
                  _full((1, w)), _full((w, w)), _full((1, w)), _full((1, w)), _full((blk, blk)),
                  _full((rows, 2 * N_STATE))],
        out_specs=(row, _full((rows, 2 * N_STATE))),
        scratch_shapes=[pltpu.VMEM((blk, 2 * N_STATE), F32),
                        pltpu.VMEM((rows, 2 * N_STATE), F32),
                        pltpu.VMEM((1, 2 * N_STATE), F32)],
        compiler_params=_cparams(),
        name=f"ssm_r{rows}",
    )(u_perm, bb, lam, jnp.broadcast_to(pows[:steps, None, :], (steps, rows, 2 * N_STATE)), cm,
      dsk.reshape(1, w), w_glu.astype(BF16), b_glu.reshape(1, w),
      g.reshape(1, w), perm_t, h0)


def _outproj_kernel(x_ref, a_ref, z_ref, gt_ref, sh_ref, sc_ref, g_ref, wo_ref, wr_ref, br_ref, ltri_ref, utri_ref,
                    x1_ref, xs_ref, slot_ref, gates_ref, meta_ref, wbf_ref):
    @pl.when(pl.program_id(0) == 0)
    def _():
        wbf_ref[...] = wo_ref[...].astype(BF16)

    a = ATTN_WIDTH
    mixed = (jnp.dot(a_ref[...].astype(BF16), wbf_ref[:a, :], preferred_element_type=F32)
             + jnp.dot(z_ref[...], wbf_ref[a:, :], preferred_element_type=F32))
    x1 = x_ref[...] + gt_ref[...] * mixed
    x1_ref[...] = x1
    h2 = (_rms(x1, g_ref[...]) * (1.0 + sc_ref[...]) + sh_ref[...]).astype(BF16)
    lg = jnp.dot(h2, wr_ref[...], preferred_element_type=F32) + br_ref[...]
    tm = lg.shape[0]
    lane = lax.broadcasted_iota(jnp.int32, (tm, V7X_LANES), 1).astype(F32)
    vals, hots = [], []
    for _ in range(TOP_K):
        m = jnp.max(lg, axis=1, keepdims=True)
        idx = jnp.min(jnp.where(lg == m, lane, float(V7X_LANES)), axis=1, keepdims=True)
        hot = lane == idx
        vals.append(m)
        hots.append(jnp.where(hot, 1.0, 0.0))
        lg = jnp.where(hot, MASKED * 2, lg)
    es = [jnp.exp(v - vals[0]) for v in vals]
    inv = 1.0 / (es[0] + es[1] + es[2] + es[3])

    member = hots[0] + hots[1] + hots[2] + hots[3]
    before = jnp.dot(ltri_ref[...], member.astype(BF16), preferred_element_type=F32)
    count = jnp.sum(member, axis=0, keepdims=True)
    padded = jnp.floor((count + (ROW_ALIGN - 1.0)) * (1.0 / ROW_ALIGN)) * ROW_ALIGN
    padded8 = jnp.broadcast_to(padded, (V7X_SUBLANES, V7X_LANES))
    start = jnp.dot(padded8.astype(BF16), utri_ref[...], preferred_element_type=F32)[0:1]
    where_to = start + before
    slots = jnp.zeros((tm, V7X_LANES), F32)
    gates = jnp.zeros((tm, V7X_LANES), F32)
    for k in range(TOP_K):
        slot_k = jnp.sum(hots[k] * where_to, axis=1, keepdims=True)
        slots = jnp.where(lane == float(k), slot_k, slots)
        gates = jnp.where(lane == float(k), es[k] * inv, gates)
    slot_ref[...] = slots
    gates_ref[...] = gates
    row = lax.broadcasted_iota(jnp.int32, (V7X_SUBLANES, V7X_LANES), 0)
    meta_ref[...] = jnp.where(row == 0, padded8, jnp.where(row == 1, jnp.broadcast_to(start, padded8.shape), 0.0))

    cap = xs_ref.shape[0]
    slots_t = jnp.transpose(slots)
    srow = lax.broadcasted_iota(jnp.int32, (cap, tm), 0).astype(F32)
    place = jnp.zeros((cap, tm), F32)
    for k in range(TOP_K):
        place = place + jnp.where(srow == slots_t[k:k + 1, :], 1.0, 0.0)
    xs_ref[...] = jnp.dot(place.astype(BF16), h2, preferred_element_type=F32).astype(BF16)


def _outproj(x, a, z, gt, sh, sc, g, w_out, wr_pad, br_pad, tm):
    n, d = x.shape
    aw = ATTN_WIDTH
    mod_spec = (pl.BlockSpec((1, d), lambda i: (0, 0)) if gt.shape[0] == 1
                else pl.BlockSpec((tm, d), lambda i: (i, 0)))
    row = lambda w: pl.BlockSpec((tm, w), lambda i: (i, 0))
    ltri = jnp.asarray(np.tril(np.ones((tm, tm), np.float32), -1), BF16)
    utri = jnp.asarray(np.triu(np.ones((V7X_LANES, V7X_LANES), np.float32), 1), BF16)
    in_specs = [row(d), row(aw), row(SSM_WIDTH), mod_spec, mod_spec, mod_spec, _full((1, d)),
                _full(w_out.shape), _full(wr_pad.shape), _full(br_pad.shape), _full(ltri.shape), _full(utri.shape)]
    args = [x, a, z, gt, sh, sc, g.reshape(1, d), w_out, wr_pad, br_pad, ltri, utri]
    return pl.pallas_call(
        _outproj_kernel,
        out_shape=(jax.ShapeDtypeStruct((n, d), F32),
                   jax.ShapeDtypeStruct((n // tm, GROUP_CAP, d), BF16),
                   jax.ShapeDtypeStruct((n, V7X_LANES), F32), jax.ShapeDtypeStruct((n, V7X_LANES), F32),
                   jax.ShapeDtypeStruct((n // tm, V7X_SUBLANES, V7X_LANES), F32)),
        grid=(n // tm,),
        in_specs=in_specs,
        out_specs=(row(d), pl.BlockSpec((None, GROUP_CAP, d), lambda i: (i, 0, 0)),
                   row(V7X_LANES), row(V7X_LANES),
                   pl.BlockSpec((None, V7X_SUBLANES, V7X_LANES), lambda i: (i, 0, 0))),
        scratch_shapes=[pltpu.VMEM(w_out.shape, BF16)],
        compiler_params=_cparams(),
        name="outproj_router",
    )(*args)


_PIECE_SIZES = tuple(MOE_TILE >> s for s in range(6))


def _moe_kernel(te_ref, tl_ref, lo_ref, hi_ref, nu_ref, gstart_ref, gsize_ref, gbase_ref, rows_ref, used_ref,
                xs_a, xs_b, wgu_ref, bgu_ref, wd_ref, bd_ref, ys_a, ys_b,
                xbuf, ybuf, zbuf, wgu_bf, wd_bf, xsem, ysem, zsem):
    t = pl.program_id(0)
    n_used = nu_ref[0]
    n_a = xs_a.shape[0]
    n_blocks = n_a + xs_b.shape[0]
    cap = xs_a.shape[1]

    def on_block(i, fn):
        @pl.when(i < n_a)
        def _():
            fn(0, i)

        @pl.when(i >= n_a)
        def _():
            fn(1, i - n_a)

    def x_copy(i, br, tr, sz, slot):
        on_block(i, lambda w, j: pltpu.make_async_copy(
            (xs_a, xs_b)[w].at[j, pl.ds(br, sz)], xbuf.at[slot, pl.ds(tr, sz)], xsem.at[slot]).start())

    def y_copy(i, br, tr, sz, slot):
        on_block(i, lambda w, j: pltpu.make_async_copy(
            ybuf.at[slot, pl.ds(tr, sz)], (ys_a, ys_b)[w].at[j, pl.ds(br, sz)], ysem.at[slot]).start())

    def z_copy(i, row, sz, start):
        def go(w, j):
            cp = pltpu.make_async_copy(zbuf.at[pl.ds(0, sz)], (ys_a, ys_b)[w].at[j, pl.ds(row, sz)], zsem)
            cp.start() if start else cp.wait()
        on_block(i, go)

    def pieces(tt, fn):
        e = te_ref[tt]
        lo = tl_ref[tt] * MOE_TILE

        def per_block(i, c):
            g = e * n_blocks + i
            s0 = gbase_ref[g]
            a = jnp.maximum(s0, lo)
            b = jnp.minimum(s0 + gsize_ref[g], lo + MOE_TILE)
            length = jnp.maximum(b - a, 0)
            src = gstart_ref[g] + (a - s0)
            dst = a - lo
            done = jnp.int32(0)
            for sz in _PIECE_SIZES:
                hit = (length & sz) != 0

                @pl.when(hit)
                def _(sz=sz, done=done):
                    fn(i, pl.multiple_of(src + done, ROW_ALIGN), pl.multiple_of(dst + done, ROW_ALIGN), sz)

                done = done + jnp.where(hit, sz, 0)
            return c

        lax.fori_loop(lo_ref[tt], hi_ref[tt], per_block, 0)

    def tile_rows(tt):
        return jnp.minimum(rows_ref[te_ref[tt]] - tl_ref[tt] * MOE_TILE, MOE_TILE)

    def wait_rows(n, sem, buf):
        for sz in _PIECE_SIZES:
            @pl.when((n & sz) != 0)
            def _(sz=sz):
                pltpu.make_async_copy(buf.at[pl.ds(0, sz)], buf.at[pl.ds(0, sz)], sem).wait()

    def fetch(tt, slot):
        pieces(tt, lambda i, br, tr, sz: x_copy(i, br, tr, sz, slot))

    def writeback(tt, slot):
        pieces(tt, lambda i, br, tr, sz: y_copy(i, br, tr, sz, slot))

    def zero_tail(i, start):
        u = used_ref[i]
        rem = cap - u
        nz = zbuf.shape[0]
        whole = lax.shift_right_logical(rem, nz.bit_length() - 1)

        def chunk(j, c):
            z_copy(i, pl.multiple_of(u + j * nz, ROW_ALIGN), nz, start)
            return c

        lax.fori_loop(0, whole, chunk, 0)
        base = u + whole * nz
        done = jnp.int32(0)
        for sz in _PIECE_SIZES:
            if sz >= nz:
                continue
            hit = (rem & sz) != 0

            @pl.when(hit)
            def _(sz=sz, done=done):
                z_copy(i, pl.multiple_of(base + done, ROW_ALIGN), sz, start)

            done = done + jnp.where(hit, sz, 0)

    @pl.when(t == 0)
    def _():
        xbuf[...] = jnp.zeros(xbuf.shape, xbuf.dtype)
        zbuf[...] = jnp.zeros(zbuf.shape, zbuf.dtype)
        lax.fori_loop(0, n_blocks, lambda i, c: (zero_tail(i, True), c)[1], 0)
        lax.fori_loop(0, n_blocks, lambda i, c: (zero_tail(i, False), c)[1], 0)
        fetch(0, 0)

    @pl.when(t < n_used)
    def _():
        slot = t % 2

        @pl.when(t + 1 < n_used)
        def _():
            fetch(t + 1, 1 - slot)

        prev = te_ref[jnp.maximum(t - 1, 0)]
        fresh = jnp.logical_or(t == 0, te_ref[t] != prev)

        @pl.when(fresh)
        def _():
            wgu_bf[...] = wgu_ref[...].astype(BF16)
            wd_bf[...] = wd_ref[...].astype(BF16)

        wait_rows(tile_rows(t), xsem.at[slot], xbuf.at[slot])

        @pl.when(t >= 2)
        def _():
            wait_rows(tile_rows(t - 2), ysem.at[slot], ybuf.at[slot])

        f = D_EXPERT
        gu = jnp.dot(xbuf[slot], wgu_bf[...], preferred_element_type=F32) + bgu_ref[...]
        gate = jnp.minimum(gu[:, :f], SWIGLU_LIMIT)
        up = jnp.clip(gu[:, f:], -SWIGLU_LIMIT, SWIGLU_LIMIT)
        act = (up + 1.0) * gate * _sigmoid(SWIGLU_ALPHA * gate)
        y = jnp.dot(act.astype(BF16), wd_bf[...], preferred_element_type=F32) + bd_ref[...]
        ybuf[slot] = y.astype(BF16)
        writeback(t, slot)

        @pl.when(t == n_used - 1)
        def _():
            wait_rows(tile_rows(t), ysem.at[slot], ybuf.at[slot])

            @pl.when(t >= 1)
            def _():
                wait_rows(tile_rows(t - 1), ysem.at[1 - slot], ybuf.at[1 - slot])


def _moe(plan, xs_a, xs_b, w_gate_up, b_gate_up, w_down, b_down):
    _, cap, d = xs_a.shape
    e, _, f2 = w_gate_up.shape
    nt = plan[0].shape[0]
    wmap = lambda t, te, *_: (te[t], 0, 0)
    anyspec = pl.BlockSpec(memory_space=pl.ANY)
    grid_spec = pltpu.PrefetchScalarGridSpec(
        num_scalar_prefetch=len(plan),
        grid=(nt,),
        in_specs=[anyspec, anyspec,
                  pl.BlockSpec((None, d, f2), wmap),
                  pl.BlockSpec((None, 1, f2), wmap),
                  pl.BlockSpec((None, f2 // 2, d), wmap),
                  pl.BlockSpec((None, 1, d), wmap)],
        out_specs=(anyspec, anyspec),
        scratch_shapes=[pltpu.VMEM((2, MOE_TILE, d), BF16), pltpu.VMEM((2, MOE_TILE, d), BF16),
                        pltpu.VMEM((MOE_TILE // 2, d), BF16),
                        pltpu.VMEM((d, f2), BF16), pltpu.VMEM((f2 // 2, d), BF16),
                        pltpu.SemaphoreType.DMA((2,)), pltpu.SemaphoreType.DMA((2,)), pltpu.SemaphoreType.DMA(())],
    )
    return pl.pallas_call(
        _moe_kernel,
        out_shape=(jax.ShapeDtypeStruct(xs_a.shape, BF16), jax.ShapeDtypeStruct(xs_b.shape, BF16)),
        grid_spec=grid_spec,
        compiler_params=_cparams(),
        name="moe_experts",
    )(*plan, xs_a, xs_b, w_gate_up, b_gate_up.reshape(e, 1, f2), w_down, b_down.reshape(e, 1, d))


def _moe_plan(group_size, group_start):
    n_blocks = group_size.shape[0]
    gsize = group_size.T.astype(jnp.int32)
    gstart = group_start.T.astype(jnp.int32)
    gbase = jnp.cumsum(gsize, axis=1) - gsize
    rows = jnp.sum(gsize, axis=1)
    tiles = (rows + MOE_TILE - 1) // MOE_TILE
    tile_end = jnp.cumsum(tiles)
    n_used = tile_end[-1:]
    nt = (n_blocks * GROUP_CAP) // MOE_TILE + N_EXPERTS
    t = jnp.arange(nt, dtype=jnp.int32)
    te = jnp.sum((tile_end[None, :] <= t[:, None]).astype(jnp.int32), axis=1)
    last = jnp.max(jnp.where(tiles > 0, jnp.arange(N_EXPERTS, dtype=jnp.int32), 0))
    te = jnp.where(t < n_used[0], jnp.minimum(te, N_EXPERTS - 1), last)
    hot = (te[:, None] == jnp.arange(N_EXPERTS, dtype=jnp.int32)[None, :]).astype(jnp.int32)
    tl = jnp.where(t < n_used[0], t - hot @ (tile_end - tiles), 0)
    lo_row = tl * MOE_TILE
    base_t = hot @ gbase
    size_t = hot @ gsize
    first = jnp.sum((base_t + size_t <= lo_row[:, None]).astype(jnp.int32), axis=1)
    stop = jnp.sum((base_t < lo_row[:, None] + MOE_TILE).astype(jnp.int32), axis=1)
    used = jnp.sum(gsize, axis=0)
    i32 = lambda z: z.astype(jnp.int32)
    return (i32(te), i32(tl), i32(first), i32(stop), i32(n_used), i32(gstart.reshape(-1)), i32(gsize.reshape(-1)),
            i32(gbase.reshape(-1)), i32(rows), i32(used))


def _final_kernel(x_ref, ys_ref, slot_ref, gates_ref, gt_ref, g_ref, y_ref):
    tm = x_ref.shape[0]
    cap = ys_ref.shape[0]
    col = lax.broadcasted_iota(jnp.int32, (tm, cap), 1).astype(F32)
    slots = slot_ref[...]
    gates = gates_ref[...]
    mix = jnp.zeros((tm, cap), F32)
    for k in range(TOP_K):
        mix = mix + jnp.where(col == slots[:, k:k + 1], gates[:, k:k + 1], 0.0)
    hi = mix.astype(BF16)
    lo = (mix - hi.astype(F32)).astype(BF16)
    ys = ys_ref[...]
    ff = jnp.dot(hi, ys, preferred_element_type=F32) + jnp.dot(lo, ys, preferred_element_type=F32)
    y_ref[...] = _rms(x_ref[...] + gt_ref[...] * ff, g_ref[...])


def _final(x1, ys, slots, gates, gt, g, tm):
    n, d = x1.shape
    cap = ys.shape[1]
    mod_spec = (pl.BlockSpec((1, d), lambda i: (0, 0)) if gt.shape[0] == 1
                else pl.BlockSpec((tm, d), lambda i: (i, 0)))
    row = lambda w: pl.BlockSpec((tm, w), lambda i: (i, 0))
    return pl.pallas_call(
        _final_kernel,
        out_shape=jax.ShapeDtypeStruct((n, d), F32),
        grid=(n // tm,),
        in_specs=[row(d), pl.BlockSpec((None, cap, d), lambda i: (i, 0, 0)),
                  row(V7X_LANES), row(V7X_LANES), mod_spec, _full((1, d))],
        out_specs=row(d),
        compiler_params=_cparams(),
        name="final_norm",
    )(x1, ys, slots, gates, gt, g.reshape(1, d))


def kernel(x_prompt, x_sample, cache_k_win, cache_v_win, state_ssm_re, state_ssm_im, c_prompt, c_sample,
           w_ada, b_ada, g_norm1, w_in, ssm_a_re, ssm_a_im, ssm_log_dt, ssm_b_re, ssm_b_im, ssm_c_re, ssm_c_im,
           ssm_d, w_glu, b_glu, g_out_attn, g_out_ssm, w_out, g_norm2, w_router, b_router, w_gate_up, b_gate_up,
           w_down, b_down, g_final):
    depth = w_ada.shape[0]
    assert depth == 1 and x_prompt.shape[0] == 1
    bp, t, d = x_prompt.shape
    bs, ts, _ = x_sample.shape
    ns = bs * ts
    l = 0
    a = ATTN_WIDTH

    n_c = bp + bs
    c_pad = -(-n_c // V7X_SUBLANES) * V7X_SUBLANES
    c_rows = jnp.concatenate([c_prompt, c_sample, jnp.zeros((c_pad - n_c, d), F32)], axis=0)
    mod = _ada_modulation(c_rows, w_ada[l], b_ada[l])
    mod_p = [mod[0:1, i * d:(i + 1) * d] for i in range(N_MOD)]
    mod_s = [jnp.repeat(mod[bp:bp + bs, i * d:(i + 1) * d], ts, axis=0) for i in range(N_MOD)]

    steps_p = TOKEN_TILE // SSM_ROWS_PROMPT
    perm_p = _chunk_perm(SSM_ROWS_PROMPT, steps_p)
    perm_s = _chunk_perm(bs, ts)
    tabs = _ssm_tables(ssm_a_re[l], ssm_a_im[l], ssm_log_dt[l], ssm_b_re[l], ssm_b_im[l],
                       ssm_c_re[l], ssm_c_im[l], max(steps_p, ts))
    wr_pad = jnp.zeros((d, V7X_LANES), F32).at[:, :N_EXPERTS].set(w_router[l]).astype(BF16)
    br_pad = jnp.full((1, V7X_LANES), MASKED, F32).at[0, :N_EXPERTS].set(b_router[l])

    xp = x_prompt.reshape(t, d)
    qp, kp, vp, kpf, vpf, up = _inproj(xp, mod_p[0], mod_p[1], g_norm1[l], w_in[l],
                                       jnp.asarray(perm_p, BF16), TOKEN_TILE)
    outs = [_attn_branch(qp, kp, vp, dil) for _, dil in DILATED_BRANCHES]
    ap = _attn_combine([o for o, _ in outs], [s for _, s in outs], g_out_attn[l])
    zeros_h = jnp.zeros((SSM_ROWS_PROMPT, 2 * N_STATE), F32)
    zp, hp = _ssm(up, zeros_h, tabs, ssm_d[l], w_glu[l], b_glu[l], g_out_ssm[l],
                  jnp.asarray(perm_p.T, BF16), SSM_ROWS_PROMPT, steps_p, True)
    x1p, xs_p, slot_p, gate_p, meta_p = _outproj(xp, ap, zp, mod_p[2], mod_p[3], mod_p[4], g_norm2[l], w_out[l],
                                                 wr_pad, br_pad, TOKEN_TILE)

    xs = x_sample.reshape(ns, d)
    qs, ks, vs, ksf, vsf, us = _inproj(xs, mod_s[0], mod_s[1], g_norm1[l], w_in[l],
                                       jnp.asarray(perm_s, BF16), ns)
    as_ = _attn_sample(qs.reshape(bs, ts, a), ks.reshape(bs, ts, a), vs.reshape(bs, ts, a),
                       cache_k_win[l].reshape(bs, -1, a), cache_v_win[l].reshape(bs, -1, a), g_out_attn[l])
    h0s = jnp.concatenate([state_ssm_re[l].reshape(bs, N_STATE), state_ssm_im[l].reshape(bs, N_STATE)], axis=1)
    zs, hs = _ssm(us, h0s, tabs, ssm_d[l], w_glu[l], b_glu[l], g_out_ssm[l],
                  jnp.asarray(perm_s.T, BF16), bs, ts, False)
    x1s, xs_s, slot_s, gate_s, meta_s = _outproj(xs, as_.reshape(ns, a), zs, mod_s[2], mod_s[3], mod_s[4],
                                                 g_norm2[l], w_out[l], wr_pad, br_pad, ns)

    meta = jnp.concatenate([meta_p, meta_s], axis=0)
    plan = _moe_plan(meta[:, 0, :N_EXPERTS], meta[:, 1, :N_EXPERTS])
    ys_p, ys_s = _moe(plan, xs_p, xs_s, w_gate_up[l], b_gate_up[l], w_down[l], b_down[l])

    y_prompt = _final(x1p, ys_p, slot_p, gate_p, mod_p[5], g_final, TOKEN_TILE).reshape(bp, t, d)
    y_sample = _final(x1s, ys_s, slot_s, gate_s, mod_s[5], g_final, ns).reshape(bs, ts, d)

    keep = min(MAX_WINDOW, t)
    shp = (1, bp, keep, N_HEADS, HEAD_DIM)
    k_win = kpf[t - keep:].reshape(shp)
    v_win = vpf[t - keep:].reshape(shp)
    st = (1, bp, SSM_GROUPS, SSM_STATE)
    hp_last = hp[SSM_ROWS_PROMPT - 1]
    ss = (1, bs, SSM_GROUPS, SSM_STATE)
    return (y_prompt, y_sample, k_win, v_win,
            hp_last[:N_STATE].reshape(st), hp_last[N_STATE:].reshape(st),
            ksf.reshape(1, bs, ts, N_HEADS, HEAD_DIM), vsf.reshape(1, bs, ts, N_HEADS, HEAD_DIM),
            hs[:, :N_STATE].reshape(ss), hs[:, N_STATE:].reshape(ss))
```

```python
import functools

import numpy as np
import jax
import jax.numpy as jnp
from jax import lax
from jax.experimental import pallas as pl
from jax.experimental.pallas import tpu as pltpu

F32 = jnp.float32
BF16 = jnp.bfloat16

D_MODEL = 1024
N_HEADS = 8
HEAD_DIM = 64
ATTN_WIDTH = N_HEADS * HEAD_DIM
DILATED_BRANCHES = ((128, 1), (512, 4), (2048, 16))
KEYS_PER_BRANCH = 129
MAX_WINDOW = 2048
SSM_WIDTH = D_MODEL - ATTN_WIDTH
SSM_CH = 16
SSM_GROUPS = SSM_WIDTH // SSM_CH
SSM_STATE = 64
N_STATE = SSM_GROUPS * SSM_STATE
N_EXPERTS = 32
TOP_K = 4
D_EXPERT = D_MODEL
SWIGLU_LIMIT = 7.0
SWIGLU_ALPHA = 1.702
N_MOD = 6
EPS = 1e-6
MASKED = -1e30

V7X_LANES = 128
V7X_SUBLANES = 8
V7X_VMEM_LIMIT_BYTES = 56 * 1024 * 1024

TOKEN_TILE = 512
Q_TILE = 128
SSM_ROWS_PROMPT = 8
MOE_TILE = 512
ROW_ALIGN = 16
GROUP_CAP = -(-(TOKEN_TILE * TOP_K + N_EXPERTS * (ROW_ALIGN - 1)) // (2 * V7X_LANES)) * (2 * V7X_LANES)


def _cparams(n_axes=1):
    return pltpu.CompilerParams(
        dimension_semantics=("arbitrary",) * n_axes,
        vmem_limit_bytes=V7X_VMEM_LIMIT_BYTES,
    )


def _full(shape):
    n = len(shape)
    return pl.BlockSpec(shape, lambda *_: (0,) * n)


def _rms(x, g):
    return x * lax.rsqrt(jnp.mean(x * x, axis=-1, keepdims=True) + EPS) * g


def _sigmoid(x):
    return 1.0 / (1.0 + jnp.exp(-x))


def _ada_kernel(c_ref, w_ref, b_ref, o_ref):
    c = c_ref[...]
    s = (c * _sigmoid(c)).astype(BF16)
    o_ref[...] = jnp.dot(s, w_ref[...].astype(BF16), preferred_element_type=F32) + b_ref[...]


def _ada_modulation(c_rows, w_ada, b_ada):
    m, d = c_rows.shape
    n = w_ada.shape[1]
    tn = n // 4
    return pl.pallas_call(
        _ada_kernel,
        out_shape=jax.ShapeDtypeStruct((m, n), F32),
        grid=(n // tn,),
        in_specs=[_full((m, d)),
                  pl.BlockSpec((d, tn), lambda j: (0, j)),
                  pl.BlockSpec((1, tn), lambda j: (0, j))],
        out_specs=pl.BlockSpec((m, tn), lambda j: (0, j)),
        compiler_params=_cparams(),
        name="ada_modulation",
    )(c_rows, w_ada, b_ada.reshape(1, n))


def _inproj_kernel(*refs, dils):
    x_ref, sh_ref, sc_ref, g_ref, w_ref, perm_ref = refs[:6]
    dperm_refs = refs[6:6 + len(dils)]
    q_ref, k_ref, v_ref, kf_ref, vf_ref, u_ref = refs[6 + len(dils):12 + len(dils)]
    dil_refs = refs[12 + len(dils):-1]
    wbf_ref = refs[-1]

    @pl.when(pl.program_id(0) == 0)
    def _():
        wbf_ref[...] = w_ref[...].astype(BF16)

    h = _rms(x_ref[...], g_ref[...]) * (1.0 + sc_ref[...]) + sh_ref[...]
    hb = h.astype(BF16)
    a = ATTN_WIDTH
    proj = jnp.dot(hb, wbf_ref[:, :3 * a], preferred_element_type=F32)
    k = proj[:, a:2 * a]
    v = proj[:, 2 * a:]
    qkv = jnp.concatenate([(proj[:, :a] * (HEAD_DIM ** -0.5)).astype(BF16), k.astype(BF16), v.astype(BF16)],
                          axis=1)
    q_ref[...] = qkv[:, :a]
    k_ref[...] = qkv[:, a:2 * a]
    v_ref[...] = qkv[:, 2 * a:]
    kf_ref[...] = k
    vf_ref[...] = v
    hp = jnp.dot(perm_ref[...], hb, preferred_element_type=F32).astype(BF16)
    u_ref[...] = jnp.dot(hp, wbf_ref[:, 3 * a:], preferred_element_type=F32)
    tm = qkv.shape[0]
    for n_d, dil in enumerate(dils):
        by_residue = jnp.dot(dperm_refs[n_d][...], qkv, preferred_element_type=F32).astype(BF16)
        per = tm // dil
        for r in range(dil):
            rows = by_residue[r * per:(r + 1) * per]
            for j in range(3):
                dil_refs[3 * n_d + j][:, r * a:(r + 1) * a] = rows[:, j * a:(j + 1) * a]


def _inproj(x, sh, sc, g, w_in, perm, tm, dils=()):
    n, d = x.shape
    a = ATTN_WIDTH
    mod_rows = sh.shape[0]
    mod_spec = (pl.BlockSpec((1, d), lambda i: (0, 0)) if mod_rows == 1
                else pl.BlockSpec((tm, d), lambda i: (i, 0)))
    row = lambda w: pl.BlockSpec((tm, w), lambda i: (i, 0))
    dperms = [jnp.asarray(_chunk_perm(tm // dil, dil), BF16) for dil in dils]
    view_shapes = tuple(jax.ShapeDtypeStruct((n // dil, dil * a), BF16) for dil in dils for _ in range(3))
    view_specs = tuple(pl.BlockSpec((tm // dil, dil * a), lambda i: (i, 0)) for dil in dils for _ in range(3))
    return pl.pallas_call(
        functools.partial(_inproj_kernel, dils=tuple(dils)),
        out_shape=(jax.ShapeDtypeStruct((n, a), BF16),) * 3
        + (jax.ShapeDtypeStruct((n, a), F32),) * 2
        + (jax.ShapeDtypeStruct((n, SSM_WIDTH), F32),) + view_shapes,
        grid=(n // tm,),
        in_specs=[row(d), mod_spec, mod_spec, _full((1, d)), _full(w_in.shape), _full((tm, tm))]
        + [_full((tm, tm))] * len(dils),
        out_specs=(row(a),) * 5 + (row(SSM_WIDTH),) + view_specs,
        scratch_shapes=[pltpu.VMEM(w_in.shape, BF16)],
        compiler_params=_cparams(),
        name="inproj",
    )(x, sh, sc, g.reshape(1, d), w_in, perm, *dperms)


def _chunk_perm(rows, steps):
    n = rows * steps
    p = np.zeros((n, n), np.float32)
    c, t = np.meshgrid(np.arange(rows), np.arange(steps), indexing="ij")
    p[(t * rows + c).ravel(), (c * steps + t).ravel()] = 1.0
    return p


def _alibi_slopes():
    return np.exp2(-8.0 * np.arange(1, N_HEADS + 1, dtype=np.float64) / N_HEADS).astype(np.float32)


def _branch_bias(dil):
    qi = np.arange(Q_TILE)[:, None]
    col = np.arange(2 * Q_TILE)[None, :]
    j = Q_TILE + qi - col
    valid = (j >= 0) & (j <= Q_TILE)
    dist = (j * dil).astype(np.float32)
    tabs = []
    for first in (True, False):
        ok = valid & (col >= Q_TILE) if first else valid
        per_head = [np.where(ok, -s * dist, np.float32(MASKED)) for s in _alibi_slopes()]
        tabs.append(np.concatenate(per_head, axis=0))
    return np.stack(tabs).astype(np.float32)


def _attn_branch_kernel(q_ref, kp_ref, kc_ref, vp_ref, vc_ref, bias_ref, o_ref, lse_ref):
    sel = jnp.minimum(pl.program_id(0), 1)
    lane = lax.broadcasted_iota(jnp.int32, (Q_TILE, V7X_LANES), 1)
    lo = lane < HEAD_DIM
    lse_acc = jnp.zeros((Q_TILE, V7X_LANES), F32)
    for p in range(N_HEADS // 2):
        cs = slice(V7X_LANES * p, V7X_LANES * (p + 1))
        q2 = q_ref[:, cs]
        zero = jnp.zeros_like(q2)
        qq = jnp.concatenate([jnp.where(lo, q2, zero), jnp.where(lo, zero, q2)], axis=0)
        kk = jnp.concatenate([kp_ref[:, cs], kc_ref[:, cs]], axis=0)
        vv = jnp.concatenate([vp_ref[:, cs], vc_ref[:, cs]], axis=0)
        s = lax.dot_general(qq, kk, (((1,), (1,)), ((), ())), preferred_element_type=F32)
        s = s + bias_ref[sel, 2 * Q_TILE * p:2 * Q_TILE * (p + 1), :]
        m = jnp.max(s, axis=1, keepdims=True)
        e = jnp.exp(s - m)
        l = jnp.sum(e, axis=1, keepdims=True)
        eb = e.astype(BF16)
        o0 = jnp.dot(eb[:Q_TILE], vv, preferred_element_type=F32) * (1.0 / l[:Q_TILE])
        o1 = jnp.dot(eb[Q_TILE:], vv, preferred_element_type=F32) * (1.0 / l[Q_TILE:])
        o_ref[:, cs] = jnp.where(lo, o0, o1).astype(o_ref.dtype)
        lse = m + jnp.log(l)
        lse_acc = jnp.where(lane == 2 * p, lse[:Q_TILE], lse_acc)
        lse_acc = jnp.where(lane == 2 * p + 1, lse[Q_TILE:], lse_acc)
    lse_ref[...] = lse_acc[:, :N_HEADS]


def _attn_branch(qv, kv, vv, dil):
    a = ATTN_WIDTH
    rows = qv.shape[0]
    t = rows * dil
    cur = pl.BlockSpec((Q_TILE, a), lambda i, r: (i, r))
    prev = pl.BlockSpec((Q_TILE, a), lambda i, r: (jnp.maximum(i - 1, 0), r))
    bias = jnp.asarray(_branch_bias(dil))
    o, lse = pl.pallas_call(
        _attn_branch_kernel,
        out_shape=(jax.ShapeDtypeStruct((rows, dil * a), BF16),
                   jax.ShapeDtypeStruct((dil, rows, N_HEADS), F32)),
        grid=(rows // Q_TILE, dil),
        in_specs=[cur, prev, cur, prev, cur, _full(bias.shape)],
        out_specs=(cur, pl.BlockSpec((None, Q_TILE, N_HEADS), lambda i, r: (r, i, 0))),
        compiler_params=_cparams(2),
        name=f"attn_branch_d{dil}",
    )(qv, kv, kv, vv, vv, bias)
    return o, lse.transpose(1, 0, 2).reshape(t, N_HEADS)


def _attn_combine_kernel(*refs, dils):
    nb = len(dils)
    o_refs = refs[:nb]
    l_refs = refs[nb:2 * nb]
    g_ref = refs[2 * nb]
    unperm_refs = refs[2 * nb + 1:-1]
    a_ref = refs[-1]
    tq, a = a_ref.shape
    outs = []
    n_u = 0
    for o_ref, dil in zip(o_refs, dils):
        if dil == 1:
            outs.append(o_ref[...].astype(F32))
            continue
        by_residue = jnp.concatenate([o_ref[:, r * a:(r + 1) * a] for r in range(dil)], axis=0)
        outs.append(jnp.dot(unperm_refs[n_u][...], by_residue, preferred_element_type=F32))
        n_u += 1
    ls = [l_ref[...] for l_ref in l_refs]
    top = functools.reduce(jnp.maximum, ls)
    ws = [jnp.exp(l - top) for l in ls]
    inv = 1.0 / functools.reduce(jnp.add, ws)
    cs = [w * inv for w in ws]
    lane = lax.broadcasted_iota(jnp.int32, (tq, V7X_LANES), 1)
    lo = lane < HEAD_DIM
    cols = []
    for p in range(N_HEADS // 2):
        sl = slice(V7X_LANES * p, V7X_LANES * (p + 1))
        acc = jnp.zeros((tq, V7X_LANES), F32)
        for c, o in zip(cs, outs):
            cexp = jnp.where(lo,
                             jnp.broadcast_to(c[:, 2 * p:2 * p + 1], (tq, V7X_LANES)),
                             jnp.broadcast_to(c[:, 2 * p + 1:2 * p + 2], (tq, V7X_LANES)))
            acc = acc + cexp * o[:, sl]
        cols.append(acc)
    o = jnp.concatenate(cols, axis=1)
    a_ref[...] = _rms(o, g_ref[...]).astype(a_ref.dtype)


def _attn_combine(os_, lses, g, dils):
    t = lses[0].shape[0]
    a = ATTN_WIDTH
    tq = TOKEN_TILE
    unperms = [jnp.asarray(_chunk_perm(tq // dil, dil).T, BF16) for dil in dils if dil > 1]
    return pl.pallas_call(
        functools.partial(_attn_combine_kernel, dils=tuple(dils)),
        out_shape=jax.ShapeDtypeStruct((t, a), BF16),
        grid=(t // tq,),
        in_specs=[pl.BlockSpec((tq // dil, dil * a), lambda i: (i, 0)) for dil in dils]
        + [pl.BlockSpec((tq, N_HEADS), lambda i: (i, 0))] * len(dils) + [_full((1, a))]
        + [_full((tq, tq))] * len(unperms),
        out_specs=pl.BlockSpec((tq, a), lambda i: (i, 0)),
        compiler_params=_cparams(),
        name="attn_combine",
    )(*os_, *lses, g.reshape(1, a), *unperms)


NEW_KEY_PAD = V7X_LANES


def _sample_bias(win, steps):
    ncol = win + NEW_KEY_PAD
    col = np.arange(ncol)[None, :]
    t = np.arange(steps)[:, None]
    is_new = col >= win
    dist = np.where(is_new, t - (col - win), win + t - col)
    ok_new = (col - win) < steps
    tabs = []
    for window, dil in DILATED_BRANCHES:
        ok = (dist >= 0) & (dist <= window) & (dist % dil == 0) & (~is_new | ok_new)
        per_head = [np.where(ok, -s * dist.astype(np.float32), np.float32(MASKED)) for s in _alibi_slopes()]
        tabs.append(np.concatenate(per_head, axis=0))
    return np.stack(tabs).astype(np.float32)


def _attn_sample_kernel(q_ref, kn_ref, vn_ref, ck_ref, cv_ref, bias_ref, g_ref, o_ref):
    steps = q_ref.shape[0]
    nq = N_HEADS * steps
    a = ATTN_WIDTH
    row = lax.broadcasted_iota(jnp.int32, (nq, a), 0)
    col = lax.broadcasted_iota(jnp.int32, (nq, a), 1)
    own = (row // steps) == (col // HEAD_DIM)
    qf = q_ref[...].astype(F32)
    qbd = jnp.where(own, jnp.concatenate([qf] * N_HEADS, axis=0), 0.0).astype(BF16)
    pad = jnp.zeros((NEW_KEY_PAD - steps, a), F32)
    k_all = jnp.concatenate([ck_ref[...], kn_ref[...].astype(F32), pad], axis=0).astype(BF16)
    v_all = jnp.concatenate([cv_ref[...], vn_ref[...].astype(F32), pad], axis=0).astype(BF16)
    s = lax.dot_general(qbd, k_all, (((1,), (1,)), ((), ())), preferred_element_type=F32)
    ms, ls, es = [], [], []
    for b in range(len(DILATED_BRANCHES)):
        sb = s + bias_ref[b]
        m = jnp.max(sb, axis=1, keepdims=True)
        e = jnp.exp(sb - m)
        ms.append(m)
        ls.append(jnp.sum(e, axis=1, keepdims=True))
        es.append(e)
    o_all = jnp.dot(jnp.concatenate(es, axis=0).astype(BF16), v_all, preferred_element_type=F32)
    top = jnp.maximum(jnp.maximum(ms[0], ms[1]), ms[2])
    num = jnp.zeros((nq, a), F32)
    den = jnp.zeros((nq, 1), F32)
    for b in range(len(DILATED_BRANCHES)):
        w = jnp.exp(ms[b] - top)
        num = num + o_all[b * nq:(b + 1) * nq] * w
        den = den + ls[b] * w
    full = jnp.where(own, num * (1.0 / den), 0.0)
    o = full[0:steps]
    for h in range(1, N_HEADS):
        o = o + full[h * steps:(h + 1) * steps]
    o_ref[...] = _rms(o, g_ref[...])


def _attn_sample(q, kn, vn, cache_k, cache_v, g):
    b, steps, a = q.shape
    win = cache_k.shape[1]
    bias = jnp.asarray(_sample_bias(win, steps))
    new = pl.BlockSpec((None, steps, a), lambda i: (i, 0, 0))
    cache = pl.BlockSpec((None, win, a), lambda i: (i, 0, 0))
    return pl.pallas_call(
        _attn_sample_kernel,
        out_shape=jax.ShapeDtypeStruct((b, steps, a), F32),
        grid=(b,),
        in_specs=[new, new, new, cache, cache, _full(bias.shape), _full((1, a))],
        out_specs=new,
        compiler_params=_cparams(),
        name="attn_sample",
    )(q, kn, vn, cache_k, cache_v, bias, g.reshape(1, a))


def _gelu_tanh(x):
    return 0.5 * x * (1.0 + jnp.tanh(np.sqrt(2.0 / np.pi).astype(np.float32) * (x + 0.044715 * (x * x * x))))


def _ssm_kernel(u_ref, bb_ref, lam_ref, pow_ref, cm_ref, dsk_ref, wglu_ref, bglu_ref, g_ref, pt_ref, h0_ref,
                z_ref, ht_ref, h_s, hin_s, carry_s, *, rows, steps, chain):
    ns = N_STATE
    cw = (4 * V7X_SUBLANES * V7X_LANES) // rows
    last = (steps - 1) * rows

    @pl.when(pl.program_id(0) == 0)
    def _():
        carry_s[...] = h0_ref[0:1, :]

    ub = u_ref[...].astype(BF16)
    n_slabs = SSM_WIDTH // V7X_LANES
    sw = ns // n_slabs
    for s in range(n_slabs):
        part = jnp.dot(ub[:, s * V7X_LANES:(s + 1) * V7X_LANES], bb_ref[s], preferred_element_type=F32)
        h_s[:, s * sw:(s + 1) * sw] = part[:, :sw]
        h_s[:, ns + s * sw:ns + (s + 1) * sw] = part[:, sw:]

    for cc in range(0, ns // cw, 2):
        crs = [slice(c * cw, (c + 1) * cw) for c in (cc, cc + 1)]
        cis = [slice(ns + c * cw, ns + (c + 1) * cw) for c in (cc, cc + 1)]
        lrs = [jnp.broadcast_to(lam_ref[0:1, cr], (rows, cw)) for cr in crs]
        lis = [jnp.broadcast_to(lam_ref[1:2, cr], (rows, cw)) for cr in crs]

        def scan_body(t, carry, crs=crs, cis=cis, lrs=lrs, lis=lis):
            rs = pl.ds(pl.multiple_of(t * rows, rows), rows)
            out = []
            for j in range(2):
                hr, hi = carry[2 * j], carry[2 * j + 1]
                nr = lrs[j] * hr - lis[j] * hi + h_s[rs, crs[j]]
                ni = lrs[j] * hi + lis[j] * hr + h_s[rs, cis[j]]
                h_s[rs, crs[j]] = nr
                h_s[rs, cis[j]] = ni
                out += [nr, ni]
            return tuple(out)

        zero = jnp.zeros((rows, cw), F32)
        lax.fori_loop(0, steps, scan_body, (zero,) * 4)

    if chain:
        ptr = pow_ref[steps - 1, 0:1, :ns]
        pti = pow_ref[steps - 1, 0:1, ns:]
        carry = carry_s[...]
        for c in range(rows):
            hin_s[c:c + 1, :] = carry
            cr_, ci_ = carry[:, :ns], carry[:, ns:]
            e = h_s[last + c:last + c + 1, :]
            carry = jnp.concatenate([ptr * cr_ - pti * ci_ + e[:, :ns],
                                     ptr * ci_ + pti * cr_ + e[:, ns:]], axis=1)
        carry_s[...] = carry
    else:
        hin_s[...] = h0_ref[...]

    for cc in range(ns // cw):
        cr = slice(cc * cw, (cc + 1) * cw)
        ci = slice(ns + cc * cw, ns + (cc + 1) * cw)
        hr0 = hin_s[:, cr]
        hi0 = hin_s[:, ci]

        def fix_body(t, _, cr=cr, ci=ci, hr0=hr0, hi0=hi0):
            rs = pl.ds(pl.multiple_of(t * rows, rows), rows)
            pr = pow_ref[t, :, cr]
            pi_ = pow_ref[t, :, ci]
            h_s[rs, cr] = h_s[rs, cr] + (pr * hr0 - pi_ * hi0)
            h_s[rs, ci] = h_s[rs, ci] + (pr * hi0 + pi_ * hr0)
            return 0

        lax.fori_loop(0, steps, fix_body, 0, unroll=4)

    ht_ref[...] = h_s[last:last + rows, :]
    ys = []
    for s in range(n_slabs):
        hs = jnp.concatenate([h_s[:, s * sw:(s + 1) * sw], h_s[:, ns + s * sw:ns + (s + 1) * sw]], axis=1)
        ys.append(jnp.dot(hs.astype(BF16), cm_ref[s], preferred_element_type=F32))
    y = jnp.concatenate(ys, axis=1) + dsk_ref[...] * u_ref[...]
    y = _gelu_tanh(y)
    gl = jnp.dot(y.astype(BF16), wglu_ref[...], preferred_element_type=F32) + bglu_ref[...]
    z = _rms(y * _sigmoid(gl), g_ref[...]).astype(BF16)
    z_ref[...] = jnp.dot(pt_ref[...], z, preferred_element_type=F32).astype(z_ref.dtype)


def _ssm_tables(ssm_a_re, ssm_a_im, ssm_log_dt, ssm_b_re, ssm_b_im, ssm_c_re, ssm_c_im, max_steps):
    g, n, ch = SSM_GROUPS, SSM_STATE, SSM_CH
    a_re = ssm_a_re.astype(F32)
    a_im = ssm_a_im.astype(F32)
    dt = jnp.exp(ssm_log_dt.astype(F32))[:, None]
    mag = jnp.exp(dt * a_re)
    lam_re = mag * jnp.cos(dt * a_im)
    lam_im = mag * jnp.sin(dt * a_im)
    nr = lam_re - 1.0
    ni = lam_im
    inv = 1.0 / (a_re * a_re + a_im * a_im)
    coef_re = (nr * a_re + ni * a_im) * inv
    coef_im = (ni * a_re - nr * a_im) * inv
    br = ssm_b_re.astype(F32)
    bi = ssm_b_im.astype(F32)
    bb_re = coef_re[..., None] * br - coef_im[..., None] * bi
    bb_im = coef_re[..., None] * bi + coef_im[..., None] * br
    gs = V7X_LANES // ch
    ns_ = g // gs
    eye = jnp.eye(gs, dtype=F32)
    bmat = lambda b: jnp.einsum("sgnc,gh->sgchn", b.reshape(ns_, gs, n, ch), eye).reshape(ns_, gs * ch, gs * n)
    cmat = lambda c: jnp.einsum("sgcn,gh->sgnhc", c.astype(F32).reshape(ns_, gs, ch, n), eye).reshape(
        ns_, gs * n, gs * ch)
    bb = jnp.concatenate([bmat(bb_re), bmat(bb_im)], axis=2).astype(BF16)
    cm = jnp.concatenate([cmat(ssm_c_re), -cmat(ssm_c_im)], axis=1).astype(BF16)
    lam = jnp.stack([lam_re.reshape(-1), lam_im.reshape(-1)])

    k = jnp.arange(1, max_steps + 1, dtype=F32)[:, None]
    kdt = k * dt.reshape(1, -1).repeat(n, axis=1)
    pmag = jnp.exp(kdt * a_re.reshape(1, -1))
    parg = kdt * a_im.reshape(1, -1)
    pows = jnp.concatenate([pmag * jnp.cos(parg), pmag * jnp.sin(parg)], axis=1)
    return bb, cm, lam, pows


def _ssm(u_perm, h0, tabs, dsk, w_glu, b_glu, g, perm_t, rows, steps, chain):
    n, w = u_perm.shape
    blk = rows * steps
    bb, cm, lam, pows = tabs
    row = pl.BlockSpec((blk, w), lambda i: (i, 0))
    kern = functools.partial(_ssm_kernel, rows=rows, steps=steps, chain=chain)
    return pl.pallas_call(
        kern,
        out_shape=(jax.ShapeDtypeStruct((n, w), BF16), jax.ShapeDtypeStruct((rows, 2 * N_STATE), F32)),
        grid=(n // blk,),
        in_specs=[row, _full(bb.shape), _full(lam.shape), _full((steps, rows, 2 * N_STATE)), _full(cm.shape),
                  _full((1, w)), _full((w, w)), _full((1, w)), _full((1, w)), _full((blk, blk)),
                  _full((rows, 2 * N_STATE))],
        out_specs=(row, _full((rows, 2 * N_STATE))),
        scratch_shapes=[pltpu.VMEM((blk, 2 * N_STATE), F32),
                        pltpu.VMEM((rows, 2 * N_STATE), F32),
                        pltpu.VMEM((1, 2 * N_STATE), F32)],
        compiler_params=_cparams(),
        name=f"ssm_r{rows}",
    )(u_perm, bb, lam, jnp.broadcast_to(pows[:steps, None, :], (steps, rows, 2 * N_STATE)), cm,
      dsk.reshape(1, w), w_glu.astype(BF16), b_glu.reshape(1, w),
      g.reshape(1, w), perm_t, h0)


def _outproj_kernel(*refs, n_real, aliased):
    ins, outs = refs[:12], refs[12 + aliased:]
    step = pl.program_id(0)

    @pl.when(step < n_real)
    def _():
        _outproj_tile(*ins, *outs)

    @pl.when(step >= n_real)
    def _():
        xs_ref = outs[1]
        xs_ref[...] = jnp.zeros(xs_ref.shape, xs_ref.dtype)


def _outproj_tile(x_ref, a_ref, z_ref, gt_ref, sh_ref, sc_ref, g_ref, wo_ref, wr_ref, br_ref, ltri_ref, utri_ref,
                  x1_ref, xs_ref, slot_ref, gates_ref, meta_ref, wbf_ref):
    @pl.when(pl.program_id(0) == 0)
    def _():
        wbf_ref[...] = wo_ref[...].astype(BF16)

    a = ATTN_WIDTH
    mixed = (jnp.dot(a_ref[...].astype(BF16), wbf_ref[:a, :], preferred_element_type=F32)
             + jnp.dot(z_ref[...], wbf_ref[a:, :], preferred_element_type=F32))
    x1 = x_ref[...] + gt_ref[...] * mixed
    x1_ref[...] = x1
    h2 = (_rms(x1, g_ref[...]) * (1.0 + sc_ref[...]) + sh_ref[...]).astype(BF16)
    lg = jnp.dot(h2, wr_ref[...], preferred_element_type=F32) + br_ref[...]
    tm = lg.shape[0]
    lane = lax.broadcasted_iota(jnp.int32, (tm, V7X_LANES), 1).astype(F32)
    vals, hots = [], []
    for _ in range(TOP_K):
        m = jnp.max(lg, axis=1, keepdims=True)
        idx = jnp.min(jnp.where(lg == m, lane, float(V7X_LANES)), axis=1, keepdims=True)
        hot = lane == idx
        vals.append(m)
        hots.append(jnp.where(hot, 1.0, 0.0))
        lg = jnp.where(hot, MASKED * 2, lg)
    es = [jnp.exp(v - vals[0]) for v in vals]
    inv = 1.0 / (es[0] + es[1] + es[2] + es[3])

    member = hots[0] + hots[1] + hots[2] + hots[3]
    before = jnp.dot(ltri_ref[...], member.astype(BF16), preferred_element_type=F32)
    count = jnp.sum(member, axis=0, keepdims=True)
    padded = jnp.floor((count + (ROW_ALIGN - 1.0)) * (1.0 / ROW_ALIGN)) * ROW_ALIGN
    padded8 = jnp.broadcast_to(padded, (V7X_SUBLANES, V7X_LANES))
    start = jnp.dot(padded8.astype(BF16), utri_ref[...], preferred_element_type=F32)[0:1]
    where_to = start + before
    slots = jnp.zeros((tm, V7X_LANES), F32)
    gates = jnp.zeros((tm, V7X_LANES), F32)
    for k in range(TOP_K):
        slot_k = jnp.sum(hots[k] * where_to, axis=1, keepdims=True)
        slots = jnp.where(lane == float(k), slot_k, slots)
        gates = jnp.where(lane == float(k), es[k] * inv, gates)
    slot_ref[...] = slots
    gates_ref[...] = gates
    row = lax.broadcasted_iota(jnp.int32, (V7X_SUBLANES, V7X_LANES), 0)
    meta_ref[...] = jnp.where(row == 0, padded8, jnp.where(row == 1, jnp.broadcast_to(start, padded8.shape), 0.0))

    cap = xs_ref.shape[0]
    slots_t = jnp.transpose(slots)
    srow = lax.broadcasted_iota(jnp.int32, (cap, tm), 0).astype(F32)
    place = jnp.zeros((cap, tm), F32)
    for k in range(TOP_K):
        place = jnp.where(srow == slots_t[k:k + 1, :], 1.0, place)
    xs_ref[...] = jnp.dot(place.astype(BF16), h2, preferred_element_type=F32).astype(BF16)


def _outproj(x, a, z, gt, sh, sc, g, w_out, wr_pad, br_pad, tm, n_blocks, block0, xs_prev=None):
    n, d = x.shape
    aw = ATTN_WIDTH
    n_real = n // tm
    aliased = xs_prev is not None
    n_steps = n_real if aliased else n_blocks
    tile = lambda i: jnp.minimum(i, n_real - 1)
    mod_spec = (pl.BlockSpec((1, d), lambda i: (0, 0)) if gt.shape[0] == 1
                else pl.BlockSpec((tm, d), lambda i: (tile(i), 0)))
    row = lambda w: pl.BlockSpec((tm, w), lambda i: (tile(i), 0))
    ltri = jnp.asarray(np.tril(np.ones((tm, tm), np.float32), -1), BF16)
    utri = jnp.asarray(np.triu(np.ones((V7X_LANES, V7X_LANES), np.float32), 1), BF16)
    in_specs = [row(d), row(aw), row(SSM_WIDTH), mod_spec, mod_spec, mod_spec, _full((1, d)),
                _full(w_out.shape), _full(wr_pad.shape), _full(br_pad.shape), _full(ltri.shape), _full(utri.shape)]
    args = [x, a, z, gt, sh, sc, g.reshape(1, d), w_out, wr_pad, br_pad, ltri, utri]
    if aliased:
        in_specs.append(pl.BlockSpec(memory_space=pl.ANY))
        args.append(xs_prev)
    return pl.pallas_call(
        functools.partial(_outproj_kernel, n_real=n_real, aliased=int(aliased)),
        out_shape=(jax.ShapeDtypeStruct((n, d), F32),
                   jax.ShapeDtypeStruct((n_blocks, GROUP_CAP, d), BF16),
                   jax.ShapeDtypeStruct((n, V7X_LANES), F32), jax.ShapeDtypeStruct((n, V7X_LANES), F32),
                   jax.ShapeDtypeStruct((n_real, V7X_SUBLANES, V7X_LANES), F32)),
        grid=(n_steps,),
        in_specs=in_specs,
        out_specs=(row(d), pl.BlockSpec((None, GROUP_CAP, d), lambda i: (block0 + i, 0, 0)),
                   row(V7X_LANES), row(V7X_LANES),
                   pl.BlockSpec((None, V7X_SUBLANES, V7X_LANES), lambda i: (tile(i), 0, 0))),
        scratch_shapes=[pltpu.VMEM(w_out.shape, BF16)],
        input_output_aliases={len(args) - 1: 1} if aliased else {},
        compiler_params=_cparams(),
        name="outproj_router",
    )(*args)


_PIECE_SIZES = tuple(MOE_TILE >> s for s in range(6))
_CHUNK_SHIFT = 6
_CHUNK = 1 << _CHUNK_SHIFT
_TAIL_SIZES = tuple(sz for sz in _PIECE_SIZES if sz < _CHUNK)


def _moe_kernel(te_ref, tl_ref, lo_ref, hi_ref, nu_ref, gstart_ref, gsize_ref, gbase_ref, rows_ref, used_ref,
                xs_hbm, wgu_ref, bgu_ref, wd_ref, bd_ref, ys_hbm,
                xbuf, ybuf, zbuf, wgu_bf, wd_bf, xsem, ysem, zsem):
    t = pl.program_id(0)
    n_used = nu_ref[0]
    n_blocks, cap = xs_hbm.shape[0], xs_hbm.shape[1]

    def x_copy(i, br, tr, sz, slot):
        pltpu.make_async_copy(xs_hbm.at[i, pl.ds(br, sz)], xbuf.at[slot, pl.ds(tr, sz)], xsem.at[slot]).start()

    def y_copy(i, br, tr, sz, slot):
        pltpu.make_async_copy(ybuf.at[slot, pl.ds(tr, sz)], ys_hbm.at[i, pl.ds(br, sz)], ysem.at[slot]).start()

    def z_copy(i, row, sz, start):
        cp = pltpu.make_async_copy(zbuf.at[pl.ds(0, sz)], ys_hbm.at[i, pl.ds(row, sz)], zsem)
        cp.start() if start else cp.wait()

    def pieces(tt, fn):
        e = te_ref[tt]
        lo = tl_ref[tt] * MOE_TILE

        def per_block(i, c):
            g = e * n_blocks + i
            s0 = gbase_ref[g]
            a = jnp.maximum(s0, lo)
            b = jnp.minimum(s0 + gsize_ref[g], lo + MOE_TILE)
            length = jnp.maximum(b - a, 0)
            src = gstart_ref[g] + (a - s0)
            dst = a - lo
            whole = lax.shift_right_logical(length, _CHUNK_SHIFT)

            def chunk(j, cc):
                off = j * _CHUNK
                fn(i, pl.multiple_of(src + off, ROW_ALIGN), pl.multiple_of(dst + off, ROW_ALIGN), _CHUNK)
                return cc

            lax.fori_loop(0, whole, chunk, 0)
            done = whole * _CHUNK
            for sz in _TAIL_SIZES:
                hit = (length & sz) != 0

                @pl.when(hit)
                def _(sz=sz, done=done):
                    fn(i, pl.multiple_of(src + done, ROW_ALIGN), pl.multiple_of(dst + done, ROW_ALIGN), sz)

                done = done + jnp.where(hit, sz, 0)
            return c

        lax.fori_loop(lo_ref[tt], hi_ref[tt], per_block, 0)

    def tile_rows(tt):
        return jnp.minimum(rows_ref[te_ref[tt]] - tl_ref[tt] * MOE_TILE, MOE_TILE)

    def wait_rows(n, sem, buf):
        def chunk(j, cc):
            pltpu.make_async_copy(buf.at[pl.ds(0, _CHUNK)], buf.at[pl.ds(0, _CHUNK)], sem).wait()
            return cc

        lax.fori_loop(0, lax.shift_right_logical(n, _CHUNK_SHIFT), chunk, 0)
        for sz in _TAIL_SIZES:
            @pl.when((n & sz) != 0)
            def _(sz=sz):
                pltpu.make_async_copy(buf.at[pl.ds(0, sz)], buf.at[pl.ds(0, sz)], sem).wait()

    def fetch(tt, slot):
        pieces(tt, lambda i, br, tr, sz: x_copy(i, br, tr, sz, slot))

    def writeback(tt, slot):
        pieces(tt, lambda i, br, tr, sz: y_copy(i, br, tr, sz, slot))

    def zero_tail(i, start):
        u = used_ref[i]
        rem = cap - u
        nz = zbuf.shape[0]
        whole = lax.shift_right_logical(rem, nz.bit_length() - 1)

        def chunk(j, c):
            z_copy(i, pl.multiple_of(u + j * nz, ROW_ALIGN), nz, start)
            return c

        lax.fori_loop(0, whole, chunk, 0)
        base = u + whole * nz
        done = jnp.int32(0)
        for sz in _PIECE_SIZES:
            if sz >= nz:
                continue
            hit = (rem & sz) != 0

            @pl.when(hit)
            def _(sz=sz, done=done):
                z_copy(i, pl.multiple_of(base + done, ROW_ALIGN), sz, start)

            done = done + jnp.where(hit, sz, 0)

    @pl.when(t == 0)
    def _():
        xbuf[...] = jnp.zeros(xbuf.shape, xbuf.dtype)
        zbuf[...] = jnp.zeros(zbuf.shape, zbuf.dtype)
        lax.fori_loop(0, n_blocks, lambda i, c: (zero_tail(i, True), c)[1], 0)
        lax.fori_loop(0, n_blocks, lambda i, c: (zero_tail(i, False), c)[1], 0)
        fetch(0, 0)

    @pl.when(t < n_used)
    def _():
        slot = t % 2

        @pl.when(t + 1 < n_used)
        def _():
            fetch(t + 1, 1 - slot)

        prev = te_ref[jnp.maximum(t - 1, 0)]
        fresh = jnp.logical_or(t == 0, te_ref[t] != prev)

        @pl.when(fresh)
        def _():
            wgu_bf[...] = wgu_ref[...].astype(BF16)
            wd_bf[...] = wd_ref[...].astype(BF16)

        wait_rows(tile_rows(t), xsem.at[slot], xbuf.at[slot])

        @pl.when(t >= 2)
        def _():
            wait_rows(tile_rows(t - 2), ysem.at[slot], ybuf.at[slot])

        f = D_EXPERT
        gu = jnp.dot(xbuf[slot], wgu_bf[...], preferred_element_type=F32) + bgu_ref[...]
        gate = jnp.minimum(gu[:, :f], SWIGLU_LIMIT)
        up = jnp.clip(gu[:, f:], -SWIGLU_LIMIT, SWIGLU_LIMIT)
        act = (up + 1.0) * gate * _sigmoid(SWIGLU_ALPHA * gate)
        y = jnp.dot(act.astype(BF16), wd_bf[...], preferred_element_type=F32) + bd_ref[...]
        ybuf[slot] = y.astype(BF16)
        writeback(t, slot)

        @pl.when(t == n_used - 1)
        def _():
            wait_rows(tile_rows(t), ysem.at[slot], ybuf.at[slot])

            @pl.when(t >= 1)
            def _():
                wait_rows(tile_rows(t - 1), ysem.at[1 - slot], ybuf.at[1 - slot])


def _moe(plan, xs, w_gate_up, b_gate_up, w_down, b_down):
    _, cap, d = xs.shape
    e, _, f2 = w_gate_up.shape
    nt = plan[0].shape[0]
    wmap = lambda t, te, *_: (te[t], 0, 0)
    anyspec = pl.BlockSpec(memory_space=pl.ANY)
    grid_spec = pltpu.PrefetchScalarGridSpec(
        num_scalar_prefetch=len(plan),
        grid=(nt,),
        in_specs=[anyspec,
                  pl.BlockSpec((None, d, f2), wmap),
                  pl.BlockSpec((None, 1, f2), wmap),
                  pl.BlockSpec((None, f2 // 2, d), wmap),
                  pl.BlockSpec((None, 1, d), wmap)],
        out_specs=anyspec,
        scratch_shapes=[pltpu.VMEM((2, MOE_TILE, d), BF16), pltpu.VMEM((2, MOE_TILE, d), BF16),
                        pltpu.VMEM((MOE_TILE // 2, d), BF16),
                        pltpu.VMEM((d, f2), BF16), pltpu.VMEM((f2 // 2, d), BF16),
                        pltpu.SemaphoreType.DMA((2,)), pltpu.SemaphoreType.DMA((2,)), pltpu.SemaphoreType.DMA(())],
    )
    return pl.pallas_call(
        _moe_kernel,
        out_shape=jax.ShapeDtypeStruct(xs.shape, BF16),
        grid_spec=grid_spec,
        compiler_params=_cparams(),
        name="moe_experts",
    )(*plan, xs, w_gate_up, b_gate_up.reshape(e, 1, f2), w_down, b_down.reshape(e, 1, d))


def _moe_plan(group_size, group_start):
    n_blocks = group_size.shape[0]
    gsize = group_size.T.astype(jnp.int32)
    gstart = group_start.T.astype(jnp.int32)
    gbase = jnp.cumsum(gsize, axis=1) - gsize
    rows = jnp.sum(gsize, axis=1)
    tiles = (rows + MOE_TILE - 1) // MOE_TILE
    tile_end = jnp.cumsum(tiles)
    n_used = tile_end[-1:]
    nt = (n_blocks * GROUP_CAP) // MOE_TILE + N_EXPERTS
    t = jnp.arange(nt, dtype=jnp.int32)
    te = jnp.sum((tile_end[None, :] <= t[:, None]).astype(jnp.int32), axis=1)
    last = jnp.max(jnp.where(tiles > 0, jnp.arange(N_EXPERTS, dtype=jnp.int32), 0))
    te = jnp.where(t < n_used[0], jnp.minimum(te, N_EXPERTS - 1), last)
    hot = (te[:, None] == jnp.arange(N_EXPERTS, dtype=jnp.int32)[None, :]).astype(jnp.int32)
    tl = jnp.where(t < n_used[0], t - hot @ (tile_end - tiles), 0)
    lo_row = tl * MOE_TILE
    base_t = hot @ gbase
    size_t = hot @ gsize
    first = jnp.sum((base_t + size_t <= lo_row[:, None]).astype(jnp.int32), axis=1)
    stop = jnp.sum((base_t < lo_row[:, None] + MOE_TILE).astype(jnp.int32), axis=1)
    used = jnp.sum(gsize, axis=0)
    i32 = lambda z: z.astype(jnp.int32)
    return (i32(te), i32(tl), i32(first), i32(stop), i32(n_used), i32(gstart.reshape(-1)), i32(gsize.reshape(-1)),
            i32(gbase.reshape(-1)), i32(rows), i32(used))


def _final_kernel(x_ref, ys_ref, slot_ref, gates_ref, gt_ref, g_ref, y_ref):
    tm = x_ref.shape[0]
    cap = ys_ref.shape[0]
    col = lax.broadcasted_iota(jnp.int32, (tm, cap), 1).astype(F32)
    slots = slot_ref[...]
    gates = gates_ref[...]
    mix = jnp.zeros((tm, cap), F32)
    for k in range(TOP_K):
        mix = jnp.where(col == slots[:, k:k + 1], gates[:, k:k + 1], mix)
    hi = mix.astype(BF16)
    lo = (mix - hi.astype(F32)).astype(BF16)
    ys = ys_ref[...]
    ff = jnp.dot(hi, ys, preferred_element_type=F32) + jnp.dot(lo, ys, preferred_element_type=F32)
    y_ref[...] = _rms(x_ref[...] + gt_ref[...] * ff, g_ref[...])


def _final(x1, ys, slots, gates, gt, g, tm, block0):
    n, d = x1.shape
    cap = ys.shape[1]
    mod_spec = (pl.BlockSpec((1, d), lambda i: (0, 0)) if gt.shape[0] == 1
                else pl.BlockSpec((tm, d), lambda i: (i, 0)))
    row = lambda w: pl.BlockSpec((tm, w), lambda i: (i, 0))
    return pl.pallas_call(
        _final_kernel,
        out_shape=jax.ShapeDtypeStruct((n, d), F32),
        grid=(n // tm,),
        in_specs=[row(d), pl.BlockSpec((None, cap, d), lambda i: (block0 + i, 0, 0)),
                  row(V7X_LANES), row(V7X_LANES), mod_spec, _full((1, d))],
        out_specs=row(d),
        compiler_params=_cparams(),
        name="final_norm",
    )(x1, ys, slots, gates, gt, g.reshape(1, d))


def kernel(x_prompt, x_sample, cache_k_win, cache_v_win, state_ssm_re, state_ssm_im, c_prompt, c_sample,
           w_ada, b_ada, g_norm1, w_in, ssm_a_re, ssm_a_im, ssm_log_dt, ssm_b_re, ssm_b_im, ssm_c_re, ssm_c_im,
           ssm_d, w_glu, b_glu, g_out_attn, g_out_ssm, w_out, g_norm2, w_router, b_router, w_gate_up, b_gate_up,
           w_down, b_down, g_final):
    depth = w_ada.shape[0]
    assert depth == 1 and x_prompt.shape[0] == 1
    bp, t, d = x_prompt.shape
    bs, ts, _ = x_sample.shape
    ns = bs * ts
    l = 0
    a = ATTN_WIDTH

    n_c = bp + bs
    c_pad = -(-n_c // V7X_SUBLANES) * V7X_SUBLANES
    c_rows = jnp.concatenate([c_prompt, c_sample, jnp.zeros((c_pad - n_c, d), F32)], axis=0)
    mod = _ada_modulation(c_rows, w_ada[l], b_ada[l])
    mod_p = [mod[0:1, i * d:(i + 1) * d] for i in range(N_MOD)]
    mod_s = [jnp.repeat(mod[bp:bp + bs, i * d:(i + 1) * d], ts, axis=0) for i in range(N_MOD)]

    steps_p = TOKEN_TILE // SSM_ROWS_PROMPT
    perm_p = _chunk_perm(SSM_ROWS_PROMPT, steps_p)
    perm_s = _chunk_perm(bs, ts)
    tabs = _ssm_tables(ssm_a_re[l], ssm_a_im[l], ssm_log_dt[l], ssm_b_re[l], ssm_b_im[l],
                       ssm_c_re[l], ssm_c_im[l], max(steps_p, ts))
    wr_pad = jnp.zeros((d, V7X_LANES), F32).at[:, :N_EXPERTS].set(w_router[l]).astype(BF16)
    br_pad = jnp.full((1, V7X_LANES), MASKED, F32).at[0, :N_EXPERTS].set(b_router[l])

    xp = x_prompt.reshape(t, d)
    dils = tuple(dil for _, dil in DILATED_BRANCHES)
    wide = tuple(dil for dil in dils if dil > 1)
    proj_p = _inproj(xp, mod_p[0], mod_p[1], g_norm1[l], w_in[l], jnp.asarray(perm_p, BF16), TOKEN_TILE, wide)
    kpf, vpf, up = proj_p[3:6]
    views = {1: proj_p[0:3]}
    for n_d, dil in enumerate(wide):
        views[dil] = proj_p[6 + 3 * n_d:9 + 3 * n_d]
    outs = [_attn_branch(*views[dil], dil) for dil in dils]
    ap = _attn_combine([o for o, _ in outs], [s for _, s in outs], g_out_attn[l], dils)
    zeros_h = jnp.zeros((SSM_ROWS_PROMPT, 2 * N_STATE), F32)
    zp, hp = _ssm(up, zeros_h, tabs, ssm_d[l], w_glu[l], b_glu[l], g_out_ssm[l],
                  jnp.asarray(perm_p.T, BF16), SSM_ROWS_PROMPT, steps_p, True)
    n_blocks_p = t // TOKEN_TILE
    n_blocks = n_blocks_p + 1
    x1p, xs_all, slot_p, gate_p, meta_p = _outproj(xp, ap, zp, mod_p[2], mod_p[3], mod_p[4], g_norm2[l], w_out[l],
                                                   wr_pad, br_pad, TOKEN_TILE, n_blocks, 0)

    xs = x_sample.reshape(ns, d)
    qs, ks, vs, ksf, vsf, us = _inproj(xs, mod_s[0], mod_s[1], g_norm1[l], w_in[l], jnp.asarray(perm_s, BF16), ns)
    as_ = _attn_sample(qs.reshape(bs, ts, a), ks.reshape(bs, ts, a), vs.reshape(bs, ts, a),
                       cache_k_win[l].reshape(bs, -1, a), cache_v_win[l].reshape(bs, -1, a), g_out_attn[l])
    h0s = jnp.concatenate([state_ssm_re[l].reshape(bs, N_STATE), state_ssm_im[l].reshape(bs, N_STATE)], axis=1)
    zs, hs = _ssm(us, h0s, tabs, ssm_d[l], w_glu[l], b_glu[l], g_out_ssm[l],
                  jnp.asarray(perm_s.T, BF16), bs, ts, False)
    x1s, xs_all, slot_s, gate_s, meta_s = _outproj(xs, as_.reshape(ns, a), zs, mod_s[2], mod_s[3], mod_s[4],
                                                   g_norm2[l], w_out[l], wr_pad, br_pad, ns, n_blocks, n_blocks_p,
                                                   xs_prev=xs_all)

    meta = jnp.concatenate([meta_p, meta_s], axis=0)
    plan = _moe_plan(meta[:, 0, :N_EXPERTS], meta[:, 1, :N_EXPERTS])
    ys_all = _moe(plan, xs_all, w_gate_up[l], b_gate_up[l], w_down[l], b_down[l])

    y_prompt = _final(x1p, ys_all, slot_p, gate_p, mod_p[5], g_final, TOKEN_TILE, 0).reshape(bp, t, d)
    y_sample = _final(x1s, ys_all, slot_s, gate_s, mod_s[5], g_final, ns, n_blocks_p).reshape(bs, ts, d)

    keep = min(MAX_WINDOW, t)
    shp = (1, bp, keep, N_HEADS, HEAD_DIM)
    k_win = kpf[t - keep:].reshape(shp)
    v_win = vpf[t - keep:].reshape(shp)
    st = (1, bp, SSM_GROUPS, SSM_STATE)
    hp_last = hp[SSM_ROWS_PROMPT - 1]
    ss = (1, bs, SSM_GROUPS, SSM_STATE)
    return (y_prompt, y_sample, k_win, v_win,
            hp_last[:N_STATE].reshape(st), hp_last[N_STATE:].reshape(st),
            ksf.reshape(1, bs, ts, N_HEADS, HEAD_DIM), vsf.reshape(1, bs, ts, N_HEADS, HEAD_DIM),
            hs[:, :N_STATE].reshape(ss), hs[:, N_STATE:].reshape(ss))
```

```python
import functools

import numpy as np
import jax
import jax.numpy as jnp
from jax import lax
from jax.experimental import pallas as pl
from jax.experimental.pallas import tpu as pltpu

F32 = jnp.float32
BF16 = jnp.bfloat16

D_MODEL = 1024
N_HEADS = 8
HEAD_DIM = 64
ATTN_WIDTH = N_HEADS * HEAD_DIM
DILATED_BRANCHES = ((128, 1), (512, 4), (2048, 16))
KEYS_PER_BRANCH = 129
MAX_WINDOW = 2048
SSM_WIDTH = D_MODEL - ATTN_WIDTH
SSM_CH = 16
SSM_GROUPS = SSM_WIDTH // SSM_CH
SSM_STATE = 64
N_STATE = SSM_GROUPS * SSM_STATE
N_EXPERTS = 32
TOP_K = 4
D_EXPERT = D_MODEL
SWIGLU_LIMIT = 7.0
SWIGLU_ALPHA = 1.702
N_MOD = 6
EPS = 1e-6
MASKED = -1e30

V7X_LANES = 128
V7X_SUBLANES = 8
V7X_VMEM_LIMIT_BYTES = 56 * 1024 * 1024

TOKEN_TILE = 512
Q_TILE = 128
Q_BLOCKS_PER_STEP = 4
SSM_ROWS_PROMPT = 8
MOE_TILE = 512
MOE_HIDDEN_BLOCK = 256
ROW_ALIGN = 16
GROUP_CAP = -(-(TOKEN_TILE * TOP_K + N_EXPERTS * (ROW_ALIGN - 1)) // (2 * V7X_LANES)) * (2 * V7X_LANES)


def _cparams(n_axes=1):
    return pltpu.CompilerParams(
        dimension_semantics=("arbitrary",) * n_axes,
        vmem_limit_bytes=V7X_VMEM_LIMIT_BYTES,
    )


def _full(shape):
    n = len(shape)
    return pl.BlockSpec(shape, lambda *_: (0,) * n)


def _rms(x, g):
    return x * lax.rsqrt(jnp.mean(x * x, axis=-1, keepdims=True) + EPS) * g


def _sigmoid(x):
    return 1.0 / (1.0 + jnp.exp(-x))


def _ada_kernel(c_ref, w_ref, b_ref, o_ref):
    c = c_ref[...]
    s = (c * _sigmoid(c)).astype(BF16)
    o_ref[...] = jnp.dot(s, w_ref[...].astype(BF16), preferred_element_type=F32) + b_ref[...]


def _ada_modulation(c_rows, w_ada, b_ada):
    m, d = c_rows.shape
    n = w_ada.shape[1]
    tn = n // 4
    return pl.pallas_call(
        _ada_kernel,
        out_shape=jax.ShapeDtypeStruct((m, n), F32),
        grid=(n // tn,),
        in_specs=[_full((m, d)),
                  pl.BlockSpec((d, tn), lambda j: (0, j)),
                  pl.BlockSpec((1, tn), lambda j: (0, j))],
        out_specs=pl.BlockSpec((m, tn), lambda j: (0, j)),
        compiler_params=_cparams(),
        name="ada_modulation",
    )(c_rows, w_ada, b_ada.reshape(1, n))


def _inproj_kernel(*refs, dils):
    x_ref, sh_ref, sc_ref, g_ref, w_ref, perm_ref = refs[:6]
    dperm_refs = refs[6:6 + len(dils)]
    q_ref, k_ref, v_ref, kf_ref, vf_ref, u_ref = refs[6 + len(dils):12 + len(dils)]
    dil_refs = refs[12 + len(dils):-1]
    wbf_ref = refs[-1]

    @pl.when(pl.program_id(0) == 0)
    def _():
        wbf_ref[...] = w_ref[...].astype(BF16)

    h = _rms(x_ref[...], g_ref[...]) * (1.0 + sc_ref[...]) + sh_ref[...]
    hb = h.astype(BF16)
    a = ATTN_WIDTH
    proj = jnp.dot(hb, wbf_ref[:, :3 * a], preferred_element_type=F32)
    k = proj[:, a:2 * a]
    v = proj[:, 2 * a:]
    qkv = jnp.concatenate([(proj[:, :a] * (HEAD_DIM ** -0.5)).astype(BF16), k.astype(BF16), v.astype(BF16)],
                          axis=1)
    q_ref[...] = qkv[:, :a]
    k_ref[...] = qkv[:, a:2 * a]
    v_ref[...] = qkv[:, 2 * a:]
    kf_ref[...] = k
    vf_ref[...] = v
    hp = jnp.dot(perm_ref[...], hb, preferred_element_type=F32).astype(BF16)
    u_ref[...] = jnp.dot(hp, wbf_ref[:, 3 * a:], preferred_element_type=F32)
    tm = qkv.shape[0]
    for n_d, dil in enumerate(dils):
        by_residue = jnp.dot(dperm_refs[n_d][...], qkv, preferred_element_type=F32).astype(BF16)
        per = tm // dil
        for r in range(dil):
            rows = by_residue[r * per:(r + 1) * per]
            for j in range(3):
                dil_refs[3 * n_d + j][:, r * a:(r + 1) * a] = rows[:, j * a:(j + 1) * a]


def _inproj(x, sh, sc, g, w_in, perm, tm, dils=()):
    n, d = x.shape
    a = ATTN_WIDTH
    mod_rows = sh.shape[0]
    mod_spec = (pl.BlockSpec((1, d), lambda i: (0, 0)) if mod_rows == 1
                else pl.BlockSpec((tm, d), lambda i: (i, 0)))
    row = lambda w: pl.BlockSpec((tm, w), lambda i: (i, 0))
    dperms = [jnp.asarray(_chunk_perm(tm // dil, dil), BF16) for dil in dils]
    view_shapes = tuple(jax.ShapeDtypeStruct((n // dil, dil * a), BF16) for dil in dils for _ in range(3))
    view_specs = tuple(pl.BlockSpec((tm // dil, dil * a), lambda i: (i, 0)) for dil in dils for _ in range(3))
    return pl.pallas_call(
        functools.partial(_inproj_kernel, dils=tuple(dils)),
        out_shape=(jax.ShapeDtypeStruct((n, a), BF16),) * 3
        + (jax.ShapeDtypeStruct((n, a), F32),) * 2
        + (jax.ShapeDtypeStruct((n, SSM_WIDTH), F32),) + view_shapes,
        grid=(n // tm,),
        in_specs=[row(d), mod_spec, mod_spec, _full((1, d)), _full(w_in.shape), _full((tm, tm))]
        + [_full((tm, tm))] * len(dils),
        out_specs=(row(a),) * 5 + (row(SSM_WIDTH),) + view_specs,
        scratch_shapes=[pltpu.VMEM(w_in.shape, BF16)],
        compiler_params=_cparams(),
        name="inproj",
    )(x, sh, sc, g.reshape(1, d), w_in, perm, *dperms)


def _chunk_perm(rows, steps):
    n = rows * steps
    p = np.zeros((n, n), np.float32)
    c, t = np.meshgrid(np.arange(rows), np.arange(steps), indexing="ij")
    p[(t * rows + c).ravel(), (c * steps + t).ravel()] = 1.0
    return p


def _alibi_slopes():
    return np.exp2(-8.0 * np.arange(1, N_HEADS + 1, dtype=np.float64) / N_HEADS).astype(np.float32)


def _branch_bias(dil):
    qi = np.arange(Q_TILE)[:, None]
    col = np.arange(2 * Q_TILE)[None, :]
    j = Q_TILE + qi - col
    valid = (j >= 0) & (j <= Q_TILE)
    dist = (j * dil).astype(np.float32)
    tabs = []
    for first in (True, False):
        ok = valid & (col >= Q_TILE) if first else valid
        per_head = [np.where(ok, -s * dist, np.float32(MASKED)) for s in _alibi_slopes()]
        tabs.append(np.concatenate(per_head, axis=0))
    return np.stack(tabs).astype(np.float32)


def _attn_branch_kernel(q_ref, kp_ref, kc_ref, vp_ref, vc_ref, bias_ref, o_ref, lse_ref):
    first_step = pl.program_id(0) == 0
    lane = lax.broadcasted_iota(jnp.int32, (Q_TILE, V7X_LANES), 1)
    lo = lane < HEAD_DIM
    for j in range(Q_BLOCKS_PER_STEP):
        rows = slice(j * Q_TILE, (j + 1) * Q_TILE)
        before = slice((j - 1) * Q_TILE, j * Q_TILE)
        sel = jnp.where(first_step, 0, 1) if j == 0 else 1
        lse_acc = jnp.zeros((Q_TILE, V7X_LANES), F32)
        for p in range(N_HEADS // 2):
            cs = slice(V7X_LANES * p, V7X_LANES * (p + 1))
            q2 = q_ref[rows, cs]
            zero = jnp.zeros_like(q2)
            qq = jnp.concatenate([jnp.where(lo, q2, zero), jnp.where(lo, zero, q2)], axis=0)
            k_before = kp_ref[:, cs] if j == 0 else kc_ref[before, cs]
            v_before = vp_ref[:, cs] if j == 0 else vc_ref[before, cs]
            kk = jnp.concatenate([k_before, kc_ref[rows, cs]], axis=0)
            vv = jnp.concatenate([v_before, vc_ref[rows, cs]], axis=0)
            s = lax.dot_general(qq, kk, (((1,), (1,)), ((), ())), preferred_element_type=F32)
            s = s + bias_ref[sel, 2 * Q_TILE * p:2 * Q_TILE * (p + 1), :]
            m = jnp.max(s, axis=1, keepdims=True)
            e = jnp.exp(s - m)
            l = jnp.sum(e, axis=1, keepdims=True)
            eb = e.astype(BF16)
            o0 = jnp.dot(eb[:Q_TILE], vv, preferred_element_type=F32) * (1.0 / l[:Q_TILE])
            o1 = jnp.dot(eb[Q_TILE:], vv, preferred_element_type=F32) * (1.0 / l[Q_TILE:])
            o_ref[rows, cs] = jnp.where(lo, o0, o1).astype(o_ref.dtype)
            lse = m + jnp.log(l)
            lse_acc = jnp.where(lane == 2 * p, lse[:Q_TILE], lse_acc)
            lse_acc = jnp.where(lane == 2 * p + 1, lse[Q_TILE:], lse_acc)
        lse_ref[rows, :] = lse_acc[:, :N_HEADS]


def _attn_branch(qv, kv, vv, dil):
    a = ATTN_WIDTH
    rows = qv.shape[0]
    t = rows * dil
    step = Q_BLOCKS_PER_STEP * Q_TILE
    cur = pl.BlockSpec((step, a), lambda i, r: (i, r))
    prev = pl.BlockSpec((Q_TILE, a), lambda i, r: (jnp.maximum(i * Q_BLOCKS_PER_STEP - 1, 0), r))
    bias = jnp.asarray(_branch_bias(dil))
    o, lse = pl.pallas_call(
        _attn_branch_kernel,
        out_shape=(jax.ShapeDtypeStruct((rows, dil * a), BF16),
                   jax.ShapeDtypeStruct((dil, rows, N_HEADS), F32)),
        grid=(rows // step, dil),
        in_specs=[cur, prev, cur, prev, cur, _full(bias.shape)],
        out_specs=(cur, pl.BlockSpec((None, step, N_HEADS), lambda i, r: (r, i, 0))),
        compiler_params=_cparams(2),
        name=f"attn_branch_d{dil}",
    )(qv, kv, kv, vv, vv, bias)
    return o, lse.transpose(1, 0, 2).reshape(t, N_HEADS)


def _attn_combine_kernel(*refs, dils):
    nb = len(dils)
    o_refs = refs[:nb]
    l_refs = refs[nb:2 * nb]
    g_ref = refs[2 * nb]
    unperm_refs = refs[2 * nb + 1:-1]
    a_ref = refs[-1]
    tq, a = a_ref.shape
    outs = []
    n_u = 0
    for o_ref, dil in zip(o_refs, dils):
        if dil == 1:
            outs.append(o_ref[...].astype(F32))
            continue
        by_residue = jnp.concatenate([o_ref[:, r * a:(r + 1) * a] for r in range(dil)], axis=0)
        outs.append(jnp.dot(unperm_refs[n_u][...], by_residue, preferred_element_type=F32))
        n_u += 1
    ls = [l_ref[...] for l_ref in l_refs]
    top = functools.reduce(jnp.maximum, ls)
    ws = [jnp.exp(l - top) for l in ls]
    inv = 1.0 / functools.reduce(jnp.add, ws)
    cs = [w * inv for w in ws]
    lane = lax.broadcasted_iota(jnp.int32, (tq, V7X_LANES), 1)
    lo = lane < HEAD_DIM
    cols = []
    for p in range(N_HEADS // 2):
        sl = slice(V7X_LANES * p, V7X_LANES * (p + 1))
        acc = jnp.zeros((tq, V7X_LANES), F32)
        for c, o in zip(cs, outs):
            cexp = jnp.where(lo,
                             jnp.broadcast_to(c[:, 2 * p:2 * p + 1], (tq, V7X_LANES)),
                             jnp.broadcast_to(c[:, 2 * p + 1:2 * p + 2], (tq, V7X_LANES)))
            acc = acc + cexp * o[:, sl]
        cols.append(acc)
    o = jnp.concatenate(cols, axis=1)
    a_ref[...] = _rms(o, g_ref[...]).astype(a_ref.dtype)


def _attn_combine(os_, lses, g, dils):
    t = lses[0].shape[0]
    a = ATTN_WIDTH
    tq = TOKEN_TILE
    unperms = [jnp.asarray(_chunk_perm(tq // dil, dil).T, BF16) for dil in dils if dil > 1]
    return pl.pallas_call(
        functools.partial(_attn_combine_kernel, dils=tuple(dils)),
        out_shape=jax.ShapeDtypeStruct((t, a), BF16),
        grid=(t // tq,),
        in_specs=[pl.BlockSpec((tq // dil, dil * a), lambda i: (i, 0)) for dil in dils]
        + [pl.BlockSpec((tq, N_HEADS), lambda i: (i, 0))] * len(dils) + [_full((1, a))]
        + [_full((tq, tq))] * len(unperms),
        out_specs=pl.BlockSpec((tq, a), lambda i: (i, 0)),
        compiler_params=_cparams(),
        name="attn_combine",
    )(*os_, *lses, g.reshape(1, a), *unperms)


NEW_KEY_PAD = V7X_LANES


def _sample_key_plan(win, steps):
    g = max(d for _, d in DILATED_BRANCHES)
    near = max(w for w, d in DILATED_BRANCHES if d < g)
    assert win % g == 0 and steps <= g and near % g == 0 and near <= win
    a0 = (win - near) // g
    pos_a = (np.arange(win // g)[:, None] * g + np.arange(steps)[None, :]).ravel()
    pos_b = (np.arange(a0, win // g)[:, None] * g + np.arange(steps, g)[None, :]).ravel()
    return g, a0, np.concatenate([pos_a, pos_b])


def _sample_bias(win, steps, pos):
    n_old = len(pos)
    col = np.arange(n_old + NEW_KEY_PAD)[None, :]
    t = np.arange(steps)[:, None]
    is_new = col >= n_old
    pos_col = np.concatenate([pos, np.zeros(NEW_KEY_PAD, np.int64)])[None, :]
    dist = np.where(is_new, t - (col - n_old), win + t - pos_col)
    ok_new = (col - n_old) < steps
    tabs = []
    for window, dil in DILATED_BRANCHES:
        ok = (dist >= 0) & (dist <= window) & (dist % dil == 0) & (~is_new | ok_new)
        per_head = [np.where(ok, -s * dist.astype(np.float32), np.float32(MASKED)) for s in _alibi_slopes()]
        tabs.append(np.concatenate(per_head, axis=0))
    return np.stack(tabs).astype(np.float32)


def _attn_sample_kernel(q_ref, kn_ref, vn_ref, ka_ref, kb_ref, va_ref, vb_ref, bias_ref, g_ref, o_ref):
    steps = q_ref.shape[0]
    nq = N_HEADS * steps
    a = ATTN_WIDTH
    row = lax.broadcasted_iota(jnp.int32, (nq, a), 0)
    col = lax.broadcasted_iota(jnp.int32, (nq, a), 1)
    own = (row // steps) == (col // HEAD_DIM)
    qf = q_ref[...].astype(F32)
    qbd = jnp.where(own, jnp.concatenate([qf] * N_HEADS, axis=0), 0.0).astype(BF16)
    pad = jnp.zeros((NEW_KEY_PAD - steps, a), F32)
    k_new = jnp.concatenate([kn_ref[...].astype(F32), pad], axis=0).astype(BF16)
    v_new = jnp.concatenate([vn_ref[...].astype(F32), pad], axis=0).astype(BF16)
    k_all = jnp.concatenate([ka_ref[...], kb_ref[...], k_new], axis=0)
    v_all = jnp.concatenate([va_ref[...], vb_ref[...], v_new], axis=0)
    s = lax.dot_general(qbd, k_all, (((1,), (1,)), ((), ())), preferred_element_type=F32)
    ms, ls, es = [], [], []
    for b in range(len(DILATED_BRANCHES)):
        sb = s + bias_ref[b]
        m = jnp.max(sb, axis=1, keepdims=True)
        e = jnp.exp(sb - m)
        ms.append(m)
        ls.append(jnp.sum(e, axis=1, keepdims=True))
        es.append(e)
    o_all = jnp.dot(jnp.concatenate(es, axis=0).astype(BF16), v_all, preferred_element_type=F32)
    top = jnp.maximum(jnp.maximum(ms[0], ms[1]), ms[2])
    num = jnp.zeros((nq, a), F32)
    den = jnp.zeros((nq, 1), F32)
    for b in range(len(DILATED_BRANCHES)):
        w = jnp.exp(ms[b] - top)
        num = num + o_all[b * nq:(b + 1) * nq] * w
        den = den + ls[b] * w
    full = jnp.where(own, num * (1.0 / den), 0.0)
    o = full[0:steps]
    for h in range(1, N_HEADS):
        o = o + full[h * steps:(h + 1) * steps]
    o_ref[...] = _rms(o, g_ref[...])


def _attn_sample(q, kn, vn, cache_k, cache_v, g):
    b, steps, a = q.shape
    win = cache_k.shape[1]
    grp, a0, pos = _sample_key_plan(win, steps)
    bias = jnp.asarray(_sample_bias(win, steps, pos))

    def kept(c):
        c5 = c.reshape(b, win // grp, grp, N_HEADS, HEAD_DIM)
        part_a = c5[:, :, :steps].astype(BF16).reshape(b, -1, a)
        part_b = c5[:, a0:, steps:].astype(BF16).reshape(b, -1, a)
        return part_a, part_b

    ka, kb = kept(cache_k)
    va, vb = kept(cache_v)
    new = pl.BlockSpec((None, steps, a), lambda i: (i, 0, 0))
    rows = lambda z: pl.BlockSpec((None, z.shape[1], a), lambda i: (i, 0, 0))
    return pl.pallas_call(
        _attn_sample_kernel,
        out_shape=jax.ShapeDtypeStruct((b, steps, a), F32),
        grid=(b,),
        in_specs=[new, new, new, rows(ka), rows(kb), rows(va), rows(vb), _full(bias.shape), _full((1, a))],
        out_specs=new,
        compiler_params=_cparams(),
        name="attn_sample",
    )(q, kn, vn, ka, kb, va, vb, bias, g.reshape(1, a))


def _gelu_tanh(x):
    return 0.5 * x * (1.0 + jnp.tanh(np.sqrt(2.0 / np.pi).astype(np.float32) * (x + 0.044715 * (x * x * x))))


def _ssm_kernel(u_ref, bb_ref, lam_ref, pow_ref, cm_ref, dsk_ref, wglu_ref, bglu_ref, g_ref, pt_ref, h0_ref,
                z_ref, ht_ref, h_s, hin_s, carry_s, *, rows, steps, chain):
    ns = N_STATE
    cw = (4 * V7X_SUBLANES * V7X_LANES) // rows
    last = (steps - 1) * rows

    @pl.when(pl.program_id(0) == 0)
    def _():
        carry_s[...] = h0_ref[0:1, :]

    ub = u_ref[...].astype(BF16)
    n_slabs = SSM_WIDTH // V7X_LANES
    sw = ns // n_slabs
    for s in range(n_slabs):
        part = jnp.dot(ub[:, s * V7X_LANES:(s + 1) * V7X_LANES], bb_ref[s], preferred_element_type=F32)
        h_s[:, s * sw:(s + 1) * sw] = part[:, :sw]
        h_s[:, ns + s * sw:ns + (s + 1) * sw] = part[:, sw:]

    for cc in range(0, ns // cw, 2):
        crs = [slice(c * cw, (c + 1) * cw) for c in (cc, cc + 1)]
        cis = [slice(ns + c * cw, ns + (c + 1) * cw) for c in (cc, cc + 1)]
        lrs = [jnp.broadcast_to(lam_ref[0:1, cr], (rows, cw)) for cr in crs]
        lis = [jnp.broadcast_to(lam_ref[1:2, cr], (rows, cw)) for cr in crs]

        def scan_body(t, carry, crs=crs, cis=cis, lrs=lrs, lis=lis):
            rs = pl.ds(pl.multiple_of(t * rows, rows), rows)
            out = []
            for j in range(2):
                hr, hi = carry[2 * j], carry[2 * j + 1]
                nr = lrs[j] * hr - lis[j] * hi + h_s[rs, crs[j]]
                ni = lrs[j] * hi + lis[j] * hr + h_s[rs, cis[j]]
                h_s[rs, crs[j]] = nr
                h_s[rs, cis[j]] = ni
                out += [nr, ni]
            return tuple(out)

        zero = jnp.zeros((rows, cw), F32)
        lax.fori_loop(0, steps, scan_body, (zero,) * 4)

    if chain:
        ptr = pow_ref[steps - 1, 0:1, :ns]
        pti = pow_ref[steps - 1, 0:1, ns:]
        carry = carry_s[...]
        for c in range(rows):
            hin_s[c:c + 1, :] = carry
            cr_, ci_ = carry[:, :ns], carry[:, ns:]
            e = h_s[last + c:last + c + 1, :]
            carry = jnp.concatenate([ptr * cr_ - pti * ci_ + e[:, :ns],
                                     ptr * ci_ + pti * cr_ + e[:, ns:]], axis=1)
        carry_s[...] = carry
    else:
        hin_s[...] = h0_ref[...]

    for cc in range(ns // cw):
        cr = slice(cc * cw, (cc + 1) * cw)
        ci = slice(ns + cc * cw, ns + (cc + 1) * cw)
        hr0 = hin_s[:, cr]
        hi0 = hin_s[:, ci]

        def fix_body(t, _, cr=cr, ci=ci, hr0=hr0, hi0=hi0):
            rs = pl.ds(pl.multiple_of(t * rows, rows), rows)
            pr = pow_ref[t, :, cr]
            pi_ = pow_ref[t, :, ci]
            h_s[rs, cr] = h_s[rs, cr] + (pr * hr0 - pi_ * hi0)
            h_s[rs, ci] = h_s[rs, ci] + (pr * hi0 + pi_ * hr0)
            return 0

        lax.fori_loop(0, steps, fix_body, 0, unroll=4)

    ht_ref[...] = h_s[last:last + rows, :]
    ys = []
    for s in range(n_slabs):
        hs = jnp.concatenate([h_s[:, s * sw:(s + 1) * sw], h_s[:, ns + s * sw:ns + (s + 1) * sw]], axis=1)
        ys.append(jnp.dot(hs.astype(BF16), cm_ref[s], preferred_element_type=F32))
    y = jnp.concatenate(ys, axis=1) + dsk_ref[...] * u_ref[...]
    y = _gelu_tanh(y)
    gl = jnp.dot(y.astype(BF16), wglu_ref[...], preferred_element_type=F32) + bglu_ref[...]
    z = _rms(y * _sigmoid(gl), g_ref[...]).astype(BF16)
    z_ref[...] = jnp.dot(pt_ref[...], z, preferred_element_type=F32).astype(z_ref.dtype)


def _ssm_tables(ssm_a_re, ssm_a_im, ssm_log_dt, ssm_b_re, ssm_b_im, ssm_c_re, ssm_c_im, max_steps):
    g, n, ch = SSM_GROUPS, SSM_STATE, SSM_CH
    a_re = ssm_a_re.astype(F32)
    a_im = ssm_a_im.astype(F32)
    dt = jnp.exp(ssm_log_dt.astype(F32))[:, None]
    mag = jnp.exp(dt * a_re)
    lam_re = mag * jnp.cos(dt * a_im)
    lam_im = mag * jnp.sin(dt * a_im)
    nr = lam_re - 1.0
    ni = lam_im
    inv = 1.0 / (a_re * a_re + a_im * a_im)
    coef_re = (nr * a_re + ni * a_im) * inv
    coef_im = (ni * a_re - nr * a_im) * inv
    br = ssm_b_re.astype(F32)
    bi = ssm_b_im.astype(F32)
    bb_re = coef_re[..., None] * br - coef_im[..., None] * bi
    bb_im = coef_re[..., None] * bi + coef_im[..., None] * br
    gs = V7X_LANES // ch
    ns_ = g // gs
    eye = jnp.eye(gs, dtype=F32)
    bmat = lambda b: jnp.einsum("sgnc,gh->sgchn", b.reshape(ns_, gs, n, ch), eye).reshape(ns_, gs * ch, gs * n)
    cmat = lambda c: jnp.einsum("sgcn,gh->sgnhc", c.astype(F32).reshape(ns_, gs, ch, n), eye).reshape(
        ns_, gs * n, gs * ch)
    bb = jnp.concatenate([bmat(bb_re), bmat(bb_im)], axis=2).astype(BF16)
    cm = jnp.concatenate([cmat(ssm_c_re), -cmat(ssm_c_im)], axis=1).astype(BF16)
    lam = jnp.stack([lam_re.reshape(-1), lam_im.reshape(-1)])

    k = jnp.arange(1, max_steps + 1, dtype=F32)[:, None]
    kdt = k * dt.reshape(1, -1).repeat(n, axis=1)
    pmag = jnp.exp(kdt * a_re.reshape(1, -1))
    parg = kdt * a_im.reshape(1, -1)
    pows = jnp.concatenate([pmag * jnp.cos(parg), pmag * jnp.sin(parg)], axis=1)
    return bb, cm, lam, pows


def _ssm(u_perm, h0, tabs, dsk, w_glu, b_glu, g, perm_t, rows, steps, chain):
    n, w = u_perm.shape
    blk = rows * steps
    bb, cm, lam, pows = tabs
    row = pl.BlockSpec((blk, w), lambda i: (i, 0))
    kern = functools.partial(_ssm_kernel, rows=rows, steps=steps, chain=chain)
    return pl.pallas_call(
        kern,
        out_shape=(jax.ShapeDtypeStruct((n, w), BF16), jax.ShapeDtypeStruct((rows, 2 * N_STATE), F32)),
        grid=(n // blk,),
        in_specs=[row, _full(bb.shape), _full(lam.shape), _full((steps, rows, 2 * N_STATE)), _full(cm.shape),
                  _full((1, w)), _full((w, w)), _full((1, w)), _full((1, w)), _full((blk, blk)),
                  _full((rows, 2 * N_STATE))],
        out_specs=(row, _full((rows, 2 * N_STATE))),
        scratch_shapes=[pltpu.VMEM((blk, 2 * N_STATE), F32),
                        pltpu.VMEM((rows, 2 * N_STATE), F32),
                        pltpu.VMEM((1, 2 * N_STATE), F32)],
        compiler_params=_cparams(),
        name=f"ssm_r{rows}",
    )(u_perm, bb, lam, jnp.broadcast_to(pows[:steps, None, :], (steps, rows, 2 * N_STATE)), cm,
      dsk.reshape(1, w), w_glu.astype(BF16), b_glu.reshape(1, w),
      g.reshape(1, w), perm_t, h0)


def _outproj_kernel(*refs, n_real, aliased):
    ins, outs = refs[:12], refs[12 + aliased:]
    step = pl.program_id(0)

    @pl.when(step < n_real)
    def _():
        _outproj_tile(*ins, *outs)

    @pl.when(step >= n_real)
    def _():
        xs_ref = outs[1]
        xs_ref[...] = jnp.zeros(xs_ref.shape, xs_ref.dtype)


def _outproj_tile(x_ref, a_ref, z_ref, gt_ref, sh_ref, sc_ref, g_ref, wo_ref, wr_ref, br_ref, ltri_ref, utri_ref,
                  x1_ref, xs_ref, slot_ref, gates_ref, meta_ref, wbf_ref):
    @pl.when(pl.program_id(0) == 0)
    def _():
        wbf_ref[...] = wo_ref[...].astype(BF16)

    a = ATTN_WIDTH
    mixed = (jnp.dot(a_ref[...].astype(BF16), wbf_ref[:a, :], preferred_element_type=F32)
             + jnp.dot(z_ref[...], wbf_ref[a:, :], preferred_element_type=F32))
    x1 = x_ref[...] + gt_ref[...] * mixed
    x1_ref[...] = x1
    h2 = (_rms(x1, g_ref[...]) * (1.0 + sc_ref[...]) + sh_ref[...]).astype(BF16)
    lg = jnp.dot(h2, wr_ref[...], preferred_element_type=F32) + br_ref[...]
    tm = lg.shape[0]
    lane = lax.broadcasted_iota(jnp.int32, (tm, V7X_LANES), 1).astype(F32)
    vals, hots = [], []
    for _ in range(TOP_K):
        m = jnp.max(lg, axis=1, keepdims=True)
        idx = jnp.min(jnp.where(lg == m, lane, float(V7X_LANES)), axis=1, keepdims=True)
        hot = lane == idx
        vals.append(m)
        hots.append(jnp.where(hot, 1.0, 0.0))
        lg = jnp.where(hot, MASKED * 2, lg)
    es = [jnp.exp(v - vals[0]) for v in vals]
    inv = 1.0 / (es[0] + es[1] + es[2] + es[3])

    member = hots[0] + hots[1] + hots[2] + hots[3]
    before = jnp.dot(ltri_ref[...], member.astype(BF16), preferred_element_type=F32)
    count = jnp.sum(member, axis=0, keepdims=True)
    padded = jnp.floor((count + (ROW_ALIGN - 1.0)) * (1.0 / ROW_ALIGN)) * ROW_ALIGN
    padded8 = jnp.broadcast_to(padded, (V7X_SUBLANES, V7X_LANES))
    start = jnp.dot(padded8.astype(BF16), utri_ref[...], preferred_element_type=F32)[0:1]
    where_to = start + before
    slots = jnp.zeros((tm, V7X_LANES), F32)
    gates = jnp.zeros((tm, V7X_LANES), F32)
    for k in range(TOP_K):
        slot_k = jnp.sum(hots[k] * where_to, axis=1, keepdims=True)
        slots = jnp.where(lane == float(k), slot_k, slots)
        gates = jnp.where(lane == float(k), es[k] * inv, gates)
    slot_ref[...] = slots
    gates_ref[...] = gates
    row = lax.broadcasted_iota(jnp.int32, (V7X_SUBLANES, V7X_LANES), 0)
    meta_ref[...] = jnp.where(row == 0, padded8, jnp.where(row == 1, jnp.broadcast_to(start, padded8.shape), 0.0))

    cap = xs_ref.shape[0]
    slots_t = jnp.transpose(slots)
    srow = lax.broadcasted_iota(jnp.int32, (cap, tm), 0).astype(F32)
    place = jnp.zeros((cap, tm), F32)
    for k in range(TOP_K):
        place = jnp.where(srow == slots_t[k:k + 1, :], 1.0, place)
    xs_ref[...] = jnp.dot(place.astype(BF16), h2, preferred_element_type=F32).astype(BF16)


def _outproj(x, a, z, gt, sh, sc, g, w_out, wr_pad, br_pad, tm, n_blocks, block0, xs_prev=None):
    n, d = x.shape
    aw = ATTN_WIDTH
    n_real = n // tm
    aliased = xs_prev is not None
    n_steps = n_real if aliased else n_blocks
    tile = lambda i: jnp.minimum(i, n_real - 1)
    mod_spec = (pl.BlockSpec((1, d), lambda i: (0, 0)) if gt.shape[0] == 1
                else pl.BlockSpec((tm, d), lambda i: (tile(i), 0)))
    row = lambda w: pl.BlockSpec((tm, w), lambda i: (tile(i), 0))
    ltri = jnp.asarray(np.tril(np.ones((tm, tm), np.float32), -1), BF16)
    utri = jnp.asarray(np.triu(np.ones((V7X_LANES, V7X_LANES), np.float32), 1), BF16)
    in_specs = [row(d), row(aw), row(SSM_WIDTH), mod_spec, mod_spec, mod_spec, _full((1, d)),
                _full(w_out.shape), _full(wr_pad.shape), _full(br_pad.shape), _full(ltri.shape), _full(utri.shape)]
    args = [x, a, z, gt, sh, sc, g.reshape(1, d), w_out, wr_pad, br_pad, ltri, utri]
    if aliased:
        in_specs.append(pl.BlockSpec(memory_space=pl.ANY))
        args.append(xs_prev)
    return pl.pallas_call(
        functools.partial(_outproj_kernel, n_real=n_real, aliased=int(aliased)),
        out_shape=(jax.ShapeDtypeStruct((n, d), F32),
                   jax.ShapeDtypeStruct((n_blocks, GROUP_CAP, d), BF16),
                   jax.ShapeDtypeStruct((n, V7X_LANES), F32), jax.ShapeDtypeStruct((n, V7X_LANES), F32),
                   jax.ShapeDtypeStruct((n_real, V7X_SUBLANES, V7X_LANES), F32)),
        grid=(n_steps,),
        in_specs=in_specs,
        out_specs=(row(d), pl.BlockSpec((None, GROUP_CAP, d), lambda i: (block0 + i, 0, 0)),
                   row(V7X_LANES), row(V7X_LANES),
                   pl.BlockSpec((None, V7X_SUBLANES, V7X_LANES), lambda i: (tile(i), 0, 0))),
        scratch_shapes=[pltpu.VMEM(w_out.shape, BF16)],
        input_output_aliases={len(args) - 1: 1} if aliased else {},
        compiler_params=_cparams(),
        name="outproj_router",
    )(*args)


_PIECE_SIZES = tuple(MOE_TILE >> s for s in range(6))
_CHUNK_SHIFT = 6
_CHUNK = 1 << _CHUNK_SHIFT
_TAIL_SIZES = tuple(sz for sz in _PIECE_SIZES if sz < _CHUNK)


def _moe_kernel(te_ref, tl_ref, lo_ref, hi_ref, nu_ref, gstart_ref, gsize_ref, gbase_ref, rows_ref, used_ref,
                xs_hbm, wgu_ref, bgu_ref, wd_ref, bd_ref, ys_hbm,
                xbuf, ybuf, zbuf, wgu_bf, wd_bf, xsem, ysem, zsem):
    t = pl.program_id(0)
    n_used = nu_ref[0]
    n_blocks, cap = xs_hbm.shape[0], xs_hbm.shape[1]

    def x_copy(i, br, tr, sz, slot):
        pltpu.make_async_copy(xs_hbm.at[i, pl.ds(br, sz)], xbuf.at[slot, pl.ds(tr, sz)], xsem.at[slot]).start()

    def y_copy(i, br, tr, sz, slot):
        pltpu.make_async_copy(ybuf.at[slot, pl.ds(tr, sz)], ys_hbm.at[i, pl.ds(br, sz)], ysem.at[slot]).start()

    def z_copy(i, row, sz, start):
        cp = pltpu.make_async_copy(zbuf.at[pl.ds(0, sz)], ys_hbm.at[i, pl.ds(row, sz)], zsem)
        cp.start() if start else cp.wait()

    def pieces(tt, fn):
        e = te_ref[tt]
        lo = tl_ref[tt] * MOE_TILE

        def per_block(i, c):
            g = e * n_blocks + i
            s0 = gbase_ref[g]
            a = jnp.maximum(s0, lo)
            b = jnp.minimum(s0 + gsize_ref[g], lo + MOE_TILE)
            length = jnp.maximum(b - a, 0)
            src = gstart_ref[g] + (a - s0)
            dst = a - lo
            whole = lax.shift_right_logical(length, _CHUNK_SHIFT)

            def chunk(j, cc):
                off = j * _CHUNK
                fn(i, pl.multiple_of(src + off, ROW_ALIGN), pl.multiple_of(dst + off, ROW_ALIGN), _CHUNK)
                return cc

            lax.fori_loop(0, whole, chunk, 0)
            done = whole * _CHUNK
            for sz in _TAIL_SIZES:
                hit = (length & sz) != 0

                @pl.when(hit)
                def _(sz=sz, done=done):
                    fn(i, pl.multiple_of(src + done, ROW_ALIGN), pl.multiple_of(dst + done, ROW_ALIGN), sz)

                done = done + jnp.where(hit, sz, 0)
            return c

        lax.fori_loop(lo_ref[tt], hi_ref[tt], per_block, 0)

    def tile_rows(tt):
        return jnp.minimum(rows_ref[te_ref[tt]] - tl_ref[tt] * MOE_TILE, MOE_TILE)

    def wait_rows(n, sem, buf):
        def chunk(j, cc):
            pltpu.make_async_copy(buf.at[pl.ds(0, _CHUNK)], buf.at[pl.ds(0, _CHUNK)], sem).wait()
            return cc

        lax.fori_loop(0, lax.shift_right_logical(n, _CHUNK_SHIFT), chunk, 0)
        for sz in _TAIL_SIZES:
            @pl.when((n & sz) != 0)
            def _(sz=sz):
                pltpu.make_async_copy(buf.at[pl.ds(0, sz)], buf.at[pl.ds(0, sz)], sem).wait()

    def fetch(tt, slot):
        pieces(tt, lambda i, br, tr, sz: x_copy(i, br, tr, sz, slot))

    def writeback(tt, slot):
        pieces(tt, lambda i, br, tr, sz: y_copy(i, br, tr, sz, slot))

    def zero_tail(i, start):
        u = used_ref[i]
        rem = cap - u
        nz = zbuf.shape[0]
        whole = lax.shift_right_logical(rem, nz.bit_length() - 1)

        def chunk(j, c):
            z_copy(i, pl.multiple_of(u + j * nz, ROW_ALIGN), nz, start)
            return c

        lax.fori_loop(0, whole, chunk, 0)
        base = u + whole * nz
        done = jnp.int32(0)
        for sz in _PIECE_SIZES:
            if sz >= nz:
                continue
            hit = (rem & sz) != 0

            @pl.when(hit)
            def _(sz=sz, done=done):
                z_copy(i, pl.multiple_of(base + done, ROW_ALIGN), sz, start)

            done = done + jnp.where(hit, sz, 0)

    @pl.when(t == 0)
    def _():
        xbuf[...] = jnp.zeros(xbuf.shape, xbuf.dtype)
        zbuf[...] = jnp.zeros(zbuf.shape, zbuf.dtype)
        lax.fori_loop(0, n_blocks, lambda i, c: (zero_tail(i, True), c)[1], 0)
        lax.fori_loop(0, n_blocks, lambda i, c: (zero_tail(i, False), c)[1], 0)
        fetch(0, 0)

    @pl.when(t < n_used)
    def _():
        slot = t % 2

        @pl.when(t + 1 < n_used)
        def _():
            fetch(t + 1, 1 - slot)

        prev = te_ref[jnp.maximum(t - 1, 0)]
        fresh = jnp.logical_or(t == 0, te_ref[t] != prev)

        @pl.when(fresh)
        def _():
            wgu_bf[...] = wgu_ref[...].astype(BF16)
            wd_bf[...] = wd_ref[...].astype(BF16)

        wait_rows(tile_rows(t), xsem.at[slot], xbuf.at[slot])

        @pl.when(t >= 2)
        def _():
            wait_rows(tile_rows(t - 2), ysem.at[slot], ybuf.at[slot])

        f = D_EXPERT
        x = xbuf[slot]
        y = None
        for j in range(f // MOE_HIDDEN_BLOCK):
            cg = slice(j * MOE_HIDDEN_BLOCK, (j + 1) * MOE_HIDDEN_BLOCK)
            cu = slice(f + j * MOE_HIDDEN_BLOCK, f + (j + 1) * MOE_HIDDEN_BLOCK)
            gate = jnp.dot(x, wgu_bf[:, cg], preferred_element_type=F32) + bgu_ref[:, cg]
            up = jnp.dot(x, wgu_bf[:, cu], preferred_element_type=F32) + bgu_ref[:, cu]
            gate = jnp.minimum(gate, SWIGLU_LIMIT)
            up = jnp.clip(up, -SWIGLU_LIMIT, SWIGLU_LIMIT)
            act = (up + 1.0) * gate * _sigmoid(SWIGLU_ALPHA * gate)
            part = jnp.dot(act.astype(BF16), wd_bf[cg, :], preferred_element_type=F32)
            y = part if y is None else y + part
        ybuf[slot] = (y + bd_ref[...]).astype(BF16)
        writeback(t, slot)

        @pl.when(t == n_used - 1)
        def _():
            wait_rows(tile_rows(t), ysem.at[slot], ybuf.at[slot])

            @pl.when(t >= 1)
            def _():
                wait_rows(tile_rows(t - 1), ysem.at[1 - slot], ybuf.at[1 - slot])


def _moe(plan, xs, w_gate_up, b_gate_up, w_down, b_down):
    _, cap, d = xs.shape
    e, _, f2 = w_gate_up.shape
    nt = plan[0].shape[0]
    wmap = lambda t, te, *_: (te[t], 0, 0)
    anyspec = pl.BlockSpec(memory_space=pl.ANY)
    grid_spec = pltpu.PrefetchScalarGridSpec(
        num_scalar_prefetch=len(plan),
        grid=(nt,),
        in_specs=[anyspec,
                  pl.BlockSpec((None, d, f2), wmap),
                  pl.BlockSpec((None, 1, f2), wmap),
                  pl.BlockSpec((None, f2 // 2, d), wmap),
                  pl.BlockSpec((None, 1, d), wmap)],
        out_specs=anyspec,
        scratch_shapes=[pltpu.VMEM((2, MOE_TILE, d), BF16), pltpu.VMEM((2, MOE_TILE, d), BF16),
                        pltpu.VMEM((MOE_TILE // 2, d), BF16),
                        pltpu.VMEM((d, f2), BF16), pltpu.VMEM((f2 // 2, d), BF16),
                        pltpu.SemaphoreType.DMA((2,)), pltpu.SemaphoreType.DMA((2,)), pltpu.SemaphoreType.DMA(())],
    )
    return pl.pallas_call(
        _moe_kernel,
        out_shape=jax.ShapeDtypeStruct(xs.shape, BF16),
        grid_spec=grid_spec,
        compiler_params=_cparams(),
        name="moe_experts",
    )(*plan, xs, w_gate_up, b_gate_up.reshape(e, 1, f2), w_down, b_down.reshape(e, 1, d))


def _moe_plan(group_size, group_start):
    n_blocks = group_size.shape[0]
    gsize = group_size.T.astype(jnp.int32)
    gstart = group_start.T.astype(jnp.int32)
    gbase = jnp.cumsum(gsize, axis=1) - gsize
    rows = jnp.sum(gsize, axis=1)
    tiles = (rows + MOE_TILE - 1) // MOE_TILE
    tile_end = jnp.cumsum(tiles)
    n_used = tile_end[-1:]
    nt = (n_blocks * GROUP_CAP) // MOE_TILE + N_EXPERTS
    t = jnp.arange(nt, dtype=jnp.int32)
    te = jnp.sum((tile_end[None, :] <= t[:, None]).astype(jnp.int32), axis=1)
    last = jnp.max(jnp.where(tiles > 0, jnp.arange(N_EXPERTS, dtype=jnp.int32), 0))
    te = jnp.where(t < n_used[0], jnp.minimum(te, N_EXPERTS - 1), last)
    hot = (te[:, None] == jnp.arange(N_EXPERTS, dtype=jnp.int32)[None, :]).astype(jnp.int32)
    tl = jnp.where(t < n_used[0], t - hot @ (tile_end - tiles), 0)
    lo_row = tl * MOE_TILE
    base_t = hot @ gbase
    size_t = hot @ gsize
    first = jnp.sum((base_t + size_t <= lo_row[:, None]).astype(jnp.int32), axis=1)
    stop = jnp.sum((base_t < lo_row[:, None] + MOE_TILE).astype(jnp.int32), axis=1)
    used = jnp.sum(gsize, axis=0)
    i32 = lambda z: z.astype(jnp.int32)
    return (i32(te), i32(tl), i32(first), i32(stop), i32(n_used), i32(gstart.reshape(-1)), i32(gsize.reshape(-1)),
            i32(gbase.reshape(-1)), i32(rows), i32(used))


def _final_kernel(x_ref, ys_ref, slot_ref, gates_ref, gt_ref, g_ref, y_ref):
    tm = x_ref.shape[0]
    cap = ys_ref.shape[0]
    col = lax.broadcasted_iota(jnp.int32, (tm, cap), 1).astype(F32)
    slots = slot_ref[...]
    gates = gates_ref[...]
    mix = jnp.zeros((tm, cap), F32)
    for k in range(TOP_K):
        mix = jnp.where(col == slots[:, k:k + 1], gates[:, k:k + 1], mix)
    hi = mix.astype(BF16)
    lo = (mix - hi.astype(F32)).astype(BF16)
    ys = ys_ref[...]
    ff = jnp.dot(hi, ys, preferred_element_type=F32) + jnp.dot(lo, ys, preferred_element_type=F32)
    y_ref[...] = _rms(x_ref[...] + gt_ref[...] * ff, g_ref[...])


def _final(x1, ys, slots, gates, gt, g, tm, block0):
    n, d = x1.shape
    cap = ys.shape[1]
    mod_spec = (pl.BlockSpec((1, d), lambda i: (0, 0)) if gt.shape[0] == 1
                else pl.BlockSpec((tm, d), lambda i: (i, 0)))
    row = lambda w: pl.BlockSpec((tm, w), lambda i: (i, 0))
    return pl.pallas_call(
        _final_kernel,
        out_shape=jax.ShapeDtypeStruct((n, d), F32),
        grid=(n // tm,),
        in_specs=[row(d), pl.BlockSpec((None, cap, d), lambda i: (block0 + i, 0, 0)),
                  row(V7X_LANES), row(V7X_LANES), mod_spec, _full((1, d))],
        out_specs=row(d),
        compiler_params=_cparams(),
        name="final_norm",
    )(x1, ys, slots, gates, gt, g.reshape(1, d))


def kernel(x_prompt, x_sample, cache_k_win, cache_v_win, state_ssm_re, state_ssm_im, c_prompt, c_sample,
           w_ada, b_ada, g_norm1, w_in, ssm_a_re, ssm_a_im, ssm_log_dt, ssm_b_re, ssm_b_im, ssm_c_re, ssm_c_im,
           ssm_d, w_glu, b_glu, g_out_attn, g_out_ssm, w_out, g_norm2, w_router, b_router, w_gate_up, b_gate_up,
           w_down, b_down, g_final):
    depth = w_ada.shape[0]
    assert depth == 1 and x_prompt.shape[0] == 1
    bp, t, d = x_prompt.shape
    bs, ts, _ = x_sample.shape
    ns = bs * ts
    l = 0
    a = ATTN_WIDTH

    n_c = bp + bs
    c_pad = -(-n_c // V7X_SUBLANES) * V7X_SUBLANES
    c_rows = jnp.concatenate([c_prompt, c_sample, jnp.zeros((c_pad - n_c, d), F32)], axis=0)
    mod = _ada_modulation(c_rows, w_ada[l], b_ada[l])
    mod_p = [mod[0:1, i * d:(i + 1) * d] for i in range(N_MOD)]
    mod_s = [jnp.repeat(mod[bp:bp + bs, i * d:(i + 1) * d], ts, axis=0) for i in range(N_MOD)]

    steps_p = TOKEN_TILE // SSM_ROWS_PROMPT
    perm_p = _chunk_perm(SSM_ROWS_PROMPT, steps_p)
    perm_s = _chunk_perm(bs, ts)
    tabs = _ssm_tables(ssm_a_re[l], ssm_a_im[l], ssm_log_dt[l], ssm_b_re[l], ssm_b_im[l],
                       ssm_c_re[l], ssm_c_im[l], max(steps_p, ts))
    wr_pad = jnp.zeros((d, V7X_LANES), F32).at[:, :N_EXPERTS].set(w_router[l]).astype(BF16)
    br_pad = jnp.full((1, V7X_LANES), MASKED, F32).at[0, :N_EXPERTS].set(b_router[l])

    xp = x_prompt.reshape(t, d)
    dils = tuple(dil for _, dil in DILATED_BRANCHES)
    wide = tuple(dil for dil in dils if dil > 1)
    proj_p = _inproj(xp, mod_p[0], mod_p[1], g_norm1[l], w_in[l], jnp.asarray(perm_p, BF16), TOKEN_TILE, wide)
    kpf, vpf, up = proj_p[3:6]
    views = {1: proj_p[0:3]}
    for n_d, dil in enumerate(wide):
        views[dil] = proj_p[6 + 3 * n_d:9 + 3 * n_d]
    outs = [_attn_branch(*views[dil], dil) for dil in dils]
    ap = _attn_combine([o for o, _ in outs], [s for _, s in outs], g_out_attn[l], dils)
    zeros_h = jnp.zeros((SSM_ROWS_PROMPT, 2 * N_STATE), F32)
    zp, hp = _ssm(up, zeros_h, tabs, ssm_d[l], w_glu[l], b_glu[l], g_out_ssm[l],
                  jnp.asarray(perm_p.T, BF16), SSM_ROWS_PROMPT, steps_p, True)
    n_blocks_p = t // TOKEN_TILE
    n_blocks = n_blocks_p + 1
    x1p, xs_all, slot_p, gate_p, meta_p = _outproj(xp, ap, zp, mod_p[2], mod_p[3], mod_p[4], g_norm2[l], w_out[l],
                                                   wr_pad, br_pad, TOKEN_TILE, n_blocks, 0)

    xs = x_sample.reshape(ns, d)
    qs, ks, vs, ksf, vsf, us = _inproj(xs, mod_s[0], mod_s[1], g_norm1[l], w_in[l], jnp.asarray(perm_s, BF16), ns)
    as_ = _attn_sample(qs.reshape(bs, ts, a), ks.reshape(bs, ts, a), vs.reshape(bs, ts, a),
                       cache_k_win[l], cache_v_win[l], g_out_attn[l])
    h0s = jnp.concatenate([state_ssm_re[l].reshape(bs, N_STATE), state_ssm_im[l].reshape(bs, N_STATE)], axis=1)
    zs, hs = _ssm(us, h0s, tabs, ssm_d[l], w_glu[l], b_glu[l], g_out_ssm[l],
                  jnp.asarray(perm_s.T, BF16), bs, ts, False)
    x1s, xs_all, slot_s, gate_s, meta_s = _outproj(xs, as_.reshape(ns, a), zs, mod_s[2], mod_s[3], mod_s[4],
                                                   g_norm2[l], w_out[l], wr_pad, br_pad, ns, n_blocks, n_blocks_p,
                                                   xs_prev=xs_all)

    meta = jnp.concatenate([meta_p, meta_s], axis=0)
    plan = _moe_plan(meta[:, 0, :N_EXPERTS], meta[:, 1, :N_EXPERTS])
    ys_all = _moe(plan, xs_all, w_gate_up[l], b_gate_up[l], w_down[l], b_down[l])

    y_prompt = _final(x1p, ys_all, slot_p, gate_p, mod_p[5], g_final, TOKEN_TILE, 0).reshape(bp, t, d)
    y_sample = _final(x1s, ys_all, slot_s, gate_s, mod_s[5], g_final, ns, n_blocks_p).reshape(bs, ts, d)

    keep = min(MAX_WINDOW, t)
    shp = (1, bp, keep, N_HEADS, HEAD_DIM)
    k_win = kpf[t - keep:].reshape(shp)
    v_win = vpf[t - keep:].reshape(shp)
    st = (1, bp, SSM_GROUPS, SSM_STATE)
    hp_last = hp[SSM_ROWS_PROMPT - 1]
    ss = (1, bs, SSM_GROUPS, SSM_STATE)
    return (y_prompt, y_sample, k_win, v_win,
            hp_last[:N_STATE].reshape(st), hp_last[N_STATE:].reshape(st),
            ksf.reshape(1, bs, ts, N_HEADS, HEAD_DIM), vsf.reshape(1, bs, ts, N_HEADS, HEAD_DIM),
            hs[:, :N_STATE].reshape(ss), hs[:, N_STATE:].reshape(ss))
```

```python
import functools

import numpy as np
import jax
import jax.numpy as jnp
from jax import lax
from jax.experimental import pallas as pl
from jax.experimental.pallas import tpu as pltpu

F32 = jnp.float32
BF16 = jnp.bfloat16

D_MODEL = 1024
N_HEADS = 8
HEAD_DIM = 64
ATTN_WIDTH = N_HEADS * HEAD_DIM
DILATED_BRANCHES = ((128, 1), (512, 4), (2048, 16))
KEYS_PER_BRANCH = 129
MAX_WINDOW = 2048
SSM_WIDTH = D_MODEL - ATTN_WIDTH
SSM_CH = 16
SSM_GROUPS = SSM_WIDTH // SSM_CH
SSM_STATE = 64
N_STATE = SSM_GROUPS * SSM_STATE
N_EXPERTS = 32
TOP_K = 4
D_EXPERT = D_MODEL
SWIGLU_LIMIT = 7.0
SWIGLU_ALPHA = 1.702
N_MOD = 6
EPS = 1e-6
MASKED = -1e30

V7X_LANES = 128
V7X_SUBLANES = 8
V7X_VMEM_LIMIT_BYTES = 56 * 1024 * 1024

TOKEN_TILE = 512
Q_TILE = 128
Q_BLOCKS_PER_STEP = 4
SSM_ROWS_PROMPT = 8
MOE_TILE = 512
ROW_ALIGN = 16
GROUP_CAP = -(-(TOKEN_TILE * TOP_K + N_EXPERTS * (ROW_ALIGN - 1)) // (2 * V7X_LANES)) * (2 * V7X_LANES)


def _cparams(n_axes=1):
    return pltpu.CompilerParams(
        dimension_semantics=("arbitrary",) * n_axes,
        vmem_limit_bytes=V7X_VMEM_LIMIT_BYTES,
    )


def _full(shape):
    n = len(shape)
    return pl.BlockSpec(shape, lambda *_: (0,) * n)


def _rms(x, g):
    return x * lax.rsqrt(jnp.mean(x * x, axis=-1, keepdims=True) + EPS) * g


def _sigmoid(x):
    return 1.0 / (1.0 + jnp.exp(-x))


def _ada_kernel(c_ref, w_ref, b_ref, o_ref):
    c = c_ref[...]
    s = (c * _sigmoid(c)).astype(BF16)
    o_ref[...] = jnp.dot(s, w_ref[...].astype(BF16), preferred_element_type=F32) + b_ref[...]


def _ada_modulation(c_rows, w_ada, b_ada):
    m, d = c_rows.shape
    n = w_ada.shape[1]
    tn = n // 4
    return pl.pallas_call(
        _ada_kernel,
        out_shape=jax.ShapeDtypeStruct((m, n), F32),
        grid=(n // tn,),
        in_specs=[_full((m, d)),
                  pl.BlockSpec((d, tn), lambda j: (0, j)),
                  pl.BlockSpec((1, tn), lambda j: (0, j))],
        out_specs=pl.BlockSpec((m, tn), lambda j: (0, j)),
        compiler_params=_cparams(),
        name="ada_modulation",
    )(c_rows, w_ada, b_ada.reshape(1, n))


def _inproj_kernel(*refs, dils):
    x_ref, sh_ref, sc_ref, g_ref, w_ref, perm_ref = refs[:6]
    dperm_refs = refs[6:6 + len(dils)]
    q_ref, k_ref, v_ref, kf_ref, vf_ref, u_ref = refs[6 + len(dils):12 + len(dils)]
    dil_refs = refs[12 + len(dils):-1]
    wbf_ref = refs[-1]

    @pl.when(pl.program_id(0) == 0)
    def _():
        wbf_ref[...] = w_ref[...].astype(BF16)

    h = _rms(x_ref[...], g_ref[...]) * (1.0 + sc_ref[...]) + sh_ref[...]
    hb = h.astype(BF16)
    a = ATTN_WIDTH
    proj = jnp.dot(hb, wbf_ref[:, :3 * a], preferred_element_type=F32)
    k = proj[:, a:2 * a]
    v = proj[:, 2 * a:]
    qkv = jnp.concatenate([(proj[:, :a] * (HEAD_DIM ** -0.5)).astype(BF16), k.astype(BF16), v.astype(BF16)],
                          axis=1)
    q_ref[...] = qkv[:, :a]
    k_ref[...] = qkv[:, a:2 * a]
    v_ref[...] = qkv[:, 2 * a:]
    kf_ref[...] = k
    vf_ref[...] = v
    hp = jnp.dot(perm_ref[...], hb, preferred_element_type=F32).astype(BF16)
    u_ref[...] = jnp.dot(hp, wbf_ref[:, 3 * a:], preferred_element_type=F32)
    tm = qkv.shape[0]
    for n_d, dil in enumerate(dils):
        by_residue = jnp.dot(dperm_refs[n_d][...], qkv, preferred_element_type=F32).astype(BF16)
        per = tm // dil
        for r in range(dil):
            rows = by_residue[r * per:(r + 1) * per]
            for j in range(3):
                dil_refs[3 * n_d + j][:, r * a:(r + 1) * a] = rows[:, j * a:(j + 1) * a]


def _inproj(x, sh, sc, g, w_in, perm, tm, dils=()):
    n, d = x.shape
    a = ATTN_WIDTH
    mod_rows = sh.shape[0]
    mod_spec = (pl.BlockSpec((1, d), lambda i: (0, 0)) if mod_rows == 1
                else pl.BlockSpec((tm, d), lambda i: (i, 0)))
    row = lambda w: pl.BlockSpec((tm, w), lambda i: (i, 0))
    dperms = [jnp.asarray(_chunk_perm(tm // dil, dil), BF16) for dil in dils]
    view_shapes = tuple(jax.ShapeDtypeStruct((n // dil, dil * a), BF16) for dil in dils for _ in range(3))
    view_specs = tuple(pl.BlockSpec((tm // dil, dil * a), lambda i: (i, 0)) for dil in dils for _ in range(3))
    return pl.pallas_call(
        functools.partial(_inproj_kernel, dils=tuple(dils)),
        out_shape=(jax.ShapeDtypeStruct((n, a), BF16),) * 3
        + (jax.ShapeDtypeStruct((n, a), F32),) * 2
        + (jax.ShapeDtypeStruct((n, SSM_WIDTH), F32),) + view_shapes,
        grid=(n // tm,),
        in_specs=[row(d), mod_spec, mod_spec, _full((1, d)), _full(w_in.shape), _full((tm, tm))]
        + [_full((tm, tm))] * len(dils),
        out_specs=(row(a),) * 5 + (row(SSM_WIDTH),) + view_specs,
        scratch_shapes=[pltpu.VMEM(w_in.shape, BF16)],
        compiler_params=_cparams(),
        name="inproj",
    )(x, sh, sc, g.reshape(1, d), w_in, perm, *dperms)


def _chunk_perm(rows, steps):
    n = rows * steps
    p = np.zeros((n, n), np.float32)
    c, t = np.meshgrid(np.arange(rows), np.arange(steps), indexing="ij")
    p[(t * rows + c).ravel(), (c * steps + t).ravel()] = 1.0
    return p


def _alibi_slopes():
    return np.exp2(-8.0 * np.arange(1, N_HEADS + 1, dtype=np.float64) / N_HEADS).astype(np.float32)


def _branch_bias(dil):
    qi = np.arange(Q_TILE)[:, None]
    col = np.arange(2 * Q_TILE)[None, :]
    j = Q_TILE + qi - col
    valid = (j >= 0) & (j <= Q_TILE)
    dist = (j * dil).astype(np.float32)
    tabs = []
    for first in (True, False):
        ok = valid & (col >= Q_TILE) if first else valid
        per_head = [np.where(ok, -s * dist, np.float32(MASKED)) for s in _alibi_slopes()]
        tabs.append(np.concatenate(per_head, axis=0))
    return np.stack(tabs).astype(np.float32)


def _attn_branch_kernel(q_ref, kp_ref, kc_ref, vp_ref, vc_ref, bias_ref, o_ref, lse_ref):
    first_step = pl.program_id(0) == 0
    lane = lax.broadcasted_iota(jnp.int32, (Q_TILE, V7X_LANES), 1)
    lo = lane < HEAD_DIM
    for j in range(Q_BLOCKS_PER_STEP):
        rows = slice(j * Q_TILE, (j + 1) * Q_TILE)
        before = slice((j - 1) * Q_TILE, j * Q_TILE)
        sel = jnp.where(first_step, 0, 1) if j == 0 else 1
        lse_acc = jnp.zeros((Q_TILE, V7X_LANES), F32)
        for p in range(N_HEADS // 2):
            cs = slice(V7X_LANES * p, V7X_LANES * (p + 1))
            q2 = q_ref[rows, cs]
            zero = jnp.zeros_like(q2)
            qq = jnp.concatenate([jnp.where(lo, q2, zero), jnp.where(lo, zero, q2)], axis=0)
            k_before = kp_ref[:, cs] if j == 0 else kc_ref[before, cs]
            v_before = vp_ref[:, cs] if j == 0 else vc_ref[before, cs]
            kk = jnp.concatenate([k_before, kc_ref[rows, cs]], axis=0)
            vv = jnp.concatenate([v_before, vc_ref[rows, cs]], axis=0)
            s = lax.dot_general(qq, kk, (((1,), (1,)), ((), ())), preferred_element_type=F32)
            s = s + bias_ref[sel, 2 * Q_TILE * p:2 * Q_TILE * (p + 1), :]
            m = jnp.max(s, axis=1, keepdims=True)
            e = jnp.exp(s - m)
            l = jnp.sum(e, axis=1, keepdims=True)
            eb = e.astype(BF16)
            o0 = jnp.dot(eb[:Q_TILE], vv, preferred_element_type=F32) * (1.0 / l[:Q_TILE])
            o1 = jnp.dot(eb[Q_TILE:], vv, preferred_element_type=F32) * (1.0 / l[Q_TILE:])
            o_ref[rows, cs] = jnp.where(lo, o0, o1).astype(o_ref.dtype)
            lse = m + jnp.log(l)
            lse_acc = jnp.where(lane == 2 * p, lse[:Q_TILE], lse_acc)
            lse_acc = jnp.where(lane == 2 * p + 1, lse[Q_TILE:], lse_acc)
        lse_ref[rows, :] = lse_acc[:, :N_HEADS]


def _attn_branch(qv, kv, vv, dil):
    a = ATTN_WIDTH
    rows = qv.shape[0]
    t = rows * dil
    step = Q_BLOCKS_PER_STEP * Q_TILE
    cur = pl.BlockSpec((step, a), lambda i, r: (i, r))
    prev = pl.BlockSpec((Q_TILE, a), lambda i, r: (jnp.maximum(i * Q_BLOCKS_PER_STEP - 1, 0), r))
    bias = jnp.asarray(_branch_bias(dil))
    o, lse = pl.pallas_call(
        _attn_branch_kernel,
        out_shape=(jax.ShapeDtypeStruct((rows, dil * a), BF16),
                   jax.ShapeDtypeStruct((dil, rows, N_HEADS), F32)),
        grid=(rows // step, dil),
        in_specs=[cur, prev, cur, prev, cur, _full(bias.shape)],
        out_specs=(cur, pl.BlockSpec((None, step, N_HEADS), lambda i, r: (r, i, 0))),
        compiler_params=_cparams(2),
        name=f"attn_branch_d{dil}",
    )(qv, kv, kv, vv, vv, bias)
    return o, lse.transpose(1, 0, 2).reshape(t, N_HEADS)


def _attn_combine_kernel(*refs, dils):
    nb = len(dils)
    o_refs = refs[:nb]
    l_refs = refs[nb:2 * nb]
    g_ref = refs[2 * nb]
    unperm_refs = refs[2 * nb + 1:-1]
    a_ref = refs[-1]
    tq, a = a_ref.shape
    outs = []
    n_u = 0
    for o_ref, dil in zip(o_refs, dils):
        if dil == 1:
            outs.append(o_ref[...].astype(F32))
            continue
        by_residue = jnp.concatenate([o_ref[:, r * a:(r + 1) * a] for r in range(dil)], axis=0)
        outs.append(jnp.dot(unperm_refs[n_u][...], by_residue, preferred_element_type=F32))
        n_u += 1
    ls = [l_ref[...] for l_ref in l_refs]
    top = functools.reduce(jnp.maximum, ls)
    ws = [jnp.exp(l - top) for l in ls]
    inv = 1.0 / functools.reduce(jnp.add, ws)
    cs = [w * inv for w in ws]
    lane = lax.broadcasted_iota(jnp.int32, (tq, V7X_LANES), 1)
    lo = lane < HEAD_DIM
    cols = []
    for p in range(N_HEADS // 2):
        sl = slice(V7X_LANES * p, V7X_LANES * (p + 1))
        acc = jnp.zeros((tq, V7X_LANES), F32)
        for c, o in zip(cs, outs):
            cexp = jnp.where(lo,
                             jnp.broadcast_to(c[:, 2 * p:2 * p + 1], (tq, V7X_LANES)),
                             jnp.broadcast_to(c[:, 2 * p + 1:2 * p + 2], (tq, V7X_LANES)))
            acc = acc + cexp * o[:, sl]
        cols.append(acc)
    o = jnp.concatenate(cols, axis=1)
    a_ref[...] = _rms(o, g_ref[...]).astype(a_ref.dtype)


def _attn_combine(os_, lses, g, dils):
    t = lses[0].shape[0]
    a = ATTN_WIDTH
    tq = TOKEN_TILE
    unperms = [jnp.asarray(_chunk_perm(tq // dil, dil).T, BF16) for dil in dils if dil > 1]
    return pl.pallas_call(
        functools.partial(_attn_combine_kernel, dils=tuple(dils)),
        out_shape=jax.ShapeDtypeStruct((t, a), BF16),
        grid=(t // tq,),
        in_specs=[pl.BlockSpec((tq // dil, dil * a), lambda i: (i, 0)) for dil in dils]
        + [pl.BlockSpec((tq, N_HEADS), lambda i: (i, 0))] * len(dils) + [_full((1, a))]
        + [_full((tq, tq))] * len(unperms),
        out_specs=pl.BlockSpec((tq, a), lambda i: (i, 0)),
        compiler_params=_cparams(),
        name="attn_combine",
    )(*os_, *lses, g.reshape(1, a), *unperms)


NEW_KEY_PAD = V7X_LANES


def _sample_key_plan(win, steps):
    g = max(d for _, d in DILATED_BRANCHES)
    near = max(w for w, d in DILATED_BRANCHES if d < g)
    assert win % g == 0 and 2 * steps == g and near % g == 0 and near <= win
    a0 = (win - near) // g
    assert a0 % (win // g - a0) == 0
    return g, a0


def _sample_bias(win, steps):
    g, a0 = _sample_key_plan(win, steps)
    heads = N_HEADS
    slopes = _alibi_slopes()
    pos_a = (np.arange(win // g)[:, None] * g + np.arange(steps)[None, :]).ravel()
    pos_b = (np.arange(a0, win // g)[:, None] * g + np.arange(steps, g)[None, :]).ravel()
    pos_new = win + np.arange(steps)
    qh = np.repeat(np.arange(heads), steps)[:, None]
    qt = np.tile(np.arange(steps), heads)[:, None]

    def table(pos_list, window, dil):
        pos = np.concatenate(pos_list)
        pad = NEW_KEY_PAD - steps * heads
        p = np.concatenate([np.repeat(pos, heads), np.full(pad, -1)])[None, :]
        h = np.concatenate([np.tile(np.arange(heads), len(pos)), np.full(pad, -1)])[None, :]
        dist = (win + qt) - p
        ok = (h == qh) & (p >= 0) & (dist >= 0) & (dist <= window) & (dist % dil == 0)
        return np.where(ok, -slopes[qh] * dist.astype(np.float32), np.float32(MASKED)).astype(np.float32)

    wide = [table([pos_a, pos_new], w, d) for w, d in DILATED_BRANCHES if d == g]
    near = [table([pos_a[a0 * steps:], pos_b, pos_new], w, d) for w, d in DILATED_BRANCHES if d < g]
    return wide[0], np.stack(near)


def _attn_sample_kernel(q_ref, kn_ref, vn_ref, ka_ref, kb_ref, va_ref, vb_ref, bw_ref, bn_ref, g_ref, o_ref):
    nq, dh = q_ref.shape
    steps = nq // N_HEADS
    flat = lambda ref: ref[...].reshape(-1, dh).astype(BF16)
    ka, kb, va, vb = flat(ka_ref), flat(kb_ref), flat(va_ref), flat(vb_ref)
    pad = jnp.zeros((NEW_KEY_PAD - kn_ref.shape[0], dh), F32)
    kn = jnp.concatenate([kn_ref[...], pad], axis=0).astype(BF16)
    vn = jnp.concatenate([vn_ref[...], pad], axis=0).astype(BF16)
    q = q_ref[...]
    scores = lambda k: lax.dot_general(q, k, (((1,), (1,)), ((), ())), preferred_element_type=F32)
    s_a, s_b, s_n = scores(ka), scores(kb), scores(kn)
    n_tail = kb.shape[0]

    def soft(s):
        m = jnp.max(s, axis=1, keepdims=True)
        e = jnp.exp(s - m)
        return m, jnp.sum(e, axis=1, keepdims=True), e.astype(BF16)

    m_w, l_w, e_w = soft(jnp.concatenate([s_a, s_n], axis=1) + bw_ref[...])
    o_w = jnp.dot(e_w, jnp.concatenate([va, vn], axis=0), preferred_element_type=F32)
    s_near = jnp.concatenate([s_a[:, s_a.shape[1] - n_tail:], s_b, s_n], axis=1)
    v_near = jnp.concatenate([va[va.shape[0] - n_tail:], vb, vn], axis=0)
    parts = [soft(s_near + bn_ref[i]) for i in range(bn_ref.shape[0])]
    o_near = jnp.dot(jnp.concatenate([e for _, _, e in parts], axis=0), v_near, preferred_element_type=F32)
    ms = [m_w] + [m for m, _, _ in parts]
    ls = [l_w] + [l for _, l, _ in parts]
    os_ = [o_w] + [o_near[i * nq:(i + 1) * nq] for i in range(len(parts))]
    top = functools.reduce(jnp.maximum, ms)
    num = jnp.zeros((nq, dh), F32)
    den = jnp.zeros((nq, 1), F32)
    for m, l, o in zip(ms, ls, os_):
        w = jnp.exp(m - top)
        num = num + o * w
        den = den + l * w
    res = num * (1.0 / den)
    sq = jnp.sum(res * res, axis=1, keepdims=True)
    tot = sq[0:steps]
    for h in range(1, N_HEADS):
        tot = tot + sq[h * steps:(h + 1) * steps]
    inv = lax.rsqrt(tot * (1.0 / ATTN_WIDTH) + EPS)
    o_ref[...] = res * jnp.concatenate([inv] * N_HEADS, axis=0) * g_ref[...]


def _attn_sample(q, kn, vn, cache_k, cache_v, g):
    b, nq, dh = q.shape
    steps = nq // N_HEADS
    win = cache_k.shape[1]
    grp, a0 = _sample_key_plan(win, steps)
    wide, near = (jnp.asarray(z) for z in _sample_bias(win, steps))
    n_grp = win // grp
    grid5 = lambda c: c.reshape(b, n_grp, grp, N_HEADS, dh)
    low = pl.BlockSpec((None, n_grp, steps, N_HEADS, dh), lambda i: (i, 0, 0, 0, 0))
    high = pl.BlockSpec((None, n_grp - a0, steps, N_HEADS, dh), lambda i: (i, a0 // (n_grp - a0), 1, 0, 0))
    new = pl.BlockSpec((None, nq, dh), lambda i: (i, 0, 0))
    g_rows = jnp.repeat(g.reshape(N_HEADS, dh), steps, axis=0)
    return pl.pallas_call(
        _attn_sample_kernel,
        out_shape=jax.ShapeDtypeStruct((b, nq, dh), F32),
        grid=(b,),
        in_specs=[new, new, new, low, high, low, high, _full(wide.shape), _full(near.shape), _full((nq, dh))],
        out_specs=new,
        compiler_params=_cparams(),
        name="attn_sample",
    )(q, kn, vn, grid5(cache_k), grid5(cache_k), grid5(cache_v), grid5(cache_v), wide, near, g_rows)


def _gelu_tanh(x):
    return 0.5 * x * (1.0 + jnp.tanh(np.sqrt(2.0 / np.pi).astype(np.float32) * (x + 0.044715 * (x * x * x))))


def _ssm_kernel(u_ref, bb_ref, lam_ref, pow_ref, cm_ref, dsk_ref, wglu_ref, bglu_ref, g_ref, pt_ref, h0_ref,
                z_ref, ht_ref, h_s, hin_s, carry_s, *, rows, steps, chain):
    ns = N_STATE
    cw = (4 * V7X_SUBLANES * V7X_LANES) // rows
    last = (steps - 1) * rows

    @pl.when(pl.program_id(0) == 0)
    def _():
        carry_s[...] = h0_ref[0:1, :]

    ub = u_ref[...].astype(BF16)
    n_slabs = SSM_WIDTH // V7X_LANES
    sw = ns // n_slabs
    for s in range(n_slabs):
        part = jnp.dot(ub[:, s * V7X_LANES:(s + 1) * V7X_LANES], bb_ref[s], preferred_element_type=F32)
        h_s[:, s * sw:(s + 1) * sw] = part[:, :sw]
        h_s[:, ns + s * sw:ns + (s + 1) * sw] = part[:, sw:]

    for cc in range(0, ns // cw, 2):
        crs = [slice(c * cw, (c + 1) * cw) for c in (cc, cc + 1)]
        cis = [slice(ns + c * cw, ns + (c + 1) * cw) for c in (cc, cc + 1)]
        lrs = [jnp.broadcast_to(lam_ref[0:1, cr], (rows, cw)) for cr in crs]
        lis = [jnp.broadcast_to(lam_ref[1:2, cr], (rows, cw)) for cr in crs]

        def scan_body(t, carry, crs=crs, cis=cis, lrs=lrs, lis=lis):
            rs = pl.ds(pl.multiple_of(t * rows, rows), rows)
            out = []
            for j in range(2):
                hr, hi = carry[2 * j], carry[2 * j + 1]
                nr = lrs[j] * hr - lis[j] * hi + h_s[rs, crs[j]]
                ni = lrs[j] * hi + lis[j] * hr + h_s[rs, cis[j]]
                h_s[rs, crs[j]] = nr
                h_s[rs, cis[j]] = ni
                out += [nr, ni]
            return tuple(out)

        zero = jnp.zeros((rows, cw), F32)
        lax.fori_loop(0, steps, scan_body, (zero,) * 4)

    if chain:
        ptr = pow_ref[steps - 1, 0:1, :ns]
        pti = pow_ref[steps - 1, 0:1, ns:]
        carry = carry_s[...]
        for c in range(rows):
            hin_s[c:c + 1, :] = carry
            cr_, ci_ = carry[:, :ns], carry[:, ns:]
            e = h_s[last + c:last + c + 1, :]
            carry = jnp.concatenate([ptr * cr_ - pti * ci_ + e[:, :ns],
                                     ptr * ci_ + pti * cr_ + e[:, ns:]], axis=1)
        carry_s[...] = carry
    else:
        hin_s[...] = h0_ref[...]

    for cc in range(ns // cw):
        cr = slice(cc * cw, (cc + 1) * cw)
        ci = slice(ns + cc * cw, ns + (cc + 1) * cw)
        hr0 = hin_s[:, cr]
        hi0 = hin_s[:, ci]

        def fix_body(t, _, cr=cr, ci=ci, hr0=hr0, hi0=hi0):
            rs = pl.ds(pl.multiple_of(t * rows, rows), rows)
            pr = pow_ref[t, :, cr]
            pi_ = pow_ref[t, :, ci]
            h_s[rs, cr] = h_s[rs, cr] + (pr * hr0 - pi_ * hi0)
            h_s[rs, ci] = h_s[rs, ci] + (pr * hi0 + pi_ * hr0)
            return 0

        lax.fori_loop(0, steps, fix_body, 0, unroll=4)

    ht_ref[...] = h_s[last:last + rows, :]
    ys = []
    for s in range(n_slabs):
        hs = jnp.concatenate([h_s[:, s * sw:(s + 1) * sw], h_s[:, ns + s * sw:ns + (s + 1) * sw]], axis=1)
        ys.append(jnp.dot(hs.astype(BF16), cm_ref[s], preferred_element_type=F32))
    y = jnp.concatenate(ys, axis=1) + dsk_ref[...] * u_ref[...]
    y = _gelu_tanh(y)
    gl = jnp.dot(y.astype(BF16), wglu_ref[...], preferred_element_type=F32) + bglu_ref[...]
    z = _rms(y * _sigmoid(gl), g_ref[...]).astype(BF16)
    z_ref[...] = jnp.dot(pt_ref[...], z, preferred_element_type=F32).astype(z_ref.dtype)


def _ssm_tables(ssm_a_re, ssm_a_im, ssm_log_dt, ssm_b_re, ssm_b_im, ssm_c_re, ssm_c_im, max_steps):
    g, n, ch = SSM_GROUPS, SSM_STATE, SSM_CH
    a_re = ssm_a_re.astype(F32)
    a_im = ssm_a_im.astype(F32)
    dt = jnp.exp(ssm_log_dt.astype(F32))[:, None]
    mag = jnp.exp(dt * a_re)
    lam_re = mag * jnp.cos(dt * a_im)
    lam_im = mag * jnp.sin(dt * a_im)
    nr = lam_re - 1.0
    ni = lam_im
    inv = 1.0 / (a_re * a_re + a_im * a_im)
    coef_re = (nr * a_re + ni * a_im) * inv
    coef_im = (ni * a_re - nr * a_im) * inv
    br = ssm_b_re.astype(F32)
    bi = ssm_b_im.astype(F32)
    bb_re = coef_re[..., None] * br - coef_im[..., None] * bi
    bb_im = coef_re[..., None] * bi + coef_im[..., None] * br
    gs = V7X_LANES // ch
    ns_ = g // gs
    eye = jnp.eye(gs, dtype=F32)
    bmat = lambda b: jnp.einsum("sgnc,gh->sgchn", b.reshape(ns_, gs, n, ch), eye).reshape(ns_, gs * ch, gs * n)
    cmat = lambda c: jnp.einsum("sgcn,gh->sgnhc", c.astype(F32).reshape(ns_, gs, ch, n), eye).reshape(
        ns_, gs * n, gs * ch)
    bb = jnp.concatenate([bmat(bb_re), bmat(bb_im)], axis=2).astype(BF16)
    cm = jnp.concatenate([cmat(ssm_c_re), -cmat(ssm_c_im)], axis=1).astype(BF16)
    lam = jnp.stack([lam_re.reshape(-1), lam_im.reshape(-1)])

    k = jnp.arange(1, max_steps + 1, dtype=F32)[:, None]
    kdt = k * dt.reshape(1, -1).repeat(n, axis=1)
    pmag = jnp.exp(kdt * a_re.reshape(1, -1))
    parg = kdt * a_im.reshape(1, -1)
    pows = jnp.concatenate([pmag * jnp.cos(parg), pmag * jnp.sin(parg)], axis=1)
    return bb, cm, lam, pows


def _ssm(u_perm, h0, tabs, dsk, w_glu, b_glu, g, perm_t, rows, steps, chain):
    n, w = u_perm.shape
    blk = rows * steps
    bb, cm, lam, pows = tabs
    row = pl.BlockSpec((blk, w), lambda i: (i, 0))
    kern = functools.partial(_ssm_kernel, rows=rows, steps=steps, chain=chain)
    return pl.pallas_call(
        kern,
        out_shape=(jax.ShapeDtypeStruct((n, w), BF16), jax.ShapeDtypeStruct((rows, 2 * N_STATE), F32)),
        grid=(n // blk,),
        in_specs=[row, _full(bb.shape), _full(lam.shape), _full((steps, rows, 2 * N_STATE)), _full(cm.shape),
                  _full((1, w)), _full((w, w)), _full((1, w)), _full((1, w)), _full((blk, blk)),
                  _full((rows, 2 * N_STATE))],
        out_specs=(row, _full((rows, 2 * N_STATE))),
        scratch_shapes=[pltpu.VMEM((blk, 2 * N_STATE), F32),
                        pltpu.VMEM((rows, 2 * N_STATE), F32),
                        pltpu.VMEM((1, 2 * N_STATE), F32)],
        compiler_params=_cparams(),
        name=f"ssm_r{rows}",
    )(u_perm, bb, lam, jnp.broadcast_to(pows[:steps, None, :], (steps, rows, 2 * N_STATE)), cm,
      dsk.reshape(1, w), w_glu.astype(BF16), b_glu.reshape(1, w),
      g.reshape(1, w), perm_t, h0)


def _outproj_kernel(*refs, n_real, aliased):
    ins, outs = refs[:12], refs[12 + aliased:]
    step = pl.program_id(0)

    @pl.when(step < n_real)
    def _():
        _outproj_tile(*ins, *outs)

    @pl.when(step >= n_real)
    def _():
        xs_ref = outs[1]
        xs_ref[...] = jnp.zeros(xs_ref.shape, xs_ref.dtype)


def _outproj_tile(x_ref, a_ref, z_ref, gt_ref, sh_ref, sc_ref, g_ref, wo_ref, wr_ref, br_ref, ltri_ref, utri_ref,
                  x1_ref, xs_ref, slot_ref, gates_ref, meta_ref, wbf_ref):
    @pl.when(pl.program_id(0) == 0)
    def _():
        wbf_ref[...] = wo_ref[...].astype(BF16)

    a = ATTN_WIDTH
    mixed = (jnp.dot(a_ref[...].astype(BF16), wbf_ref[:a, :], preferred_element_type=F32)
             + jnp.dot(z_ref[...], wbf_ref[a:, :], preferred_element_type=F32))
    x1 = x_ref[...] + gt_ref[...] * mixed
    x1_ref[...] = x1
    h2 = (_rms(x1, g_ref[...]) * (1.0 + sc_ref[...]) + sh_ref[...]).astype(BF16)
    lg = jnp.dot(h2, wr_ref[...], preferred_element_type=F32) + br_ref[...]
    tm = lg.shape[0]
    lane = lax.broadcasted_iota(jnp.int32, (tm, V7X_LANES), 1).astype(F32)
    vals, hots = [], []
    for _ in range(TOP_K):
        m = jnp.max(lg, axis=1, keepdims=True)
        idx = jnp.min(jnp.where(lg == m, lane, float(V7X_LANES)), axis=1, keepdims=True)
        hot = lane == idx
        vals.append(m)
        hots.append(jnp.where(hot, 1.0, 0.0))
        lg = jnp.where(hot, MASKED * 2, lg)
    es = [jnp.exp(v - vals[0]) for v in vals]
    inv = 1.0 / (es[0] + es[1] + es[2] + es[3])

    member = hots[0] + hots[1] + hots[2] + hots[3]
    before = jnp.dot(ltri_ref[...], member.astype(BF16), preferred_element_type=F32)
    count = jnp.sum(member, axis=0, keepdims=True)
    padded = jnp.floor((count + (ROW_ALIGN - 1.0)) * (1.0 / ROW_ALIGN)) * ROW_ALIGN
    padded8 = jnp.broadcast_to(padded, (V7X_SUBLANES, V7X_LANES))
    start = jnp.dot(padded8.astype(BF16), utri_ref[...], preferred_element_type=F32)[0:1]
    where_to = start + before
    slots = jnp.zeros((tm, V7X_LANES), F32)
    gates = jnp.zeros((tm, V7X_LANES), F32)
    for k in range(TOP_K):
        slot_k = jnp.sum(hots[k] * where_to, axis=1, keepdims=True)
        slots = jnp.where(lane == float(k), slot_k, slots)
        gates = jnp.where(lane == float(k), es[k] * inv, gates)
    slot_ref[...] = slots
    gates_ref[...] = gates
    row = lax.broadcasted_iota(jnp.int32, (V7X_SUBLANES, V7X_LANES), 0)
    meta_ref[...] = jnp.where(row == 0, padded8, jnp.where(row == 1, jnp.broadcast_to(start, padded8.shape), 0.0))

    cap = xs_ref.shape[0]
    slots_t = jnp.transpose(slots)
    srow = lax.broadcasted_iota(jnp.int32, (cap, tm), 0).astype(F32)
    place = jnp.zeros((cap, tm), F32)
    for k in range(TOP_K):
        place = jnp.where(srow == slots_t[k:k + 1, :], 1.0, place)
    xs_ref[...] = jnp.dot(place.astype(BF16), h2, preferred_element_type=F32).astype(BF16)


def _outproj(x, a, z, gt, sh, sc, g, w_out, wr_pad, br_pad, tm, n_blocks, block0, xs_prev=None):
    n, d = x.shape
    aw = ATTN_WIDTH
    n_real = n // tm
    aliased = xs_prev is not None
    n_steps = n_real if aliased else n_blocks
    tile = lambda i: jnp.minimum(i, n_real - 1)
    mod_spec = (pl.BlockSpec((1, d), lambda i: (0, 0)) if gt.shape[0] == 1
                else pl.BlockSpec((tm, d), lambda i: (tile(i), 0)))
    row = lambda w: pl.BlockSpec((tm, w), lambda i: (tile(i), 0))
    ltri = jnp.asarray(np.tril(np.ones((tm, tm), np.float32), -1), BF16)
    utri = jnp.asarray(np.triu(np.ones((V7X_LANES, V7X_LANES), np.float32), 1), BF16)
    in_specs = [row(d), row(aw), row(SSM_WIDTH), mod_spec, mod_spec, mod_spec, _full((1, d)),
                _full(w_out.shape), _full(wr_pad.shape), _full(br_pad.shape), _full(ltri.shape), _full(utri.shape)]
    args = [x, a, z, gt, sh, sc, g.reshape(1, d), w_out, wr_pad, br_pad, ltri, utri]
    if aliased:
        in_specs.append(pl.BlockSpec(memory_space=pl.ANY))
        args.append(xs_prev)
    return pl.pallas_call(
        functools.partial(_outproj_kernel, n_real=n_real, aliased=int(aliased)),
        out_shape=(jax.ShapeDtypeStruct((n, d), F32),
                   jax.ShapeDtypeStruct((n_blocks, GROUP_CAP, d), BF16),
                   jax.ShapeDtypeStruct((n, V7X_LANES), F32), jax.ShapeDtypeStruct((n, V7X_LANES), F32),
                   jax.ShapeDtypeStruct((n_real, V7X_SUBLANES, V7X_LANES), F32)),
        grid=(n_steps,),
        in_specs=in_specs,
        out_specs=(row(d), pl.BlockSpec((None, GROUP_CAP, d), lambda i: (block0 + i, 0, 0)),
                   row(V7X_LANES), row(V7X_LANES),
                   pl.BlockSpec((None, V7X_SUBLANES, V7X_LANES), lambda i: (tile(i), 0, 0))),
        scratch_shapes=[pltpu.VMEM(w_out.shape, BF16)],
        input_output_aliases={len(args) - 1: 1} if aliased else {},
        compiler_params=_cparams(),
        name="outproj_router",
    )(*args)


_PIECE_SIZES = tuple(MOE_TILE >> s for s in range(6))
_CHUNK_SHIFT = 6
_CHUNK = 1 << _CHUNK_SHIFT
_TAIL_SIZES = tuple(sz for sz in _PIECE_SIZES if sz < _CHUNK)


def _moe_kernel(te_ref, tl_ref, lo_ref, hi_ref, nu_ref, gstart_ref, gsize_ref, gbase_ref, rows_ref, used_ref,
                xs_hbm, wgu_ref, bgu_ref, wd_ref, bd_ref, ys_hbm,
                xbuf, ybuf, zbuf, wgu_bf, wd_bf, xsem, ysem, zsem):
    t = pl.program_id(0)
    n_used = nu_ref[0]
    n_blocks, cap = xs_hbm.shape[0], xs_hbm.shape[1]

    def x_copy(i, br, tr, sz, slot):
        pltpu.make_async_copy(xs_hbm.at[i, pl.ds(br, sz)], xbuf.at[slot, pl.ds(tr, sz)], xsem.at[slot]).start()

    def y_copy(i, br, tr, sz, slot):
        pltpu.make_async_copy(ybuf.at[slot, pl.ds(tr, sz)], ys_hbm.at[i, pl.ds(br, sz)], ysem.at[slot]).start()

    def z_copy(i, row, sz, start):
        cp = pltpu.make_async_copy(zbuf.at[pl.ds(0, sz)], ys_hbm.at[i, pl.ds(row, sz)], zsem)
        cp.start() if start else cp.wait()

    def pieces(tt, fn):
        e = te_ref[tt]
        lo = tl_ref[tt] * MOE_TILE

        def per_block(i, c):
            g = e * n_blocks + i
            s0 = gbase_ref[g]
            a = jnp.maximum(s0, lo)
            b = jnp.minimum(s0 + gsize_ref[g], lo + MOE_TILE)
            length = jnp.maximum(b - a, 0)
            src = gstart_ref[g] + (a - s0)
            dst = a - lo
            whole = lax.shift_right_logical(length, _CHUNK_SHIFT)

            def chunk(j, cc):
                off = j * _CHUNK
                fn(i, pl.multiple_of(src + off, ROW_ALIGN), pl.multiple_of(dst + off, ROW_ALIGN), _CHUNK)
                return cc

            lax.fori_loop(0, whole, chunk, 0)
            done = whole * _CHUNK
            for sz in _TAIL_SIZES:
                hit = (length & sz) != 0

                @pl.when(hit)
                def _(sz=sz, done=done):
                    fn(i, pl.multiple_of(src + done, ROW_ALIGN), pl.multiple_of(dst + done, ROW_ALIGN), sz)

                done = done + jnp.where(hit, sz, 0)
            return c

        lax.fori_loop(lo_ref[tt], hi_ref[tt], per_block, 0)

    def tile_rows(tt):
        return jnp.minimum(rows_ref[te_ref[tt]] - tl_ref[tt] * MOE_TILE, MOE_TILE)

    def wait_rows(n, sem, buf):
        def chunk(j, cc):
            pltpu.make_async_copy(buf.at[pl.ds(0, _CHUNK)], buf.at[pl.ds(0, _CHUNK)], sem).wait()
            return cc

        lax.fori_loop(0, lax.shift_right_logical(n, _CHUNK_SHIFT), chunk, 0)
        for sz in _TAIL_SIZES:
            @pl.when((n & sz) != 0)
            def _(sz=sz):
                pltpu.make_async_copy(buf.at[pl.ds(0, sz)], buf.at[pl.ds(0, sz)], sem).wait()

    def fetch(tt, slot):
        pieces(tt, lambda i, br, tr, sz: x_copy(i, br, tr, sz, slot))

    def writeback(tt, slot):
        pieces(tt, lambda i, br, tr, sz: y_copy(i, br, tr, sz, slot))

    def zero_tail(i, start):
        u = used_ref[i]
        rem = cap - u
        nz = zbuf.shape[0]
        whole = lax.shift_right_logical(rem, nz.bit_length() - 1)

        def chunk(j, c):
            z_copy(i, pl.multiple_of(u + j * nz, ROW_ALIGN), nz, start)
            return c

        lax.fori_loop(0, whole, chunk, 0)
        base = u + whole * nz
        done = jnp.int32(0)
        for sz in _PIECE_SIZES:
            if sz >= nz:
                continue
            hit = (rem & sz) != 0

            @pl.when(hit)
            def _(sz=sz, done=done):
                z_copy(i, pl.multiple_of(base + done, ROW_ALIGN), sz, start)

            done = done + jnp.where(hit, sz, 0)

    @pl.when(t == 0)
    def _():
        xbuf[...] = jnp.zeros(xbuf.shape, xbuf.dtype)
        zbuf[...] = jnp.zeros(zbuf.shape, zbuf.dtype)
        lax.fori_loop(0, n_blocks, lambda i, c: (zero_tail(i, True), c)[1], 0)
        lax.fori_loop(0, n_blocks, lambda i, c: (zero_tail(i, False), c)[1], 0)
        fetch(0, 0)

    @pl.when(t < n_used)
    def _():
        slot = t % 2

        @pl.when(t + 1 < n_used)
        def _():
            fetch(t + 1, 1 - slot)

        prev = te_ref[jnp.maximum(t - 1, 0)]
        fresh = jnp.logical_or(t == 0, te_ref[t] != prev)

        @pl.when(fresh)
        def _():
            wgu_bf[...] = wgu_ref[...].astype(BF16)
            wd_bf[...] = wd_ref[...].astype(BF16)

        wait_rows(tile_rows(t), xsem.at[slot], xbuf.at[slot])

        @pl.when(t >= 2)
        def _():
            wait_rows(tile_rows(t - 2), ysem.at[slot], ybuf.at[slot])

        f = D_EXPERT
        gu = jnp.dot(xbuf[slot], wgu_bf[...], preferred_element_type=F32) + bgu_ref[...]
        gate = jnp.minimum(gu[:, :f], SWIGLU_LIMIT)
        up = jnp.clip(gu[:, f:], -SWIGLU_LIMIT, SWIGLU_LIMIT)
        act = (up + 1.0) * gate * _sigmoid(SWIGLU_ALPHA * gate)
        y = jnp.dot(act.astype(BF16), wd_bf[...], preferred_element_type=F32) + bd_ref[...]
        ybuf[slot] = y.astype(BF16)
        writeback(t, slot)

        @pl.when(t == n_used - 1)
        def _():
            wait_rows(tile_rows(t), ysem.at[slot], ybuf.at[slot])

            @pl.when(t >= 1)
            def _():
                wait_rows(tile_rows(t - 1), ysem.at[1 - slot], ybuf.at[1 - slot])


def _moe(plan, xs, w_gate_up, b_gate_up, w_down, b_down):
    _, cap, d = xs.shape
    e, _, f2 = w_gate_up.shape
    nt = plan[0].shape[0]
    wmap = lambda t, te, *_: (te[t], 0, 0)
    anyspec = pl.BlockSpec(memory_space=pl.ANY)
    grid_spec = pltpu.PrefetchScalarGridSpec(
        num_scalar_prefetch=len(plan),
        grid=(nt,),
        in_specs=[anyspec,
                  pl.BlockSpec((None, d, f2), wmap),
                  pl.BlockSpec((None, 1, f2), wmap),
                  pl.BlockSpec((None, f2 // 2, d), wmap),
                  pl.BlockSpec((None, 1, d), wmap)],
        out_specs=anyspec,
        scratch_shapes=[pltpu.VMEM((2, MOE_TILE, d), BF16), pltpu.VMEM((2, MOE_TILE, d), BF16),
                        pltpu.VMEM((MOE_TILE // 2, d), BF16),
                        pltpu.VMEM((d, f2), BF16), pltpu.VMEM((f2 // 2, d), BF16),
                        pltpu.SemaphoreType.DMA((2,)), pltpu.SemaphoreType.DMA((2,)), pltpu.SemaphoreType.DMA(())],
    )
    return pl.pallas_call(
        _moe_kernel,
        out_shape=jax.ShapeDtypeStruct(xs.shape, BF16),
        grid_spec=grid_spec,
        compiler_params=_cparams(),
        name="moe_experts",
    )(*plan, xs, w_gate_up, b_gate_up.reshape(e, 1, f2), w_down, b_down.reshape(e, 1, d))


def _moe_plan(group_size, group_start):
    n_blocks = group_size.shape[0]
    gsize = group_size.T.astype(jnp.int32)
    gstart = group_start.T.astype(jnp.int32)
    gbase = jnp.cumsum(gsize, axis=1) - gsize
    rows = jnp.sum(gsize, axis=1)
    tiles = (rows + MOE_TILE - 1) // MOE_TILE
    tile_end = jnp.cumsum(tiles)
    n_used = tile_end[-1:]
    nt = (n_blocks * GROUP_CAP) // MOE_TILE + N_EXPERTS
    t = jnp.arange(nt, dtype=jnp.int32)
    te = jnp.sum((tile_end[None, :] <= t[:, None]).astype(jnp.int32), axis=1)
    last = jnp.max(jnp.where(tiles > 0, jnp.arange(N_EXPERTS, dtype=jnp.int32), 0))
    te = jnp.where(t < n_used[0], jnp.minimum(te, N_EXPERTS - 1), last)
    hot = (te[:, None] == jnp.arange(N_EXPERTS, dtype=jnp.int32)[None, :]).astype(jnp.int32)
    tl = jnp.where(t < n_used[0], t - hot @ (tile_end - tiles), 0)
    lo_row = tl * MOE_TILE
    base_t = hot @ gbase
    size_t = hot @ gsize
    first = jnp.sum((base_t + size_t <= lo_row[:, None]).astype(jnp.int32), axis=1)
    stop = jnp.sum((base_t < lo_row[:, None] + MOE_TILE).astype(jnp.int32), axis=1)
    used = jnp.sum(gsize, axis=0)
    i32 = lambda z: z.astype(jnp.int32)
    return (i32(te), i32(tl), i32(first), i32(stop), i32(n_used), i32(gstart.reshape(-1)), i32(gsize.reshape(-1)),
            i32(gbase.reshape(-1)), i32(rows), i32(used))


def _final_kernel(x_ref, ys_ref, slot_ref, gates_ref, gt_ref, g_ref, y_ref):
    tm = x_ref.shape[0]
    cap = ys_ref.shape[0]
    col = lax.broadcasted_iota(jnp.int32, (tm, cap), 1).astype(F32)
    slots = slot_ref[...]
    gates = gates_ref[...]
    mix = jnp.zeros((tm, cap), F32)
    for k in range(TOP_K):
        mix = jnp.where(col == slots[:, k:k + 1], gates[:, k:k + 1], mix)
    hi = mix.astype(BF16)
    lo = (mix - hi.astype(F32)).astype(BF16)
    ys = ys_ref[...]
    ff = jnp.dot(hi, ys, preferred_element_type=F32) + jnp.dot(lo, ys, preferred_element_type=F32)
    y_ref[...] = _rms(x_ref[...] + gt_ref[...] * ff, g_ref[...])


def _final(x1, ys, slots, gates, gt, g, tm, block0):
    n, d = x1.shape
    cap = ys.shape[1]
    mod_spec = (pl.BlockSpec((1, d), lambda i: (0, 0)) if gt.shape[0] == 1
                else pl.BlockSpec((tm, d), lambda i: (i, 0)))
    row = lambda w: pl.BlockSpec((tm, w), lambda i: (i, 0))
    return pl.pallas_call(
        _final_kernel,
        out_shape=jax.ShapeDtypeStruct((n, d), F32),
        grid=(n // tm,),
        in_specs=[row(d), pl.BlockSpec((None, cap, d), lambda i: (block0 + i, 0, 0)),
                  row(V7X_LANES), row(V7X_LANES), mod_spec, _full((1, d))],
        out_specs=row(d),
        compiler_params=_cparams(),
        name="final_norm",
    )(x1, ys, slots, gates, gt, g.reshape(1, d))


def kernel(x_prompt, x_sample, cache_k_win, cache_v_win, state_ssm_re, state_ssm_im, c_prompt, c_sample,
           w_ada, b_ada, g_norm1, w_in, ssm_a_re, ssm_a_im, ssm_log_dt, ssm_b_re, ssm_b_im, ssm_c_re, ssm_c_im,
           ssm_d, w_glu, b_glu, g_out_attn, g_out_ssm, w_out, g_norm2, w_router, b_router, w_gate_up, b_gate_up,
           w_down, b_down, g_final):
    depth = w_ada.shape[0]
    assert depth == 1 and x_prompt.shape[0] == 1
    bp, t, d = x_prompt.shape
    bs, ts, _ = x_sample.shape
    ns = bs * ts
    l = 0
    a = ATTN_WIDTH

    n_c = bp + bs
    c_pad = -(-n_c // V7X_SUBLANES) * V7X_SUBLANES
    c_rows = jnp.concatenate([c_prompt, c_sample, jnp.zeros((c_pad - n_c, d), F32)], axis=0)
    mod = _ada_modulation(c_rows, w_ada[l], b_ada[l])
    mod_p = [mod[0:1, i * d:(i + 1) * d] for i in range(N_MOD)]
    mod_s = [jnp.repeat(mod[bp:bp + bs, i * d:(i + 1) * d], ts, axis=0) for i in range(N_MOD)]

    steps_p = TOKEN_TILE // SSM_ROWS_PROMPT
    perm_p = _chunk_perm(SSM_ROWS_PROMPT, steps_p)
    perm_s = _chunk_perm(bs, ts)
    tabs = _ssm_tables(ssm_a_re[l], ssm_a_im[l], ssm_log_dt[l], ssm_b_re[l], ssm_b_im[l],
                       ssm_c_re[l], ssm_c_im[l], max(steps_p, ts))
    wr_pad = jnp.zeros((d, V7X_LANES), F32).at[:, :N_EXPERTS].set(w_router[l]).astype(BF16)
    br_pad = jnp.full((1, V7X_LANES), MASKED, F32).at[0, :N_EXPERTS].set(b_router[l])

    xp = x_prompt.reshape(t, d)
    dils = tuple(dil for _, dil in DILATED_BRANCHES)
    wide = tuple(dil for dil in dils if dil > 1)
    proj_p = _inproj(xp, mod_p[0], mod_p[1], g_norm1[l], w_in[l], jnp.asarray(perm_p, BF16), TOKEN_TILE, wide)
    kpf, vpf, up = proj_p[3:6]
    views = {1: proj_p[0:3]}
    for n_d, dil in enumerate(wide):
        views[dil] = proj_p[6 + 3 * n_d:9 + 3 * n_d]
    outs = [_attn_branch(*views[dil], dil) for dil in dils]
    ap = _attn_combine([o for o, _ in outs], [s for _, s in outs], g_out_attn[l], dils)
    zeros_h = jnp.zeros((SSM_ROWS_PROMPT, 2 * N_STATE), F32)
    zp, hp = _ssm(up, zeros_h, tabs, ssm_d[l], w_glu[l], b_glu[l], g_out_ssm[l],
                  jnp.asarray(perm_p.T, BF16), SSM_ROWS_PROMPT, steps_p, True)
    n_blocks_p = t // TOKEN_TILE
    n_blocks = n_blocks_p + 1
    x1p, xs_all, slot_p, gate_p, meta_p = _outproj(xp, ap, zp, mod_p[2], mod_p[3], mod_p[4], g_norm2[l], w_out[l],
                                                   wr_pad, br_pad, TOKEN_TILE, n_blocks, 0)

    xs = x_sample.reshape(ns, d)
    qs, ks, vs, ksf, vsf, us = _inproj(xs, mod_s[0], mod_s[1], g_norm1[l], w_in[l], jnp.asarray(perm_s, BF16), ns)
    by_head = lambda z: z.reshape(bs, ts, N_HEADS, HEAD_DIM).transpose(0, 2, 1, 3).reshape(bs, N_HEADS * ts, HEAD_DIM)
    as_ = _attn_sample(by_head(qs), ksf.reshape(bs, ts * N_HEADS, HEAD_DIM), vsf.reshape(bs, ts * N_HEADS, HEAD_DIM),
                       cache_k_win[l], cache_v_win[l], g_out_attn[l])
    as_ = as_.reshape(bs, N_HEADS, ts, HEAD_DIM).transpose(0, 2, 1, 3)
    h0s = jnp.concatenate([state_ssm_re[l].reshape(bs, N_STATE), state_ssm_im[l].reshape(bs, N_STATE)], axis=1)
    zs, hs = _ssm(us, h0s, tabs, ssm_d[l], w_glu[l], b_glu[l], g_out_ssm[l],
                  jnp.asarray(perm_s.T, BF16), bs, ts, False)
    x1s, xs_all, slot_s, gate_s, meta_s = _outproj(xs, as_.reshape(ns, a), zs, mod_s[2], mod_s[3], mod_s[4],
                                                   g_norm2[l], w_out[l], wr_pad, br_pad, ns, n_blocks, n_blocks_p,
                                                   xs_prev=xs_all)

    meta = jnp.concatenate([meta_p, meta_s], axis=0)
    plan = _moe_plan(meta[:, 0, :N_EXPERTS], meta[:, 1, :N_EXPERTS])
    ys_all = _moe(plan, xs_all, w_gate_up[l], b_gate_up[l], w_down[l], b_down[l])

    y_prompt = _final(x1p, ys_all, slot_p, gate_p, mod_p[5], g_final, TOKEN_TILE, 0).reshape(bp, t, d)
    y_sample = _final(x1s, ys_all, slot_s, gate_s, mod_s[5], g_final, ns, n_blocks_p).reshape(bs, ts, d)

    keep = min(MAX_WINDOW, t)
    shp = (1, bp, keep, N_HEADS, HEAD_DIM)
    k_win = kpf[t - keep:].reshape(shp)
    v_win = vpf[t - keep:].reshape(shp)
    st = (1, bp, SSM_GROUPS, SSM_STATE)
    hp_last = hp[SSM_ROWS_PROMPT - 1]
    ss = (1, bs, SSM_GROUPS, SSM_STATE)
    return (y_prompt, y_sample, k_win, v_win,
            hp_last[:N_STATE].reshape(st), hp_last[N_STATE:].reshape(st),
            ksf.reshape(1, bs, ts, N_HEADS, HEAD_DIM), vsf.reshape(1, bs, ts, N_HEADS, HEAD_DIM),
            hs[:, :N_STATE].reshape(ss), hs[:, N_STATE:].reshape(ss))
```

```python
import functools

import numpy as np
import jax
import jax.numpy as jnp
from jax import lax
from jax.experimental import pallas as pl
from jax.experimental.pallas import tpu as pltpu

F32 = jnp.float32
BF16 = jnp.bfloat16

D_MODEL = 1024
N_HEADS = 8
HEAD_DIM = 64
ATTN_WIDTH = N_HEADS * HEAD_DIM
DILATED_BRANCHES = ((128, 1), (512, 4), (2048, 16))
KEYS_PER_BRANCH = 129
MAX_WINDOW = 2048
SSM_WIDTH = D_MODEL - ATTN_WIDTH
SSM_CH = 16
SSM_GROUPS = SSM_WIDTH // SSM_CH
SSM_STATE = 64
N_STATE = SSM_GROUPS * SSM_STATE
N_EXPERTS = 32
TOP_K = 4
D_EXPERT = D_MODEL
SWIGLU_LIMIT = 7.0
SWIGLU_ALPHA = 1.702
N_MOD = 6
EPS = 1e-6
MASKED = -1e30

V7X_LANES = 128
V7X_SUBLANES = 8
V7X_VMEM_LIMIT_BYTES = 56 * 1024 * 1024

TOKEN_TILE = 512
Q_TILE = 128
Q_BLOCKS_PER_STEP = 4
SSM_ROWS_PROMPT = 8
MOE_TILE = 512
ROW_ALIGN = 16
GROUP_CAP = -(-(TOKEN_TILE * TOP_K + N_EXPERTS * (ROW_ALIGN - 1)) // (2 * V7X_LANES)) * (2 * V7X_LANES)


def _cparams(n_axes=1):
    return pltpu.CompilerParams(
        dimension_semantics=("arbitrary",) * n_axes,
        vmem_limit_bytes=V7X_VMEM_LIMIT_BYTES,
    )


def _full(shape):
    n = len(shape)
    return pl.BlockSpec(shape, lambda *_: (0,) * n)


def _rms(x, g):
    return x * lax.rsqrt(jnp.mean(x * x, axis=-1, keepdims=True) + EPS) * g


def _sigmoid(x):
    return 1.0 / (1.0 + jnp.exp(-x))


def _ada_kernel(c_ref, w_ref, b_ref, o_ref):
    c = c_ref[...]
    s = (c * _sigmoid(c)).astype(BF16)
    o_ref[...] = jnp.dot(s, w_ref[...].astype(BF16), preferred_element_type=F32) + b_ref[...]


def _ada_modulation(c_rows, w_ada, b_ada):
    m, d = c_rows.shape
    n = w_ada.shape[1]
    tn = n // 4
    return pl.pallas_call(
        _ada_kernel,
        out_shape=jax.ShapeDtypeStruct((m, n), F32),
        grid=(n // tn,),
        in_specs=[_full((m, d)),
                  pl.BlockSpec((d, tn), lambda j: (0, j)),
                  pl.BlockSpec((1, tn), lambda j: (0, j))],
        out_specs=pl.BlockSpec((m, tn), lambda j: (0, j)),
        compiler_params=_cparams(),
        name="ada_modulation",
    )(c_rows, w_ada, b_ada.reshape(1, n))


def _inproj_kernel(*refs, dils):
    x_ref, sh_ref, sc_ref, g_ref, w_ref, perm_ref = refs[:6]
    dperm_refs = refs[6:6 + len(dils)]
    q_ref, k_ref, v_ref, kf_ref, vf_ref, u_ref = refs[6 + len(dils):12 + len(dils)]
    dil_refs = refs[12 + len(dils):-1]
    wbf_ref = refs[-1]

    @pl.when(pl.program_id(0) == 0)
    def _():
        wbf_ref[...] = w_ref[...].astype(BF16)

    h = _rms(x_ref[...], g_ref[...]) * (1.0 + sc_ref[...]) + sh_ref[...]
    hb = h.astype(BF16)
    a = ATTN_WIDTH
    proj = jnp.dot(hb, wbf_ref[:, :3 * a], preferred_element_type=F32)
    k = proj[:, a:2 * a]
    v = proj[:, 2 * a:]
    qkv = jnp.concatenate([(proj[:, :a] * (HEAD_DIM ** -0.5)).astype(BF16), k.astype(BF16), v.astype(BF16)],
                          axis=1)
    q_ref[...] = qkv[:, :a]
    k_ref[...] = qkv[:, a:2 * a]
    v_ref[...] = qkv[:, 2 * a:]
    kf_ref[...] = k
    vf_ref[...] = v
    hp = jnp.dot(perm_ref[...], hb, preferred_element_type=F32).astype(BF16)
    u_ref[...] = jnp.dot(hp, wbf_ref[:, 3 * a:], preferred_element_type=F32)
    tm = qkv.shape[0]
    for n_d, dil in enumerate(dils):
        by_residue = jnp.dot(dperm_refs[n_d][...], qkv, preferred_element_type=F32).astype(BF16)
        per = tm // dil
        for r in range(dil):
            rows = by_residue[r * per:(r + 1) * per]
            for j in range(3):
                dil_refs[3 * n_d + j][:, r * a:(r + 1) * a] = rows[:, j * a:(j + 1) * a]


def _inproj(x, sh, sc, g, w_in, perm, tm, dils=()):
    n, d = x.shape
    a = ATTN_WIDTH
    mod_rows = sh.shape[0]
    mod_spec = (pl.BlockSpec((1, d), lambda i: (0, 0)) if mod_rows == 1
                else pl.BlockSpec((tm, d), lambda i: (i, 0)))
    row = lambda w: pl.BlockSpec((tm, w), lambda i: (i, 0))
    dperms = [jnp.asarray(_chunk_perm(tm // dil, dil), BF16) for dil in dils]
    view_shapes = tuple(jax.ShapeDtypeStruct((n // dil, dil * a), BF16) for dil in dils for _ in range(3))
    view_specs = tuple(pl.BlockSpec((tm // dil, dil * a), lambda i: (i, 0)) for dil in dils for _ in range(3))
    return pl.pallas_call(
        functools.partial(_inproj_kernel, dils=tuple(dils)),
        out_shape=(jax.ShapeDtypeStruct((n, a), BF16),) * 3
        + (jax.ShapeDtypeStruct((n, a), F32),) * 2
        + (jax.ShapeDtypeStruct((n, SSM_WIDTH), F32),) + view_shapes,
        grid=(n // tm,),
        in_specs=[row(d), mod_spec, mod_spec, _full((1, d)), _full(w_in.shape), _full((tm, tm))]
        + [_full((tm, tm))] * len(dils),
        out_specs=(row(a),) * 5 + (row(SSM_WIDTH),) + view_specs,
        scratch_shapes=[pltpu.VMEM(w_in.shape, BF16)],
        compiler_params=_cparams(),
        name="inproj",
    )(x, sh, sc, g.reshape(1, d), w_in, perm, *dperms)


def _chunk_perm(rows, steps):
    n = rows * steps
    p = np.zeros((n, n), np.float32)
    c, t = np.meshgrid(np.arange(rows), np.arange(steps), indexing="ij")
    p[(t * rows + c).ravel(), (c * steps + t).ravel()] = 1.0
    return p


def _alibi_slopes():
    return np.exp2(-8.0 * np.arange(1, N_HEADS + 1, dtype=np.float64) / N_HEADS).astype(np.float32)


def _branch_bias(dil):
    qi = np.arange(Q_TILE)[:, None]
    col = np.arange(2 * Q_TILE)[None, :]
    j = Q_TILE + qi - col
    valid = (j >= 0) & (j <= Q_TILE)
    dist = (j * dil).astype(np.float32)
    tabs = []
    for first in (True, False):
        ok = valid & (col >= Q_TILE) if first else valid
        per_head = [np.where(ok, -s * dist, np.float32(MASKED)) for s in _alibi_slopes()]
        tabs.append(np.concatenate(per_head, axis=0))
    return np.stack(tabs).astype(np.float32)


def _attn_branch_kernel(q_ref, kp_ref, kc_ref, vp_ref, vc_ref, bias_ref, o_ref, lse_ref):
    first_step = pl.program_id(0) == 0
    lane = lax.broadcasted_iota(jnp.int32, (Q_TILE, V7X_LANES), 1)
    lo = lane < HEAD_DIM
    for j in range(Q_BLOCKS_PER_STEP):
        rows = slice(j * Q_TILE, (j + 1) * Q_TILE)
        before = slice((j - 1) * Q_TILE, j * Q_TILE)
        sel = jnp.where(first_step, 0, 1) if j == 0 else 1
        lse_acc = jnp.zeros((Q_TILE, V7X_LANES), F32)
        for p in range(N_HEADS // 2):
            cs = slice(V7X_LANES * p, V7X_LANES * (p + 1))
            q2 = q_ref[rows, cs]
            zero = jnp.zeros_like(q2)
            qq = jnp.concatenate([jnp.where(lo, q2, zero), jnp.where(lo, zero, q2)], axis=0)
            k_before = kp_ref[:, cs] if j == 0 else kc_ref[before, cs]
            v_before = vp_ref[:, cs] if j == 0 else vc_ref[before, cs]
            kk = jnp.concatenate([k_before, kc_ref[rows, cs]], axis=0)
            vv = jnp.concatenate([v_before, vc_ref[rows, cs]], axis=0)
            s = lax.dot_general(qq, kk, (((1,), (1,)), ((), ())), preferred_element_type=F32)
            s = s + bias_ref[sel, 2 * Q_TILE * p:2 * Q_TILE * (p + 1), :]
            m = jnp.max(s, axis=1, keepdims=True)
            e = jnp.exp(s - m)
            l = jnp.sum(e, axis=1, keepdims=True)
            eb = e.astype(BF16)
            o0 = jnp.dot(eb[:Q_TILE], vv, preferred_element_type=F32) * (1.0 / l[:Q_TILE])
            o1 = jnp.dot(eb[Q_TILE:], vv, preferred_element_type=F32) * (1.0 / l[Q_TILE:])
            o_ref[rows, cs] = jnp.where(lo, o0, o1).astype(o_ref.dtype)
            lse = m + jnp.log(l)
            lse_acc = jnp.where(lane == 2 * p, lse[:Q_TILE], lse_acc)
            lse_acc = jnp.where(lane == 2 * p + 1, lse[Q_TILE:], lse_acc)
        lse_ref[rows, :] = lse_acc[:, :N_HEADS]


def _attn_branch(qv, kv, vv, dil):
    a = ATTN_WIDTH
    rows = qv.shape[0]
    t = rows * dil
    step = Q_BLOCKS_PER_STEP * Q_TILE
    cur = pl.BlockSpec((step, a), lambda i, r: (i, r))
    prev = pl.BlockSpec((Q_TILE, a), lambda i, r: (jnp.maximum(i * Q_BLOCKS_PER_STEP - 1, 0), r))
    bias = jnp.asarray(_branch_bias(dil))
    o, lse = pl.pallas_call(
        _attn_branch_kernel,
        out_shape=(jax.ShapeDtypeStruct((rows, dil * a), BF16),
                   jax.ShapeDtypeStruct((dil, rows, N_HEADS), F32)),
        grid=(rows // step, dil),
        in_specs=[cur, prev, cur, prev, cur, _full(bias.shape)],
        out_specs=(cur, pl.BlockSpec((None, step, N_HEADS), lambda i, r: (r, i, 0))),
        compiler_params=_cparams(2),
        name=f"attn_branch_d{dil}",
    )(qv, kv, kv, vv, vv, bias)
    return o, lse.transpose(1, 0, 2).reshape(t, N_HEADS)


def _attn_combine_kernel(*refs, dils):
    nb = len(dils)
    o_refs = refs[:nb]
    l_refs = refs[nb:2 * nb]
    g_ref = refs[2 * nb]
    unperm_refs = refs[2 * nb + 1:-1]
    a_ref = refs[-1]
    tq, a = a_ref.shape
    outs = []
    n_u = 0
    for o_ref, dil in zip(o_refs, dils):
        if dil == 1:
            outs.append(o_ref[...].astype(F32))
            continue
        by_residue = jnp.concatenate([o_ref[:, r * a:(r + 1) * a] for r in range(dil)], axis=0)
        outs.append(jnp.dot(unperm_refs[n_u][...], by_residue, preferred_element_type=F32))
        n_u += 1
    ls = [l_ref[...] for l_ref in l_refs]
    top = functools.reduce(jnp.maximum, ls)
    ws = [jnp.exp(l - top) for l in ls]
    inv = 1.0 / functools.reduce(jnp.add, ws)
    cs = [w * inv for w in ws]
    lane = lax.broadcasted_iota(jnp.int32, (tq, V7X_LANES), 1)
    lo = lane < HEAD_DIM
    cols = []
    for p in range(N_HEADS // 2):
        sl = slice(V7X_LANES * p, V7X_LANES * (p + 1))
        acc = jnp.zeros((tq, V7X_LANES), F32)
        for c, o in zip(cs, outs):
            cexp = jnp.where(lo,
                             jnp.broadcast_to(c[:, 2 * p:2 * p + 1], (tq, V7X_LANES)),
                             jnp.broadcast_to(c[:, 2 * p + 1:2 * p + 2], (tq, V7X_LANES)))
            acc = acc + cexp * o[:, sl]
        cols.append(acc)
    o = jnp.concatenate(cols, axis=1)
    a_ref[...] = _rms(o, g_ref[...]).astype(a_ref.dtype)


def _attn_combine(os_, lses, g, dils):
    t = lses[0].shape[0]
    a = ATTN_WIDTH
    tq = TOKEN_TILE
    unperms = [jnp.asarray(_chunk_perm(tq // dil, dil).T, BF16) for dil in dils if dil > 1]
    return pl.pallas_call(
        functools.partial(_attn_combine_kernel, dils=tuple(dils)),
        out_shape=jax.ShapeDtypeStruct((t, a), BF16),
        grid=(t // tq,),
        in_specs=[pl.BlockSpec((tq // dil, dil * a), lambda i: (i, 0)) for dil in dils]
        + [pl.BlockSpec((tq, N_HEADS), lambda i: (i, 0))] * len(dils) + [_full((1, a))]
        + [_full((tq, tq))] * len(unperms),
        out_specs=pl.BlockSpec((tq, a), lambda i: (i, 0)),
        compiler_params=_cparams(),
        name="attn_combine",
    )(*os_, *lses, g.reshape(1, a), *unperms)


NEW_KEY_PAD = V7X_LANES


def _sample_key_plan(win, steps):
    g = max(d for _, d in DILATED_BRANCHES)
    near = max(w for w, d in DILATED_BRANCHES if d < g)
    assert win % g == 0 and 2 * steps == g and near % g == 0 and near <= win
    a0 = (win - near) // g
    assert a0 % (win // g - a0) == 0
    return g, a0


def _sample_bias(win, steps):
    g, a0 = _sample_key_plan(win, steps)
    heads = N_HEADS
    slopes = _alibi_slopes()
    pos_a = (np.arange(win // g)[:, None] * g + np.arange(steps)[None, :]).ravel()
    pos_b = (np.arange(a0, win // g)[:, None] * g + np.arange(steps, g)[None, :]).ravel()
    pos_new = win + np.arange(steps)
    qh = np.repeat(np.arange(heads), steps)[:, None]
    qt = np.tile(np.arange(steps), heads)[:, None]

    def table(pos_list, window, dil):
        pos = np.concatenate(pos_list)
        pad = NEW_KEY_PAD - steps * heads
        p = np.concatenate([np.repeat(pos, heads), np.full(pad, -1)])[None, :]
        h = np.concatenate([np.tile(np.arange(heads), len(pos)), np.full(pad, -1)])[None, :]
        dist = (win + qt) - p
        ok = (h == qh) & (p >= 0) & (dist >= 0) & (dist <= window) & (dist % dil == 0)
        return np.where(ok, -slopes[qh] * dist.astype(np.float32), np.float32(MASKED)).astype(np.float32)

    wide = [table([pos_a, pos_new], w, d) for w, d in DILATED_BRANCHES if d == g]
    near = [table([pos_a[a0 * steps:], pos_b, pos_new], w, d) for w, d in DILATED_BRANCHES if d < g]
    return wide[0], np.stack(near)


def _attn_sample_kernel(q_ref, kn_ref, vn_ref, ck_ref, cv_ref, bw_ref, bn_ref, g_ref, o_ref, *, grp, a0):
    nq, dh = q_ref.shape
    steps = nq // N_HEADS
    win = ck_ref.shape[0]

    def kept(ref):
        grid5 = ref[...].reshape(win // grp, grp, N_HEADS, dh)
        low = grid5[:, :steps].reshape(-1, dh).astype(BF16)
        high = grid5[a0:, steps:].reshape(-1, dh).astype(BF16)
        return low, high

    ka, kb = kept(ck_ref)
    va, vb = kept(cv_ref)
    pad = jnp.zeros((NEW_KEY_PAD - kn_ref.shape[0], dh), F32)
    kn = jnp.concatenate([kn_ref[...], pad], axis=0).astype(BF16)
    vn = jnp.concatenate([vn_ref[...], pad], axis=0).astype(BF16)
    q = q_ref[...]
    scores = lambda k: lax.dot_general(q, k, (((1,), (1,)), ((), ())), preferred_element_type=F32)
    s_a, s_b, s_n = scores(ka), scores(kb), scores(kn)
    n_tail = kb.shape[0]

    def soft(s):
        m = jnp.max(s, axis=1, keepdims=True)
        e = jnp.exp(s - m)
        return m, jnp.sum(e, axis=1, keepdims=True), e.astype(BF16)

    m_w, l_w, e_w = soft(jnp.concatenate([s_a, s_n], axis=1) + bw_ref[...])
    o_w = jnp.dot(e_w, jnp.concatenate([va, vn], axis=0), preferred_element_type=F32)
    s_near = jnp.concatenate([s_a[:, s_a.shape[1] - n_tail:], s_b, s_n], axis=1)
    v_near = jnp.concatenate([va[va.shape[0] - n_tail:], vb, vn], axis=0)
    parts = [soft(s_near + bn_ref[i]) for i in range(bn_ref.shape[0])]
    o_near = jnp.dot(jnp.concatenate([e for _, _, e in parts], axis=0), v_near, preferred_element_type=F32)
    ms = [m_w] + [m for m, _, _ in parts]
    ls = [l_w] + [l for _, l, _ in parts]
    os_ = [o_w] + [o_near[i * nq:(i + 1) * nq] for i in range(len(parts))]
    top = functools.reduce(jnp.maximum, ms)
    num = jnp.zeros((nq, dh), F32)
    den = jnp.zeros((nq, 1), F32)
    for m, l, o in zip(ms, ls, os_):
        w = jnp.exp(m - top)
        num = num + o * w
        den = den + l * w
    res = num * (1.0 / den)
    sq = jnp.sum(res * res, axis=1, keepdims=True)
    tot = sq[0:steps]
    for h in range(1, N_HEADS):
        tot = tot + sq[h * steps:(h + 1) * steps]
    inv = lax.rsqrt(tot * (1.0 / ATTN_WIDTH) + EPS)
    o_ref[...] = res * jnp.concatenate([inv] * N_HEADS, axis=0) * g_ref[...]


def _attn_sample(q, kn, vn, cache_k, cache_v, layer, g):
    b, nq, dh = q.shape
    steps = nq // N_HEADS
    win = cache_k.shape[2]
    grp, a0 = _sample_key_plan(win, steps)
    wide, near = (jnp.asarray(z) for z in _sample_bias(win, steps))
    cache = pl.BlockSpec((None, None, win, N_HEADS, dh), lambda i: (layer, i, 0, 0, 0))
    new = pl.BlockSpec((None, nq, dh), lambda i: (i, 0, 0))
    once = lambda shape: pl.BlockSpec(shape, lambda i: (0,) * len(shape), pipeline_mode=pl.Buffered(1))
    g_rows = jnp.repeat(g.reshape(N_HEADS, dh), steps, axis=0)
    return pl.pallas_call(
        functools.partial(_attn_sample_kernel, grp=grp, a0=a0),
        out_shape=jax.ShapeDtypeStruct((b, nq, dh), F32),
        grid=(b,),
        in_specs=[new, new, new, cache, cache, once(wide.shape), once(near.shape), _full((nq, dh))],
        out_specs=new,
        compiler_params=_cparams(),
        name="attn_sample",
    )(q, kn, vn, cache_k, cache_v, wide, near, g_rows)


def _gelu_tanh(x):
    return 0.5 * x * (1.0 + jnp.tanh(np.sqrt(2.0 / np.pi).astype(np.float32) * (x + 0.044715 * (x * x * x))))


def _ssm_kernel(u_ref, bb_ref, lam_ref, pow_ref, cm_ref, dsk_ref, wglu_ref, bglu_ref, g_ref, pt_ref, h0_ref,
                z_ref, ht_ref, h_s, hin_s, carry_s, *, rows, steps, chain):
    ns = N_STATE
    cw = (4 * V7X_SUBLANES * V7X_LANES) // rows
    last = (steps - 1) * rows

    @pl.when(pl.program_id(0) == 0)
    def _():
        carry_s[...] = h0_ref[0:1, :]

    ub = u_ref[...].astype(BF16)
    n_slabs = SSM_WIDTH // V7X_LANES
    sw = ns // n_slabs
    for s in range(n_slabs):
        part = jnp.dot(ub[:, s * V7X_LANES:(s + 1) * V7X_LANES], bb_ref[s], preferred_element_type=F32)
        h_s[:, s * sw:(s + 1) * sw] = part[:, :sw]
        h_s[:, ns + s * sw:ns + (s + 1) * sw] = part[:, sw:]

    for cc in range(0, ns // cw, 2):
        crs = [slice(c * cw, (c + 1) * cw) for c in (cc, cc + 1)]
        cis = [slice(ns + c * cw, ns + (c + 1) * cw) for c in (cc, cc + 1)]
        lrs = [jnp.broadcast_to(lam_ref[0:1, cr], (rows, cw)) for cr in crs]
        lis = [jnp.broadcast_to(lam_ref[1:2, cr], (rows, cw)) for cr in crs]

        def scan_body(t, carry, crs=crs, cis=cis, lrs=lrs, lis=lis):
            rs = pl.ds(pl.multiple_of(t * rows, rows), rows)
            out = []
            for j in range(2):
                hr, hi = carry[2 * j], carry[2 * j + 1]
                nr = lrs[j] * hr - lis[j] * hi + h_s[rs, crs[j]]
                ni = lrs[j] * hi + lis[j] * hr + h_s[rs, cis[j]]
                h_s[rs, crs[j]] = nr
                h_s[rs, cis[j]] = ni
                out += [nr, ni]
            return tuple(out)

        zero = jnp.zeros((rows, cw), F32)
        lax.fori_loop(0, steps, scan_body, (zero,) * 4)

    if chain:
        ptr = pow_ref[steps - 1, 0:1, :ns]
        pti = pow_ref[steps - 1, 0:1, ns:]
        carry = carry_s[...]
        for c in range(rows):
            hin_s[c:c + 1, :] = carry
            cr_, ci_ = carry[:, :ns], carry[:, ns:]
            e = h_s[last + c:last + c + 1, :]
            carry = jnp.concatenate([ptr * cr_ - pti * ci_ + e[:, :ns],
                                     ptr * ci_ + pti * cr_ + e[:, ns:]], axis=1)
        carry_s[...] = carry
    else:
        hin_s[...] = h0_ref[...]

    for cc in range(ns // cw):
        cr = slice(cc * cw, (cc + 1) * cw)
        ci = slice(ns + cc * cw, ns + (cc + 1) * cw)
        hr0 = hin_s[:, cr]
        hi0 = hin_s[:, ci]

        def fix_body(t, _, cr=cr, ci=ci, hr0=hr0, hi0=hi0):
            rs = pl.ds(pl.multiple_of(t * rows, rows), rows)
            pr = pow_ref[t, :, cr]
            pi_ = pow_ref[t, :, ci]
            h_s[rs, cr] = h_s[rs, cr] + (pr * hr0 - pi_ * hi0)
            h_s[rs, ci] = h_s[rs, ci] + (pr * hi0 + pi_ * hr0)
            return 0

        lax.fori_loop(0, steps, fix_body, 0, unroll=4)

    ht_ref[...] = h_s[last:last + rows, :]
    ys = []
    for s in range(n_slabs):
        hs = jnp.concatenate([h_s[:, s * sw:(s + 1) * sw], h_s[:, ns + s * sw:ns + (s + 1) * sw]], axis=1)
        ys.append(jnp.dot(hs.astype(BF16), cm_ref[s], preferred_element_type=F32))
    y = jnp.concatenate(ys, axis=1) + dsk_ref[...] * u_ref[...]
    y = _gelu_tanh(y)
    gl = jnp.dot(y.astype(BF16), wglu_ref[...], preferred_element_type=F32) + bglu_ref[...]
    z = _rms(y * _sigmoid(gl), g_ref[...]).astype(BF16)
    z_ref[...] = jnp.dot(pt_ref[...], z, preferred_element_type=F32).astype(z_ref.dtype)


def _ssm_tables(ssm_a_re, ssm_a_im, ssm_log_dt, ssm_b_re, ssm_b_im, ssm_c_re, ssm_c_im, max_steps):
    g, n, ch = SSM_GROUPS, SSM_STATE, SSM_CH
    a_re = ssm_a_re.astype(F32)
    a_im = ssm_a_im.astype(F32)
    dt = jnp.exp(ssm_log_dt.astype(F32))[:, None]
    mag = jnp.exp(dt * a_re)
    lam_re = mag * jnp.cos(dt * a_im)
    lam_im = mag * jnp.sin(dt * a_im)
    nr = lam_re - 1.0
    ni = lam_im
    inv = 1.0 / (a_re * a_re + a_im * a_im)
    coef_re = (nr * a_re + ni * a_im) * inv
    coef_im = (ni * a_re - nr * a_im) * inv
    br = ssm_b_re.astype(F32)
    bi = ssm_b_im.astype(F32)
    bb_re = coef_re[..., None] * br - coef_im[..., None] * bi
    bb_im = coef_re[..., None] * bi + coef_im[..., None] * br
    gs = V7X_LANES // ch
    ns_ = g // gs
    eye = jnp.eye(gs, dtype=F32)
    bmat = lambda b: jnp.einsum("sgnc,gh->sgchn", b.reshape(ns_, gs, n, ch), eye).reshape(ns_, gs * ch, gs * n)
    cmat = lambda c: jnp.einsum("sgcn,gh->sgnhc", c.astype(F32).reshape(ns_, gs, ch, n), eye).reshape(
        ns_, gs * n, gs * ch)
    bb = jnp.concatenate([bmat(bb_re), bmat(bb_im)], axis=2).astype(BF16)
    cm = jnp.concatenate([cmat(ssm_c_re), -cmat(ssm_c_im)], axis=1).astype(BF16)
    lam = jnp.stack([lam_re.reshape(-1), lam_im.reshape(-1)])

    k = jnp.arange(1, max_steps + 1, dtype=F32)[:, None]
    kdt = k * dt.reshape(1, -1).repeat(n, axis=1)
    pmag = jnp.exp(kdt * a_re.reshape(1, -1))
    parg = kdt * a_im.reshape(1, -1)
    pows = jnp.concatenate([pmag * jnp.cos(parg), pmag * jnp.sin(parg)], axis=1)
    return bb, cm, lam, pows


def _ssm(u_perm, h0, tabs, dsk, w_glu, b_glu, g, perm_t, rows, steps, chain):
    n, w = u_perm.shape
    blk = rows * steps
    bb, cm, lam, pows = tabs
    row = pl.BlockSpec((blk, w), lambda i: (i, 0))
    kern = functools.partial(_ssm_kernel, rows=rows, steps=steps, chain=chain)
    return pl.pallas_call(
        kern,
        out_shape=(jax.ShapeDtypeStruct((n, w), BF16), jax.ShapeDtypeStruct((rows, 2 * N_STATE), F32)),
        grid=(n // blk,),
        in_specs=[row, _full(bb.shape), _full(lam.shape), _full((steps, rows, 2 * N_STATE)), _full(cm.shape),
                  _full((1, w)), _full((w, w)), _full((1, w)), _full((1, w)), _full((blk, blk)),
                  _full((rows, 2 * N_STATE))],
        out_specs=(row, _full((rows, 2 * N_STATE))),
        scratch_shapes=[pltpu.VMEM((blk, 2 * N_STATE), F32),
                        pltpu.VMEM((rows, 2 * N_STATE), F32),
                        pltpu.VMEM((1, 2 * N_STATE), F32)],
        compiler_params=_cparams(),
        name=f"ssm_r{rows}",
    )(u_perm, bb, lam, jnp.broadcast_to(pows[:steps, None, :], (steps, rows, 2 * N_STATE)), cm,
      dsk.reshape(1, w), w_glu.astype(BF16), b_glu.reshape(1, w),
      g.reshape(1, w), perm_t, h0)


def _outproj_kernel(*refs, n_real, aliased):
    ins, outs = refs[:12], refs[12 + aliased:]
    step = pl.program_id(0)

    @pl.when(step < n_real)
    def _():
        _outproj_tile(*ins, *outs)

    @pl.when(step >= n_real)
    def _():
        xs_ref = outs[1]
        xs_ref[...] = jnp.zeros(xs_ref.shape, xs_ref.dtype)


def _outproj_tile(x_ref, a_ref, z_ref, gt_ref, sh_ref, sc_ref, g_ref, wo_ref, wr_ref, br_ref, ltri_ref, utri_ref,
                  x1_ref, xs_ref, slot_ref, gates_ref, meta_ref, wbf_ref):
    @pl.when(pl.program_id(0) == 0)
    def _():
        wbf_ref[...] = wo_ref[...].astype(BF16)

    a = ATTN_WIDTH
    mixed = (jnp.dot(a_ref[...].astype(BF16), wbf_ref[:a, :], preferred_element_type=F32)
             + jnp.dot(z_ref[...], wbf_ref[a:, :], preferred_element_type=F32))
    x1 = x_ref[...] + gt_ref[...] * mixed
    x1_ref[...] = x1
    h2 = (_rms(x1, g_ref[...]) * (1.0 + sc_ref[...]) + sh_ref[...]).astype(BF16)
    lg = jnp.dot(h2, wr_ref[...], preferred_element_type=F32) + br_ref[...]
    tm = lg.shape[0]
    lane = lax.broadcasted_iota(jnp.int32, (tm, V7X_LANES), 1).astype(F32)
    vals, hots = [], []
    for _ in range(TOP_K):
        m = jnp.max(lg, axis=1, keepdims=True)
        idx = jnp.min(jnp.where(lg == m, lane, float(V7X_LANES)), axis=1, keepdims=True)
        hot = lane == idx
        vals.append(m)
        hots.append(jnp.where(hot, 1.0, 0.0))
        lg = jnp.where(hot, MASKED * 2, lg)
    es = [jnp.exp(v - vals[0]) for v in vals]
    inv = 1.0 / (es[0] + es[1] + es[2] + es[3])

    member = hots[0] + hots[1] + hots[2] + hots[3]
    before = jnp.dot(ltri_ref[...], member.astype(BF16), preferred_element_type=F32)
    count = jnp.sum(member, axis=0, keepdims=True)
    padded = jnp.floor((count + (ROW_ALIGN - 1.0)) * (1.0 / ROW_ALIGN)) * ROW_ALIGN
    padded8 = jnp.broadcast_to(padded, (V7X_SUBLANES, V7X_LANES))
    start = jnp.dot(padded8.astype(BF16), utri_ref[...], preferred_element_type=F32)[0:1]
    where_to = start + before
    slots = jnp.zeros((tm, V7X_LANES), F32)
    gates = jnp.zeros((tm, V7X_LANES), F32)
    for k in range(TOP_K):
        slot_k = jnp.sum(hots[k] * where_to, axis=1, keepdims=True)
        slots = jnp.where(lane == float(k), slot_k, slots)
        gates = jnp.where(lane == float(k), es[k] * inv, gates)
    slot_ref[...] = slots
    gates_ref[...] = gates
    row = lax.broadcasted_iota(jnp.int32, (V7X_SUBLANES, V7X_LANES), 0)
    meta_ref[...] = jnp.where(row == 0, padded8, jnp.where(row == 1, jnp.broadcast_to(start, padded8.shape), 0.0))

    cap = xs_ref.shape[0]
    slots_t = jnp.transpose(slots)
    srow = lax.broadcasted_iota(jnp.int32, (cap, tm), 0).astype(F32)
    place = jnp.zeros((cap, tm), F32)
    for k in range(TOP_K):
        place = jnp.where(srow == slots_t[k:k + 1, :], 1.0, place)
    xs_ref[...] = jnp.dot(place.astype(BF16), h2, preferred_element_type=F32).astype(BF16)


def _outproj(x, a, z, gt, sh, sc, g, w_out, wr_pad, br_pad, tm, n_blocks, block0, xs_prev=None):
    n, d = x.shape
    aw = ATTN_WIDTH
    n_real = n // tm
    aliased = xs_prev is not None
    n_steps = n_real if aliased else n_blocks
    tile = lambda i: jnp.minimum(i, n_real - 1)
    mod_spec = (pl.BlockSpec((1, d), lambda i: (0, 0)) if gt.shape[0] == 1
                else pl.BlockSpec((tm, d), lambda i: (tile(i), 0)))
    row = lambda w: pl.BlockSpec((tm, w), lambda i: (tile(i), 0))
    ltri = jnp.asarray(np.tril(np.ones((tm, tm), np.float32), -1), BF16)
    utri = jnp.asarray(np.triu(np.ones((V7X_LANES, V7X_LANES), np.float32), 1), BF16)
    in_specs = [row(d), row(aw), row(SSM_WIDTH), mod_spec, mod_spec, mod_spec, _full((1, d)),
                _full(w_out.shape), _full(wr_pad.shape), _full(br_pad.shape), _full(ltri.shape), _full(utri.shape)]
    args = [x, a, z, gt, sh, sc, g.reshape(1, d), w_out, wr_pad, br_pad, ltri, utri]
    if aliased:
        in_specs.append(pl.BlockSpec(memory_space=pl.ANY))
        args.append(xs_prev)
    return pl.pallas_call(
        functools.partial(_outproj_kernel, n_real=n_real, aliased=int(aliased)),
        out_shape=(jax.ShapeDtypeStruct((n, d), F32),
                   jax.ShapeDtypeStruct((n_blocks, GROUP_CAP, d), BF16),
                   jax.ShapeDtypeStruct((n, V7X_LANES), F32), jax.ShapeDtypeStruct((n, V7X_LANES), F32),
                   jax.ShapeDtypeStruct((n_real, V7X_SUBLANES, V7X_LANES), F32)),
        grid=(n_steps,),
        in_specs=in_specs,
        out_specs=(row(d), pl.BlockSpec((None, GROUP_CAP, d), lambda i: (block0 + i, 0, 0)),
                   row(V7X_LANES), row(V7X_LANES),
                   pl.BlockSpec((None, V7X_SUBLANES, V7X_LANES), lambda i: (tile(i), 0, 0))),
        scratch_shapes=[pltpu.VMEM(w_out.shape, BF16)],
        input_output_aliases={len(args) - 1: 1} if aliased else {},
        compiler_params=_cparams(),
        name="outproj_router",
    )(*args)


_PIECE_SIZES = tuple(MOE_TILE >> s for s in range(6))
_CHUNK_SHIFT = 6
_CHUNK = 1 << _CHUNK_SHIFT
_TAIL_SIZES = tuple(sz for sz in _PIECE_SIZES if sz < _CHUNK)


def _moe_kernel(te_ref, tl_ref, lo_ref, hi_ref, nu_ref, gstart_ref, gsize_ref, gbase_ref, rows_ref, used_ref,
                xs_hbm, wgu_ref, bgu_ref, wd_ref, bd_ref, ys_hbm,
                xbuf, ybuf, zbuf, wgu_bf, wd_bf, xsem, ysem, zsem):
    t = pl.program_id(0)
    n_used = nu_ref[0]
    n_blocks, cap = xs_hbm.shape[0], xs_hbm.shape[1]

    def x_copy(i, br, tr, sz, slot):
        pltpu.make_async_copy(xs_hbm.at[i, pl.ds(br, sz)], xbuf.at[slot, pl.ds(tr, sz)], xsem.at[slot]).start()

    def y_copy(i, br, tr, sz, slot):
        pltpu.make_async_copy(ybuf.at[slot, pl.ds(tr, sz)], ys_hbm.at[i, pl.ds(br, sz)], ysem.at[slot]).start()

    def z_copy(i, row, sz, start):
        cp = pltpu.make_async_copy(zbuf.at[pl.ds(0, sz)], ys_hbm.at[i, pl.ds(row, sz)], zsem)
        cp.start() if start else cp.wait()

    def pieces(tt, fn):
        e = te_ref[tt]
        lo = tl_ref[tt] * MOE_TILE

        def per_block(i, c):
            g = e * n_blocks + i
            s0 = gbase_ref[g]
            a = jnp.maximum(s0, lo)
            b = jnp.minimum(s0 + gsize_ref[g], lo + MOE_TILE)
            length = jnp.maximum(b - a, 0)
            src = gstart_ref[g] + (a - s0)
            dst = a - lo
            whole = lax.shift_right_logical(length, _CHUNK_SHIFT)

            def chunk(j, cc):
                off = j * _CHUNK
                fn(i, pl.multiple_of(src + off, ROW_ALIGN), pl.multiple_of(dst + off, ROW_ALIGN), _CHUNK)
                return cc

            lax.fori_loop(0, whole, chunk, 0)
            done = whole * _CHUNK
            for sz in _TAIL_SIZES:
                hit = (length & sz) != 0

                @pl.when(hit)
                def _(sz=sz, done=done):
                    fn(i, pl.multiple_of(src + done, ROW_ALIGN), pl.multiple_of(dst + done, ROW_ALIGN), sz)

                done = done + jnp.where(hit, sz, 0)
            return c

        lax.fori_loop(lo_ref[tt], hi_ref[tt], per_block, 0)

    def tile_rows(tt):
        return jnp.minimum(rows_ref[te_ref[tt]] - tl_ref[tt] * MOE_TILE, MOE_TILE)

    def wait_rows(n, sem, buf):
        def chunk(j, cc):
            pltpu.make_async_copy(buf.at[pl.ds(0, _CHUNK)], buf.at[pl.ds(0, _CHUNK)], sem).wait()
            return cc

        lax.fori_loop(0, lax.shift_right_logical(n, _CHUNK_SHIFT), chunk, 0)
        for sz in _TAIL_SIZES:
            @pl.when((n & sz) != 0)
            def _(sz=sz):
                pltpu.make_async_copy(buf.at[pl.ds(0, sz)], buf.at[pl.ds(0, sz)], sem).wait()

    def fetch(tt, slot):
        pieces(tt, lambda i, br, tr, sz: x_copy(i, br, tr, sz, slot))

    def writeback(tt, slot):
        pieces(tt, lambda i, br, tr, sz: y_copy(i, br, tr, sz, slot))

    def zero_tail(i, start):
        u = used_ref[i]
        rem = cap - u
        nz = zbuf.shape[0]
        whole = lax.shift_right_logical(rem, nz.bit_length() - 1)

        def chunk(j, c):
            z_copy(i, pl.multiple_of(u + j * nz, ROW_ALIGN), nz, start)
            return c

        lax.fori_loop(0, whole, chunk, 0)
        base = u + whole * nz
        done = jnp.int32(0)
        for sz in _PIECE_SIZES:
            if sz >= nz:
                continue
            hit = (rem & sz) != 0

            @pl.when(hit)
            def _(sz=sz, done=done):
                z_copy(i, pl.multiple_of(base + done, ROW_ALIGN), sz, start)

            done = done + jnp.where(hit, sz, 0)

    @pl.when(t == 0)
    def _():
        xbuf[...] = jnp.zeros(xbuf.shape, xbuf.dtype)
        zbuf[...] = jnp.zeros(zbuf.shape, zbuf.dtype)
        lax.fori_loop(0, n_blocks, lambda i, c: (zero_tail(i, True), c)[1], 0)
        lax.fori_loop(0, n_blocks, lambda i, c: (zero_tail(i, False), c)[1], 0)
        fetch(0, 0)

    @pl.when(t < n_used)
    def _():
        slot = t % 2

        @pl.when(t + 1 < n_used)
        def _():
            fetch(t + 1, 1 - slot)

        prev = te_ref[jnp.maximum(t - 1, 0)]
        fresh = jnp.logical_or(t == 0, te_ref[t] != prev)

        @pl.when(fresh)
        def _():
            wgu_bf[...] = wgu_ref[...].astype(BF16)
            wd_bf[...] = wd_ref[...].astype(BF16)

        wait_rows(tile_rows(t), xsem.at[slot], xbuf.at[slot])

        @pl.when(t >= 2)
        def _():
            wait_rows(tile_rows(t - 2), ysem.at[slot], ybuf.at[slot])

        f = D_EXPERT
        gu = jnp.dot(xbuf[slot], wgu_bf[...], preferred_element_type=F32) + bgu_ref[...]
        gate = jnp.minimum(gu[:, :f], SWIGLU_LIMIT)
        up = jnp.clip(gu[:, f:], -SWIGLU_LIMIT, SWIGLU_LIMIT)
        act = (up + 1.0) * gate * _sigmoid(SWIGLU_ALPHA * gate)
        y = jnp.dot(act.astype(BF16), wd_bf[...], preferred_element_type=F32) + bd_ref[...]
        ybuf[slot] = y.astype(BF16)
        writeback(t, slot)

        @pl.when(t == n_used - 1)
        def _():
            wait_rows(tile_rows(t), ysem.at[slot], ybuf.at[slot])

            @pl.when(t >= 1)
            def _():
                wait_rows(tile_rows(t - 1), ysem.at[1 - slot], ybuf.at[1 - slot])


def _moe(plan, xs, w_gate_up, b_gate_up, w_down, b_down):
    _, cap, d = xs.shape
    e, _, f2 = w_gate_up.shape
    nt = plan[0].shape[0]
    wmap = lambda t, te, *_: (te[t], 0, 0)
    anyspec = pl.BlockSpec(memory_space=pl.ANY)
    grid_spec = pltpu.PrefetchScalarGridSpec(
        num_scalar_prefetch=len(plan),
        grid=(nt,),
        in_specs=[anyspec,
                  pl.BlockSpec((None, d, f2), wmap),
                  pl.BlockSpec((None, 1, f2), wmap),
                  pl.BlockSpec((None, f2 // 2, d), wmap),
                  pl.BlockSpec((None, 1, d), wmap)],
        out_specs=anyspec,
        scratch_shapes=[pltpu.VMEM((2, MOE_TILE, d), BF16), pltpu.VMEM((2, MOE_TILE, d), BF16),
                        pltpu.VMEM((MOE_TILE // 2, d), BF16),
                        pltpu.VMEM((d, f2), BF16), pltpu.VMEM((f2 // 2, d), BF16),
                        pltpu.SemaphoreType.DMA((2,)), pltpu.SemaphoreType.DMA((2,)), pltpu.SemaphoreType.DMA(())],
    )
    return pl.pallas_call(
        _moe_kernel,
        out_shape=jax.ShapeDtypeStruct(xs.shape, BF16),
        grid_spec=grid_spec,
        compiler_params=_cparams(),
        name="moe_experts",
    )(*plan, xs, w_gate_up, b_gate_up.reshape(e, 1, f2), w_down, b_down.reshape(e, 1, d))


def _moe_plan(group_size, group_start):
    n_blocks = group_size.shape[0]
    gsize = group_size.T.astype(jnp.int32)
    gstart = group_start.T.astype(jnp.int32)
    gbase = jnp.cumsum(gsize, axis=1) - gsize
    rows = jnp.sum(gsize, axis=1)
    tiles = (rows + MOE_TILE - 1) // MOE_TILE
    tile_end = jnp.cumsum(tiles)
    n_used = tile_end[-1:]
    nt = (n_blocks * GROUP_CAP) // MOE_TILE + N_EXPERTS
    t = jnp.arange(nt, dtype=jnp.int32)
    te = jnp.sum((tile_end[None, :] <= t[:, None]).astype(jnp.int32), axis=1)
    last = jnp.max(jnp.where(tiles > 0, jnp.arange(N_EXPERTS, dtype=jnp.int32), 0))
    te = jnp.where(t < n_used[0], jnp.minimum(te, N_EXPERTS - 1), last)
    hot = (te[:, None] == jnp.arange(N_EXPERTS, dtype=jnp.int32)[None, :]).astype(jnp.int32)
    tl = jnp.where(t < n_used[0], t - hot @ (tile_end - tiles), 0)
    lo_row = tl * MOE_TILE
    base_t = hot @ gbase
    size_t = hot @ gsize
    first = jnp.sum((base_t + size_t <= lo_row[:, None]).astype(jnp.int32), axis=1)
    stop = jnp.sum((base_t < lo_row[:, None] + MOE_TILE).astype(jnp.int32), axis=1)
    used = jnp.sum(gsize, axis=0)
    i32 = lambda z: z.astype(jnp.int32)
    return (i32(te), i32(tl), i32(first), i32(stop), i32(n_used), i32(gstart.reshape(-1)), i32(gsize.reshape(-1)),
            i32(gbase.reshape(-1)), i32(rows), i32(used))


def _final_kernel(x_ref, ys_ref, slot_ref, gates_ref, gt_ref, g_ref, y_ref):
    tm = x_ref.shape[0]
    cap = ys_ref.shape[0]
    col = lax.broadcasted_iota(jnp.int32, (tm, cap), 1).astype(F32)
    slots = slot_ref[...]
    gates = gates_ref[...]
    mix = jnp.zeros((tm, cap), F32)
    for k in range(TOP_K):
        mix = jnp.where(col == slots[:, k:k + 1], gates[:, k:k + 1], mix)
    hi = mix.astype(BF16)
    lo = (mix - hi.astype(F32)).astype(BF16)
    ys = ys_ref[...]
    ff = jnp.dot(hi, ys, preferred_element_type=F32) + jnp.dot(lo, ys, preferred_element_type=F32)
    y_ref[...] = _rms(x_ref[...] + gt_ref[...] * ff, g_ref[...])


def _final(x1, ys, slots, gates, gt, g, tm, block0):
    n, d = x1.shape
    cap = ys.shape[1]
    mod_spec = (pl.BlockSpec((1, d), lambda i: (0, 0)) if gt.shape[0] == 1
                else pl.BlockSpec((tm, d), lambda i: (i, 0)))
    row = lambda w: pl.BlockSpec((tm, w), lambda i: (i, 0))
    return pl.pallas_call(
        _final_kernel,
        out_shape=jax.ShapeDtypeStruct((n, d), F32),
        grid=(n // tm,),
        in_specs=[row(d), pl.BlockSpec((None, cap, d), lambda i: (block0 + i, 0, 0)),
                  row(V7X_LANES), row(V7X_LANES), mod_spec, _full((1, d))],
        out_specs=row(d),
        compiler_params=_cparams(),
        name="final_norm",
    )(x1, ys, slots, gates, gt, g.reshape(1, d))


def kernel(x_prompt, x_sample, cache_k_win, cache_v_win, state_ssm_re, state_ssm_im, c_prompt, c_sample,
           w_ada, b_ada, g_norm1, w_in, ssm_a_re, ssm_a_im, ssm_log_dt, ssm_b_re, ssm_b_im, ssm_c_re, ssm_c_im,
           ssm_d, w_glu, b_glu, g_out_attn, g_out_ssm, w_out, g_norm2, w_router, b_router, w_gate_up, b_gate_up,
           w_down, b_down, g_final):
    depth = w_ada.shape[0]
    assert depth == 1 and x_prompt.shape[0] == 1
    bp, t, d = x_prompt.shape
    bs, ts, _ = x_sample.shape
    ns = bs * ts
    l = 0
    a = ATTN_WIDTH

    n_c = bp + bs
    c_pad = -(-n_c // V7X_SUBLANES) * V7X_SUBLANES
    c_rows = jnp.concatenate([c_prompt, c_sample, jnp.zeros((c_pad - n_c, d), F32)], axis=0)
    mod = _ada_modulation(c_rows, w_ada[l], b_ada[l])
    mod_p = [mod[0:1, i * d:(i + 1) * d] for i in range(N_MOD)]
    mod_s = [jnp.repeat(mod[bp:bp + bs, i * d:(i + 1) * d], ts, axis=0) for i in range(N_MOD)]

    steps_p = TOKEN_TILE // SSM_ROWS_PROMPT
    perm_p = _chunk_perm(SSM_ROWS_PROMPT, steps_p)
    perm_s = _chunk_perm(bs, ts)
    tabs = _ssm_tables(ssm_a_re[l], ssm_a_im[l], ssm_log_dt[l], ssm_b_re[l], ssm_b_im[l],
                       ssm_c_re[l], ssm_c_im[l], max(steps_p, ts))
    wr_pad = jnp.zeros((d, V7X_LANES), F32).at[:, :N_EXPERTS].set(w_router[l]).astype(BF16)
    br_pad = jnp.full((1, V7X_LANES), MASKED, F32).at[0, :N_EXPERTS].set(b_router[l])

    xp = x_prompt.reshape(t, d)
    dils = tuple(dil for _, dil in DILATED_BRANCHES)
    wide = tuple(dil for dil in dils if dil > 1)
    proj_p = _inproj(xp, mod_p[0], mod_p[1], g_norm1[l], w_in[l], jnp.asarray(perm_p, BF16), TOKEN_TILE, wide)
    kpf, vpf, up = proj_p[3:6]
    views = {1: proj_p[0:3]}
    for n_d, dil in enumerate(wide):
        views[dil] = proj_p[6 + 3 * n_d:9 + 3 * n_d]
    outs = [_attn_branch(*views[dil], dil) for dil in dils]
    ap = _attn_combine([o for o, _ in outs], [s for _, s in outs], g_out_attn[l], dils)
    zeros_h = jnp.zeros((SSM_ROWS_PROMPT, 2 * N_STATE), F32)
    zp, hp = _ssm(up, zeros_h, tabs, ssm_d[l], w_glu[l], b_glu[l], g_out_ssm[l],
                  jnp.asarray(perm_p.T, BF16), SSM_ROWS_PROMPT, steps_p, True)
    n_blocks_p = t // TOKEN_TILE
    n_blocks = n_blocks_p + 1
    x1p, xs_all, slot_p, gate_p, meta_p = _outproj(xp, ap, zp, mod_p[2], mod_p[3], mod_p[4], g_norm2[l], w_out[l],
                                                   wr_pad, br_pad, TOKEN_TILE, n_blocks, 0)

    xs = x_sample.reshape(ns, d)
    qs, ks, vs, ksf, vsf, us = _inproj(xs, mod_s[0], mod_s[1], g_norm1[l], w_in[l], jnp.asarray(perm_s, BF16), ns)
    by_head = lambda z: z.reshape(bs, ts, N_HEADS, HEAD_DIM).transpose(0, 2, 1, 3).reshape(bs, N_HEADS * ts, HEAD_DIM)
    as_ = _attn_sample(by_head(qs), ksf.reshape(bs, ts * N_HEADS, HEAD_DIM), vsf.reshape(bs, ts * N_HEADS, HEAD_DIM),
                       cache_k_win, cache_v_win, l, g_out_attn[l])
    as_ = as_.reshape(bs, N_HEADS, ts, HEAD_DIM).transpose(0, 2, 1, 3)
    h0s = jnp.concatenate([state_ssm_re[l].reshape(bs, N_STATE), state_ssm_im[l].reshape(bs, N_STATE)], axis=1)
    zs, hs = _ssm(us, h0s, tabs, ssm_d[l], w_glu[l], b_glu[l], g_out_ssm[l],
                  jnp.asarray(perm_s.T, BF16), bs, ts, False)
    x1s, xs_all, slot_s, gate_s, meta_s = _outproj(xs, as_.reshape(ns, a), zs, mod_s[2], mod_s[3], mod_s[4],
                                                   g_norm2[l], w_out[l], wr_pad, br_pad, ns, n_blocks, n_blocks_p,
                                                   xs_prev=xs_all)

    meta = jnp.concatenate([meta_p, meta_s], axis=0)
    plan = _moe_plan(meta[:, 0, :N_EXPERTS], meta[:, 1, :N_EXPERTS])
    ys_all = _moe(plan, xs_all, w_gate_up[l], b_gate_up[l], w_down[l], b_down[l])

    y_prompt = _final(x1p, ys_all, slot_p, gate_p, mod_p[5], g_final, TOKEN_TILE, 0).reshape(bp, t, d)
    y_sample = _final(x1s, ys_all, slot_s, gate_s, mod_s[5], g_final, ns, n_blocks_p).reshape(bs, ts, d)

    keep = min(MAX_WINDOW, t)
    shp = (1, bp, keep, N_HEADS, HEAD_DIM)
    k_win = kpf[t - keep:].reshape(shp)
    v_win = vpf[t - keep:].reshape(shp)
    st = (1, bp, SSM_GROUPS, SSM_STATE)
    hp_last = hp[SSM_ROWS_PROMPT - 1]
    ss = (1, bs, SSM_GROUPS, SSM_STATE)
    return (y_prompt, y_sample, k_win, v_win,
            hp_last[:N_STATE].reshape(st), hp_last[N_STATE:].reshape(st),
            ksf.reshape(1, bs, ts, N_HEADS, HEAD_DIM), vsf.reshape(1, bs, ts, N_HEADS, HEAD_DIM),
            hs[:, :N_STATE].reshape(ss), hs[:, N_STATE:].reshape(ss))
```

```python
import functools

import numpy as np
import jax
import jax.numpy as jnp
from jax import lax
from jax.experimental import pallas as pl
from jax.experimental.pallas import tpu as pltpu

F32 = jnp.float32
BF16 = jnp.bfloat16

D_MODEL = 1024
N_HEADS = 8
HEAD_DIM = 64
ATTN_WIDTH = N_HEADS * HEAD_DIM
DILATED_BRANCHES = ((128, 1), (512, 4), (2048, 16))
KEYS_PER_BRANCH = 129
MAX_WINDOW = 2048
SSM_WIDTH = D_MODEL - ATTN_WIDTH
SSM_CH = 16
SSM_GROUPS = SSM_WIDTH // SSM_CH
SSM_STATE = 64
N_STATE = SSM_GROUPS * SSM_STATE
N_EXPERTS = 32
TOP_K = 4
D_EXPERT = D_MODEL
SWIGLU_LIMIT = 7.0
SWIGLU_ALPHA = 1.702
N_MOD = 6
EPS = 1e-6
MASKED = -1e30

V7X_LANES = 128
V7X_SUBLANES = 8
V7X_VMEM_LIMIT_BYTES = 56 * 1024 * 1024

TOKEN_TILE = 512
Q_TILE = 128
Q_BLOCKS_PER_STEP = 4
SSM_ROWS_PROMPT = 8
MOE_TILE = 512
ROW_ALIGN = 16
GROUP_CAP = -(-(TOKEN_TILE * TOP_K + N_EXPERTS * (ROW_ALIGN - 1)) // (2 * V7X_LANES)) * (2 * V7X_LANES)


def _cparams(n_axes=1):
    return pltpu.CompilerParams(
        dimension_semantics=("arbitrary",) * n_axes,
        vmem_limit_bytes=V7X_VMEM_LIMIT_BYTES,
    )


def _full(shape):
    n = len(shape)
    return pl.BlockSpec(shape, lambda *_: (0,) * n)


def _rms(x, g):
    return x * lax.rsqrt(jnp.mean(x * x, axis=-1, keepdims=True) + EPS) * g


def _sigmoid(x):
    return 1.0 / (1.0 + jnp.exp(-x))


def _ada_kernel(c_ref, w_ref, b_ref, o_ref):
    c = c_ref[...]
    s = (c * _sigmoid(c)).astype(BF16)
    o_ref[...] = jnp.dot(s, w_ref[...].astype(BF16), preferred_element_type=F32) + b_ref[...]


def _ada_modulation(c_rows, w_ada, b_ada):
    m, d = c_rows.shape
    n = w_ada.shape[1]
    tn = n // 4
    return pl.pallas_call(
        _ada_kernel,
        out_shape=jax.ShapeDtypeStruct((m, n), F32),
        grid=(n // tn,),
        in_specs=[_full((m, d)),
                  pl.BlockSpec((d, tn), lambda j: (0, j)),
                  pl.BlockSpec((1, tn), lambda j: (0, j))],
        out_specs=pl.BlockSpec((m, tn), lambda j: (0, j)),
        compiler_params=_cparams(),
        name="ada_modulation",
    )(c_rows, w_ada, b_ada.reshape(1, n))


def _inproj_kernel(*refs, dils):
    x_ref, sh_ref, sc_ref, g_ref, w_ref, perm_ref = refs[:6]
    dperm_refs = refs[6:6 + len(dils)]
    q_ref, k_ref, v_ref, kf_ref, vf_ref, u_ref = refs[6 + len(dils):12 + len(dils)]
    dil_refs = refs[12 + len(dils):-1]
    wbf_ref = refs[-1]

    @pl.when(pl.program_id(0) == 0)
    def _():
        wbf_ref[...] = w_ref[...].astype(BF16)

    h = _rms(x_ref[...], g_ref[...]) * (1.0 + sc_ref[...]) + sh_ref[...]
    hb = h.astype(BF16)
    a = ATTN_WIDTH
    proj = jnp.dot(hb, wbf_ref[:, :3 * a], preferred_element_type=F32)
    k = proj[:, a:2 * a]
    v = proj[:, 2 * a:]
    qkv = jnp.concatenate([(proj[:, :a] * (HEAD_DIM ** -0.5)).astype(BF16), k.astype(BF16), v.astype(BF16)],
                          axis=1)
    q_ref[...] = qkv[:, :a]
    k_ref[...] = qkv[:, a:2 * a]
    v_ref[...] = qkv[:, 2 * a:]
    kf_ref[...] = k
    vf_ref[...] = v
    hp = jnp.dot(perm_ref[...], hb, preferred_element_type=F32).astype(BF16)
    u_ref[...] = jnp.dot(hp, wbf_ref[:, 3 * a:], preferred_element_type=F32)
    tm = qkv.shape[0]
    for n_d, dil in enumerate(dils):
        by_residue = jnp.dot(dperm_refs[n_d][...], qkv, preferred_element_type=F32).astype(BF16)
        per = tm // dil
        for r in range(dil):
            rows = by_residue[r * per:(r + 1) * per]
            for j in range(3):
                dil_refs[3 * n_d + j][:, r * a:(r + 1) * a] = rows[:, j * a:(j + 1) * a]


def _inproj(x, sh, sc, g, w_in, perm, tm, dils=()):
    n, d = x.shape
    a = ATTN_WIDTH
    mod_rows = sh.shape[0]
    mod_spec = (pl.BlockSpec((1, d), lambda i: (0, 0)) if mod_rows == 1
                else pl.BlockSpec((tm, d), lambda i: (i, 0)))
    row = lambda w: pl.BlockSpec((tm, w), lambda i: (i, 0))
    dperms = [jnp.asarray(_chunk_perm(tm // dil, dil), BF16) for dil in dils]
    view_shapes = tuple(jax.ShapeDtypeStruct((n // dil, dil * a), BF16) for dil in dils for _ in range(3))
    view_specs = tuple(pl.BlockSpec((tm // dil, dil * a), lambda i: (i, 0)) for dil in dils for _ in range(3))
    return pl.pallas_call(
        functools.partial(_inproj_kernel, dils=tuple(dils)),
        out_shape=(jax.ShapeDtypeStruct((n, a), BF16),) * 3
        + (jax.ShapeDtypeStruct((n, a), F32),) * 2
        + (jax.ShapeDtypeStruct((n, SSM_WIDTH), F32),) + view_shapes,
        grid=(n // tm,),
        in_specs=[row(d), mod_spec, mod_spec, _full((1, d)), _full(w_in.shape), _full((tm, tm))]
        + [_full((tm, tm))] * len(dils),
        out_specs=(row(a),) * 5 + (row(SSM_WIDTH),) + view_specs,
        scratch_shapes=[pltpu.VMEM(w_in.shape, BF16)],
        compiler_params=_cparams(),
        name="inproj",
    )(x, sh, sc, g.reshape(1, d), w_in, perm, *dperms)


def _chunk_perm(rows, steps):
    n = rows * steps
    p = np.zeros((n, n), np.float32)
    c, t = np.meshgrid(np.arange(rows), np.arange(steps), indexing="ij")
    p[(t * rows + c).ravel(), (c * steps + t).ravel()] = 1.0
    return p


def _alibi_slopes():
    return np.exp2(-8.0 * np.arange(1, N_HEADS + 1, dtype=np.float64) / N_HEADS).astype(np.float32)


def _branch_bias(dil):
    qi = np.arange(Q_TILE)[:, None]
    col = np.arange(2 * Q_TILE)[None, :]
    j = Q_TILE + qi - col
    valid = (j >= 0) & (j <= Q_TILE)
    dist = (j * dil).astype(np.float32)
    tabs = []
    for first in (True, False):
        ok = valid & (col >= Q_TILE) if first else valid
        per_head = [np.where(ok, -s * dist, np.float32(MASKED)) for s in _alibi_slopes()]
        tabs.append(np.concatenate(per_head, axis=0))
    return np.stack(tabs).astype(np.float32)


def _attn_branch_kernel(q_ref, kp_ref, kc_ref, vp_ref, vc_ref, bias_ref, o_ref, lse_ref):
    first_step = pl.program_id(0) == 0
    lane = lax.broadcasted_iota(jnp.int32, (Q_TILE, V7X_LANES), 1)
    lo = lane < HEAD_DIM
    for j in range(Q_BLOCKS_PER_STEP):
        rows = slice(j * Q_TILE, (j + 1) * Q_TILE)
        before = slice((j - 1) * Q_TILE, j * Q_TILE)
        sel = jnp.where(first_step, 0, 1) if j == 0 else 1
        lse_acc = jnp.zeros((Q_TILE, V7X_LANES), F32)
        for p in range(N_HEADS // 2):
            cs = slice(V7X_LANES * p, V7X_LANES * (p + 1))
            q2 = q_ref[rows, cs]
            zero = jnp.zeros_like(q2)
            qq = jnp.concatenate([jnp.where(lo, q2, zero), jnp.where(lo, zero, q2)], axis=0)
            k_before = kp_ref[:, cs] if j == 0 else kc_ref[before, cs]
            v_before = vp_ref[:, cs] if j == 0 else vc_ref[before, cs]
            kk = jnp.concatenate([k_before, kc_ref[rows, cs]], axis=0)
            vv = jnp.concatenate([v_before, vc_ref[rows, cs]], axis=0)
            s = lax.dot_general(qq, kk, (((1,), (1,)), ((), ())), preferred_element_type=F32)
            s = s + bias_ref[sel, 2 * Q_TILE * p:2 * Q_TILE * (p + 1), :]
            m = jnp.max(s, axis=1, keepdims=True)
            e = jnp.exp(s - m)
            l = jnp.sum(e, axis=1, keepdims=True)
            eb = e.astype(BF16)
            o0 = jnp.dot(eb[:Q_TILE], vv, preferred_element_type=F32) * (1.0 / l[:Q_TILE])
            o1 = jnp.dot(eb[Q_TILE:], vv, preferred_element_type=F32) * (1.0 / l[Q_TILE:])
            o_ref[rows, cs] = jnp.where(lo, o0, o1).astype(o_ref.dtype)
            lse = m + jnp.log(l)
            lse_acc = jnp.where(lane == 2 * p, lse[:Q_TILE], lse_acc)
            lse_acc = jnp.where(lane == 2 * p + 1, lse[Q_TILE:], lse_acc)
        lse_ref[rows, :] = lse_acc[:, :N_HEADS]


def _attn_branch(qv, kv, vv, dil):
    a = ATTN_WIDTH
    rows = qv.shape[0]
    t = rows * dil
    step = Q_BLOCKS_PER_STEP * Q_TILE
    cur = pl.BlockSpec((step, a), lambda i, r: (i, r))
    prev = pl.BlockSpec((Q_TILE, a), lambda i, r: (jnp.maximum(i * Q_BLOCKS_PER_STEP - 1, 0), r))
    bias = jnp.asarray(_branch_bias(dil))
    o, lse = pl.pallas_call(
        _attn_branch_kernel,
        out_shape=(jax.ShapeDtypeStruct((rows, dil * a), BF16),
                   jax.ShapeDtypeStruct((dil, rows, N_HEADS), F32)),
        grid=(rows // step, dil),
        in_specs=[cur, prev, cur, prev, cur, _full(bias.shape)],
        out_specs=(cur, pl.BlockSpec((None, step, N_HEADS), lambda i, r: (r, i, 0))),
        compiler_params=_cparams(2),
        name=f"attn_branch_d{dil}",
    )(qv, kv, kv, vv, vv, bias)
    return o, lse.transpose(1, 0, 2).reshape(t, N_HEADS)


def _attn_combine_kernel(*refs, dils):
    nb = len(dils)
    o_refs = refs[:nb]
    l_refs = refs[nb:2 * nb]
    g_ref = refs[2 * nb]
    unperm_refs = refs[2 * nb + 1:-1]
    a_ref = refs[-1]
    tq, a = a_ref.shape
    outs = []
    n_u = 0
    for o_ref, dil in zip(o_refs, dils):
        if dil == 1:
            outs.append(o_ref[...].astype(F32))
            continue
        by_residue = jnp.concatenate([o_ref[:, r * a:(r + 1) * a] for r in range(dil)], axis=0)
        outs.append(jnp.dot(unperm_refs[n_u][...], by_residue, preferred_element_type=F32))
        n_u += 1
    ls = [l_ref[...] for l_ref in l_refs]
    top = functools.reduce(jnp.maximum, ls)
    ws = [jnp.exp(l - top) for l in ls]
    inv = 1.0 / functools.reduce(jnp.add, ws)
    cs = [w * inv for w in ws]
    lane = lax.broadcasted_iota(jnp.int32, (tq, V7X_LANES), 1)
    lo = lane < HEAD_DIM
    cols = []
    for p in range(N_HEADS // 2):
        sl = slice(V7X_LANES * p, V7X_LANES * (p + 1))
        acc = jnp.zeros((tq, V7X_LANES), F32)
        for c, o in zip(cs, outs):
            cexp = jnp.where(lo,
                             jnp.broadcast_to(c[:, 2 * p:2 * p + 1], (tq, V7X_LANES)),
                             jnp.broadcast_to(c[:, 2 * p + 1:2 * p + 2], (tq, V7X_LANES)))
            acc = acc + cexp * o[:, sl]
        cols.append(acc)
    o = jnp.concatenate(cols, axis=1)
    a_ref[...] = _rms(o, g_ref[...]).astype(a_ref.dtype)


def _attn_combine(os_, lses, g, dils):
    t = lses[0].shape[0]
    a = ATTN_WIDTH
    tq = TOKEN_TILE
    unperms = [jnp.asarray(_chunk_perm(tq // dil, dil).T, BF16) for dil in dils if dil > 1]
    return pl.pallas_call(
        functools.partial(_attn_combine_kernel, dils=tuple(dils)),
        out_shape=jax.ShapeDtypeStruct((t, a), BF16),
        grid=(t // tq,),
        in_specs=[pl.BlockSpec((tq // dil, dil * a), lambda i: (i, 0)) for dil in dils]
        + [pl.BlockSpec((tq, N_HEADS), lambda i: (i, 0))] * len(dils) + [_full((1, a))]
        + [_full((tq, tq))] * len(unperms),
        out_specs=pl.BlockSpec((tq, a), lambda i: (i, 0)),
        compiler_params=_cparams(),
        name="attn_combine",
    )(*os_, *lses, g.reshape(1, a), *unperms)


NEW_KEY_PAD = V7X_LANES


def _sample_bias(win, steps):
    slopes = _alibi_slopes()[None, :, None, None]
    t = np.arange(steps)[None, None, :, None]

    def table(dist, live):
        tabs = []
        for window, dil in DILATED_BRANCHES:
            ok = live & (dist >= 0) & (dist <= window) & (dist % dil == 0)
            tabs.append(np.where(ok, -slopes * dist.astype(np.float32), np.float32(MASKED))[0])
        return np.stack(tabs).astype(np.float32)

    pos = np.arange(win)[None, None, None, :]
    col = np.arange(NEW_KEY_PAD)[None, None, None, :]
    return table(win + t - pos, np.bool_(True)), table(t - col, col < steps)


def _attn_sample_kernel(q_ref, kn_ref, vn_ref, kt_ref, vt_ref, bo_ref, bn_ref, g_ref, o_ref):
    nb = bo_ref.shape[0]
    steps = q_ref.shape[1]
    nt = lambda p, v: lax.dot_general(p, v, (((1,), (1,)), ((), ())), preferred_element_type=F32)
    res = []
    sq = jnp.zeros((steps, 1), F32)
    for h in range(N_HEADS):
        q = q_ref[h]
        vt = vt_ref[h].astype(BF16)
        vn = vn_ref[h].astype(BF16)
        s_old = jnp.dot(q, kt_ref[h].astype(BF16), preferred_element_type=F32)
        s_new = jnp.dot(q, kn_ref[h].astype(BF16), preferred_element_type=F32)
        ms, ls, e_old, e_new = [], [], [], []
        for b in range(nb):
            so = s_old + bo_ref[b, h]
            sn = s_new + bn_ref[b, h]
            m = jnp.maximum(jnp.max(so, axis=1, keepdims=True), jnp.max(sn, axis=1, keepdims=True))
            eo = jnp.exp(so - m)
            en = jnp.exp(sn - m)
            ms.append(m)
            ls.append(jnp.sum(eo, axis=1, keepdims=True) + jnp.sum(en, axis=1, keepdims=True))
            e_old.append(eo)
            e_new.append(en)
        o_all = (nt(jnp.concatenate(e_old, axis=0).astype(BF16), vt)
                 + nt(jnp.concatenate(e_new, axis=0).astype(BF16), vn))
        top = functools.reduce(jnp.maximum, ms)
        num = jnp.zeros((steps, HEAD_DIM), F32)
        den = jnp.zeros((steps, 1), F32)
        for b in range(nb):
            w = jnp.exp(ms[b] - top)
            num = num + o_all[b * steps:(b + 1) * steps] * w
            den = den + ls[b] * w
        r = num * (1.0 / den)
        res.append(r)
        sq = sq + jnp.sum(r * r, axis=1, keepdims=True)
    inv = lax.rsqrt(sq * (1.0 / ATTN_WIDTH) + EPS)
    for h in range(N_HEADS):
        o_ref[h] = res[h] * inv * g_ref[h]


def _attn_sample(q, kn_t, vn_t, cache_kt, cache_vt, layer, g):
    b, heads, steps, dh = q.shape
    win = cache_kt.shape[-1]
    b_old, b_new = (jnp.asarray(z) for z in _sample_bias(win, steps))
    per_b = lambda *tail: pl.BlockSpec((None,) + tail, lambda i: (i,) + (0,) * len(tail))
    cache = pl.BlockSpec((None, None, heads, dh, win), lambda i: (layer, i, 0, 0, 0))
    return pl.pallas_call(
        _attn_sample_kernel,
        out_shape=jax.ShapeDtypeStruct((b, heads, steps, dh), F32),
        grid=(b,),
        in_specs=[per_b(heads, steps, dh), per_b(heads, dh, NEW_KEY_PAD), per_b(heads, dh, NEW_KEY_PAD),
                  cache, cache, _full(b_old.shape), _full(b_new.shape), _full((heads, 1, dh))],
        out_specs=per_b(heads, steps, dh),
        compiler_params=_cparams(),
        name="attn_sample",
    )(q, kn_t, vn_t, cache_kt, cache_vt, b_old, b_new, g.reshape(heads, 1, dh))


def _gelu_tanh(x):
    return 0.5 * x * (1.0 + jnp.tanh(np.sqrt(2.0 / np.pi).astype(np.float32) * (x + 0.044715 * (x * x * x))))


def _ssm_kernel(u_ref, bb_ref, lam_ref, pow_ref, cm_ref, dsk_ref, wglu_ref, bglu_ref, g_ref, pt_ref, h0_ref,
                z_ref, ht_ref, h_s, hin_s, carry_s, *, rows, steps, chain):
    ns = N_STATE
    cw = (4 * V7X_SUBLANES * V7X_LANES) // rows
    last = (steps - 1) * rows

    @pl.when(pl.program_id(0) == 0)
    def _():
        carry_s[...] = h0_ref[0:1, :]

    ub = u_ref[...].astype(BF16)
    n_slabs = SSM_WIDTH // V7X_LANES
    sw = ns // n_slabs
    for s in range(n_slabs):
        part = jnp.dot(ub[:, s * V7X_LANES:(s + 1) * V7X_LANES], bb_ref[s], preferred_element_type=F32)
        h_s[:, s * sw:(s + 1) * sw] = part[:, :sw]
        h_s[:, ns + s * sw:ns + (s + 1) * sw] = part[:, sw:]

    for cc in range(0, ns // cw, 2):
        crs = [slice(c * cw, (c + 1) * cw) for c in (cc, cc + 1)]
        cis = [slice(ns + c * cw, ns + (c + 1) * cw) for c in (cc, cc + 1)]
        lrs = [jnp.broadcast_to(lam_ref[0:1, cr], (rows, cw)) for cr in crs]
        lis = [jnp.broadcast_to(lam_ref[1:2, cr], (rows, cw)) for cr in crs]

        def scan_body(t, carry, crs=crs, cis=cis, lrs=lrs, lis=lis):
            rs = pl.ds(pl.multiple_of(t * rows, rows), rows)
            out = []
            for j in range(2):
                hr, hi = carry[2 * j], carry[2 * j + 1]
                nr = lrs[j] * hr - lis[j] * hi + h_s[rs, crs[j]]
                ni = lrs[j] * hi + lis[j] * hr + h_s[rs, cis[j]]
                h_s[rs, crs[j]] = nr
                h_s[rs, cis[j]] = ni
                out += [nr, ni]
            return tuple(out)

        zero = jnp.zeros((rows, cw), F32)
        lax.fori_loop(0, steps, scan_body, (zero,) * 4)

    if chain:
        ptr = pow_ref[steps - 1, 0:1, :ns]
        pti = pow_ref[steps - 1, 0:1, ns:]
        carry = carry_s[...]
        for c in range(rows):
            hin_s[c:c + 1, :] = carry
            cr_, ci_ = carry[:, :ns], carry[:, ns:]
            e = h_s[last + c:last + c + 1, :]
            carry = jnp.concatenate([ptr * cr_ - pti * ci_ + e[:, :ns],
                                     ptr * ci_ + pti * cr_ + e[:, ns:]], axis=1)
        carry_s[...] = carry
    else:
        hin_s[...] = h0_ref[...]

    for cc in range(ns // cw):
        cr = slice(cc * cw, (cc + 1) * cw)
        ci = slice(ns + cc * cw, ns + (cc + 1) * cw)
        hr0 = hin_s[:, cr]
        hi0 = hin_s[:, ci]

        def fix_body(t, _, cr=cr, ci=ci, hr0=hr0, hi0=hi0):
            rs = pl.ds(pl.multiple_of(t * rows, rows), rows)
            pr = pow_ref[t, :, cr]
            pi_ = pow_ref[t, :, ci]
            h_s[rs, cr] = h_s[rs, cr] + (pr * hr0 - pi_ * hi0)
            h_s[rs, ci] = h_s[rs, ci] + (pr * hi0 + pi_ * hr0)
            return 0

        lax.fori_loop(0, steps, fix_body, 0, unroll=4)

    ht_ref[...] = h_s[last:last + rows, :]
    ys = []
    for s in range(n_slabs):
        hs = jnp.concatenate([h_s[:, s * sw:(s + 1) * sw], h_s[:, ns + s * sw:ns + (s + 1) * sw]], axis=1)
        ys.append(jnp.dot(hs.astype(BF16), cm_ref[s], preferred_element_type=F32))
    y = jnp.concatenate(ys, axis=1) + dsk_ref[...] * u_ref[...]
    y = _gelu_tanh(y)
    gl = jnp.dot(y.astype(BF16), wglu_ref[...], preferred_element_type=F32) + bglu_ref[...]
    z = _rms(y * _sigmoid(gl), g_ref[...]).astype(BF16)
    z_ref[...] = jnp.dot(pt_ref[...], z, preferred_element_type=F32).astype(z_ref.dtype)


def _ssm_tables(ssm_a_re, ssm_a_im, ssm_log_dt, ssm_b_re, ssm_b_im, ssm_c_re, ssm_c_im, max_steps):
    g, n, ch = SSM_GROUPS, SSM_STATE, SSM_CH
    a_re = ssm_a_re.astype(F32)
    a_im = ssm_a_im.astype(F32)
    dt = jnp.exp(ssm_log_dt.astype(F32))[:, None]
    mag = jnp.exp(dt * a_re)
    lam_re = mag * jnp.cos(dt * a_im)
    lam_im = mag * jnp.sin(dt * a_im)
    nr = lam_re - 1.0
    ni = lam_im
    inv = 1.0 / (a_re * a_re + a_im * a_im)
    coef_re = (nr * a_re + ni * a_im) * inv
    coef_im = (ni * a_re - nr * a_im) * inv
    br = ssm_b_re.astype(F32)
    bi = ssm_b_im.astype(F32)
    bb_re = coef_re[..., None] * br - coef_im[..., None] * bi
    bb_im = coef_re[..., None] * bi + coef_im[..., None] * br
    gs = V7X_LANES // ch
    ns_ = g // gs
    eye = jnp.eye(gs, dtype=F32)
    bmat = lambda b: jnp.einsum("sgnc,gh->sgchn", b.reshape(ns_, gs, n, ch), eye).reshape(ns_, gs * ch, gs * n)
    cmat = lambda c: jnp.einsum("sgcn,gh->sgnhc", c.astype(F32).reshape(ns_, gs, ch, n), eye).reshape(
        ns_, gs * n, gs * ch)
    bb = jnp.concatenate([bmat(bb_re), bmat(bb_im)], axis=2).astype(BF16)
    cm = jnp.concatenate([cmat(ssm_c_re), -cmat(ssm_c_im)], axis=1).astype(BF16)
    lam = jnp.stack([lam_re.reshape(-1), lam_im.reshape(-1)])

    k = jnp.arange(1, max_steps + 1, dtype=F32)[:, None]
    kdt = k * dt.reshape(1, -1).repeat(n, axis=1)
    pmag = jnp.exp(kdt * a_re.reshape(1, -1))
    parg = kdt * a_im.reshape(1, -1)
    pows = jnp.concatenate([pmag * jnp.cos(parg), pmag * jnp.sin(parg)], axis=1)
    return bb, cm, lam, pows


def _ssm(u_perm, h0, tabs, dsk, w_glu, b_glu, g, perm_t, rows, steps, chain):
    n, w = u_perm.shape
    blk = rows * steps
    bb, cm, lam, pows = tabs
    row = pl.BlockSpec((blk, w), lambda i: (i, 0))
    kern = functools.partial(_ssm_kernel, rows=rows, steps=steps, chain=chain)
    return pl.pallas_call(
        kern,
        out_shape=(jax.ShapeDtypeStruct((n, w), BF16), jax.ShapeDtypeStruct((rows, 2 * N_STATE), F32)),
        grid=(n // blk,),
        in_specs=[row, _full(bb.shape), _full(lam.shape), _full((steps, rows, 2 * N_STATE)), _full(cm.shape),
                  _full((1, w)), _full((w, w)), _full((1, w)), _full((1, w)), _full((blk, blk)),
                  _full((rows, 2 * N_STATE))],
        out_specs=(row, _full((rows, 2 * N_STATE))),
        scratch_shapes=[pltpu.VMEM((blk, 2 * N_STATE), F32),
                        pltpu.VMEM((rows, 2 * N_STATE), F32),
                        pltpu.VMEM((1, 2 * N_STATE), F32)],
        compiler_params=_cparams(),
        name=f"ssm_r{rows}",
    )(u_perm, bb, lam, jnp.broadcast_to(pows[:steps, None, :], (steps, rows, 2 * N_STATE)), cm,
      dsk.reshape(1, w), w_glu.astype(BF16), b_glu.reshape(1, w),
      g.reshape(1, w), perm_t, h0)


def _outproj_kernel(*refs, n_real, aliased):
    ins, outs = refs[:12], refs[12 + aliased:]
    step = pl.program_id(0)

    @pl.when(step < n_real)
    def _():
        _outproj_tile(*ins, *outs)

    @pl.when(step >= n_real)
    def _():
        xs_ref = outs[1]
        xs_ref[...] = jnp.zeros(xs_ref.shape, xs_ref.dtype)


def _outproj_tile(x_ref, a_ref, z_ref, gt_ref, sh_ref, sc_ref, g_ref, wo_ref, wr_ref, br_ref, ltri_ref, utri_ref,
                  x1_ref, xs_ref, slot_ref, gates_ref, meta_ref, wbf_ref):
    @pl.when(pl.program_id(0) == 0)
    def _():
        wbf_ref[...] = wo_ref[...].astype(BF16)

    a = ATTN_WIDTH
    mixed = (jnp.dot(a_ref[...].astype(BF16), wbf_ref[:a, :], preferred_element_type=F32)
             + jnp.dot(z_ref[...], wbf_ref[a:, :], preferred_element_type=F32))
    x1 = x_ref[...] + gt_ref[...] * mixed
    x1_ref[...] = x1
    h2 = (_rms(x1, g_ref[...]) * (1.0 + sc_ref[...]) + sh_ref[...]).astype(BF16)
    lg = jnp.dot(h2, wr_ref[...], preferred_element_type=F32) + br_ref[...]
    tm = lg.shape[0]
    lane = lax.broadcasted_iota(jnp.int32, (tm, V7X_LANES), 1).astype(F32)
    vals, hots = [], []
    for _ in range(TOP_K):
        m = jnp.max(lg, axis=1, keepdims=True)
        idx = jnp.min(jnp.where(lg == m, lane, float(V7X_LANES)), axis=1, keepdims=True)
        hot = lane == idx
        vals.append(m)
        hots.append(jnp.where(hot, 1.0, 0.0))
        lg = jnp.where(hot, MASKED * 2, lg)
    es = [jnp.exp(v - vals[0]) for v in vals]
    inv = 1.0 / (es[0] + es[1] + es[2] + es[3])

    member = hots[0] + hots[1] + hots[2] + hots[3]
    before = jnp.dot(ltri_ref[...], member.astype(BF16), preferred_element_type=F32)
    count = jnp.sum(member, axis=0, keepdims=True)
    padded = jnp.floor((count + (ROW_ALIGN - 1.0)) * (1.0 / ROW_ALIGN)) * ROW_ALIGN
    padded8 = jnp.broadcast_to(padded, (V7X_SUBLANES, V7X_LANES))
    start = jnp.dot(padded8.astype(BF16), utri_ref[...], preferred_element_type=F32)[0:1]
    where_to = start + before
    slots = jnp.zeros((tm, V7X_LANES), F32)
    gates = jnp.zeros((tm, V7X_LANES), F32)
    for k in range(TOP_K):
        slot_k = jnp.sum(hots[k] * where_to, axis=1, keepdims=True)
        slots = jnp.where(lane == float(k), slot_k, slots)
        gates = jnp.where(lane == float(k), es[k] * inv, gates)
    slot_ref[...] = slots
    gates_ref[...] = gates
    row = lax.broadcasted_iota(jnp.int32, (V7X_SUBLANES, V7X_LANES), 0)
    meta_ref[...] = jnp.where(row == 0, padded8, jnp.where(row == 1, jnp.broadcast_to(start, padded8.shape), 0.0))

    cap = xs_ref.shape[0]
    slots_t = jnp.transpose(slots)
    srow = lax.broadcasted_iota(jnp.int32, (cap, tm), 0).astype(F32)
    place = jnp.zeros((cap, tm), F32)
    for k in range(TOP_K):
        place = jnp.where(srow == slots_t[k:k + 1, :], 1.0, place)
    xs_ref[...] = jnp.dot(place.astype(BF16), h2, preferred_element_type=F32).astype(BF16)


def _outproj(x, a, z, gt, sh, sc, g, w_out, wr_pad, br_pad, tm, n_blocks, block0, xs_prev=None):
    n, d = x.shape
    aw = ATTN_WIDTH
    n_real = n // tm
    aliased = xs_prev is not None
    n_steps = n_real if aliased else n_blocks
    tile = lambda i: jnp.minimum(i, n_real - 1)
    mod_spec = (pl.BlockSpec((1, d), lambda i: (0, 0)) if gt.shape[0] == 1
                else pl.BlockSpec((tm, d), lambda i: (tile(i), 0)))
    row = lambda w: pl.BlockSpec((tm, w), lambda i: (tile(i), 0))
    ltri = jnp.asarray(np.tril(np.ones((tm, tm), np.float32), -1), BF16)
    utri = jnp.asarray(np.triu(np.ones((V7X_LANES, V7X_LANES), np.float32), 1), BF16)
    in_specs = [row(d), row(aw), row(SSM_WIDTH), mod_spec, mod_spec, mod_spec, _full((1, d)),
                _full(w_out.shape), _full(wr_pad.shape), _full(br_pad.shape), _full(ltri.shape), _full(utri.shape)]
    args = [x, a, z, gt, sh, sc, g.reshape(1, d), w_out, wr_pad, br_pad, ltri, utri]
    if aliased:
        in_specs.append(pl.BlockSpec(memory_space=pl.ANY))
        args.append(xs_prev)
    return pl.pallas_call(
        functools.partial(_outproj_kernel, n_real=n_real, aliased=int(aliased)),
        out_shape=(jax.ShapeDtypeStruct((n, d), F32),
                   jax.ShapeDtypeStruct((n_blocks, GROUP_CAP, d), BF16),
                   jax.ShapeDtypeStruct((n, V7X_LANES), F32), jax.ShapeDtypeStruct((n, V7X_LANES), F32),
                   jax.ShapeDtypeStruct((n_real, V7X_SUBLANES, V7X_LANES), F32)),
        grid=(n_steps,),
        in_specs=in_specs,
        out_specs=(row(d), pl.BlockSpec((None, GROUP_CAP, d), lambda i: (block0 + i, 0, 0)),
                   row(V7X_LANES), row(V7X_LANES),
                   pl.BlockSpec((None, V7X_SUBLANES, V7X_LANES), lambda i: (tile(i), 0, 0))),
        scratch_shapes=[pltpu.VMEM(w_out.shape, BF16)],
        input_output_aliases={len(args) - 1: 1} if aliased else {},
        compiler_params=_cparams(),
        name="outproj_router",
    )(*args)


_PIECE_SIZES = tuple(MOE_TILE >> s for s in range(6))
_CHUNK_SHIFT = 6
_CHUNK = 1 << _CHUNK_SHIFT
_TAIL_SIZES = tuple(sz for sz in _PIECE_SIZES if sz < _CHUNK)


def _moe_kernel(te_ref, tl_ref, lo_ref, hi_ref, nu_ref, gstart_ref, gsize_ref, gbase_ref, rows_ref, used_ref,
                xs_hbm, wgu_ref, bgu_ref, wd_ref, bd_ref, ys_hbm,
                xbuf, ybuf, zbuf, wgu_bf, wd_bf, xsem, ysem, zsem):
    t = pl.program_id(0)
    n_used = nu_ref[0]
    n_blocks, cap = xs_hbm.shape[0], xs_hbm.shape[1]

    def x_copy(i, br, tr, sz, slot):
        pltpu.make_async_copy(xs_hbm.at[i, pl.ds(br, sz)], xbuf.at[slot, pl.ds(tr, sz)], xsem.at[slot]).start()

    def y_copy(i, br, tr, sz, slot):
        pltpu.make_async_copy(ybuf.at[slot, pl.ds(tr, sz)], ys_hbm.at[i, pl.ds(br, sz)], ysem.at[slot]).start()

    def z_copy(i, row, sz, start):
        cp = pltpu.make_async_copy(zbuf.at[pl.ds(0, sz)], ys_hbm.at[i, pl.ds(row, sz)], zsem)
        cp.start() if start else cp.wait()

    def pieces(tt, fn):
        e = te_ref[tt]
        lo = tl_ref[tt] * MOE_TILE

        def per_block(i, c):
            g = e * n_blocks + i
            s0 = gbase_ref[g]
            a = jnp.maximum(s0, lo)
            b = jnp.minimum(s0 + gsize_ref[g], lo + MOE_TILE)
            length = jnp.maximum(b - a, 0)
            src = gstart_ref[g] + (a - s0)
            dst = a - lo
            whole = lax.shift_right_logical(length, _CHUNK_SHIFT)

            def chunk(j, cc):
                off = j * _CHUNK
                fn(i, pl.multiple_of(src + off, ROW_ALIGN), pl.multiple_of(dst + off, ROW_ALIGN), _CHUNK)
                return cc

            lax.fori_loop(0, whole, chunk, 0)
            done = whole * _CHUNK
            for sz in _TAIL_SIZES:
                hit = (length & sz) != 0

                @pl.when(hit)
                def _(sz=sz, done=done):
                    fn(i, pl.multiple_of(src + done, ROW_ALIGN), pl.multiple_of(dst + done, ROW_ALIGN), sz)

                done = done + jnp.where(hit, sz, 0)
            return c

        lax.fori_loop(lo_ref[tt], hi_ref[tt], per_block, 0)

    def tile_rows(tt):
        return jnp.minimum(rows_ref[te_ref[tt]] - tl_ref[tt] * MOE_TILE, MOE_TILE)

    def wait_rows(n, sem, buf):
        def chunk(j, cc):
            pltpu.make_async_copy(buf.at[pl.ds(0, _CHUNK)], buf.at[pl.ds(0, _CHUNK)], sem).wait()
            return cc

        lax.fori_loop(0, lax.shift_right_logical(n, _CHUNK_SHIFT), chunk, 0)
        for sz in _TAIL_SIZES:
            @pl.when((n & sz) != 0)
            def _(sz=sz):
                pltpu.make_async_copy(buf.at[pl.ds(0, sz)], buf.at[pl.ds(0, sz)], sem).wait()

    def fetch(tt, slot):
        pieces(tt, lambda i, br, tr, sz: x_copy(i, br, tr, sz, slot))

    def writeback(tt, slot):
        pieces(tt, lambda i, br, tr, sz: y_copy(i, br, tr, sz, slot))

    def zero_tail(i, start):
        u = used_ref[i]
        rem = cap - u
        nz = zbuf.shape[0]
        whole = lax.shift_right_logical(rem, nz.bit_length() - 1)

        def chunk(j, c):
            z_copy(i, pl.multiple_of(u + j * nz, ROW_ALIGN), nz, start)
            return c

        lax.fori_loop(0, whole, chunk, 0)
        base = u + whole * nz
        done = jnp.int32(0)
        for sz in _PIECE_SIZES:
            if sz >= nz:
                continue
            hit = (rem & sz) != 0

            @pl.when(hit)
            def _(sz=sz, done=done):
                z_copy(i, pl.multiple_of(base + done, ROW_ALIGN), sz, start)

            done = done + jnp.where(hit, sz, 0)

    @pl.when(t == 0)
    def _():
        xbuf[...] = jnp.zeros(xbuf.shape, xbuf.dtype)
        zbuf[...] = jnp.zeros(zbuf.shape, zbuf.dtype)
        lax.fori_loop(0, n_blocks, lambda i, c: (zero_tail(i, True), c)[1], 0)
        lax.fori_loop(0, n_blocks, lambda i, c: (zero_tail(i, False), c)[1], 0)
        fetch(0, 0)

    @pl.when(t < n_used)
    def _():
        slot = t % 2

        @pl.when(t + 1 < n_used)
        def _():
            fetch(t + 1, 1 - slot)

        prev = te_ref[jnp.maximum(t - 1, 0)]
        fresh = jnp.logical_or(t == 0, te_ref[t] != prev)

        @pl.when(fresh)
        def _():
            wgu_bf[...] = wgu_ref[...].astype(BF16)
            wd_bf[...] = wd_ref[...].astype(BF16)

        wait_rows(tile_rows(t), xsem.at[slot], xbuf.at[slot])

        @pl.when(t >= 2)
        def _():
            wait_rows(tile_rows(t - 2), ysem.at[slot], ybuf.at[slot])

        f = D_EXPERT
        gu = jnp.dot(xbuf[slot], wgu_bf[...], preferred_element_type=F32) + bgu_ref[...]
        gate = jnp.minimum(gu[:, :f], SWIGLU_LIMIT)
        up = jnp.clip(gu[:, f:], -SWIGLU_LIMIT, SWIGLU_LIMIT)
        act = (up + 1.0) * gate * _sigmoid(SWIGLU_ALPHA * gate)
        y = jnp.dot(act.astype(BF16), wd_bf[...], preferred_element_type=F32) + bd_ref[...]
        ybuf[slot] = y.astype(BF16)
        writeback(t, slot)

        @pl.when(t == n_used - 1)
        def _():
            wait_rows(tile_rows(t), ysem.at[slot], ybuf.at[slot])

            @pl.when(t >= 1)
            def _():
                wait_rows(tile_rows(t - 1), ysem.at[1 - slot], ybuf.at[1 - slot])


def _moe(plan, xs, w_gate_up, b_gate_up, w_down, b_down):
    _, cap, d = xs.shape
    e, _, f2 = w_gate_up.shape
    nt = plan[0].shape[0]
    wmap = lambda t, te, *_: (te[t], 0, 0)
    anyspec = pl.BlockSpec(memory_space=pl.ANY)
    grid_spec = pltpu.PrefetchScalarGridSpec(
        num_scalar_prefetch=len(plan),
        grid=(nt,),
        in_specs=[anyspec,
                  pl.BlockSpec((None, d, f2), wmap),
                  pl.BlockSpec((None, 1, f2), wmap),
                  pl.BlockSpec((None, f2 // 2, d), wmap),
                  pl.BlockSpec((None, 1, d), wmap)],
        out_specs=anyspec,
        scratch_shapes=[pltpu.VMEM((2, MOE_TILE, d), BF16), pltpu.VMEM((2, MOE_TILE, d), BF16),
                        pltpu.VMEM((MOE_TILE // 2, d), BF16),
                        pltpu.VMEM((d, f2), BF16), pltpu.VMEM((f2 // 2, d), BF16),
                        pltpu.SemaphoreType.DMA((2,)), pltpu.SemaphoreType.DMA((2,)), pltpu.SemaphoreType.DMA(())],
    )
    return pl.pallas_call(
        _moe_kernel,
        out_shape=jax.ShapeDtypeStruct(xs.shape, BF16),
        grid_spec=grid_spec,
        compiler_params=_cparams(),
        name="moe_experts",
    )(*plan, xs, w_gate_up, b_gate_up.reshape(e, 1, f2), w_down, b_down.reshape(e, 1, d))


def _moe_plan(group_size, group_start):
    n_blocks = group_size.shape[0]
    gsize = group_size.T.astype(jnp.int32)
    gstart = group_start.T.astype(jnp.int32)
    gbase = jnp.cumsum(gsize, axis=1) - gsize
    rows = jnp.sum(gsize, axis=1)
    tiles = (rows + MOE_TILE - 1) // MOE_TILE
    tile_end = jnp.cumsum(tiles)
    n_used = tile_end[-1:]
    nt = (n_blocks * GROUP_CAP) // MOE_TILE + N_EXPERTS
    t = jnp.arange(nt, dtype=jnp.int32)
    te = jnp.sum((tile_end[None, :] <= t[:, None]).astype(jnp.int32), axis=1)
    last = jnp.max(jnp.where(tiles > 0, jnp.arange(N_EXPERTS, dtype=jnp.int32), 0))
    te = jnp.where(t < n_used[0], jnp.minimum(te, N_EXPERTS - 1), last)
    hot = (te[:, None] == jnp.arange(N_EXPERTS, dtype=jnp.int32)[None, :]).astype(jnp.int32)
    tl = jnp.where(t < n_used[0], t - hot @ (tile_end - tiles), 0)
    lo_row = tl * MOE_TILE
    base_t = hot @ gbase
    size_t = hot @ gsize
    first = jnp.sum((base_t + size_t <= lo_row[:, None]).astype(jnp.int32), axis=1)
    stop = jnp.sum((base_t < lo_row[:, None] + MOE_TILE).astype(jnp.int32), axis=1)
    used = jnp.sum(gsize, axis=0)
    i32 = lambda z: z.astype(jnp.int32)
    return (i32(te), i32(tl), i32(first), i32(stop), i32(n_used), i32(gstart.reshape(-1)), i32(gsize.reshape(-1)),
            i32(gbase.reshape(-1)), i32(rows), i32(used))


def _final_kernel(x_ref, ys_ref, slot_ref, gates_ref, gt_ref, g_ref, y_ref):
    tm = x_ref.shape[0]
    cap = ys_ref.shape[0]
    col = lax.broadcasted_iota(jnp.int32, (tm, cap), 1).astype(F32)
    slots = slot_ref[...]
    gates = gates_ref[...]
    mix = jnp.zeros((tm, cap), F32)
    for k in range(TOP_K):
        mix = jnp.where(col == slots[:, k:k + 1], gates[:, k:k + 1], mix)
    hi = mix.astype(BF16)
    lo = (mix - hi.astype(F32)).astype(BF16)
    ys = ys_ref[...]
    ff = jnp.dot(hi, ys, preferred_element_type=F32) + jnp.dot(lo, ys, preferred_element_type=F32)
    y_ref[...] = _rms(x_ref[...] + gt_ref[...] * ff, g_ref[...])


def _final(x1, ys, slots, gates, gt, g, tm, block0):
    n, d = x1.shape
    cap = ys.shape[1]
    mod_spec = (pl.BlockSpec((1, d), lambda i: (0, 0)) if gt.shape[0] == 1
                else pl.BlockSpec((tm, d), lambda i: (i, 0)))
    row = lambda w: pl.BlockSpec((tm, w), lambda i: (i, 0))
    return pl.pallas_call(
        _final_kernel,
        out_shape=jax.ShapeDtypeStruct((n, d), F32),
        grid=(n // tm,),
        in_specs=[row(d), pl.BlockSpec((None, cap, d), lambda i: (block0 + i, 0, 0)),
                  row(V7X_LANES), row(V7X_LANES), mod_spec, _full((1, d))],
        out_specs=row(d),
        compiler_params=_cparams(),
        name="final_norm",
    )(x1, ys, slots, gates, gt, g.reshape(1, d))


def kernel(x_prompt, x_sample, cache_k_win, cache_v_win, state_ssm_re, state_ssm_im, c_prompt, c_sample,
           w_ada, b_ada, g_norm1, w_in, ssm_a_re, ssm_a_im, ssm_log_dt, ssm_b_re, ssm_b_im, ssm_c_re, ssm_c_im,
           ssm_d, w_glu, b_glu, g_out_attn, g_out_ssm, w_out, g_norm2, w_router, b_router, w_gate_up, b_gate_up,
           w_down, b_down, g_final):
    depth = w_ada.shape[0]
    assert depth == 1 and x_prompt.shape[0] == 1
    bp, t, d = x_prompt.shape
    bs, ts, _ = x_sample.shape
    ns = bs * ts
    l = 0
    a = ATTN_WIDTH

    n_c = bp + bs
    c_pad = -(-n_c // V7X_SUBLANES) * V7X_SUBLANES
    c_rows = jnp.concatenate([c_prompt, c_sample, jnp.zeros((c_pad - n_c, d), F32)], axis=0)
    mod = _ada_modulation(c_rows, w_ada[l], b_ada[l])
    mod_p = [mod[0:1, i * d:(i + 1) * d] for i in range(N_MOD)]
    mod_s = [jnp.repeat(mod[bp:bp + bs, i * d:(i + 1) * d], ts, axis=0) for i in range(N_MOD)]

    steps_p = TOKEN_TILE // SSM_ROWS_PROMPT
    perm_p = _chunk_perm(SSM_ROWS_PROMPT, steps_p)
    perm_s = _chunk_perm(bs, ts)
    tabs = _ssm_tables(ssm_a_re[l], ssm_a_im[l], ssm_log_dt[l], ssm_b_re[l], ssm_b_im[l],
                       ssm_c_re[l], ssm_c_im[l], max(steps_p, ts))
    wr_pad = jnp.zeros((d, V7X_LANES), F32).at[:, :N_EXPERTS].set(w_router[l]).astype(BF16)
    br_pad = jnp.full((1, V7X_LANES), MASKED, F32).at[0, :N_EXPERTS].set(b_router[l])

    xp = x_prompt.reshape(t, d)
    dils = tuple(dil for _, dil in DILATED_BRANCHES)
    wide = tuple(dil for dil in dils if dil > 1)
    proj_p = _inproj(xp, mod_p[0], mod_p[1], g_norm1[l], w_in[l], jnp.asarray(perm_p, BF16), TOKEN_TILE, wide)
    kpf, vpf, up = proj_p[3:6]
    views = {1: proj_p[0:3]}
    for n_d, dil in enumerate(wide):
        views[dil] = proj_p[6 + 3 * n_d:9 + 3 * n_d]
    outs = [_attn_branch(*views[dil], dil) for dil in dils]
    ap = _attn_combine([o for o, _ in outs], [s for _, s in outs], g_out_attn[l], dils)
    zeros_h = jnp.zeros((SSM_ROWS_PROMPT, 2 * N_STATE), F32)
    zp, hp = _ssm(up, zeros_h, tabs, ssm_d[l], w_glu[l], b_glu[l], g_out_ssm[l],
                  jnp.asarray(perm_p.T, BF16), SSM_ROWS_PROMPT, steps_p, True)
    n_blocks_p = t // TOKEN_TILE
    n_blocks = n_blocks_p + 1
    x1p, xs_all, slot_p, gate_p, meta_p = _outproj(xp, ap, zp, mod_p[2], mod_p[3], mod_p[4], g_norm2[l], w_out[l],
                                                   wr_pad, br_pad, TOKEN_TILE, n_blocks, 0)

    xs = x_sample.reshape(ns, d)
    qs, ks, vs, ksf, vsf, us = _inproj(xs, mod_s[0], mod_s[1], g_norm1[l], w_in[l], jnp.asarray(perm_s, BF16), ns)
    split = lambda z: z.reshape(bs, ts, N_HEADS, HEAD_DIM)
    new_t = lambda z: jnp.pad(split(z).transpose(0, 2, 3, 1), ((0, 0), (0, 0), (0, 0), (0, NEW_KEY_PAD - ts)))
    as_ = _attn_sample(split(qs).transpose(0, 2, 1, 3), new_t(ksf), new_t(vsf),
                       cache_k_win.transpose(0, 1, 3, 4, 2), cache_v_win.transpose(0, 1, 3, 4, 2), l, g_out_attn[l])
    as_ = as_.transpose(0, 2, 1, 3)
    h0s = jnp.concatenate([state_ssm_re[l].reshape(bs, N_STATE), state_ssm_im[l].reshape(bs, N_STATE)], axis=1)
    zs, hs = _ssm(us, h0s, tabs, ssm_d[l], w_glu[l], b_glu[l], g_out_ssm[l],
                  jnp.asarray(perm_s.T, BF16), bs, ts, False)
    x1s, xs_all, slot_s, gate_s, meta_s = _outproj(xs, as_.reshape(ns, a), zs, mod_s[2], mod_s[3], mod_s[4],
                                                   g_norm2[l], w_out[l], wr_pad, br_pad, ns, n_blocks, n_blocks_p,
                                                   xs_prev=xs_all)

    meta = jnp.concatenate([meta_p, meta_s], axis=0)
    plan = _moe_plan(meta[:, 0, :N_EXPERTS], meta[:, 1, :N_EXPERTS])
    ys_all = _moe(plan, xs_all, w_gate_up[l], b_gate_up[l], w_down[l], b_down[l])

    y_prompt = _final(x1p, ys_all, slot_p, gate_p, mod_p[5], g_final, TOKEN_TILE, 0).reshape(bp, t, d)
    y_sample = _final(x1s, ys_all, slot_s, gate_s, mod_s[5], g_final, ns, n_blocks_p).reshape(bs, ts, d)

    keep = min(MAX_WINDOW, t)
    shp = (1, bp, keep, N_HEADS, HEAD_DIM)
    k_win = kpf[t - keep:].reshape(shp)
    v_win = vpf[t - keep:].reshape(shp)
    st = (1, bp, SSM_GROUPS, SSM_STATE)
    hp_last = hp[SSM_ROWS_PROMPT - 1]
    ss = (1, bs, SSM_GROUPS, SSM_STATE)
    return (y_prompt, y_sample, k_win, v_win,
            hp_last[:N_STATE].reshape(st), hp_last[N_STATE:].reshape(st),
            ksf.reshape(1, bs, ts, N_HEADS, HEAD_DIM), vsf.reshape(1, bs, ts, N_HEADS, HEAD_DIM),
            hs[:, :N_STATE].reshape(ss), hs[:, N_STATE:].reshape(ss))
```

```python
import functools

import numpy as np
import jax
import jax.numpy as jnp
from jax import lax
from jax.experimental import pallas as pl
from jax.experimental.pallas import tpu as pltpu

F32 = jnp.float32
BF16 = jnp.bfloat16

D_MODEL = 1024
N_HEADS = 8
HEAD_DIM = 64
ATTN_WIDTH = N_HEADS * HEAD_DIM
DILATED_BRANCHES = ((128, 1), (512, 4), (2048, 16))
KEYS_PER_BRANCH = 129
MAX_WINDOW = 2048
SSM_WIDTH = D_MODEL - ATTN_WIDTH
SSM_CH = 16
SSM_GROUPS = SSM_WIDTH // SSM_CH
SSM_STATE = 64
N_STATE = SSM_GROUPS * SSM_STATE
N_EXPERTS = 32
TOP_K = 4
D_EXPERT = D_MODEL
SWIGLU_LIMIT = 7.0
SWIGLU_ALPHA = 1.702
N_MOD = 6
EPS = 1e-6
MASKED = -1e30

V7X_LANES = 128
V7X_SUBLANES = 8
V7X_VMEM_LIMIT_BYTES = 56 * 1024 * 1024

TOKEN_TILE = 512
Q_TILE = 128
Q_BLOCKS_PER_STEP = 4
SSM_ROWS_PROMPT = 8
MOE_TILE = 512
ROW_ALIGN = 16
GROUP_CAP = -(-(TOKEN_TILE * TOP_K + N_EXPERTS * (ROW_ALIGN - 1)) // (2 * V7X_LANES)) * (2 * V7X_LANES)


def _cparams(n_axes=1):
    return pltpu.CompilerParams(
        dimension_semantics=("arbitrary",) * n_axes,
        vmem_limit_bytes=V7X_VMEM_LIMIT_BYTES,
    )


def _full(shape):
    n = len(shape)
    return pl.BlockSpec(shape, lambda *_: (0,) * n)


def _rms(x, g):
    return x * lax.rsqrt(jnp.mean(x * x, axis=-1, keepdims=True) + EPS) * g


def _sigmoid(x):
    return 1.0 / (1.0 + jnp.exp(-x))


def _ada_kernel(c_ref, w_ref, b_ref, o_ref):
    c = c_ref[...]
    s = (c * _sigmoid(c)).astype(BF16)
    o_ref[...] = jnp.dot(s, w_ref[...].astype(BF16), preferred_element_type=F32) + b_ref[...]


def _ada_modulation(c_rows, w_ada, b_ada):
    m, d = c_rows.shape
    n = w_ada.shape[1]
    tn = n // 4
    return pl.pallas_call(
        _ada_kernel,
        out_shape=jax.ShapeDtypeStruct((m, n), F32),
        grid=(n // tn,),
        in_specs=[_full((m, d)),
                  pl.BlockSpec((d, tn), lambda j: (0, j)),
                  pl.BlockSpec((1, tn), lambda j: (0, j))],
        out_specs=pl.BlockSpec((m, tn), lambda j: (0, j)),
        compiler_params=_cparams(),
        name="ada_modulation",
    )(c_rows, w_ada, b_ada.reshape(1, n))


def _inproj_kernel(*refs, dils):
    x_ref, sh_ref, sc_ref, g_ref, w_ref, perm_ref = refs[:6]
    dperm_refs = refs[6:6 + len(dils)]
    q_ref, k_ref, v_ref, kf_ref, vf_ref, u_ref = refs[6 + len(dils):12 + len(dils)]
    dil_refs = refs[12 + len(dils):-1]
    wbf_ref = refs[-1]

    @pl.when(pl.program_id(0) == 0)
    def _():
        wbf_ref[...] = w_ref[...].astype(BF16)

    h = _rms(x_ref[...], g_ref[...]) * (1.0 + sc_ref[...]) + sh_ref[...]
    hb = h.astype(BF16)
    a = ATTN_WIDTH
    proj = jnp.dot(hb, wbf_ref[:, :3 * a], preferred_element_type=F32)
    k = proj[:, a:2 * a]
    v = proj[:, 2 * a:]
    qkv = jnp.concatenate([(proj[:, :a] * (HEAD_DIM ** -0.5)).astype(BF16), k.astype(BF16), v.astype(BF16)],
                          axis=1)
    q_ref[...] = qkv[:, :a]
    k_ref[...] = qkv[:, a:2 * a]
    v_ref[...] = qkv[:, 2 * a:]
    kf_ref[...] = k
    vf_ref[...] = v
    hp = jnp.dot(perm_ref[...], hb, preferred_element_type=F32).astype(BF16)
    u_ref[...] = jnp.dot(hp, wbf_ref[:, 3 * a:], preferred_element_type=F32)
    tm = qkv.shape[0]
    for n_d, dil in enumerate(dils):
        by_residue = jnp.dot(dperm_refs[n_d][...], qkv, preferred_element_type=F32).astype(BF16)
        per = tm // dil
        for r in range(dil):
            rows = by_residue[r * per:(r + 1) * per]
            for j in range(3):
                dil_refs[3 * n_d + j][:, r * a:(r + 1) * a] = rows[:, j * a:(j + 1) * a]


def _inproj(x, sh, sc, g, w_in, perm, tm, dils=()):
    n, d = x.shape
    a = ATTN_WIDTH
    mod_rows = sh.shape[0]
    mod_spec = (pl.BlockSpec((1, d), lambda i: (0, 0)) if mod_rows == 1
                else pl.BlockSpec((tm, d), lambda i: (i, 0)))
    row = lambda w: pl.BlockSpec((tm, w), lambda i: (i, 0))
    dperms = [jnp.asarray(_chunk_perm(tm // dil, dil), BF16) for dil in dils]
    view_shapes = tuple(jax.ShapeDtypeStruct((n // dil, dil * a), BF16) for dil in dils for _ in range(3))
    view_specs = tuple(pl.BlockSpec((tm // dil, dil * a), lambda i: (i, 0)) for dil in dils for _ in range(3))
    return pl.pallas_call(
        functools.partial(_inproj_kernel, dils=tuple(dils)),
        out_shape=(jax.ShapeDtypeStruct((n, a), BF16),) * 3
        + (jax.ShapeDtypeStruct((n, a), F32),) * 2
        + (jax.ShapeDtypeStruct((n, SSM_WIDTH), F32),) + view_shapes,
        grid=(n // tm,),
        in_specs=[row(d), mod_spec, mod_spec, _full((1, d)), _full(w_in.shape), _full((tm, tm))]
        + [_full((tm, tm))] * len(dils),
        out_specs=(row(a),) * 5 + (row(SSM_WIDTH),) + view_specs,
        scratch_shapes=[pltpu.VMEM(w_in.shape, BF16)],
        compiler_params=_cparams(),
        name="inproj",
    )(x, sh, sc, g.reshape(1, d), w_in, perm, *dperms)


def _chunk_perm(rows, steps):
    n = rows * steps
    p = np.zeros((n, n), np.float32)
    c, t = np.meshgrid(np.arange(rows), np.arange(steps), indexing="ij")
    p[(t * rows + c).ravel(), (c * steps + t).ravel()] = 1.0
    return p


def _alibi_slopes():
    return np.exp2(-8.0 * np.arange(1, N_HEADS + 1, dtype=np.float64) / N_HEADS).astype(np.float32)


def _branch_bias(dil):
    qi = np.arange(Q_TILE)[:, None]
    col = np.arange(2 * Q_TILE)[None, :]
    j = Q_TILE + qi - col
    valid = (j >= 0) & (j <= Q_TILE)
    dist = (j * dil).astype(np.float32)
    tabs = []
    for first in (True, False):
        ok = valid & (col >= Q_TILE) if first else valid
        per_head = [np.where(ok, -s * dist, np.float32(MASKED)) for s in _alibi_slopes()]
        tabs.append(np.concatenate(per_head, axis=0))
    return np.stack(tabs).astype(np.float32)


def _attn_branch_kernel(q_ref, kp_ref, kc_ref, vp_ref, vc_ref, bias_ref, o_ref, lse_ref):
    first_step = pl.program_id(0) == 0
    lane = lax.broadcasted_iota(jnp.int32, (Q_TILE, V7X_LANES), 1)
    lo = lane < HEAD_DIM
    for j in range(Q_BLOCKS_PER_STEP):
        rows = slice(j * Q_TILE, (j + 1) * Q_TILE)
        before = slice((j - 1) * Q_TILE, j * Q_TILE)
        sel = jnp.where(first_step, 0, 1) if j == 0 else 1
        lse_acc = jnp.zeros((Q_TILE, V7X_LANES), F32)
        for p in range(N_HEADS // 2):
            cs = slice(V7X_LANES * p, V7X_LANES * (p + 1))
            q2 = q_ref[rows, cs]
            zero = jnp.zeros_like(q2)
            qq = jnp.concatenate([jnp.where(lo, q2, zero), jnp.where(lo, zero, q2)], axis=0)
            k_before = kp_ref[:, cs] if j == 0 else kc_ref[before, cs]
            v_before = vp_ref[:, cs] if j == 0 else vc_ref[before, cs]
            kk = jnp.concatenate([k_before, kc_ref[rows, cs]], axis=0)
            vv = jnp.concatenate([v_before, vc_ref[rows, cs]], axis=0)
            s = lax.dot_general(qq, kk, (((1,), (1,)), ((), ())), preferred_element_type=F32)
            s = s + bias_ref[sel, 2 * Q_TILE * p:2 * Q_TILE * (p + 1), :]
            m = jnp.max(s, axis=1, keepdims=True)
            e = jnp.exp(s - m)
            l = jnp.sum(e, axis=1, keepdims=True)
            eb = e.astype(BF16)
            o0 = jnp.dot(eb[:Q_TILE], vv, preferred_element_type=F32) * (1.0 / l[:Q_TILE])
            o1 = jnp.dot(eb[Q_TILE:], vv, preferred_element_type=F32) * (1.0 / l[Q_TILE:])
            o_ref[rows, cs] = jnp.where(lo, o0, o1).astype(o_ref.dtype)
            lse = m + jnp.log(l)
            lse_acc = jnp.where(lane == 2 * p, lse[:Q_TILE], lse_acc)
            lse_acc = jnp.where(lane == 2 * p + 1, lse[Q_TILE:], lse_acc)
        lse_ref[rows, :] = lse_acc[:, :N_HEADS]


def _attn_branch(qv, kv, vv, dil):
    a = ATTN_WIDTH
    rows = qv.shape[0]
    t = rows * dil
    step = Q_BLOCKS_PER_STEP * Q_TILE
    cur = pl.BlockSpec((step, a), lambda i, r: (i, r))
    prev = pl.BlockSpec((Q_TILE, a), lambda i, r: (jnp.maximum(i * Q_BLOCKS_PER_STEP - 1, 0), r))
    bias = jnp.asarray(_branch_bias(dil))
    o, lse = pl.pallas_call(
        _attn_branch_kernel,
        out_shape=(jax.ShapeDtypeStruct((rows, dil * a), BF16),
                   jax.ShapeDtypeStruct((dil, rows, N_HEADS), F32)),
        grid=(rows // step, dil),
        in_specs=[cur, prev, cur, prev, cur, _full(bias.shape)],
        out_specs=(cur, pl.BlockSpec((None, step, N_HEADS), lambda i, r: (r, i, 0))),
        compiler_params=_cparams(2),
        name=f"attn_branch_d{dil}",
    )(qv, kv, kv, vv, vv, bias)
    return o, lse.transpose(1, 0, 2).reshape(t, N_HEADS)


def _attn_combine_kernel(*refs, dils):
    nb = len(dils)
    o_refs = refs[:nb]
    l_refs = refs[nb:2 * nb]
    g_ref = refs[2 * nb]
    unperm_refs = refs[2 * nb + 1:-1]
    a_ref = refs[-1]
    tq, a = a_ref.shape
    outs = []
    n_u = 0
    for o_ref, dil in zip(o_refs, dils):
        if dil == 1:
            outs.append(o_ref[...].astype(F32))
            continue
        by_residue = jnp.concatenate([o_ref[:, r * a:(r + 1) * a] for r in range(dil)], axis=0)
        outs.append(jnp.dot(unperm_refs[n_u][...], by_residue, preferred_element_type=F32))
        n_u += 1
    ls = [l_ref[...] for l_ref in l_refs]
    top = functools.reduce(jnp.maximum, ls)
    ws = [jnp.exp(l - top) for l in ls]
    inv = 1.0 / functools.reduce(jnp.add, ws)
    cs = [w * inv for w in ws]
    lane = lax.broadcasted_iota(jnp.int32, (tq, V7X_LANES), 1)
    lo = lane < HEAD_DIM
    cols = []
    for p in range(N_HEADS // 2):
        sl = slice(V7X_LANES * p, V7X_LANES * (p + 1))
        acc = jnp.zeros((tq, V7X_LANES), F32)
        for c, o in zip(cs, outs):
            cexp = jnp.where(lo,
                             jnp.broadcast_to(c[:, 2 * p:2 * p + 1], (tq, V7X_LANES)),
                             jnp.broadcast_to(c[:, 2 * p + 1:2 * p + 2], (tq, V7X_LANES)))
            acc = acc + cexp * o[:, sl]
        cols.append(acc)
    o = jnp.concatenate(cols, axis=1)
    a_ref[...] = _rms(o, g_ref[...]).astype(a_ref.dtype)


def _attn_combine(os_, lses, g, dils):
    t = lses[0].shape[0]
    a = ATTN_WIDTH
    tq = TOKEN_TILE
    unperms = [jnp.asarray(_chunk_perm(tq // dil, dil).T, BF16) for dil in dils if dil > 1]
    return pl.pallas_call(
        functools.partial(_attn_combine_kernel, dils=tuple(dils)),
        out_shape=jax.ShapeDtypeStruct((t, a), BF16),
        grid=(t // tq,),
        in_specs=[pl.BlockSpec((tq // dil, dil * a), lambda i: (i, 0)) for dil in dils]
        + [pl.BlockSpec((tq, N_HEADS), lambda i: (i, 0))] * len(dils) + [_full((1, a))]
        + [_full((tq, tq))] * len(unperms),
        out_specs=pl.BlockSpec((tq, a), lambda i: (i, 0)),
        compiler_params=_cparams(),
        name="attn_combine",
    )(*os_, *lses, g.reshape(1, a), *unperms)


NEW_KEY_PAD = V7X_LANES


def _sample_bias(win, steps):
    slopes = _alibi_slopes()[None, :, None, None]
    t = np.arange(steps)[None, None, :, None]

    def table(dist, live):
        tabs = []
        for window, dil in DILATED_BRANCHES:
            ok = live & (dist >= 0) & (dist <= window) & (dist % dil == 0)
            tabs.append(np.where(ok, -slopes * dist.astype(np.float32), np.float32(MASKED))[0])
        return np.stack(tabs).astype(np.float32)

    pos = np.arange(win)[None, None, None, :]
    col = np.arange(NEW_KEY_PAD)[None, None, None, :]
    return table(win + t - pos, np.bool_(True)), table(t - col, col < steps)


def _attn_sample_kernel(q_ref, kn_ref, vn_ref, kt_ref, vt_ref, bo_ref, bn_ref, g_ref, o_ref):
    nb = bo_ref.shape[0]
    steps = q_ref.shape[1]
    nt = lambda p, v: lax.dot_general(p, v, (((1,), (1,)), ((), ())), preferred_element_type=F32)
    res = []
    sq = jnp.zeros((steps, 1), F32)
    for h in range(N_HEADS):
        q = q_ref[h]
        vt = vt_ref[h].astype(BF16)
        vn = vn_ref[h].astype(BF16)
        s_old = jnp.dot(q, kt_ref[h].astype(BF16), preferred_element_type=F32)
        s_new = jnp.dot(q, kn_ref[h].astype(BF16), preferred_element_type=F32)
        ms, ls, e_old, e_new = [], [], [], []
        for b in range(nb):
            so = s_old + bo_ref[b, h]
            sn = s_new + bn_ref[b, h]
            m = jnp.maximum(jnp.max(so, axis=1, keepdims=True), jnp.max(sn, axis=1, keepdims=True))
            eo = jnp.exp(so - m)
            en = jnp.exp(sn - m)
            ms.append(m)
            ls.append(jnp.sum(eo, axis=1, keepdims=True) + jnp.sum(en, axis=1, keepdims=True))
            e_old.append(eo)
            e_new.append(en)
        o_all = (nt(jnp.concatenate(e_old, axis=0).astype(BF16), vt)
                 + nt(jnp.concatenate(e_new, axis=0).astype(BF16), vn))
        top = functools.reduce(jnp.maximum, ms)
        num = jnp.zeros((steps, HEAD_DIM), F32)
        den = jnp.zeros((steps, 1), F32)
        for b in range(nb):
            w = jnp.exp(ms[b] - top)
            num = num + o_all[b * steps:(b + 1) * steps] * w
            den = den + ls[b] * w
        r = num * (1.0 / den)
        res.append(r)
        sq = sq + jnp.sum(r * r, axis=1, keepdims=True)
    inv = lax.rsqrt(sq * (1.0 / ATTN_WIDTH) + EPS)
    for h in range(N_HEADS):
        o_ref[h] = res[h] * inv * g_ref[h]


def _attn_sample(q, kn_t, vn_t, cache_kt, cache_vt, layer, g):
    b, heads, steps, dh = q.shape
    win = cache_kt.shape[-1]
    b_old, b_new = (jnp.asarray(z) for z in _sample_bias(win, steps))
    per_b = lambda *tail: pl.BlockSpec((None,) + tail, lambda i: (i,) + (0,) * len(tail))
    cache = pl.BlockSpec((None, None, heads, dh, win), lambda i: (layer, i, 0, 0, 0))
    return pl.pallas_call(
        _attn_sample_kernel,
        out_shape=jax.ShapeDtypeStruct((b, heads, steps, dh), F32),
        grid=(b,),
        in_specs=[per_b(heads, steps, dh), per_b(heads, dh, NEW_KEY_PAD), per_b(heads, dh, NEW_KEY_PAD),
                  cache, cache, _full(b_old.shape), _full(b_new.shape), _full((heads, 1, dh))],
        out_specs=per_b(heads, steps, dh),
        compiler_params=_cparams(),
        name="attn_sample",
    )(q, kn_t, vn_t, cache_kt, cache_vt, b_old, b_new, g.reshape(heads, 1, dh))


def _gelu_tanh(x):
    return 0.5 * x * (1.0 + jnp.tanh(np.sqrt(2.0 / np.pi).astype(np.float32) * (x + 0.044715 * (x * x * x))))


def _ssm_kernel(u_ref, bb_ref, lam_ref, pow_ref, cm_ref, dsk_ref, wglu_ref, bglu_ref, g_ref, pt_ref, h0_ref,
                z_ref, ht_ref, h_s, hin_s, carry_s, *, rows, steps, chain):
    ns = N_STATE
    cw = (4 * V7X_SUBLANES * V7X_LANES) // rows
    last = (steps - 1) * rows

    @pl.when(pl.program_id(0) == 0)
    def _():
        carry_s[...] = h0_ref[0:1, :]

    ub = u_ref[...].astype(BF16)
    n_slabs = SSM_WIDTH // V7X_LANES
    sw = ns // n_slabs
    for s in range(n_slabs):
        part = jnp.dot(ub[:, s * V7X_LANES:(s + 1) * V7X_LANES], bb_ref[s], preferred_element_type=F32)
        h_s[:, s * sw:(s + 1) * sw] = part[:, :sw]
        h_s[:, ns + s * sw:ns + (s + 1) * sw] = part[:, sw:]

    for cc in range(0, ns // cw, 2):
        crs = [slice(c * cw, (c + 1) * cw) for c in (cc, cc + 1)]
        cis = [slice(ns + c * cw, ns + (c + 1) * cw) for c in (cc, cc + 1)]
        lrs = [jnp.broadcast_to(lam_ref[0:1, cr], (rows, cw)) for cr in crs]
        lis = [jnp.broadcast_to(lam_ref[1:2, cr], (rows, cw)) for cr in crs]

        def scan_body(t, carry, crs=crs, cis=cis, lrs=lrs, lis=lis):
            rs = pl.ds(pl.multiple_of(t * rows, rows), rows)
            out = []
            for j in range(2):
                hr, hi = carry[2 * j], carry[2 * j + 1]
                nr = lrs[j] * hr - lis[j] * hi + h_s[rs, crs[j]]
                ni = lrs[j] * hi + lis[j] * hr + h_s[rs, cis[j]]
                h_s[rs, crs[j]] = nr
                h_s[rs, cis[j]] = ni
                out += [nr, ni]
            return tuple(out)

        zero = jnp.zeros((rows, cw), F32)
        lax.fori_loop(0, steps, scan_body, (zero,) * 4)

    if chain:
        ptr = pow_ref[steps - 1, 0:1, :ns]
        pti = pow_ref[steps - 1, 0:1, ns:]
        carry = carry_s[...]
        for c in range(rows):
            hin_s[c:c + 1, :] = carry
            cr_, ci_ = carry[:, :ns], carry[:, ns:]
            e = h_s[last + c:last + c + 1, :]
            carry = jnp.concatenate([ptr * cr_ - pti * ci_ + e[:, :ns],
                                     ptr * ci_ + pti * cr_ + e[:, ns:]], axis=1)
        carry_s[...] = carry
    else:
        hin_s[...] = h0_ref[...]

    for cc in range(ns // cw):
        cr = slice(cc * cw, (cc + 1) * cw)
        ci = slice(ns + cc * cw, ns + (cc + 1) * cw)
        hr0 = hin_s[:, cr]
        hi0 = hin_s[:, ci]

        def fix_body(t, _, cr=cr, ci=ci, hr0=hr0, hi0=hi0):
            rs = pl.ds(pl.multiple_of(t * rows, rows), rows)
            pr = pow_ref[t, :, cr]
            pi_ = pow_ref[t, :, ci]
            h_s[rs, cr] = h_s[rs, cr] + (pr * hr0 - pi_ * hi0)
            h_s[rs, ci] = h_s[rs, ci] + (pr * hi0 + pi_ * hr0)
            return 0

        lax.fori_loop(0, steps, fix_body, 0, unroll=4)

    ht_ref[...] = h_s[last:last + rows, :]
    ys = []
    for s in range(n_slabs):
        hs = jnp.concatenate([h_s[:, s * sw:(s + 1) * sw], h_s[:, ns + s * sw:ns + (s + 1) * sw]], axis=1)
        ys.append(jnp.dot(hs.astype(BF16), cm_ref[s], preferred_element_type=F32))
    y = jnp.concatenate(ys, axis=1) + dsk_ref[...] * u_ref[...]
    y = _gelu_tanh(y)
    gl = jnp.dot(y.astype(BF16), wglu_ref[...], preferred_element_type=F32) + bglu_ref[...]
    z = _rms(y * _sigmoid(gl), g_ref[...]).astype(BF16)
    z_ref[...] = jnp.dot(pt_ref[...], z, preferred_element_type=F32).astype(z_ref.dtype)


def _ssm_tables(ssm_a_re, ssm_a_im, ssm_log_dt, ssm_b_re, ssm_b_im, ssm_c_re, ssm_c_im, max_steps):
    g, n, ch = SSM_GROUPS, SSM_STATE, SSM_CH
    a_re = ssm_a_re.astype(F32)
    a_im = ssm_a_im.astype(F32)
    dt = jnp.exp(ssm_log_dt.astype(F32))[:, None]
    mag = jnp.exp(dt * a_re)
    lam_re = mag * jnp.cos(dt * a_im)
    lam_im = mag * jnp.sin(dt * a_im)
    nr = lam_re - 1.0
    ni = lam_im
    inv = 1.0 / (a_re * a_re + a_im * a_im)
    coef_re = (nr * a_re + ni * a_im) * inv
    coef_im = (ni * a_re - nr * a_im) * inv
    br = ssm_b_re.astype(F32)
    bi = ssm_b_im.astype(F32)
    bb_re = coef_re[..., None] * br - coef_im[..., None] * bi
    bb_im = coef_re[..., None] * bi + coef_im[..., None] * br
    gs = V7X_LANES // ch
    ns_ = g // gs
    eye = jnp.eye(gs, dtype=F32)
    bmat = lambda b: jnp.einsum("sgnc,gh->sgchn", b.reshape(ns_, gs, n, ch), eye).reshape(ns_, gs * ch, gs * n)
    cmat = lambda c: jnp.einsum("sgcn,gh->sgnhc", c.astype(F32).reshape(ns_, gs, ch, n), eye).reshape(
        ns_, gs * n, gs * ch)
    bb = jnp.concatenate([bmat(bb_re), bmat(bb_im)], axis=2).astype(BF16)
    cm = jnp.concatenate([cmat(ssm_c_re), -cmat(ssm_c_im)], axis=1).astype(BF16)
    lam = jnp.stack([lam_re.reshape(-1), lam_im.reshape(-1)])

    k = jnp.arange(1, max_steps + 1, dtype=F32)[:, None]
    kdt = k * dt.reshape(1, -1).repeat(n, axis=1)
    pmag = jnp.exp(kdt * a_re.reshape(1, -1))
    parg = kdt * a_im.reshape(1, -1)
    pows = jnp.concatenate([pmag * jnp.cos(parg), pmag * jnp.sin(parg)], axis=1)
    return bb, cm, lam, pows


def _ssm(u_perm, h0, tabs, dsk, w_glu, b_glu, g, perm_t, rows, steps, chain):
    n, w = u_perm.shape
    blk = rows * steps
    bb, cm, lam, pows = tabs
    row = pl.BlockSpec((blk, w), lambda i: (i, 0))
    kern = functools.partial(_ssm_kernel, rows=rows, steps=steps, chain=chain)
    return pl.pallas_call(
        kern,
        out_shape=(jax.ShapeDtypeStruct((n, w), BF16), jax.ShapeDtypeStruct((rows, 2 * N_STATE), F32)),
        grid=(n // blk,),
        in_specs=[row, _full(bb.shape), _full(lam.shape), _full((steps, rows, 2 * N_STATE)), _full(cm.shape),
                  _full((1, w)), _full((w, w)), _full((1, w)), _full((1, w)), _full((blk, blk)),
                  _full((rows, 2 * N_STATE))],
        out_specs=(row, _full((rows, 2 * N_STATE))),
        scratch_shapes=[pltpu.VMEM((blk, 2 * N_STATE), F32),
                        pltpu.VMEM((rows, 2 * N_STATE), F32),
                        pltpu.VMEM((1, 2 * N_STATE), F32)],
        compiler_params=_cparams(),
        name=f"ssm_r{rows}",
    )(u_perm, bb, lam, jnp.broadcast_to(pows[:steps, None, :], (steps, rows, 2 * N_STATE)), cm,
      dsk.reshape(1, w), w_glu.astype(BF16), b_glu.reshape(1, w),
      g.reshape(1, w), perm_t, h0)


def _outproj_kernel(*refs, n_real, aliased):
    ins, outs = refs[:12], refs[12 + aliased:]
    step = pl.program_id(0)

    @pl.when(step < n_real)
    def _():
        _outproj_tile(*ins, *outs)

    @pl.when(step >= n_real)
    def _():
        xs_ref = outs[1]
        xs_ref[...] = jnp.zeros(xs_ref.shape, xs_ref.dtype)


def _outproj_tile(x_ref, a_ref, z_ref, gt_ref, sh_ref, sc_ref, g_ref, wo_ref, wr_ref, br_ref, ltri_ref, utri_ref,
                  x1_ref, xs_ref, slot_ref, gates_ref, meta_ref, wbf_ref):
    @pl.when(pl.program_id(0) == 0)
    def _():
        wbf_ref[...] = wo_ref[...].astype(BF16)

    a = ATTN_WIDTH
    mixed = (jnp.dot(a_ref[...].astype(BF16), wbf_ref[:a, :], preferred_element_type=F32)
             + jnp.dot(z_ref[...], wbf_ref[a:, :], preferred_element_type=F32))
    x1 = x_ref[...] + gt_ref[...] * mixed
    x1_ref[...] = x1
    h2 = (_rms(x1, g_ref[...]) * (1.0 + sc_ref[...]) + sh_ref[...]).astype(BF16)
    lg = jnp.dot(h2, wr_ref[...], preferred_element_type=F32) + br_ref[...]
    tm = lg.shape[0]
    lane = lax.broadcasted_iota(jnp.int32, (tm, V7X_LANES), 1).astype(F32)
    vals, hots = [], []
    for _ in range(TOP_K):
        m = jnp.max(lg, axis=1, keepdims=True)
        idx = jnp.min(jnp.where(lg == m, lane, float(V7X_LANES)), axis=1, keepdims=True)
        hot = lane == idx
        vals.append(m)
        hots.append(jnp.where(hot, 1.0, 0.0))
        lg = jnp.where(hot, MASKED * 2, lg)
    es = [jnp.exp(v - vals[0]) for v in vals]
    inv = 1.0 / (es[0] + es[1] + es[2] + es[3])

    member = hots[0] + hots[1] + hots[2] + hots[3]
    before = jnp.dot(ltri_ref[...], member.astype(BF16), preferred_element_type=F32)
    count = jnp.sum(member, axis=0, keepdims=True)
    padded = jnp.floor((count + (ROW_ALIGN - 1.0)) * (1.0 / ROW_ALIGN)) * ROW_ALIGN
    padded8 = jnp.broadcast_to(padded, (V7X_SUBLANES, V7X_LANES))
    start = jnp.dot(padded8.astype(BF16), utri_ref[...], preferred_element_type=F32)[0:1]
    where_to = start + before
    slots = jnp.zeros((tm, V7X_LANES), F32)
    gates = jnp.zeros((tm, V7X_LANES), F32)
    for k in range(TOP_K):
        slot_k = jnp.sum(hots[k] * where_to, axis=1, keepdims=True)
        slots = jnp.where(lane == float(k), slot_k, slots)
        gates = jnp.where(lane == float(k), es[k] * inv, gates)
    slot_ref[...] = slots
    gates_ref[...] = gates
    row = lax.broadcasted_iota(jnp.int32, (V7X_SUBLANES, V7X_LANES), 0)
    meta_ref[...] = jnp.where(row == 0, padded8, jnp.where(row == 1, jnp.broadcast_to(start, padded8.shape), 0.0))

    cap = xs_ref.shape[0]
    slots_t = jnp.transpose(slots)
    srow = lax.broadcasted_iota(jnp.int32, (cap, tm), 0).astype(F32)
    place = jnp.zeros((cap, tm), F32)
    for k in range(TOP_K):
        place = jnp.where(srow == slots_t[k:k + 1, :], 1.0, place)
    xs_ref[...] = jnp.dot(place.astype(BF16), h2, preferred_element_type=F32).astype(BF16)


def _outproj(x, a, z, gt, sh, sc, g, w_out, wr_pad, br_pad, tm, n_blocks, block0, xs_prev=None):
    n, d = x.shape
    aw = ATTN_WIDTH
    n_real = n // tm
    aliased = xs_prev is not None
    n_steps = n_real if aliased else n_blocks
    tile = lambda i: jnp.minimum(i, n_real - 1)
    mod_spec = (pl.BlockSpec((1, d), lambda i: (0, 0)) if gt.shape[0] == 1
                else pl.BlockSpec((tm, d), lambda i: (tile(i), 0)))
    row = lambda w: pl.BlockSpec((tm, w), lambda i: (tile(i), 0))
    ltri = jnp.asarray(np.tril(np.ones((tm, tm), np.float32), -1), BF16)
    utri = jnp.asarray(np.triu(np.ones((V7X_LANES, V7X_LANES), np.float32), 1), BF16)
    in_specs = [row(d), row(aw), row(SSM_WIDTH), mod_spec, mod_spec, mod_spec, _full((1, d)),
                _full(w_out.shape), _full(wr_pad.shape), _full(br_pad.shape), _full(ltri.shape), _full(utri.shape)]
    args = [x, a, z, gt, sh, sc, g.reshape(1, d), w_out, wr_pad, br_pad, ltri, utri]
    if aliased:
        in_specs.append(pl.BlockSpec(memory_space=pl.ANY))
        args.append(xs_prev)
    return pl.pallas_call(
        functools.partial(_outproj_kernel, n_real=n_real, aliased=int(aliased)),
        out_shape=(jax.ShapeDtypeStruct((n, d), F32),
                   jax.ShapeDtypeStruct((n_blocks, GROUP_CAP, d), BF16),
                   jax.ShapeDtypeStruct((n, V7X_LANES), F32), jax.ShapeDtypeStruct((n, V7X_LANES), F32),
                   jax.ShapeDtypeStruct((n_real, V7X_SUBLANES, V7X_LANES), F32)),
        grid=(n_steps,),
        in_specs=in_specs,
        out_specs=(row(d), pl.BlockSpec((None, GROUP_CAP, d), lambda i: (block0 + i, 0, 0)),
                   row(V7X_LANES), row(V7X_LANES),
                   pl.BlockSpec((None, V7X_SUBLANES, V7X_LANES), lambda i: (tile(i), 0, 0))),
        scratch_shapes=[pltpu.VMEM(w_out.shape, BF16)],
        input_output_aliases={len(args) - 1: 1} if aliased else {},
        compiler_params=_cparams(),
        name="outproj_router",
    )(*args)


_PIECE_SIZES = tuple(MOE_TILE >> s for s in range(6))
_CHUNK_SHIFT = 6
_CHUNK = 1 << _CHUNK_SHIFT
_TAIL_SIZES = tuple(sz for sz in _PIECE_SIZES if sz < _CHUNK)


def _moe_kernel(te_ref, tl_ref, lo_ref, hi_ref, nu_ref, gstart_ref, gsize_ref, gbase_ref, rows_ref, used_ref,
                xs_hbm, wgu_ref, bgu_ref, wd_ref, bd_ref, ys_hbm,
                xbuf, ybuf, zbuf, wgu_bf, wd_bf, xsem, ysem, zsem):
    t = pl.program_id(0)
    n_used = nu_ref[0]
    n_blocks, cap = xs_hbm.shape[0], xs_hbm.shape[1]

    def x_copy(i, br, tr, sz, slot):
        pltpu.make_async_copy(xs_hbm.at[i, pl.ds(br, sz)], xbuf.at[slot, pl.ds(tr, sz)], xsem.at[slot]).start()

    def y_copy(i, br, tr, sz, slot):
        pltpu.make_async_copy(ybuf.at[slot, pl.ds(tr, sz)], ys_hbm.at[i, pl.ds(br, sz)], ysem.at[slot]).start()

    def z_copy(i, row, sz, start):
        cp = pltpu.make_async_copy(zbuf.at[pl.ds(0, sz)], ys_hbm.at[i, pl.ds(row, sz)], zsem)
        cp.start() if start else cp.wait()

    def pieces(tt, fn):
        e = te_ref[tt]
        lo = tl_ref[tt] * MOE_TILE

        def per_block(i, c):
            g = e * n_blocks + i
            s0 = gbase_ref[g]
            a = jnp.maximum(s0, lo)
            b = jnp.minimum(s0 + gsize_ref[g], lo + MOE_TILE)
            length = jnp.maximum(b - a, 0)
            src = gstart_ref[g] + (a - s0)
            dst = a - lo
            whole = lax.shift_right_logical(length, _CHUNK_SHIFT)

            def chunk(j, cc):
                off = j * _CHUNK
                fn(i, pl.multiple_of(src + off, ROW_ALIGN), pl.multiple_of(dst + off, ROW_ALIGN), _CHUNK)
                return cc

            lax.fori_loop(0, whole, chunk, 0)
            done = whole * _CHUNK
            for sz in _TAIL_SIZES:
                hit = (length & sz) != 0

                @pl.when(hit)
                def _(sz=sz, done=done):
                    fn(i, pl.multiple_of(src + done, ROW_ALIGN), pl.multiple_of(dst + done, ROW_ALIGN), sz)

                done = done + jnp.where(hit, sz, 0)
            return c

        lax.fori_loop(lo_ref[tt], hi_ref[tt], per_block, 0)

    def tile_rows(tt):
        return jnp.minimum(rows_ref[te_ref[tt]] - tl_ref[tt] * MOE_TILE, MOE_TILE)

    def wait_rows(n, sem, buf):
        def chunk(j, cc):
            pltpu.make_async_copy(buf.at[pl.ds(0, _CHUNK)], buf.at[pl.ds(0, _CHUNK)], sem).wait()
            return cc

        lax.fori_loop(0, lax.shift_right_logical(n, _CHUNK_SHIFT), chunk, 0)
        for sz in _TAIL_SIZES:
            @pl.when((n & sz) != 0)
            def _(sz=sz):
                pltpu.make_async_copy(buf.at[pl.ds(0, sz)], buf.at[pl.ds(0, sz)], sem).wait()

    def fetch(tt, slot):
        pieces(tt, lambda i, br, tr, sz: x_copy(i, br, tr, sz, slot))

    def writeback(tt, slot):
        pieces(tt, lambda i, br, tr, sz: y_copy(i, br, tr, sz, slot))

    def zero_tail(i, start):
        u = used_ref[i]
        rem = cap - u
        nz = zbuf.shape[0]
        whole = lax.shift_right_logical(rem, nz.bit_length() - 1)

        def chunk(j, c):
            z_copy(i, pl.multiple_of(u + j * nz, ROW_ALIGN), nz, start)
            return c

        lax.fori_loop(0, whole, chunk, 0)
        base = u + whole * nz
        done = jnp.int32(0)
        for sz in _PIECE_SIZES:
            if sz >= nz:
                continue
            hit = (rem & sz) != 0

            @pl.when(hit)
            def _(sz=sz, done=done):
                z_copy(i, pl.multiple_of(base + done, ROW_ALIGN), sz, start)

            done = done + jnp.where(hit, sz, 0)

    @pl.when(t == 0)
    def _():
        xbuf[...] = jnp.zeros(xbuf.shape, xbuf.dtype)
        zbuf[...] = jnp.zeros(zbuf.shape, zbuf.dtype)
        lax.fori_loop(0, n_blocks, lambda i, c: (zero_tail(i, True), c)[1], 0)
        lax.fori_loop(0, n_blocks, lambda i, c: (zero_tail(i, False), c)[1], 0)
        fetch(0, 0)

    @pl.when(t < n_used)
    def _():
        slot = t % 2

        @pl.when(t + 1 < n_used)
        def _():
            fetch(t + 1, 1 - slot)

        prev = te_ref[jnp.maximum(t - 1, 0)]
        fresh = jnp.logical_or(t == 0, te_ref[t] != prev)

        @pl.when(fresh)
        def _():
            wgu_bf[...] = wgu_ref[...].astype(BF16)
            wd_bf[...] = wd_ref[...].astype(BF16)

        wait_rows(tile_rows(t), xsem.at[slot], xbuf.at[slot])

        @pl.when(t >= 2)
        def _():
            wait_rows(tile_rows(t - 2), ysem.at[slot], ybuf.at[slot])

        def expert_mlp(m):
            f = D_EXPERT
            gu = jnp.dot(xbuf[slot, pl.ds(0, m)], wgu_bf[...], preferred_element_type=F32) + bgu_ref[...]
            gate = jnp.minimum(gu[:, :f], SWIGLU_LIMIT)
            up = jnp.clip(gu[:, f:], -SWIGLU_LIMIT, SWIGLU_LIMIT)
            act = (up + 1.0) * gate * _sigmoid(SWIGLU_ALPHA * gate)
            y = jnp.dot(act.astype(BF16), wd_bf[...], preferred_element_type=F32) + bd_ref[...]
            ybuf[slot, pl.ds(0, m)] = y.astype(BF16)

        filled = tile_rows(t)

        @pl.when(filled > MOE_TILE // 2)
        def _():
            expert_mlp(MOE_TILE)

        @pl.when(filled <= MOE_TILE // 2)
        def _():
            expert_mlp(MOE_TILE // 2)

        writeback(t, slot)

        @pl.when(t == n_used - 1)
        def _():
            wait_rows(tile_rows(t), ysem.at[slot], ybuf.at[slot])

            @pl.when(t >= 1)
            def _():
                wait_rows(tile_rows(t - 1), ysem.at[1 - slot], ybuf.at[1 - slot])


def _moe(plan, xs, w_gate_up, b_gate_up, w_down, b_down):
    _, cap, d = xs.shape
    e, _, f2 = w_gate_up.shape
    nt = plan[0].shape[0]
    wmap = lambda t, te, *_: (te[t], 0, 0)
    anyspec = pl.BlockSpec(memory_space=pl.ANY)
    grid_spec = pltpu.PrefetchScalarGridSpec(
        num_scalar_prefetch=len(plan),
        grid=(nt,),
        in_specs=[anyspec,
                  pl.BlockSpec((None, d, f2), wmap),
                  pl.BlockSpec((None, 1, f2), wmap),
                  pl.BlockSpec((None, f2 // 2, d), wmap),
                  pl.BlockSpec((None, 1, d), wmap)],
        out_specs=anyspec,
        scratch_shapes=[pltpu.VMEM((2, MOE_TILE, d), BF16), pltpu.VMEM((2, MOE_TILE, d), BF16),
                        pltpu.VMEM((MOE_TILE // 2, d), BF16),
                        pltpu.VMEM((d, f2), BF16), pltpu.VMEM((f2 // 2, d), BF16),
                        pltpu.SemaphoreType.DMA((2,)), pltpu.SemaphoreType.DMA((2,)), pltpu.SemaphoreType.DMA(())],
    )
    return pl.pallas_call(
        _moe_kernel,
        out_shape=jax.ShapeDtypeStruct(xs.shape, BF16),
        grid_spec=grid_spec,
        compiler_params=_cparams(),
        name="moe_experts",
    )(*plan, xs, w_gate_up, b_gate_up.reshape(e, 1, f2), w_down, b_down.reshape(e, 1, d))


def _moe_plan(group_size, group_start):
    n_blocks = group_size.shape[0]
    gsize = group_size.T.astype(jnp.int32)
    gstart = group_start.T.astype(jnp.int32)
    gbase = jnp.cumsum(gsize, axis=1) - gsize
    rows = jnp.sum(gsize, axis=1)
    tiles = (rows + MOE_TILE - 1) // MOE_TILE
    tile_end = jnp.cumsum(tiles)
    n_used = tile_end[-1:]
    nt = (n_blocks * GROUP_CAP) // MOE_TILE + N_EXPERTS
    t = jnp.arange(nt, dtype=jnp.int32)
    te = jnp.sum((tile_end[None, :] <= t[:, None]).astype(jnp.int32), axis=1)
    last = jnp.max(jnp.where(tiles > 0, jnp.arange(N_EXPERTS, dtype=jnp.int32), 0))
    te = jnp.where(t < n_used[0], jnp.minimum(te, N_EXPERTS - 1), last)
    hot = (te[:, None] == jnp.arange(N_EXPERTS, dtype=jnp.int32)[None, :]).astype(jnp.int32)
    tl = jnp.where(t < n_used[0], t - hot @ (tile_end - tiles), 0)
    lo_row = tl * MOE_TILE
    base_t = hot @ gbase
    size_t = hot @ gsize
    first = jnp.sum((base_t + size_t <= lo_row[:, None]).astype(jnp.int32), axis=1)
    stop = jnp.sum((base_t < lo_row[:, None] + MOE_TILE).astype(jnp.int32), axis=1)
    used = jnp.sum(gsize, axis=0)
    i32 = lambda z: z.astype(jnp.int32)
    return (i32(te), i32(tl), i32(first), i32(stop), i32(n_used), i32(gstart.reshape(-1)), i32(gsize.reshape(-1)),
            i32(gbase.reshape(-1)), i32(rows), i32(used))


def _final_kernel(x_ref, ys_ref, slot_ref, gates_ref, gt_ref, g_ref, y_ref):
    tm = x_ref.shape[0]
    cap = ys_ref.shape[0]
    col = lax.broadcasted_iota(jnp.int32, (tm, cap), 1).astype(F32)
    slots = slot_ref[...]
    gates = gates_ref[...]
    mix = jnp.zeros((tm, cap), F32)
    for k in range(TOP_K):
        mix = jnp.where(col == slots[:, k:k + 1], gates[:, k:k + 1], mix)
    ff = jnp.dot(mix.astype(BF16), ys_ref[...], preferred_element_type=F32)
    y_ref[...] = _rms(x_ref[...] + gt_ref[...] * ff, g_ref[...])


def _final(x1, ys, slots, gates, gt, g, tm, block0):
    n, d = x1.shape
    cap = ys.shape[1]
    mod_spec = (pl.BlockSpec((1, d), lambda i: (0, 0)) if gt.shape[0] == 1
                else pl.BlockSpec((tm, d), lambda i: (i, 0)))
    row = lambda w: pl.BlockSpec((tm, w), lambda i: (i, 0))
    return pl.pallas_call(
        _final_kernel,
        out_shape=jax.ShapeDtypeStruct((n, d), F32),
        grid=(n // tm,),
        in_specs=[row(d), pl.BlockSpec((None, cap, d), lambda i: (block0 + i, 0, 0)),
                  row(V7X_LANES), row(V7X_LANES), mod_spec, _full((1, d))],
        out_specs=row(d),
        compiler_params=_cparams(),
        name="final_norm",
    )(x1, ys, slots, gates, gt, g.reshape(1, d))


def kernel(x_prompt, x_sample, cache_k_win, cache_v_win, state_ssm_re, state_ssm_im, c_prompt, c_sample,
           w_ada, b_ada, g_norm1, w_in, ssm_a_re, ssm_a_im, ssm_log_dt, ssm_b_re, ssm_b_im, ssm_c_re, ssm_c_im,
           ssm_d, w_glu, b_glu, g_out_attn, g_out_ssm, w_out, g_norm2, w_router, b_router, w_gate_up, b_gate_up,
           w_down, b_down, g_final):
    depth = w_ada.shape[0]
    assert depth == 1 and x_prompt.shape[0] == 1
    bp, t, d = x_prompt.shape
    bs, ts, _ = x_sample.shape
    ns = bs * ts
    l = 0
    a = ATTN_WIDTH

    n_c = bp + bs
    c_pad = -(-n_c // V7X_SUBLANES) * V7X_SUBLANES
    c_rows = jnp.concatenate([c_prompt, c_sample, jnp.zeros((c_pad - n_c, d), F32)], axis=0)
    mod = _ada_modulation(c_rows, w_ada[l], b_ada[l])
    mod_p = [mod[0:1, i * d:(i + 1) * d] for i in range(N_MOD)]
    mod_s = [jnp.repeat(mod[bp:bp + bs, i * d:(i + 1) * d], ts, axis=0) for i in range(N_MOD)]

    steps_p = TOKEN_TILE // SSM_ROWS_PROMPT
    perm_p = _chunk_perm(SSM_ROWS_PROMPT, steps_p)
    perm_s = _chunk_perm(bs, ts)
    tabs = _ssm_tables(ssm_a_re[l], ssm_a_im[l], ssm_log_dt[l], ssm_b_re[l], ssm_b_im[l],
                       ssm_c_re[l], ssm_c_im[l], max(steps_p, ts))
    wr_pad = jnp.zeros((d, V7X_LANES), F32).at[:, :N_EXPERTS].set(w_router[l]).astype(BF16)
    br_pad = jnp.full((1, V7X_LANES), MASKED, F32).at[0, :N_EXPERTS].set(b_router[l])

    xp = x_prompt.reshape(t, d)
    dils = tuple(dil for _, dil in DILATED_BRANCHES)
    wide = tuple(dil for dil in dils if dil > 1)
    proj_p = _inproj(xp, mod_p[0], mod_p[1], g_norm1[l], w_in[l], jnp.asarray(perm_p, BF16), TOKEN_TILE, wide)
    kpf, vpf, up = proj_p[3:6]
    views = {1: proj_p[0:3]}
    for n_d, dil in enumerate(wide):
        views[dil] = proj_p[6 + 3 * n_d:9 + 3 * n_d]
    outs = [_attn_branch(*views[dil], dil) for dil in dils]
    ap = _attn_combine([o for o, _ in outs], [s for _, s in outs], g_out_attn[l], dils)
    zeros_h = jnp.zeros((SSM_ROWS_PROMPT, 2 * N_STATE), F32)
    zp, hp = _ssm(up, zeros_h, tabs, ssm_d[l], w_glu[l], b_glu[l], g_out_ssm[l],
                  jnp.asarray(perm_p.T, BF16), SSM_ROWS_PROMPT, steps_p, True)
    n_blocks_p = t // TOKEN_TILE
    n_blocks = n_blocks_p + 1
    x1p, xs_all, slot_p, gate_p, meta_p = _outproj(xp, ap, zp, mod_p[2], mod_p[3], mod_p[4], g_norm2[l], w_out[l],
                                                   wr_pad, br_pad, TOKEN_TILE, n_blocks, 0)

    xs = x_sample.reshape(ns, d)
    qs, ks, vs, ksf, vsf, us = _inproj(xs, mod_s[0], mod_s[1], g_norm1[l], w_in[l], jnp.asarray(perm_s, BF16), ns)
    split = lambda z: z.reshape(bs, ts, N_HEADS, HEAD_DIM)
    new_t = lambda z: jnp.pad(split(z).transpose(0, 2, 3, 1), ((0, 0), (0, 0), (0, 0), (0, NEW_KEY_PAD - ts)))
    as_ = _attn_sample(split(qs).transpose(0, 2, 1, 3), new_t(ksf), new_t(vsf),
                       cache_k_win.transpose(0, 1, 3, 4, 2), cache_v_win.transpose(0, 1, 3, 4, 2), l, g_out_attn[l])
    as_ = as_.transpose(0, 2, 1, 3)
    h0s = jnp.concatenate([state_ssm_re[l].reshape(bs, N_STATE), state_ssm_im[l].reshape(bs, N_STATE)], axis=1)
    zs, hs = _ssm(us, h0s, tabs, ssm_d[l], w_glu[l], b_glu[l], g_out_ssm[l],
                  jnp.asarray(perm_s.T, BF16), bs, ts, False)
    x1s, xs_all, slot_s, gate_s, meta_s = _outproj(xs, as_.reshape(ns, a), zs, mod_s[2], mod_s[3], mod_s[4],
                                                   g_norm2[l], w_out[l], wr_pad, br_pad, ns, n_blocks, n_blocks_p,
                                                   xs_prev=xs_all)

    meta = jnp.concatenate([meta_p, meta_s], axis=0)
    plan = _moe_plan(meta[:, 0, :N_EXPERTS], meta[:, 1, :N_EXPERTS])
    ys_all = _moe(plan, xs_all, w_gate_up[l], b_gate_up[l], w_down[l], b_down[l])

    y_prompt = _final(x1p, ys_all, slot_p, gate_p, mod_p[5], g_final, TOKEN_TILE, 0).reshape(bp, t, d)
    y_sample = _final(x1s, ys_all, slot_s, gate_s, mod_s[5], g_final, ns, n_blocks_p).reshape(bs, ts, d)

    keep = min(MAX_WINDOW, t)
    shp = (1, bp, keep, N_HEADS, HEAD_DIM)
    k_win = kpf[t - keep:].reshape(shp)
    v_win = vpf[t - keep:].reshape(shp)
    st = (1, bp, SSM_GROUPS, SSM_STATE)
    hp_last = hp[SSM_ROWS_PROMPT - 1]
    ss = (1, bs, SSM_GROUPS, SSM_STATE)
    return (y_prompt, y_sample, k_win, v_win,
            hp_last[:N_STATE].reshape(st), hp_last[N_STATE:].reshape(st),
            ksf.reshape(1, bs, ts, N_HEADS, HEAD_DIM), vsf.reshape(1, bs, ts, N_HEADS, HEAD_DIM),
            hs[:, :N_STATE].reshape(ss), hs[:, N_STATE:].reshape(ss))
```

```python
import functools

import numpy as np
import jax
import jax.numpy as jnp
from jax import lax
from jax.experimental import pallas as pl
from jax.experimental.pallas import tpu as pltpu

F32 = jnp.float32
BF16 = jnp.bfloat16

D_MODEL = 1024
N_HEADS = 8
HEAD_DIM = 64
ATTN_WIDTH = N_HEADS * HEAD_DIM
DILATED_BRANCHES = ((128, 1), (512, 4), (2048, 16))
KEYS_PER_BRANCH = 129
MAX_WINDOW = 2048
SSM_WIDTH = D_MODEL - ATTN_WIDTH
SSM_CH = 16
SSM_GROUPS = SSM_WIDTH // SSM_CH
SSM_STATE = 64
N_STATE = SSM_GROUPS * SSM_STATE
N_EXPERTS = 32
TOP_K = 4
D_EXPERT = D_MODEL
SWIGLU_LIMIT = 7.0
SWIGLU_ALPHA = 1.702
N_MOD = 6
EPS = 1e-6
MASKED = -1e30

V7X_LANES = 128
V7X_SUBLANES = 8
V7X_VMEM_LIMIT_BYTES = 56 * 1024 * 1024

TOKEN_TILE = 512
Q_TILE = 128
Q_BLOCKS_PER_STEP = 4
SSM_ROWS_PROMPT = 8
MOE_TILE = 512
ROW_ALIGN = 16
GROUP_CAP = -(-(TOKEN_TILE * TOP_K + N_EXPERTS * (ROW_ALIGN - 1)) // (2 * V7X_LANES)) * (2 * V7X_LANES)


def _cparams(n_axes=1):
    return pltpu.CompilerParams(
        dimension_semantics=("arbitrary",) * n_axes,
        vmem_limit_bytes=V7X_VMEM_LIMIT_BYTES,
    )


def _full(shape):
    n = len(shape)
    return pl.BlockSpec(shape, lambda *_: (0,) * n)


def _rms(x, g):
    return x * lax.rsqrt(jnp.mean(x * x, axis=-1, keepdims=True) + EPS) * g


def _sigmoid(x):
    return 1.0 / (1.0 + jnp.exp(-x))


def _ada_kernel(c_ref, w_ref, b_ref, o_ref):
    c = c_ref[...]
    s = (c * _sigmoid(c)).astype(BF16)
    o_ref[...] = jnp.dot(s, w_ref[...].astype(BF16), preferred_element_type=F32) + b_ref[...]


def _ada_modulation(c_rows, w_ada, b_ada):
    m, d = c_rows.shape
    n = w_ada.shape[1]
    tn = n // 4
    return pl.pallas_call(
        _ada_kernel,
        out_shape=jax.ShapeDtypeStruct((m, n), F32),
        grid=(n // tn,),
        in_specs=[_full((m, d)),
                  pl.BlockSpec((d, tn), lambda j: (0, j)),
                  pl.BlockSpec((1, tn), lambda j: (0, j))],
        out_specs=pl.BlockSpec((m, tn), lambda j: (0, j)),
        compiler_params=_cparams(),
        name="ada_modulation",
    )(c_rows, w_ada, b_ada.reshape(1, n))


def _inproj_kernel(*refs, dils):
    x_ref, sh_ref, sc_ref, g_ref, w_ref, perm_ref = refs[:6]
    dperm_refs = refs[6:6 + len(dils)]
    q_ref, k_ref, v_ref, kf_ref, vf_ref, u_ref = refs[6 + len(dils):12 + len(dils)]
    dil_refs = refs[12 + len(dils):-1]
    wbf_ref = refs[-1]

    @pl.when(pl.program_id(0) == 0)
    def _():
        wbf_ref[...] = w_ref[...].astype(BF16)

    h = _rms(x_ref[...], g_ref[...]) * (1.0 + sc_ref[...]) + sh_ref[...]
    hb = h.astype(BF16)
    a = ATTN_WIDTH
    proj = jnp.dot(hb, wbf_ref[:, :3 * a], preferred_element_type=F32)
    k = proj[:, a:2 * a]
    v = proj[:, 2 * a:]
    qkv = jnp.concatenate([(proj[:, :a] * (HEAD_DIM ** -0.5)).astype(BF16), k.astype(BF16), v.astype(BF16)],
                          axis=1)
    q_ref[...] = qkv[:, :a]
    k_ref[...] = qkv[:, a:2 * a]
    v_ref[...] = qkv[:, 2 * a:]
    kf_ref[...] = k
    vf_ref[...] = v
    hp = jnp.dot(perm_ref[...], hb, preferred_element_type=F32).astype(BF16)
    u_ref[...] = jnp.dot(hp, wbf_ref[:, 3 * a:], preferred_element_type=F32)
    tm = qkv.shape[0]
    for n_d, dil in enumerate(dils):
        by_residue = jnp.dot(dperm_refs[n_d][...], qkv, preferred_element_type=F32).astype(BF16)
        per = tm // dil
        for r in range(dil):
            rows = by_residue[r * per:(r + 1) * per]
            for j in range(3):
                dil_refs[3 * n_d + j][:, r * a:(r + 1) * a] = rows[:, j * a:(j + 1) * a]


def _inproj(x, sh, sc, g, w_in, perm, tm, dils=()):
    n, d = x.shape
    a = ATTN_WIDTH
    mod_rows = sh.shape[0]
    mod_spec = (pl.BlockSpec((1, d), lambda i: (0, 0)) if mod_rows == 1
                else pl.BlockSpec((tm, d), lambda i: (i, 0)))
    row = lambda w: pl.BlockSpec((tm, w), lambda i: (i, 0))
    dperms = [jnp.asarray(_chunk_perm(tm // dil, dil), BF16) for dil in dils]
    view_shapes = tuple(jax.ShapeDtypeStruct((n // dil, dil * a), BF16) for dil in dils for _ in range(3))
    view_specs = tuple(pl.BlockSpec((tm // dil, dil * a), lambda i: (i, 0)) for dil in dils for _ in range(3))
    return pl.pallas_call(
        functools.partial(_inproj_kernel, dils=tuple(dils)),
        out_shape=(jax.ShapeDtypeStruct((n, a), BF16),) * 3
        + (jax.ShapeDtypeStruct((n, a), F32),) * 2
        + (jax.ShapeDtypeStruct((n, SSM_WIDTH), F32),) + view_shapes,
        grid=(n // tm,),
        in_specs=[row(d), mod_spec, mod_spec, _full((1, d)), _full(w_in.shape), _full((tm, tm))]
        + [_full((tm, tm))] * len(dils),
        out_specs=(row(a),) * 5 + (row(SSM_WIDTH),) + view_specs,
        scratch_shapes=[pltpu.VMEM(w_in.shape, BF16)],
        compiler_params=_cparams(),
        name="inproj",
    )(x, sh, sc, g.reshape(1, d), w_in, perm, *dperms)


def _chunk_perm(rows, steps):
    n = rows * steps
    p = np.zeros((n, n), np.float32)
    c, t = np.meshgrid(np.arange(rows), np.arange(steps), indexing="ij")
    p[(t * rows + c).ravel(), (c * steps + t).ravel()] = 1.0
    return p


def _alibi_slopes():
    return np.exp2(-8.0 * np.arange(1, N_HEADS + 1, dtype=np.float64) / N_HEADS).astype(np.float32)


def _branch_bias(dil):
    qi = np.arange(Q_TILE)[:, None]
    col = np.arange(2 * Q_TILE)[None, :]
    j = Q_TILE + qi - col
    valid = (j >= 0) & (j <= Q_TILE)
    dist = (j * dil).astype(np.float32)
    tabs = []
    for first in (True, False):
        ok = valid & (col >= Q_TILE) if first else valid
        per_head = [np.where(ok, -s * dist, np.float32(MASKED)) for s in _alibi_slopes()]
        tabs.append(np.concatenate(per_head, axis=0))
    return np.stack(tabs).astype(np.float32)


def _attn_branch_kernel(q_ref, kp_ref, kc_ref, vp_ref, vc_ref, bias_ref, o_ref, lse_ref):
    first_step = pl.program_id(0) == 0
    lane = lax.broadcasted_iota(jnp.int32, (Q_TILE, V7X_LANES), 1)
    lo = lane < HEAD_DIM
    for j in range(Q_BLOCKS_PER_STEP):
        rows = slice(j * Q_TILE, (j + 1) * Q_TILE)
        before = slice((j - 1) * Q_TILE, j * Q_TILE)
        sel = jnp.where(first_step, 0, 1) if j == 0 else 1
        lse_acc = jnp.zeros((Q_TILE, V7X_LANES), F32)
        for p in range(N_HEADS // 2):
            cs = slice(V7X_LANES * p, V7X_LANES * (p + 1))
            q2 = q_ref[rows, cs]
            zero = jnp.zeros_like(q2)
            qq = jnp.concatenate([jnp.where(lo, q2, zero), jnp.where(lo, zero, q2)], axis=0)
            k_before = kp_ref[:, cs] if j == 0 else kc_ref[before, cs]
            v_before = vp_ref[:, cs] if j == 0 else vc_ref[before, cs]
            kk = jnp.concatenate([k_before, kc_ref[rows, cs]], axis=0)
            vv = jnp.concatenate([v_before, vc_ref[rows, cs]], axis=0)
            s = lax.dot_general(qq, kk, (((1,), (1,)), ((), ())), preferred_element_type=F32)
            s = s + bias_ref[sel, 2 * Q_TILE * p:2 * Q_TILE * (p + 1), :]
            m = jnp.max(s, axis=1, keepdims=True)
            e = jnp.exp(s - m)
            l = jnp.sum(e, axis=1, keepdims=True)
            eb = e.astype(BF16)
            o0 = jnp.dot(eb[:Q_TILE], vv, preferred_element_type=F32) * (1.0 / l[:Q_TILE])
            o1 = jnp.dot(eb[Q_TILE:], vv, preferred_element_type=F32) * (1.0 / l[Q_TILE:])
            o_ref[rows, cs] = jnp.where(lo, o0, o1).astype(o_ref.dtype)
            lse = m + jnp.log(l)
            lse_acc = jnp.where(lane == 2 * p, lse[:Q_TILE], lse_acc)
            lse_acc = jnp.where(lane == 2 * p + 1, lse[Q_TILE:], lse_acc)
        lse_ref[rows, :] = lse_acc[:, :N_HEADS]


def _attn_branch(qv, kv, vv, dil):
    a = ATTN_WIDTH
    rows = qv.shape[0]
    t = rows * dil
    step = Q_BLOCKS_PER_STEP * Q_TILE
    cur = pl.BlockSpec((step, a), lambda i, r: (i, r))
    prev = pl.BlockSpec((Q_TILE, a), lambda i, r: (jnp.maximum(i * Q_BLOCKS_PER_STEP - 1, 0), r))
    bias = jnp.asarray(_branch_bias(dil))
    o, lse = pl.pallas_call(
        _attn_branch_kernel,
        out_shape=(jax.ShapeDtypeStruct((rows, dil * a), BF16),
                   jax.ShapeDtypeStruct((dil, rows, N_HEADS), F32)),
        grid=(rows // step, dil),
        in_specs=[cur, prev, cur, prev, cur, _full(bias.shape)],
        out_specs=(cur, pl.BlockSpec((None, step, N_HEADS), lambda i, r: (r, i, 0))),
        compiler_params=_cparams(2),
        name=f"attn_branch_d{dil}",
    )(qv, kv, kv, vv, vv, bias)
    return o, lse.transpose(1, 0, 2).reshape(t, N_HEADS)


def _attn_combine_kernel(*refs, dils):
    nb = len(dils)
    o_refs = refs[:nb]
    l_refs = refs[nb:2 * nb]
    g_ref = refs[2 * nb]
    unperm_refs = refs[2 * nb + 1:-1]
    a_ref = refs[-1]
    tq, a = a_ref.shape
    outs = []
    n_u = 0
    for o_ref, dil in zip(o_refs, dils):
        if dil == 1:
            outs.append(o_ref[...].astype(F32))
            continue
        by_residue = jnp.concatenate([o_ref[:, r * a:(r + 1) * a] for r in range(dil)], axis=0)
        outs.append(jnp.dot(unperm_refs[n_u][...], by_residue, preferred_element_type=F32))
        n_u += 1
    ls = [l_ref[...] for l_ref in l_refs]
    top = functools.reduce(jnp.maximum, ls)
    ws = [jnp.exp(l - top) for l in ls]
    inv = 1.0 / functools.reduce(jnp.add, ws)
    cs = [w * inv for w in ws]
    lane = lax.broadcasted_iota(jnp.int32, (tq, V7X_LANES), 1)
    lo = lane < HEAD_DIM
    cols = []
    for p in range(N_HEADS // 2):
        sl = slice(V7X_LANES * p, V7X_LANES * (p + 1))
        acc = jnp.zeros((tq, V7X_LANES), F32)
        for c, o in zip(cs, outs):
            cexp = jnp.where(lo,
                             jnp.broadcast_to(c[:, 2 * p:2 * p + 1], (tq, V7X_LANES)),
                             jnp.broadcast_to(c[:, 2 * p + 1:2 * p + 2], (tq, V7X_LANES)))
            acc = acc + cexp * o[:, sl]
        cols.append(acc)
    o = jnp.concatenate(cols, axis=1)
    a_ref[...] = _rms(o, g_ref[...]).astype(a_ref.dtype)


def _attn_combine(os_, lses, g, dils):
    t = lses[0].shape[0]
    a = ATTN_WIDTH
    tq = TOKEN_TILE
    unperms = [jnp.asarray(_chunk_perm(tq // dil, dil).T, BF16) for dil in dils if dil > 1]
    return pl.pallas_call(
        functools.partial(_attn_combine_kernel, dils=tuple(dils)),
        out_shape=jax.ShapeDtypeStruct((t, a), BF16),
        grid=(t // tq,),
        in_specs=[pl.BlockSpec((tq // dil, dil * a), lambda i: (i, 0)) for dil in dils]
        + [pl.BlockSpec((tq, N_HEADS), lambda i: (i, 0))] * len(dils) + [_full((1, a))]
        + [_full((tq, tq))] * len(unperms),
        out_specs=pl.BlockSpec((tq, a), lambda i: (i, 0)),
        compiler_params=_cparams(),
        name="attn_combine",
    )(*os_, *lses, g.reshape(1, a), *unperms)


NEW_KEY_PAD = V7X_LANES


def _sample_bias(win, steps):
    slopes = _alibi_slopes()[None, :, None, None]
    t = np.arange(steps)[None, None, :, None]

    def table(dist, live):
        tabs = []
        for window, dil in DILATED_BRANCHES:
            ok = live & (dist >= 0) & (dist <= window) & (dist % dil == 0)
            tabs.append(np.where(ok, -slopes * dist.astype(np.float32), np.float32(MASKED))[0])
        return np.stack(tabs).astype(np.float32)

    pos = np.arange(win)[None, None, None, :]
    col = np.arange(NEW_KEY_PAD)[None, None, None, :]
    return table(win + t - pos, np.bool_(True)), table(t - col, col < steps)


def _attn_sample_kernel(q_ref, kn_ref, vn_ref, kt_ref, vt_ref, bo_ref, bn_ref, g_ref, o_ref):
    nb = bo_ref.shape[0]
    steps = q_ref.shape[1]
    nt = lambda p, v: lax.dot_general(p, v, (((1,), (1,)), ((), ())), preferred_element_type=F32)
    res = []
    sq = jnp.zeros((steps, 1), F32)
    for h in range(N_HEADS):
        q = q_ref[h]
        vt = vt_ref[h].astype(BF16)
        vn = vn_ref[h].astype(BF16)
        s_old = jnp.dot(q, kt_ref[h].astype(BF16), preferred_element_type=F32)
        s_new = jnp.dot(q, kn_ref[h].astype(BF16), preferred_element_type=F32)
        ms, ls, e_old, e_new = [], [], [], []
        for b in range(nb):
            so = s_old + bo_ref[b, h]
            sn = s_new + bn_ref[b, h]
            m = jnp.maximum(jnp.max(so, axis=1, keepdims=True), jnp.max(sn, axis=1, keepdims=True))
            eo = jnp.exp(so - m)
            en = jnp.exp(sn - m)
            ms.append(m)
            ls.append(jnp.sum(eo, axis=1, keepdims=True) + jnp.sum(en, axis=1, keepdims=True))
            e_old.append(eo)
            e_new.append(en)
        o_all = (nt(jnp.concatenate(e_old, axis=0).astype(BF16), vt)
                 + nt(jnp.concatenate(e_new, axis=0).astype(BF16), vn))
        top = functools.reduce(jnp.maximum, ms)
        num = jnp.zeros((steps, HEAD_DIM), F32)
        den = jnp.zeros((steps, 1), F32)
        for b in range(nb):
            w = jnp.exp(ms[b] - top)
            num = num + o_all[b * steps:(b + 1) * steps] * w
            den = den + ls[b] * w
        r = num * (1.0 / den)
        res.append(r)
        sq = sq + jnp.sum(r * r, axis=1, keepdims=True)
    inv = lax.rsqrt(sq * (1.0 / ATTN_WIDTH) + EPS)
    for h in range(N_HEADS):
        o_ref[h] = res[h] * inv * g_ref[h]


def _attn_sample(q, kn_t, vn_t, cache_kt, cache_vt, layer, g):
    b, heads, steps, dh = q.shape
    win = cache_kt.shape[-1]
    b_old, b_new = (jnp.asarray(z) for z in _sample_bias(win, steps))
    per_b = lambda *tail: pl.BlockSpec((None,) + tail, lambda i: (i,) + (0,) * len(tail))
    cache = pl.BlockSpec((None, None, heads, dh, win), lambda i: (layer, i, 0, 0, 0))
    return pl.pallas_call(
        _attn_sample_kernel,
        out_shape=jax.ShapeDtypeStruct((b, heads, steps, dh), F32),
        grid=(b,),
        in_specs=[per_b(heads, steps, dh), per_b(heads, dh, NEW_KEY_PAD), per_b(heads, dh, NEW_KEY_PAD),
                  cache, cache, _full(b_old.shape), _full(b_new.shape), _full((heads, 1, dh))],
        out_specs=per_b(heads, steps, dh),
        compiler_params=_cparams(),
        name="attn_sample",
    )(q, kn_t, vn_t, cache_kt, cache_vt, b_old, b_new, g.reshape(heads, 1, dh))


def _gelu_tanh(x):
    return 0.5 * x * (1.0 + jnp.tanh(np.sqrt(2.0 / np.pi).astype(np.float32) * (x + 0.044715 * (x * x * x))))


def _ssm_kernel(u_ref, bb_ref, lam_ref, pow_ref, cm_ref, dsk_ref, wglu_ref, bglu_ref, g_ref, pt_ref, h0_ref,
                z_ref, ht_ref, h_s, hin_s, carry_s, *, rows, steps, chain):
    ns = N_STATE
    cw = (4 * V7X_SUBLANES * V7X_LANES) // rows
    last = (steps - 1) * rows

    @pl.when(pl.program_id(0) == 0)
    def _():
        carry_s[...] = h0_ref[0:1, :]

    ub = u_ref[...].astype(BF16)
    n_slabs = SSM_WIDTH // V7X_LANES
    sw = ns // n_slabs
    for s in range(n_slabs):
        part = jnp.dot(ub[:, s * V7X_LANES:(s + 1) * V7X_LANES], bb_ref[s], preferred_element_type=F32)
        h_s[:, s * sw:(s + 1) * sw] = part[:, :sw]
        h_s[:, ns + s * sw:ns + (s + 1) * sw] = part[:, sw:]

    for cc in range(0, ns // cw, 2):
        crs = [slice(c * cw, (c + 1) * cw) for c in (cc, cc + 1)]
        cis = [slice(ns + c * cw, ns + (c + 1) * cw) for c in (cc, cc + 1)]
        lrs = [jnp.broadcast_to(lam_ref[0:1, cr], (rows, cw)) for cr in crs]
        lis = [jnp.broadcast_to(lam_ref[1:2, cr], (rows, cw)) for cr in crs]

        def scan_body(t, carry, crs=crs, cis=cis, lrs=lrs, lis=lis):
            rs = pl.ds(pl.multiple_of(t * rows, rows), rows)
            out = []
            for j in range(2):
                hr, hi = carry[2 * j], carry[2 * j + 1]
                nr = lrs[j] * hr - lis[j] * hi + h_s[rs, crs[j]]
                ni = lrs[j] * hi + lis[j] * hr + h_s[rs, cis[j]]
                h_s[rs, crs[j]] = nr
                h_s[rs, cis[j]] = ni
                out += [nr, ni]
            return tuple(out)

        zero = jnp.zeros((rows, cw), F32)
        lax.fori_loop(0, steps, scan_body, (zero,) * 4)

    if chain:
        ptr = pow_ref[steps - 1, 0:1, :ns]
        pti = pow_ref[steps - 1, 0:1, ns:]
        carry = carry_s[...]
        for c in range(rows):
            hin_s[c:c + 1, :] = carry
            cr_, ci_ = carry[:, :ns], carry[:, ns:]
            e = h_s[last + c:last + c + 1, :]
            carry = jnp.concatenate([ptr * cr_ - pti * ci_ + e[:, :ns],
                                     ptr * ci_ + pti * cr_ + e[:, ns:]], axis=1)
        carry_s[...] = carry
    else:
        hin_s[...] = h0_ref[...]

    for cc in range(ns // cw):
        cr = slice(cc * cw, (cc + 1) * cw)
        ci = slice(ns + cc * cw, ns + (cc + 1) * cw)
        hr0 = hin_s[:, cr]
        hi0 = hin_s[:, ci]

        def fix_body(t, _, cr=cr, ci=ci, hr0=hr0, hi0=hi0):
            rs = pl.ds(pl.multiple_of(t * rows, rows), rows)
            pr = pow_ref[t, :, cr]
            pi_ = pow_ref[t, :, ci]
            h_s[rs, cr] = h_s[rs, cr] + (pr * hr0 - pi_ * hi0)
            h_s[rs, ci] = h_s[rs, ci] + (pr * hi0 + pi_ * hr0)
            return 0

        lax.fori_loop(0, steps, fix_body, 0, unroll=8)

    ht_ref[...] = h_s[last:last + rows, :]
    ys = []
    for s in range(n_slabs):
        hs = jnp.concatenate([h_s[:, s * sw:(s + 1) * sw], h_s[:, ns + s * sw:ns + (s + 1) * sw]], axis=1)
        ys.append(jnp.dot(hs.astype(BF16), cm_ref[s], preferred_element_type=F32))
    y = jnp.concatenate(ys, axis=1) + dsk_ref[...] * u_ref[...]
    y = _gelu_tanh(y)
    gl = jnp.dot(y.astype(BF16), wglu_ref[...], preferred_element_type=F32) + bglu_ref[...]
    z = _rms(y * _sigmoid(gl), g_ref[...]).astype(BF16)
    z_ref[...] = jnp.dot(pt_ref[...], z, preferred_element_type=F32).astype(z_ref.dtype)


def _ssm_tables(ssm_a_re, ssm_a_im, ssm_log_dt, ssm_b_re, ssm_b_im, ssm_c_re, ssm_c_im, max_steps):
    g, n, ch = SSM_GROUPS, SSM_STATE, SSM_CH
    a_re = ssm_a_re.astype(F32)
    a_im = ssm_a_im.astype(F32)
    dt = jnp.exp(ssm_log_dt.astype(F32))[:, None]
    mag = jnp.exp(dt * a_re)
    lam_re = mag * jnp.cos(dt * a_im)
    lam_im = mag * jnp.sin(dt * a_im)
    nr = lam_re - 1.0
    ni = lam_im
    inv = 1.0 / (a_re * a_re + a_im * a_im)
    coef_re = (nr * a_re + ni * a_im) * inv
    coef_im = (ni * a_re - nr * a_im) * inv
    br = ssm_b_re.astype(F32)
    bi = ssm_b_im.astype(F32)
    bb_re = coef_re[..., None] * br - coef_im[..., None] * bi
    bb_im = coef_re[..., None] * bi + coef_im[..., None] * br
    gs = V7X_LANES // ch
    ns_ = g // gs
    eye = jnp.eye(gs, dtype=F32)
    bmat = lambda b: jnp.einsum("sgnc,gh->sgchn", b.reshape(ns_, gs, n, ch), eye).reshape(ns_, gs * ch, gs * n)
    cmat = lambda c: jnp.einsum("sgcn,gh->sgnhc", c.astype(F32).reshape(ns_, gs, ch, n), eye).reshape(
        ns_, gs * n, gs * ch)
    bb = jnp.concatenate([bmat(bb_re), bmat(bb_im)], axis=2).astype(BF16)
    cm = jnp.concatenate([cmat(ssm_c_re), -cmat(ssm_c_im)], axis=1).astype(BF16)
    lam = jnp.stack([lam_re.reshape(-1), lam_im.reshape(-1)])

    k = jnp.arange(1, max_steps + 1, dtype=F32)[:, None]
    kdt = k * dt.reshape(1, -1).repeat(n, axis=1)
    pmag = jnp.exp(kdt * a_re.reshape(1, -1))
    parg = kdt * a_im.reshape(1, -1)
    pows = jnp.concatenate([pmag * jnp.cos(parg), pmag * jnp.sin(parg)], axis=1)
    return bb, cm, lam, pows


def _ssm(u_perm, h0, tabs, dsk, w_glu, b_glu, g, perm_t, rows, steps, chain):
    n, w = u_perm.shape
    blk = rows * steps
    bb, cm, lam, pows = tabs
    row = pl.BlockSpec((blk, w), lambda i: (i, 0))
    kern = functools.partial(_ssm_kernel, rows=rows, steps=steps, chain=chain)
    return pl.pallas_call(
        kern,
        out_shape=(jax.ShapeDtypeStruct((n, w), BF16), jax.ShapeDtypeStruct((rows, 2 * N_STATE), F32)),
        grid=(n // blk,),
        in_specs=[row, _full(bb.shape), _full(lam.shape), _full((steps, rows, 2 * N_STATE)), _full(cm.shape),
                  _full((1, w)), _full((w, w)), _full((1, w)), _full((1, w)), _full((blk, blk)),
                  _full((rows, 2 * N_STATE))],
        out_specs=(row, _full((rows, 2 * N_STATE))),
        scratch_shapes=[pltpu.VMEM((blk, 2 * N_STATE), F32),
                        pltpu.VMEM((rows, 2 * N_STATE), F32),
                        pltpu.VMEM((1, 2 * N_STATE), F32)],
        compiler_params=_cparams(),
        name=f"ssm_r{rows}",
    )(u_perm, bb, lam, jnp.broadcast_to(pows[:steps, None, :], (steps, rows, 2 * N_STATE)), cm,
      dsk.reshape(1, w), w_glu.astype(BF16), b_glu.reshape(1, w),
      g.reshape(1, w), perm_t, h0)


def _outproj_kernel(*refs, n_real, aliased):
    ins, outs = refs[:12], refs[12 + aliased:]
    step = pl.program_id(0)

    @pl.when(step < n_real)
    def _():
        _outproj_tile(*ins, *outs)

    @pl.when(step >= n_real)
    def _():
        xs_ref = outs[1]
        xs_ref[...] = jnp.zeros(xs_ref.shape, xs_ref.dtype)


def _outproj_tile(x_ref, a_ref, z_ref, gt_ref, sh_ref, sc_ref, g_ref, wo_ref, wr_ref, br_ref, ltri_ref, utri_ref,
                  x1_ref, xs_ref, slot_ref, gates_ref, meta_ref, wbf_ref):
    @pl.when(pl.program_id(0) == 0)
    def _():
        wbf_ref[...] = wo_ref[...].astype(BF16)

    a = ATTN_WIDTH
    mixed = (jnp.dot(a_ref[...].astype(BF16), wbf_ref[:a, :], preferred_element_type=F32)
             + jnp.dot(z_ref[...], wbf_ref[a:, :], preferred_element_type=F32))
    x1 = x_ref[...] + gt_ref[...] * mixed
    x1_ref[...] = x1
    h2 = (_rms(x1, g_ref[...]) * (1.0 + sc_ref[...]) + sh_ref[...]).astype(BF16)
    lg = jnp.dot(h2, wr_ref[...], preferred_element_type=F32) + br_ref[...]
    tm = lg.shape[0]
    lane = lax.broadcasted_iota(jnp.int32, (tm, V7X_LANES), 1).astype(F32)
    vals, hots = [], []
    for _ in range(TOP_K):
        m = jnp.max(lg, axis=1, keepdims=True)
        idx = jnp.min(jnp.where(lg == m, lane, float(V7X_LANES)), axis=1, keepdims=True)
        hot = lane == idx
        vals.append(m)
        hots.append(jnp.where(hot, 1.0, 0.0))
        lg = jnp.where(hot, MASKED * 2, lg)
    es = [jnp.exp(v - vals[0]) for v in vals]
    inv = 1.0 / (es[0] + es[1] + es[2] + es[3])

    member = hots[0] + hots[1] + hots[2] + hots[3]
    before = jnp.dot(ltri_ref[...], member.astype(BF16), preferred_element_type=F32)
    count = jnp.sum(member, axis=0, keepdims=True)
    padded = jnp.floor((count + (ROW_ALIGN - 1.0)) * (1.0 / ROW_ALIGN)) * ROW_ALIGN
    padded8 = jnp.broadcast_to(padded, (V7X_SUBLANES, V7X_LANES))
    start = jnp.dot(padded8.astype(BF16), utri_ref[...], preferred_element_type=F32)[0:1]
    where_to = start + before
    slots = jnp.zeros((tm, V7X_LANES), F32)
    gates = jnp.zeros((tm, V7X_LANES), F32)
    for k in range(TOP_K):
        slot_k = jnp.sum(hots[k] * where_to, axis=1, keepdims=True)
        slots = jnp.where(lane == float(k), slot_k, slots)
        gates = jnp.where(lane == float(k), es[k] * inv, gates)
    slot_ref[...] = slots
    gates_ref[...] = gates
    row = lax.broadcasted_iota(jnp.int32, (V7X_SUBLANES, V7X_LANES), 0)
    meta_ref[...] = jnp.where(row == 0, padded8, jnp.where(row == 1, jnp.broadcast_to(start, padded8.shape), 0.0))

    cap = xs_ref.shape[0]
    slots_t = jnp.transpose(slots)
    srow = lax.broadcasted_iota(jnp.int32, (cap, tm), 0).astype(F32)
    place = jnp.zeros((cap, tm), F32)
    for k in range(TOP_K):
        place = jnp.where(srow == slots_t[k:k + 1, :], 1.0, place)
    xs_ref[...] = jnp.dot(place.astype(BF16), h2, preferred_element_type=F32).astype(BF16)


def _outproj(x, a, z, gt, sh, sc, g, w_out, wr_pad, br_pad, tm, n_blocks, block0, xs_prev=None):
    n, d = x.shape
    aw = ATTN_WIDTH
    n_real = n // tm
    aliased = xs_prev is not None
    n_steps = n_real if aliased else n_blocks
    tile = lambda i: jnp.minimum(i, n_real - 1)
    mod_spec = (pl.BlockSpec((1, d), lambda i: (0, 0)) if gt.shape[0] == 1
                else pl.BlockSpec((tm, d), lambda i: (tile(i), 0)))
    row = lambda w: pl.BlockSpec((tm, w), lambda i: (tile(i), 0))
    ltri = jnp.asarray(np.tril(np.ones((tm, tm), np.float32), -1), BF16)
    utri = jnp.asarray(np.triu(np.ones((V7X_LANES, V7X_LANES), np.float32), 1), BF16)
    in_specs = [row(d), row(aw), row(SSM_WIDTH), mod_spec, mod_spec, mod_spec, _full((1, d)),
                _full(w_out.shape), _full(wr_pad.shape), _full(br_pad.shape), _full(ltri.shape), _full(utri.shape)]
    args = [x, a, z, gt, sh, sc, g.reshape(1, d), w_out, wr_pad, br_pad, ltri, utri]
    if aliased:
        in_specs.append(pl.BlockSpec(memory_space=pl.ANY))
        args.append(xs_prev)
    return pl.pallas_call(
        functools.partial(_outproj_kernel, n_real=n_real, aliased=int(aliased)),
        out_shape=(jax.ShapeDtypeStruct((n, d), F32),
                   jax.ShapeDtypeStruct((n_blocks, GROUP_CAP, d), BF16),
                   jax.ShapeDtypeStruct((n, V7X_LANES), F32), jax.ShapeDtypeStruct((n, V7X_LANES), F32),
                   jax.ShapeDtypeStruct((n_real, V7X_SUBLANES, V7X_LANES), F32)),
        grid=(n_steps,),
        in_specs=in_specs,
        out_specs=(row(d), pl.BlockSpec((None, GROUP_CAP, d), lambda i: (block0 + i, 0, 0)),
                   row(V7X_LANES), row(V7X_LANES),
                   pl.BlockSpec((None, V7X_SUBLANES, V7X_LANES), lambda i: (tile(i), 0, 0))),
        scratch_shapes=[pltpu.VMEM(w_out.shape, BF16)],
        input_output_aliases={len(args) - 1: 1} if aliased else {},
        compiler_params=_cparams(),
        name="outproj_router",
    )(*args)


_PIECE_SIZES = tuple(MOE_TILE >> s for s in range(6))
_CHUNK_SHIFT = 6
_CHUNK = 1 << _CHUNK_SHIFT
_TAIL_SIZES = tuple(sz for sz in _PIECE_SIZES if sz < _CHUNK)


def _moe_kernel(te_ref, tl_ref, lo_ref, hi_ref, nu_ref, gstart_ref, gsize_ref, gbase_ref, rows_ref, used_ref,
                xs_hbm, wgu_ref, bgu_ref, wd_ref, bd_ref, ys_hbm,
                xbuf, ybuf, zbuf, wgu_bf, wd_bf, xsem, ysem, zsem):
    t = pl.program_id(0)
    n_used = nu_ref[0]
    n_blocks, cap = xs_hbm.shape[0], xs_hbm.shape[1]

    def x_copy(i, br, tr, sz, slot):
        pltpu.make_async_copy(xs_hbm.at[i, pl.ds(br, sz)], xbuf.at[slot, pl.ds(tr, sz)], xsem.at[slot]).start()

    def y_copy(i, br, tr, sz, slot):
        pltpu.make_async_copy(ybuf.at[slot, pl.ds(tr, sz)], ys_hbm.at[i, pl.ds(br, sz)], ysem.at[slot]).start()

    def z_copy(i, row, sz, start):
        cp = pltpu.make_async_copy(zbuf.at[pl.ds(0, sz)], ys_hbm.at[i, pl.ds(row, sz)], zsem)
        cp.start() if start else cp.wait()

    def pieces(tt, fn):
        e = te_ref[tt]
        lo = tl_ref[tt] * MOE_TILE

        def per_block(i, c):
            g = e * n_blocks + i
            s0 = gbase_ref[g]
            a = jnp.maximum(s0, lo)
            b = jnp.minimum(s0 + gsize_ref[g], lo + MOE_TILE)
            length = jnp.maximum(b - a, 0)
            src = gstart_ref[g] + (a - s0)
            dst = a - lo
            whole = lax.shift_right_logical(length, _CHUNK_SHIFT)

            def chunk(j, cc):
                off = j * _CHUNK
                fn(i, pl.multiple_of(src + off, ROW_ALIGN), pl.multiple_of(dst + off, ROW_ALIGN), _CHUNK)
                return cc

            lax.fori_loop(0, whole, chunk, 0)
            done = whole * _CHUNK
            for sz in _TAIL_SIZES:
                hit = (length & sz) != 0

                @pl.when(hit)
                def _(sz=sz, done=done):
                    fn(i, pl.multiple_of(src + done, ROW_ALIGN), pl.multiple_of(dst + done, ROW_ALIGN), sz)

                done = done + jnp.where(hit, sz, 0)
            return c

        lax.fori_loop(lo_ref[tt], hi_ref[tt], per_block, 0)

    def tile_rows(tt):
        return jnp.minimum(rows_ref[te_ref[tt]] - tl_ref[tt] * MOE_TILE, MOE_TILE)

    def wait_rows(n, sem, buf):
        def chunk(j, cc):
            pltpu.make_async_copy(buf.at[pl.ds(0, _CHUNK)], buf.at[pl.ds(0, _CHUNK)], sem).wait()
            return cc

        lax.fori_loop(0, lax.shift_right_logical(n, _CHUNK_SHIFT), chunk, 0)
        for sz in _TAIL_SIZES:
            @pl.when((n & sz) != 0)
            def _(sz=sz):
                pltpu.make_async_copy(buf.at[pl.ds(0, sz)], buf.at[pl.ds(0, sz)], sem).wait()

    def fetch(tt, slot):
        pieces(tt, lambda i, br, tr, sz: x_copy(i, br, tr, sz, slot))

    def writeback(tt, slot):
        pieces(tt, lambda i, br, tr, sz: y_copy(i, br, tr, sz, slot))

    def zero_tail(i, start):
        u = used_ref[i]
        rem = cap - u
        nz = zbuf.shape[0]
        whole = lax.shift_right_logical(rem, nz.bit_length() - 1)

        def chunk(j, c):
            z_copy(i, pl.multiple_of(u + j * nz, ROW_ALIGN), nz, start)
            return c

        lax.fori_loop(0, whole, chunk, 0)
        base = u + whole * nz
        done = jnp.int32(0)
        for sz in _PIECE_SIZES:
            if sz >= nz:
                continue
            hit = (rem & sz) != 0

            @pl.when(hit)
            def _(sz=sz, done=done):
                z_copy(i, pl.multiple_of(base + done, ROW_ALIGN), sz, start)

            done = done + jnp.where(hit, sz, 0)

    @pl.when(t == 0)
    def _():
        xbuf[...] = jnp.zeros(xbuf.shape, xbuf.dtype)
        zbuf[...] = jnp.zeros(zbuf.shape, zbuf.dtype)
        lax.fori_loop(0, n_blocks, lambda i, c: (zero_tail(i, True), c)[1], 0)
        lax.fori_loop(0, n_blocks, lambda i, c: (zero_tail(i, False), c)[1], 0)
        fetch(0, 0)

    @pl.when(t < n_used)
    def _():
        slot = t % 2

        @pl.when(t + 1 < n_used)
        def _():
            fetch(t + 1, 1 - slot)

        prev = te_ref[jnp.maximum(t - 1, 0)]
        fresh = jnp.logical_or(t == 0, te_ref[t] != prev)

        @pl.when(fresh)
        def _():
            wgu_bf[...] = wgu_ref[...].astype(BF16)
            wd_bf[...] = wd_ref[...].astype(BF16)

        wait_rows(tile_rows(t), xsem.at[slot], xbuf.at[slot])

        @pl.when(t >= 2)
        def _():
            wait_rows(tile_rows(t - 2), ysem.at[slot], ybuf.at[slot])

        def expert_mlp(m):
            f = D_EXPERT
            gu = jnp.dot(xbuf[slot, pl.ds(0, m)], wgu_bf[...], preferred_element_type=F32) + bgu_ref[...]
            gate = jnp.minimum(gu[:, :f], SWIGLU_LIMIT)
            up = jnp.clip(gu[:, f:], -SWIGLU_LIMIT, SWIGLU_LIMIT)
            act = (up + 1.0) * gate * _sigmoid(SWIGLU_ALPHA * gate)
            y = jnp.dot(act.astype(BF16), wd_bf[...], preferred_element_type=F32) + bd_ref[...]
            ybuf[slot, pl.ds(0, m)] = y.astype(BF16)

        quarter = MOE_TILE // 4
        n_quarters = lax.shift_right_logical(tile_rows(t) + (quarter - 1), quarter.bit_length() - 1)
        for nq in range(1, 5):
            @pl.when(n_quarters == nq)
            def _(nq=nq):
                expert_mlp(nq * quarter)

        writeback(t, slot)

        @pl.when(t == n_used - 1)
        def _():
            wait_rows(tile_rows(t), ysem.at[slot], ybuf.at[slot])

            @pl.when(t >= 1)
            def _():
                wait_rows(tile_rows(t - 1), ysem.at[1 - slot], ybuf.at[1 - slot])


def _moe(plan, xs, w_gate_up, b_gate_up, w_down, b_down):
    _, cap, d = xs.shape
    e, _, f2 = w_gate_up.shape
    nt = plan[0].shape[0]
    wmap = lambda t, te, *_: (te[t], 0, 0)
    anyspec = pl.BlockSpec(memory_space=pl.ANY)
    grid_spec = pltpu.PrefetchScalarGridSpec(
        num_scalar_prefetch=len(plan),
        grid=(nt,),
        in_specs=[anyspec,
                  pl.BlockSpec((None, d, f2), wmap),
                  pl.BlockSpec((None, 1, f2), wmap),
                  pl.BlockSpec((None, f2 // 2, d), wmap),
                  pl.BlockSpec((None, 1, d), wmap)],
        out_specs=anyspec,
        scratch_shapes=[pltpu.VMEM((2, MOE_TILE, d), BF16), pltpu.VMEM((2, MOE_TILE, d), BF16),
                        pltpu.VMEM((MOE_TILE // 2, d), BF16),
                        pltpu.VMEM((d, f2), BF16), pltpu.VMEM((f2 // 2, d), BF16),
                        pltpu.SemaphoreType.DMA((2,)), pltpu.SemaphoreType.DMA((2,)), pltpu.SemaphoreType.DMA(())],
    )
    return pl.pallas_call(
        _moe_kernel,
        out_shape=jax.ShapeDtypeStruct(xs.shape, BF16),
        grid_spec=grid_spec,
        compiler_params=_cparams(),
        name="moe_experts",
    )(*plan, xs, w_gate_up, b_gate_up.reshape(e, 1, f2), w_down, b_down.reshape(e, 1, d))


def _moe_plan(group_size, group_start):
    n_blocks = group_size.shape[0]
    gsize = group_size.T.astype(jnp.int32)
    gstart = group_start.T.astype(jnp.int32)
    gbase = jnp.cumsum(gsize, axis=1) - gsize
    rows = jnp.sum(gsize, axis=1)
    tiles = (rows + MOE_TILE - 1) // MOE_TILE
    tile_end = jnp.cumsum(tiles)
    n_used = tile_end[-1:]
    nt = (n_blocks * GROUP_CAP) // MOE_TILE + N_EXPERTS
    t = jnp.arange(nt, dtype=jnp.int32)
    te = jnp.sum((tile_end[None, :] <= t[:, None]).astype(jnp.int32), axis=1)
    last = jnp.max(jnp.where(tiles > 0, jnp.arange(N_EXPERTS, dtype=jnp.int32), 0))
    te = jnp.where(t < n_used[0], jnp.minimum(te, N_EXPERTS - 1), last)
    hot = (te[:, None] == jnp.arange(N_EXPERTS, dtype=jnp.int32)[None, :]).astype(jnp.int32)
    tl = jnp.where(t < n_used[0], t - hot @ (tile_end - tiles), 0)
    lo_row = tl * MOE_TILE
    base_t = hot @ gbase
    size_t = hot @ gsize
    first = jnp.sum((base_t + size_t <= lo_row[:, None]).astype(jnp.int32), axis=1)
    stop = jnp.sum((base_t < lo_row[:, None] + MOE_TILE).astype(jnp.int32), axis=1)
    used = jnp.sum(gsize, axis=0)
    i32 = lambda z: z.astype(jnp.int32)
    return (i32(te), i32(tl), i32(first), i32(stop), i32(n_used), i32(gstart.reshape(-1)), i32(gsize.reshape(-1)),
            i32(gbase.reshape(-1)), i32(rows), i32(used))


def _final_kernel(x_ref, ys_ref, slot_ref, gates_ref, gt_ref, g_ref, y_ref):
    tm = x_ref.shape[0]
    cap = ys_ref.shape[0]
    col = lax.broadcasted_iota(jnp.int32, (tm, cap), 1).astype(F32)
    slots = slot_ref[...]
    gates = gates_ref[...]
    mix = jnp.zeros((tm, cap), F32)
    for k in range(TOP_K):
        mix = jnp.where(col == slots[:, k:k + 1], gates[:, k:k + 1], mix)
    ff = jnp.dot(mix.astype(BF16), ys_ref[...], preferred_element_type=F32)
    y_ref[...] = _rms(x_ref[...] + gt_ref[...] * ff, g_ref[...])


def _final(x1, ys, slots, gates, gt, g, tm, block0):
    n, d = x1.shape
    cap = ys.shape[1]
    mod_spec = (pl.BlockSpec((1, d), lambda i: (0, 0)) if gt.shape[0] == 1
                else pl.BlockSpec((tm, d), lambda i: (i, 0)))
    row = lambda w: pl.BlockSpec((tm, w), lambda i: (i, 0))
    return pl.pallas_call(
        _final_kernel,
        out_shape=jax.ShapeDtypeStruct((n, d), F32),
        grid=(n // tm,),
        in_specs=[row(d), pl.BlockSpec((None, cap, d), lambda i: (block0 + i, 0, 0)),
                  row(V7X_LANES), row(V7X_LANES), mod_spec, _full((1, d))],
        out_specs=row(d),
        compiler_params=_cparams(),
        name="final_norm",
    )(x1, ys, slots, gates, gt, g.reshape(1, d))


def kernel(x_prompt, x_sample, cache_k_win, cache_v_win, state_ssm_re, state_ssm_im, c_prompt, c_sample,
           w_ada, b_ada, g_norm1, w_in, ssm_a_re, ssm_a_im, ssm_log_dt, ssm_b_re, ssm_b_im, ssm_c_re, ssm_c_im,
           ssm_d, w_glu, b_glu, g_out_attn, g_out_ssm, w_out, g_norm2, w_router, b_router, w_gate_up, b_gate_up,
           w_down, b_down, g_final):
    depth = w_ada.shape[0]
    assert depth == 1 and x_prompt.shape[0] == 1
    bp, t, d = x_prompt.shape
    bs, ts, _ = x_sample.shape
    ns = bs * ts
    l = 0
    a = ATTN_WIDTH

    n_c = bp + bs
    c_pad = -(-n_c // V7X_SUBLANES) * V7X_SUBLANES
    c_rows = jnp.concatenate([c_prompt, c_sample, jnp.zeros((c_pad - n_c, d), F32)], axis=0)
    mod = _ada_modulation(c_rows, w_ada[l], b_ada[l])
    mod_p = [mod[0:1, i * d:(i + 1) * d] for i in range(N_MOD)]
    mod_s = [jnp.repeat(mod[bp:bp + bs, i * d:(i + 1) * d], ts, axis=0) for i in range(N_MOD)]

    steps_p = TOKEN_TILE // SSM_ROWS_PROMPT
    perm_p = _chunk_perm(SSM_ROWS_PROMPT, steps_p)
    perm_s = _chunk_perm(bs, ts)
    tabs = _ssm_tables(ssm_a_re[l], ssm_a_im[l], ssm_log_dt[l], ssm_b_re[l], ssm_b_im[l],
                       ssm_c_re[l], ssm_c_im[l], max(steps_p, ts))
    wr_pad = jnp.zeros((d, V7X_LANES), F32).at[:, :N_EXPERTS].set(w_router[l]).astype(BF16)
    br_pad = jnp.full((1, V7X_LANES), MASKED, F32).at[0, :N_EXPERTS].set(b_router[l])

    xp = x_prompt.reshape(t, d)
    dils = tuple(dil for _, dil in DILATED_BRANCHES)
    wide = tuple(dil for dil in dils if dil > 1)
    proj_p = _inproj(xp, mod_p[0], mod_p[1], g_norm1[l], w_in[l], jnp.asarray(perm_p, BF16), TOKEN_TILE, wide)
    kpf, vpf, up = proj_p[3:6]
    views = {1: proj_p[0:3]}
    for n_d, dil in enumerate(wide):
        views[dil] = proj_p[6 + 3 * n_d:9 + 3 * n_d]
    outs = [_attn_branch(*views[dil], dil) for dil in dils]
    ap = _attn_combine([o for o, _ in outs], [s for _, s in outs], g_out_attn[l], dils)
    zeros_h = jnp.zeros((SSM_ROWS_PROMPT, 2 * N_STATE), F32)
    zp, hp = _ssm(up, zeros_h, tabs, ssm_d[l], w_glu[l], b_glu[l], g_out_ssm[l],
                  jnp.asarray(perm_p.T, BF16), SSM_ROWS_PROMPT, steps_p, True)
    n_blocks_p = t // TOKEN_TILE
    n_blocks = n_blocks_p + 1
    x1p, xs_all, slot_p, gate_p, meta_p = _outproj(xp, ap, zp, mod_p[2], mod_p[3], mod_p[4], g_norm2[l], w_out[l],
                                                   wr_pad, br_pad, TOKEN_TILE, n_blocks, 0)

    xs = x_sample.reshape(ns, d)
    qs, ks, vs, ksf, vsf, us = _inproj(xs, mod_s[0], mod_s[1], g_norm1[l], w_in[l], jnp.asarray(perm_s, BF16), ns)
    split = lambda z: z.reshape(bs, ts, N_HEADS, HEAD_DIM)
    new_t = lambda z: jnp.pad(split(z).transpose(0, 2, 3, 1), ((0, 0), (0, 0), (0, 0), (0, NEW_KEY_PAD - ts)))
    as_ = _attn_sample(split(qs).transpose(0, 2, 1, 3), new_t(ksf), new_t(vsf),
                       cache_k_win.transpose(0, 1, 3, 4, 2), cache_v_win.transpose(0, 1, 3, 4, 2), l, g_out_attn[l])
    as_ = as_.transpose(0, 2, 1, 3)
    h0s = jnp.concatenate([state_ssm_re[l].reshape(bs, N_STATE), state_ssm_im[l].reshape(bs, N_STATE)], axis=1)
    zs, hs = _ssm(us, h0s, tabs, ssm_d[l], w_glu[l], b_glu[l], g_out_ssm[l],
                  jnp.asarray(perm_s.T, BF16), bs, ts, False)
    x1s, xs_all, slot_s, gate_s, meta_s = _outproj(xs, as_.reshape(ns, a), zs, mod_s[2], mod_s[3], mod_s[4],
                                                   g_norm2[l], w_out[l], wr_pad, br_pad, ns, n_blocks, n_blocks_p,
                                                   xs_prev=xs_all)

    meta = jnp.concatenate([meta_p, meta_s], axis=0)
    plan = _moe_plan(meta[:, 0, :N_EXPERTS], meta[:, 1, :N_EXPERTS])
    ys_all = _moe(plan, xs_all, w_gate_up[l], b_gate_up[l], w_down[l], b_down[l])

    y_prompt = _final(x1p, ys_all, slot_p, gate_p, mod_p[5], g_final, TOKEN_TILE, 0).reshape(bp, t, d)
    y_sample = _final(x1s, ys_all, slot_s, gate_s, mod_s[5], g_final, ns, n_blocks_p).reshape(bs, ts, d)

    keep = min(MAX_WINDOW, t)
    shp = (1, bp, keep, N_HEADS, HEAD_DIM)
    k_win = kpf[t - keep:].reshape(shp)
    v_win = vpf[t - keep:].reshape(shp)
    st = (1, bp, SSM_GROUPS, SSM_STATE)
    hp_last = hp[SSM_ROWS_PROMPT - 1]
    ss = (1, bs, SSM_GROUPS, SSM_STATE)
    return (y_prompt, y_sample, k_win, v_win,
            hp_last[:N_STATE].reshape(st), hp_last[N_STATE:].reshape(st),
            ksf.reshape(1, bs, ts, N_HEADS, HEAD_DIM), vsf.reshape(1, bs, ts, N_HEADS, HEAD_DIM),
            hs[:, :N_STATE].reshape(ss), hs[:, N_STATE:].reshape(ss))
```

```python
import functools

import numpy as np
import jax
import jax.numpy as jnp
from jax import lax
from jax.experimental import pallas as pl
from jax.experimental.pallas import tpu as pltpu

F32 = jnp.float32
BF16 = jnp.bfloat16

D_MODEL = 1024
N_HEADS = 8
HEAD_DIM = 64
ATTN_WIDTH = N_HEADS * HEAD_DIM
DILATED_BRANCHES = ((128, 1), (512, 4), (2048, 16))
KEYS_PER_BRANCH = 129
MAX_WINDOW = 2048
SSM_WIDTH = D_MODEL - ATTN_WIDTH
SSM_CH = 16
SSM_GROUPS = SSM_WIDTH // SSM_CH
SSM_STATE = 64
N_STATE = SSM_GROUPS * SSM_STATE
N_EXPERTS = 32
TOP_K = 4
D_EXPERT = D_MODEL
SWIGLU_LIMIT = 7.0
SWIGLU_ALPHA = 1.702
N_MOD = 6
EPS = 1e-6
MASKED = -1e30

V7X_LANES = 128
V7X_SUBLANES = 8
V7X_VMEM_LIMIT_BYTES = 56 * 1024 * 1024

TOKEN_TILE = 512
Q_TILE = 128
Q_BLOCKS_PER_STEP = 4
SSM_ROWS_PROMPT = 8
MOE_TILE = 512
ROW_ALIGN = 16
GROUP_CAP = -(-(TOKEN_TILE * TOP_K + N_EXPERTS * (ROW_ALIGN - 1)) // (2 * V7X_LANES)) * (2 * V7X_LANES)


def _cparams(n_axes=1):
    return pltpu.CompilerParams(
        dimension_semantics=("arbitrary",) * n_axes,
        vmem_limit_bytes=V7X_VMEM_LIMIT_BYTES,
    )


def _full(shape):
    n = len(shape)
    return pl.BlockSpec(shape, lambda *_: (0,) * n)


def _rms(x, g):
    return x * lax.rsqrt(jnp.mean(x * x, axis=-1, keepdims=True) + EPS) * g


def _sigmoid(x):
    return 1.0 / (1.0 + jnp.exp(-x))


def _ada_kernel(c_ref, w_ref, b_ref, o_ref):
    c = c_ref[...]
    s = (c * _sigmoid(c)).astype(BF16)
    o_ref[...] = jnp.dot(s, w_ref[...].astype(BF16), preferred_element_type=F32) + b_ref[...]


def _ada_modulation(c_rows, w_ada, b_ada):
    m, d = c_rows.shape
    n = w_ada.shape[1]
    tn = n // 4
    return pl.pallas_call(
        _ada_kernel,
        out_shape=jax.ShapeDtypeStruct((m, n), F32),
        grid=(n // tn,),
        in_specs=[_full((m, d)),
                  pl.BlockSpec((d, tn), lambda j: (0, j)),
                  pl.BlockSpec((1, tn), lambda j: (0, j))],
        out_specs=pl.BlockSpec((m, tn), lambda j: (0, j)),
        compiler_params=_cparams(),
        name="ada_modulation",
    )(c_rows, w_ada, b_ada.reshape(1, n))


def _inproj_kernel(*refs, dils):
    x_ref, sh_ref, sc_ref, g_ref, w_ref, perm_ref = refs[:6]
    dperm_refs = refs[6:6 + len(dils)]
    q_ref, k_ref, v_ref, kf_ref, vf_ref, u_ref = refs[6 + len(dils):12 + len(dils)]
    dil_refs = refs[12 + len(dils):-1]
    wbf_ref = refs[-1]

    @pl.when(pl.program_id(0) == 0)
    def _():
        wbf_ref[...] = w_ref[...].astype(BF16)

    h = _rms(x_ref[...], g_ref[...]) * (1.0 + sc_ref[...]) + sh_ref[...]
    hb = h.astype(BF16)
    a = ATTN_WIDTH
    proj = jnp.dot(hb, wbf_ref[:, :3 * a], preferred_element_type=F32)
    k = proj[:, a:2 * a]
    v = proj[:, 2 * a:]
    qkv = jnp.concatenate([(proj[:, :a] * (HEAD_DIM ** -0.5)).astype(BF16), k.astype(BF16), v.astype(BF16)],
                          axis=1)
    q_ref[...] = qkv[:, :a]
    k_ref[...] = qkv[:, a:2 * a]
    v_ref[...] = qkv[:, 2 * a:]
    kf_ref[...] = k
    vf_ref[...] = v
    hp = jnp.dot(perm_ref[...], hb, preferred_element_type=F32).astype(BF16)
    u_ref[...] = jnp.dot(hp, wbf_ref[:, 3 * a:], preferred_element_type=F32)
    tm = qkv.shape[0]
    for n_d, dil in enumerate(dils):
        by_residue = jnp.dot(dperm_refs[n_d][...], qkv, preferred_element_type=F32).astype(BF16)
        per = tm // dil
        for r in range(dil):
            rows = by_residue[r * per:(r + 1) * per]
            for j in range(3):
                dil_refs[3 * n_d + j][:, r * a:(r + 1) * a] = rows[:, j * a:(j + 1) * a]


def _inproj(x, sh, sc, g, w_in, perm, tm, dils=()):
    n, d = x.shape
    a = ATTN_WIDTH
    mod_rows = sh.shape[0]
    mod_spec = (pl.BlockSpec((1, d), lambda i: (0, 0)) if mod_rows == 1
                else pl.BlockSpec((tm, d), lambda i: (i, 0)))
    row = lambda w: pl.BlockSpec((tm, w), lambda i: (i, 0))
    dperms = [jnp.asarray(_chunk_perm(tm // dil, dil), BF16) for dil in dils]
    view_shapes = tuple(jax.ShapeDtypeStruct((n // dil, dil * a), BF16) for dil in dils for _ in range(3))
    view_specs = tuple(pl.BlockSpec((tm // dil, dil * a), lambda i: (i, 0)) for dil in dils for _ in range(3))
    return pl.pallas_call(
        functools.partial(_inproj_kernel, dils=tuple(dils)),
        out_shape=(jax.ShapeDtypeStruct((n, a), BF16),) * 3
        + (jax.ShapeDtypeStruct((n, a), F32),) * 2
        + (jax.ShapeDtypeStruct((n, SSM_WIDTH), F32),) + view_shapes,
        grid=(n // tm,),
        in_specs=[row(d), mod_spec, mod_spec, _full((1, d)), _full(w_in.shape), _full((tm, tm))]
        + [_full((tm, tm))] * len(dils),
        out_specs=(row(a),) * 5 + (row(SSM_WIDTH),) + view_specs,
        scratch_shapes=[pltpu.VMEM(w_in.shape, BF16)],
        compiler_params=_cparams(),
        name="inproj",
    )(x, sh, sc, g.reshape(1, d), w_in, perm, *dperms)


def _chunk_perm(rows, steps):
    n = rows * steps
    p = np.zeros((n, n), np.float32)
    c, t = np.meshgrid(np.arange(rows), np.arange(steps), indexing="ij")
    p[(t * rows + c).ravel(), (c * steps + t).ravel()] = 1.0
    return p


def _alibi_slopes():
    return np.exp2(-8.0 * np.arange(1, N_HEADS + 1, dtype=np.float64) / N_HEADS).astype(np.float32)


def _branch_bias(dil):
    qi = np.arange(Q_TILE)[:, None]
    col = np.arange(2 * Q_TILE)[None, :]
    j = Q_TILE + qi - col
    valid = (j >= 0) & (j <= Q_TILE)
    dist = (j * dil).astype(np.float32)
    tabs = []
    for first in (True, False):
        ok = valid & (col >= Q_TILE) if first else valid
        per_head = [np.where(ok, -s * dist, np.float32(MASKED)) for s in _alibi_slopes()]
        tabs.append(np.concatenate(per_head, axis=0))
    return np.stack(tabs).astype(np.float32)


def _attn_branch_kernel(q_ref, kp_ref, kc_ref, vp_ref, vc_ref, bias_ref, o_ref, lse_ref):
    first_step = pl.program_id(0) == 0
    lane = lax.broadcasted_iota(jnp.int32, (Q_TILE, V7X_LANES), 1)
    lo = lane < HEAD_DIM
    for j in range(Q_BLOCKS_PER_STEP):
        rows = slice(j * Q_TILE, (j + 1) * Q_TILE)
        before = slice((j - 1) * Q_TILE, j * Q_TILE)
        sel = jnp.where(first_step, 0, 1) if j == 0 else 1
        lse_acc = jnp.zeros((Q_TILE, V7X_LANES), F32)
        for p in range(N_HEADS // 2):
            cs = slice(V7X_LANES * p, V7X_LANES * (p + 1))
            q2 = q_ref[rows, cs]
            zero = jnp.zeros_like(q2)
            qq = jnp.concatenate([jnp.where(lo, q2, zero), jnp.where(lo, zero, q2)], axis=0)
            k_before = kp_ref[:, cs] if j == 0 else kc_ref[before, cs]
            v_before = vp_ref[:, cs] if j == 0 else vc_ref[before, cs]
            kk = jnp.concatenate([k_before, kc_ref[rows, cs]], axis=0)
            vv = jnp.concatenate([v_before, vc_ref[rows, cs]], axis=0)
            s = lax.dot_general(qq, kk, (((1,), (1,)), ((), ())), preferred_element_type=F32)
            s = s + bias_ref[sel, 2 * Q_TILE * p:2 * Q_TILE * (p + 1), :]
            m = jnp.max(s, axis=1, keepdims=True)
            e = jnp.exp(s - m)
            l = jnp.sum(e, axis=1, keepdims=True)
            eb = e.astype(BF16)
            o0 = jnp.dot(eb[:Q_TILE], vv, preferred_element_type=F32) * (1.0 / l[:Q_TILE])
            o1 = jnp.dot(eb[Q_TILE:], vv, preferred_element_type=F32) * (1.0 / l[Q_TILE:])
            o_ref[rows, cs] = jnp.where(lo, o0, o1).astype(o_ref.dtype)
            lse = m + jnp.log(l)
            lse_acc = jnp.where(lane == 2 * p, lse[:Q_TILE], lse_acc)
            lse_acc = jnp.where(lane == 2 * p + 1, lse[Q_TILE:], lse_acc)
        lse_ref[rows, :] = lse_acc[:, :N_HEADS]


def _attn_branch(qv, kv, vv, dil):
    a = ATTN_WIDTH
    rows = qv.shape[0]
    t = rows * dil
    step = Q_BLOCKS_PER_STEP * Q_TILE
    cur = pl.BlockSpec((step, a), lambda i, r: (i, r))
    prev = pl.BlockSpec((Q_TILE, a), lambda i, r: (jnp.maximum(i * Q_BLOCKS_PER_STEP - 1, 0), r))
    bias = jnp.asarray(_branch_bias(dil))
    o, lse = pl.pallas_call(
        _attn_branch_kernel,
        out_shape=(jax.ShapeDtypeStruct((rows, dil * a), BF16),
                   jax.ShapeDtypeStruct((dil, rows, N_HEADS), F32)),
        grid=(rows // step, dil),
        in_specs=[cur, prev, cur, prev, cur, _full(bias.shape)],
        out_specs=(cur, pl.BlockSpec((None, step, N_HEADS), lambda i, r: (r, i, 0))),
        compiler_params=_cparams(2),
        name=f"attn_branch_d{dil}",
    )(qv, kv, kv, vv, vv, bias)
    return o, lse.transpose(1, 0, 2).reshape(t, N_HEADS)


def _attn_combine_kernel(*refs, dils):
    nb = len(dils)
    o_refs = refs[:nb]
    l_refs = refs[nb:2 * nb]
    g_ref = refs[2 * nb]
    unperm_refs = refs[2 * nb + 1:-1]
    a_ref = refs[-1]
    tq, a = a_ref.shape
    outs = []
    n_u = 0
    for o_ref, dil in zip(o_refs, dils):
        if dil == 1:
            outs.append(o_ref[...].astype(F32))
            continue
        by_residue = jnp.concatenate([o_ref[:, r * a:(r + 1) * a] for r in range(dil)], axis=0)
        outs.append(jnp.dot(unperm_refs[n_u][...], by_residue, preferred_element_type=F32))
        n_u += 1
    ls = [l_ref[...] for l_ref in l_refs]
    top = functools.reduce(jnp.maximum, ls)
    ws = [jnp.exp(l - top) for l in ls]
    inv = 1.0 / functools.reduce(jnp.add, ws)
    cs = [w * inv for w in ws]
    lane = lax.broadcasted_iota(jnp.int32, (tq, V7X_LANES), 1)
    lo = lane < HEAD_DIM
    cols = []
    for p in range(N_HEADS // 2):
        sl = slice(V7X_LANES * p, V7X_LANES * (p + 1))
        acc = jnp.zeros((tq, V7X_LANES), F32)
        for c, o in zip(cs, outs):
            cexp = jnp.where(lo,
                             jnp.broadcast_to(c[:, 2 * p:2 * p + 1], (tq, V7X_LANES)),
                             jnp.broadcast_to(c[:, 2 * p + 1:2 * p + 2], (tq, V7X_LANES)))
            acc = acc + cexp * o[:, sl]
        cols.append(acc)
    o = jnp.concatenate(cols, axis=1)
    a_ref[...] = _rms(o, g_ref[...]).astype(a_ref.dtype)


def _attn_combine(os_, lses, g, dils):
    t = lses[0].shape[0]
    a = ATTN_WIDTH
    tq = TOKEN_TILE
    unperms = [jnp.asarray(_chunk_perm(tq // dil, dil).T, BF16) for dil in dils if dil > 1]
    return pl.pallas_call(
        functools.partial(_attn_combine_kernel, dils=tuple(dils)),
        out_shape=jax.ShapeDtypeStruct((t, a), BF16),
        grid=(t // tq,),
        in_specs=[pl.BlockSpec((tq // dil, dil * a), lambda i: (i, 0)) for dil in dils]
        + [pl.BlockSpec((tq, N_HEADS), lambda i: (i, 0))] * len(dils) + [_full((1, a))]
        + [_full((tq, tq))] * len(unperms),
        out_specs=pl.BlockSpec((tq, a), lambda i: (i, 0)),
        compiler_params=_cparams(),
        name="attn_combine",
    )(*os_, *lses, g.reshape(1, a), *unperms)


NEW_KEY_PAD = V7X_LANES


def _sample_bias(win, steps):
    slopes = _alibi_slopes()[None, :, None, None]
    t = np.arange(steps)[None, None, :, None]

    def table(dist, live):
        tabs = []
        for window, dil in DILATED_BRANCHES:
            ok = live & (dist >= 0) & (dist <= window) & (dist % dil == 0)
            tabs.append(np.where(ok, -slopes * dist.astype(np.float32), np.float32(MASKED))[0])
        return np.stack(tabs).astype(np.float32)

    pos = np.arange(win)[None, None, None, :]
    col = np.arange(NEW_KEY_PAD)[None, None, None, :]
    return table(win + t - pos, np.bool_(True)), table(t - col, col < steps)


def _attn_sample_kernel(q_ref, kn_ref, vn_ref, kt_ref, vt_ref, bo_ref, bn_ref, g_ref, o_ref):
    nb = bo_ref.shape[0]
    steps = q_ref.shape[1]
    nt = lambda p, v: lax.dot_general(p, v, (((1,), (1,)), ((), ())), preferred_element_type=F32)
    res = []
    sq = jnp.zeros((steps, 1), F32)
    for h in range(N_HEADS):
        q = q_ref[h]
        vt = vt_ref[h].astype(BF16)
        vn = vn_ref[h].astype(BF16)
        s_old = jnp.dot(q, kt_ref[h].astype(BF16), preferred_element_type=F32)
        s_new = jnp.dot(q, kn_ref[h].astype(BF16), preferred_element_type=F32)
        ms, ls, e_old, e_new = [], [], [], []
        for b in range(nb):
            so = s_old + bo_ref[b, h]
            sn = s_new + bn_ref[b, h]
            m = jnp.maximum(jnp.max(so, axis=1, keepdims=True), jnp.max(sn, axis=1, keepdims=True))
            eo = jnp.exp(so - m)
            en = jnp.exp(sn - m)
            ms.append(m)
            ls.append(jnp.sum(eo, axis=1, keepdims=True) + jnp.sum(en, axis=1, keepdims=True))
            e_old.append(eo)
            e_new.append(en)
        o_all = (nt(jnp.concatenate(e_old, axis=0).astype(BF16), vt)
                 + nt(jnp.concatenate(e_new, axis=0).astype(BF16), vn))
        top = functools.reduce(jnp.maximum, ms)
        num = jnp.zeros((steps, HEAD_DIM), F32)
        den = jnp.zeros((steps, 1), F32)
        for b in range(nb):
            w = jnp.exp(ms[b] - top)
            num = num + o_all[b * steps:(b + 1) * steps] * w
            den = den + ls[b] * w
        r = num * (1.0 / den)
        res.append(r)
        sq = sq + jnp.sum(r * r, axis=1, keepdims=True)
    inv = lax.rsqrt(sq * (1.0 / ATTN_WIDTH) + EPS)
    for h in range(N_HEADS):
        o_ref[h] = res[h] * inv * g_ref[h]


def _attn_sample(q, kn_t, vn_t, cache_kt, cache_vt, layer, g):
    b, heads, steps, dh = q.shape
    win = cache_kt.shape[-1]
    b_old, b_new = (jnp.asarray(z) for z in _sample_bias(win, steps))
    per_b = lambda *tail: pl.BlockSpec((None,) + tail, lambda i: (i,) + (0,) * len(tail))
    cache = pl.BlockSpec((None, None, heads, dh, win), lambda i: (layer, i, 0, 0, 0))
    return pl.pallas_call(
        _attn_sample_kernel,
        out_shape=jax.ShapeDtypeStruct((b, heads, steps, dh), F32),
        grid=(b,),
        in_specs=[per_b(heads, steps, dh), per_b(heads, dh, NEW_KEY_PAD), per_b(heads, dh, NEW_KEY_PAD),
                  cache, cache, _full(b_old.shape), _full(b_new.shape), _full((heads, 1, dh))],
        out_specs=per_b(heads, steps, dh),
        compiler_params=_cparams(),
        name="attn_sample",
    )(q, kn_t, vn_t, cache_kt, cache_vt, b_old, b_new, g.reshape(heads, 1, dh))


def _gelu_tanh(x):
    return 0.5 * x * (1.0 + jnp.tanh(np.sqrt(2.0 / np.pi).astype(np.float32) * (x + 0.044715 * (x * x * x))))


def _ssm_kernel(u_ref, bb_ref, lam_ref, pow_ref, cm_ref, dsk_ref, wglu_ref, bglu_ref, g_ref, pt_ref, h0_ref,
                z_ref, ht_ref, h_s, hin_s, carry_s, *, rows, steps, chain):
    ns = N_STATE
    cw = (4 * V7X_SUBLANES * V7X_LANES) // rows
    last = (steps - 1) * rows

    @pl.when(pl.program_id(0) == 0)
    def _():
        carry_s[...] = h0_ref[0:1, :]

    ub = u_ref[...].astype(BF16)
    n_slabs = SSM_WIDTH // V7X_LANES
    sw = ns // n_slabs
    for s in range(n_slabs):
        part = jnp.dot(ub[:, s * V7X_LANES:(s + 1) * V7X_LANES], bb_ref[s], preferred_element_type=F32)
        h_s[:, s * sw:(s + 1) * sw] = part[:, :sw]
        h_s[:, ns + s * sw:ns + (s + 1) * sw] = part[:, sw:]

    for cc in range(0, ns // cw, 2):
        crs = [slice(c * cw, (c + 1) * cw) for c in (cc, cc + 1)]
        cis = [slice(ns + c * cw, ns + (c + 1) * cw) for c in (cc, cc + 1)]
        lrs = [jnp.broadcast_to(lam_ref[0:1, cr], (rows, cw)) for cr in crs]
        lis = [jnp.broadcast_to(lam_ref[1:2, cr], (rows, cw)) for cr in crs]

        def scan_body(t, carry, crs=crs, cis=cis, lrs=lrs, lis=lis):
            rs = pl.ds(pl.multiple_of(t * rows, rows), rows)
            out = []
            for j in range(2):
                hr, hi = carry[2 * j], carry[2 * j + 1]
                nr = lrs[j] * hr - lis[j] * hi + h_s[rs, crs[j]]
                ni = lrs[j] * hi + lis[j] * hr + h_s[rs, cis[j]]
                h_s[rs, crs[j]] = nr
                h_s[rs, cis[j]] = ni
                out += [nr, ni]
            return tuple(out)

        zero = jnp.zeros((rows, cw), F32)
        lax.fori_loop(0, steps, scan_body, (zero,) * 4)

    if chain:
        ptr = pow_ref[steps - 1, 0:1, :ns]
        pti = pow_ref[steps - 1, 0:1, ns:]
        carry = carry_s[...]
        for c in range(rows):
            hin_s[c:c + 1, :] = carry
            cr_, ci_ = carry[:, :ns], carry[:, ns:]
            e = h_s[last + c:last + c + 1, :]
            carry = jnp.concatenate([ptr * cr_ - pti * ci_ + e[:, :ns],
                                     ptr * ci_ + pti * cr_ + e[:, ns:]], axis=1)
        carry_s[...] = carry
    else:
        hin_s[...] = h0_ref[...]

    for cc in range(ns // cw):
        cr = slice(cc * cw, (cc + 1) * cw)
        ci = slice(ns + cc * cw, ns + (cc + 1) * cw)
        hr0 = hin_s[:, cr]
        hi0 = hin_s[:, ci]

        def fix_body(t, _, cr=cr, ci=ci, hr0=hr0, hi0=hi0):
            rs = pl.ds(pl.multiple_of(t * rows, rows), rows)
            pr = pow_ref[t, :, cr]
            pi_ = pow_ref[t, :, ci]
            h_s[rs, cr] = h_s[rs, cr] + (pr * hr0 - pi_ * hi0)
            h_s[rs, ci] = h_s[rs, ci] + (pr * hi0 + pi_ * hr0)
            return 0

        lax.fori_loop(0, steps, fix_body, 0, unroll=8)

    ht_ref[...] = h_s[last:last + rows, :]
    ys = []
    for s in range(n_slabs):
        hs = jnp.concatenate([h_s[:, s * sw:(s + 1) * sw], h_s[:, ns + s * sw:ns + (s + 1) * sw]], axis=1)
        ys.append(jnp.dot(hs.astype(BF16), cm_ref[s], preferred_element_type=F32))
    y = jnp.concatenate(ys, axis=1) + dsk_ref[...] * u_ref[...]
    y = _gelu_tanh(y)
    gl = jnp.dot(y.astype(BF16), wglu_ref[...], preferred_element_type=F32) + bglu_ref[...]
    z = _rms(y * _sigmoid(gl), g_ref[...]).astype(BF16)
    z_ref[...] = jnp.dot(pt_ref[...], z, preferred_element_type=F32).astype(z_ref.dtype)


def _ssm_tables(ssm_a_re, ssm_a_im, ssm_log_dt, ssm_b_re, ssm_b_im, ssm_c_re, ssm_c_im, max_steps):
    g, n, ch = SSM_GROUPS, SSM_STATE, SSM_CH
    a_re = ssm_a_re.astype(F32)
    a_im = ssm_a_im.astype(F32)
    dt = jnp.exp(ssm_log_dt.astype(F32))[:, None]
    mag = jnp.exp(dt * a_re)
    lam_re = mag * jnp.cos(dt * a_im)
    lam_im = mag * jnp.sin(dt * a_im)
    nr = lam_re - 1.0
    ni = lam_im
    inv = 1.0 / (a_re * a_re + a_im * a_im)
    coef_re = (nr * a_re + ni * a_im) * inv
    coef_im = (ni * a_re - nr * a_im) * inv
    br = ssm_b_re.astype(F32)
    bi = ssm_b_im.astype(F32)
    bb_re = coef_re[..., None] * br - coef_im[..., None] * bi
    bb_im = coef_re[..., None] * bi + coef_im[..., None] * br
    gs = V7X_LANES // ch
    ns_ = g // gs
    eye = jnp.eye(gs, dtype=F32)
    bmat = lambda b: jnp.einsum("sgnc,gh->sgchn", b.reshape(ns_, gs, n, ch), eye).reshape(ns_, gs * ch, gs * n)
    cmat = lambda c: jnp.einsum("sgcn,gh->sgnhc", c.astype(F32).reshape(ns_, gs, ch, n), eye).reshape(
        ns_, gs * n, gs * ch)
    bb = jnp.concatenate([bmat(bb_re), bmat(bb_im)], axis=2).astype(BF16)
    cm = jnp.concatenate([cmat(ssm_c_re), -cmat(ssm_c_im)], axis=1).astype(BF16)
    lam = jnp.stack([lam_re.reshape(-1), lam_im.reshape(-1)])

    k = jnp.arange(1, max_steps + 1, dtype=F32)[:, None]
    kdt = k * dt.reshape(1, -1).repeat(n, axis=1)
    pmag = jnp.exp(kdt * a_re.reshape(1, -1))
    parg = kdt * a_im.reshape(1, -1)
    pows = jnp.concatenate([pmag * jnp.cos(parg), pmag * jnp.sin(parg)], axis=1)
    return bb, cm, lam, pows


def _ssm(u_perm, h0, tabs, dsk, w_glu, b_glu, g, perm_t, rows, steps, chain):
    n, w = u_perm.shape
    blk = rows * steps
    bb, cm, lam, pows = tabs
    row = pl.BlockSpec((blk, w), lambda i: (i, 0))
    kern = functools.partial(_ssm_kernel, rows=rows, steps=steps, chain=chain)
    return pl.pallas_call(
        kern,
        out_shape=(jax.ShapeDtypeStruct((n, w), BF16), jax.ShapeDtypeStruct((rows, 2 * N_STATE), F32)),
        grid=(n // blk,),
        in_specs=[row, _full(bb.shape), _full(lam.shape), _full((steps, rows, 2 * N_STATE)), _full(cm.shape),
                  _full((1, w)), _full((w, w)), _full((1, w)), _full((1, w)), _full((blk, blk)),
                  _full((rows, 2 * N_STATE))],
        out_specs=(row, _full((rows, 2 * N_STATE))),
        scratch_shapes=[pltpu.VMEM((blk, 2 * N_STATE), F32),
                        pltpu.VMEM((rows, 2 * N_STATE), F32),
                        pltpu.VMEM((1, 2 * N_STATE), F32)],
        compiler_params=_cparams(),
        name=f"ssm_r{rows}",
    )(u_perm, bb, lam, jnp.broadcast_to(pows[:steps, None, :], (steps, rows, 2 * N_STATE)), cm,
      dsk.reshape(1, w), w_glu.astype(BF16), b_glu.reshape(1, w),
      g.reshape(1, w), perm_t, h0)


def _outproj_kernel(*refs, n_real, aliased):
    ins, outs = refs[:12], refs[12 + aliased:]
    step = pl.program_id(0)

    @pl.when(step < n_real)
    def _():
        _outproj_tile(*ins, *outs)

    @pl.when(step >= n_real)
    def _():
        xs_ref = outs[1]
        xs_ref[...] = jnp.zeros(xs_ref.shape, xs_ref.dtype)


def _outproj_tile(x_ref, a_ref, z_ref, gt_ref, sh_ref, sc_ref, g_ref, wo_ref, wr_ref, br_ref, ltri_ref, utri_ref,
                  x1_ref, xs_ref, slot_ref, gates_ref, meta_ref, wbf_ref):
    @pl.when(pl.program_id(0) == 0)
    def _():
        wbf_ref[...] = wo_ref[...].astype(BF16)

    a = ATTN_WIDTH
    mixed = (jnp.dot(a_ref[...].astype(BF16), wbf_ref[:a, :], preferred_element_type=F32)
             + jnp.dot(z_ref[...], wbf_ref[a:, :], preferred_element_type=F32))
    x1 = x_ref[...] + gt_ref[...] * mixed
    x1_ref[...] = x1
    h2 = (_rms(x1, g_ref[...]) * (1.0 + sc_ref[...]) + sh_ref[...]).astype(BF16)
    lg = jnp.dot(h2, wr_ref[...], preferred_element_type=F32) + br_ref[...]
    tm = lg.shape[0]
    lane = lax.broadcasted_iota(jnp.int32, (tm, V7X_LANES), 1).astype(F32)
    vals, hots = [], []
    for _ in range(TOP_K):
        m = jnp.max(lg, axis=1, keepdims=True)
        idx = jnp.min(jnp.where(lg == m, lane, float(V7X_LANES)), axis=1, keepdims=True)
        hot = lane == idx
        vals.append(m)
        hots.append(jnp.where(hot, 1.0, 0.0))
        lg = jnp.where(hot, MASKED * 2, lg)
    es = [jnp.exp(v - vals[0]) for v in vals]
    inv = 1.0 / (es[0] + es[1] + es[2] + es[3])

    member = hots[0] + hots[1] + hots[2] + hots[3]
    before = jnp.dot(ltri_ref[...], member.astype(BF16), preferred_element_type=F32)
    count = jnp.sum(member, axis=0, keepdims=True)
    padded = jnp.floor((count + (ROW_ALIGN - 1.0)) * (1.0 / ROW_ALIGN)) * ROW_ALIGN
    padded8 = jnp.broadcast_to(padded, (V7X_SUBLANES, V7X_LANES))
    start = jnp.dot(padded8.astype(BF16), utri_ref[...], preferred_element_type=F32)[0:1]
    where_to = start + before
    slots = jnp.zeros((tm, V7X_LANES), F32)
    gates = jnp.zeros((tm, V7X_LANES), F32)
    for k in range(TOP_K):
        slot_k = jnp.sum(hots[k] * where_to, axis=1, keepdims=True)
        slots = jnp.where(lane == float(k), slot_k, slots)
        gates = jnp.where(lane == float(k), es[k] * inv, gates)
    slot_ref[...] = slots
    gates_ref[...] = gates
    row = lax.broadcasted_iota(jnp.int32, (V7X_SUBLANES, V7X_LANES), 0)
    meta_ref[...] = jnp.where(row == 0, padded8, jnp.where(row == 1, jnp.broadcast_to(start, padded8.shape), 0.0))

    cap = xs_ref.shape[0]
    slots_t = jnp.transpose(slots)
    srow = lax.broadcasted_iota(jnp.int32, (cap, tm), 0).astype(F32)
    place = jnp.zeros((cap, tm), F32)
    for k in range(TOP_K):
        place = jnp.where(srow == slots_t[k:k + 1, :], 1.0, place)
    xs_ref[...] = jnp.dot(place.astype(BF16), h2, preferred_element_type=F32).astype(BF16)


def _outproj(x, a, z, gt, sh, sc, g, w_out, wr_pad, br_pad, tm, n_blocks, block0, xs_prev=None):
    n, d = x.shape
    aw = ATTN_WIDTH
    n_real = n // tm
    aliased = xs_prev is not None
    n_steps = n_real if aliased else n_blocks
    tile = lambda i: jnp.minimum(i, n_real - 1)
    mod_spec = (pl.BlockSpec((1, d), lambda i: (0, 0)) if gt.shape[0] == 1
                else pl.BlockSpec((tm, d), lambda i: (tile(i), 0)))
    row = lambda w: pl.BlockSpec((tm, w), lambda i: (tile(i), 0))
    ltri = jnp.asarray(np.tril(np.ones((tm, tm), np.float32), -1), BF16)
    utri = jnp.asarray(np.triu(np.ones((V7X_LANES, V7X_LANES), np.float32), 1), BF16)
    in_specs = [row(d), row(aw), row(SSM_WIDTH), mod_spec, mod_spec, mod_spec, _full((1, d)),
                _full(w_out.shape), _full(wr_pad.shape), _full(br_pad.shape), _full(ltri.shape), _full(utri.shape)]
    args = [x, a, z, gt, sh, sc, g.reshape(1, d), w_out, wr_pad, br_pad, ltri, utri]
    if aliased:
        in_specs.append(pl.BlockSpec(memory_space=pl.ANY))
        args.append(xs_prev)
    return pl.pallas_call(
        functools.partial(_outproj_kernel, n_real=n_real, aliased=int(aliased)),
        out_shape=(jax.ShapeDtypeStruct((n, d), F32),
                   jax.ShapeDtypeStruct((n_blocks, GROUP_CAP, d), BF16),
                   jax.ShapeDtypeStruct((n, V7X_LANES), F32), jax.ShapeDtypeStruct((n, V7X_LANES), F32),
                   jax.ShapeDtypeStruct((n_real, V7X_SUBLANES, V7X_LANES), F32)),
        grid=(n_steps,),
        in_specs=in_specs,
        out_specs=(row(d), pl.BlockSpec((None, GROUP_CAP, d), lambda i: (block0 + i, 0, 0)),
                   row(V7X_LANES), row(V7X_LANES),
                   pl.BlockSpec((None, V7X_SUBLANES, V7X_LANES), lambda i: (tile(i), 0, 0))),
        scratch_shapes=[pltpu.VMEM(w_out.shape, BF16)],
        input_output_aliases={len(args) - 1: 1} if aliased else {},
        compiler_params=_cparams(),
        name="outproj_router",
    )(*args)


_PIECE_SIZES = tuple(MOE_TILE >> s for s in range(6))
_CHUNK_SHIFT = 6
_CHUNK = 1 << _CHUNK_SHIFT
_TAIL_SIZES = tuple(sz for sz in _PIECE_SIZES if sz < _CHUNK)


def _moe_kernel(te_ref, tl_ref, lo_ref, hi_ref, nu_ref, gstart_ref, gsize_ref, gbase_ref, rows_ref, used_ref,
                nxt_ref, xs_hbm, wgu_hbm, bgu_ref, wd_hbm, bd_ref, ys_hbm,
                xbuf, ybuf, zbuf, wgu_f32, wd_f32, wgu_bf, wd_bf, xsem, ysem, zsem, wsem):
    t = pl.program_id(0)
    n_used = nu_ref[0]
    n_blocks, cap = xs_hbm.shape[0], xs_hbm.shape[1]

    def weight_copies(e):
        return (pltpu.make_async_copy(wgu_hbm.at[e], wgu_f32, wsem.at[0]),
                pltpu.make_async_copy(wd_hbm.at[e], wd_f32, wsem.at[1]))

    def x_copy(i, br, tr, sz, slot):
        pltpu.make_async_copy(xs_hbm.at[i, pl.ds(br, sz)], xbuf.at[slot, pl.ds(tr, sz)], xsem.at[slot]).start()

    def y_copy(i, br, tr, sz, slot):
        pltpu.make_async_copy(ybuf.at[slot, pl.ds(tr, sz)], ys_hbm.at[i, pl.ds(br, sz)], ysem.at[slot]).start()

    def z_copy(i, row, sz, start):
        cp = pltpu.make_async_copy(zbuf.at[pl.ds(0, sz)], ys_hbm.at[i, pl.ds(row, sz)], zsem)
        cp.start() if start else cp.wait()

    def pieces(tt, fn):
        e = te_ref[tt]
        lo = tl_ref[tt] * MOE_TILE

        def per_block(i, c):
            g = e * n_blocks + i
            s0 = gbase_ref[g]
            a = jnp.maximum(s0, lo)
            b = jnp.minimum(s0 + gsize_ref[g], lo + MOE_TILE)
            length = jnp.maximum(b - a, 0)
            src = gstart_ref[g] + (a - s0)
            dst = a - lo
            whole = lax.shift_right_logical(length, _CHUNK_SHIFT)

            def chunk(j, cc):
                off = j * _CHUNK
                fn(i, pl.multiple_of(src + off, ROW_ALIGN), pl.multiple_of(dst + off, ROW_ALIGN), _CHUNK)
                return cc

            lax.fori_loop(0, whole, chunk, 0)
            done = whole * _CHUNK
            for sz in _TAIL_SIZES:
                hit = (length & sz) != 0

                @pl.when(hit)
                def _(sz=sz, done=done):
                    fn(i, pl.multiple_of(src + done, ROW_ALIGN), pl.multiple_of(dst + done, ROW_ALIGN), sz)

                done = done + jnp.where(hit, sz, 0)
            return c

        lax.fori_loop(lo_ref[tt], hi_ref[tt], per_block, 0)

    def tile_rows(tt):
        return jnp.minimum(rows_ref[te_ref[tt]] - tl_ref[tt] * MOE_TILE, MOE_TILE)

    def wait_rows(n, sem, buf):
        def chunk(j, cc):
            pltpu.make_async_copy(buf.at[pl.ds(0, _CHUNK)], buf.at[pl.ds(0, _CHUNK)], sem).wait()
            return cc

        lax.fori_loop(0, lax.shift_right_logical(n, _CHUNK_SHIFT), chunk, 0)
        for sz in _TAIL_SIZES:
            @pl.when((n & sz) != 0)
            def _(sz=sz):
                pltpu.make_async_copy(buf.at[pl.ds(0, sz)], buf.at[pl.ds(0, sz)], sem).wait()

    def fetch(tt, slot):
        pieces(tt, lambda i, br, tr, sz: x_copy(i, br, tr, sz, slot))

    def writeback(tt, slot):
        pieces(tt, lambda i, br, tr, sz: y_copy(i, br, tr, sz, slot))

    def zero_tail(i, start):
        u = used_ref[i]
        rem = cap - u
        nz = zbuf.shape[0]
        whole = lax.shift_right_logical(rem, nz.bit_length() - 1)

        def chunk(j, c):
            z_copy(i, pl.multiple_of(u + j * nz, ROW_ALIGN), nz, start)
            return c

        lax.fori_loop(0, whole, chunk, 0)
        base = u + whole * nz
        done = jnp.int32(0)
        for sz in _PIECE_SIZES:
            if sz >= nz:
                continue
            hit = (rem & sz) != 0

            @pl.when(hit)
            def _(sz=sz, done=done):
                z_copy(i, pl.multiple_of(base + done, ROW_ALIGN), sz, start)

            done = done + jnp.where(hit, sz, 0)

    @pl.when(t == 0)
    def _():
        xbuf[...] = jnp.zeros(xbuf.shape, xbuf.dtype)
        zbuf[...] = jnp.zeros(zbuf.shape, zbuf.dtype)
        lax.fori_loop(0, n_blocks, lambda i, c: (zero_tail(i, True), c)[1], 0)
        lax.fori_loop(0, n_blocks, lambda i, c: (zero_tail(i, False), c)[1], 0)
        fetch(0, 0)
        for cp in weight_copies(te_ref[0]):
            cp.start()

    @pl.when(t < n_used)
    def _():
        slot = t % 2

        @pl.when(t + 1 < n_used)
        def _():
            fetch(t + 1, 1 - slot)

        @pl.when(tl_ref[t] == 0)
        def _():
            for cp in weight_copies(te_ref[t]):
                cp.wait()
            wgu_bf[...] = wgu_f32[...].astype(BF16)
            wd_bf[...] = wd_f32[...].astype(BF16)
            nxt = nxt_ref[te_ref[t]]

            @pl.when(nxt >= 0)
            def _():
                for cp in weight_copies(nxt):
                    cp.start()

        wait_rows(tile_rows(t), xsem.at[slot], xbuf.at[slot])

        @pl.when(t >= 2)
        def _():
            wait_rows(tile_rows(t - 2), ysem.at[slot], ybuf.at[slot])

        def expert_mlp(m):
            f = D_EXPERT
            gu = jnp.dot(xbuf[slot, pl.ds(0, m)], wgu_bf[...], preferred_element_type=F32) + bgu_ref[...]
            gate = jnp.minimum(gu[:, :f], SWIGLU_LIMIT)
            up = jnp.clip(gu[:, f:], -SWIGLU_LIMIT, SWIGLU_LIMIT)
            act = (up + 1.0) * gate * _sigmoid(SWIGLU_ALPHA * gate)
            y = jnp.dot(act.astype(BF16), wd_bf[...], preferred_element_type=F32) + bd_ref[...]
            ybuf[slot, pl.ds(0, m)] = y.astype(BF16)

        quarter = MOE_TILE // 4
        n_quarters = lax.shift_right_logical(tile_rows(t) + (quarter - 1), quarter.bit_length() - 1)
        for nq in range(1, 5):
            @pl.when(n_quarters == nq)
            def _(nq=nq):
                expert_mlp(nq * quarter)

        writeback(t, slot)

        @pl.when(t == n_used - 1)
        def _():
            wait_rows(tile_rows(t), ysem.at[slot], ybuf.at[slot])

            @pl.when(t >= 1)
            def _():
                wait_rows(tile_rows(t - 1), ysem.at[1 - slot], ybuf.at[1 - slot])


def _moe(plan, xs, w_gate_up, b_gate_up, w_down, b_down):
    _, cap, d = xs.shape
    e, _, f2 = w_gate_up.shape
    nt = plan[0].shape[0]
    wmap = lambda t, te, *_: (te[t], 0, 0)
    anyspec = pl.BlockSpec(memory_space=pl.ANY)
    grid_spec = pltpu.PrefetchScalarGridSpec(
        num_scalar_prefetch=len(plan),
        grid=(nt,),
        in_specs=[anyspec,
                  anyspec,
                  pl.BlockSpec((None, 1, f2), wmap),
                  anyspec,
                  pl.BlockSpec((None, 1, d), wmap)],
        out_specs=anyspec,
        scratch_shapes=[pltpu.VMEM((2, MOE_TILE, d), BF16), pltpu.VMEM((2, MOE_TILE, d), BF16),
                        pltpu.VMEM((MOE_TILE // 2, d), BF16),
                        pltpu.VMEM((d, f2), F32), pltpu.VMEM((f2 // 2, d), F32),
                        pltpu.VMEM((d, f2), BF16), pltpu.VMEM((f2 // 2, d), BF16),
                        pltpu.SemaphoreType.DMA((2,)), pltpu.SemaphoreType.DMA((2,)), pltpu.SemaphoreType.DMA(()),
                        pltpu.SemaphoreType.DMA((2,))],
    )
    return pl.pallas_call(
        _moe_kernel,
        out_shape=jax.ShapeDtypeStruct(xs.shape, BF16),
        grid_spec=grid_spec,
        compiler_params=_cparams(),
        name="moe_experts",
    )(*plan, xs, w_gate_up, b_gate_up.reshape(e, 1, f2), w_down, b_down.reshape(e, 1, d))


def _moe_plan(group_size, group_start):
    n_blocks = group_size.shape[0]
    gsize = group_size.T.astype(jnp.int32)
    gstart = group_start.T.astype(jnp.int32)
    gbase = jnp.cumsum(gsize, axis=1) - gsize
    rows = jnp.sum(gsize, axis=1)
    tiles = (rows + MOE_TILE - 1) // MOE_TILE
    tile_end = jnp.cumsum(tiles)
    n_used = tile_end[-1:]
    nt = (n_blocks * GROUP_CAP) // MOE_TILE + N_EXPERTS
    t = jnp.arange(nt, dtype=jnp.int32)
    te = jnp.sum((tile_end[None, :] <= t[:, None]).astype(jnp.int32), axis=1)
    last = jnp.max(jnp.where(tiles > 0, jnp.arange(N_EXPERTS, dtype=jnp.int32), 0))
    te = jnp.where(t < n_used[0], jnp.minimum(te, N_EXPERTS - 1), last)
    hot = (te[:, None] == jnp.arange(N_EXPERTS, dtype=jnp.int32)[None, :]).astype(jnp.int32)
    tl = jnp.where(t < n_used[0], t - hot @ (tile_end - tiles), 0)
    lo_row = tl * MOE_TILE
    base_t = hot @ gbase
    size_t = hot @ gsize
    first = jnp.sum((base_t + size_t <= lo_row[:, None]).astype(jnp.int32), axis=1)
    stop = jnp.sum((base_t < lo_row[:, None] + MOE_TILE).astype(jnp.int32), axis=1)
    used = jnp.sum(gsize, axis=0)
    ids = jnp.arange(N_EXPERTS, dtype=jnp.int32)
    later = (ids[None, :] > ids[:, None]) & (tiles[None, :] > 0)
    nxt = jnp.min(jnp.where(later, ids[None, :], N_EXPERTS), axis=1)
    nxt = jnp.where(nxt < N_EXPERTS, nxt, -1)
    i32 = lambda z: z.astype(jnp.int32)
    return (i32(te), i32(tl), i32(first), i32(stop), i32(n_used), i32(gstart.reshape(-1)), i32(gsize.reshape(-1)),
            i32(gbase.reshape(-1)), i32(rows), i32(used), i32(nxt))


def _final_kernel(x_ref, ys_ref, slot_ref, gates_ref, gt_ref, g_ref, y_ref):
    tm = x_ref.shape[0]
    cap = ys_ref.shape[0]
    col = lax.broadcasted_iota(jnp.int32, (tm, cap), 1).astype(F32)
    slots = slot_ref[...]
    gates = gates_ref[...]
    mix = jnp.zeros((tm, cap), F32)
    for k in range(TOP_K):
        mix = jnp.where(col == slots[:, k:k + 1], gates[:, k:k + 1], mix)
    ff = jnp.dot(mix.astype(BF16), ys_ref[...], preferred_element_type=F32)
    y_ref[...] = _rms(x_ref[...] + gt_ref[...] * ff, g_ref[...])


def _final(x1, ys, slots, gates, gt, g, tm, block0):
    n, d = x1.shape
    cap = ys.shape[1]
    mod_spec = (pl.BlockSpec((1, d), lambda i: (0, 0)) if gt.shape[0] == 1
                else pl.BlockSpec((tm, d), lambda i: (i, 0)))
    row = lambda w: pl.BlockSpec((tm, w), lambda i: (i, 0))
    return pl.pallas_call(
        _final_kernel,
        out_shape=jax.ShapeDtypeStruct((n, d), F32),
        grid=(n // tm,),
        in_specs=[row(d), pl.BlockSpec((None, cap, d), lambda i: (block0 + i, 0, 0)),
                  row(V7X_LANES), row(V7X_LANES), mod_spec, _full((1, d))],
        out_specs=row(d),
        compiler_params=_cparams(),
        name="final_norm",
    )(x1, ys, slots, gates, gt, g.reshape(1, d))


def kernel(x_prompt, x_sample, cache_k_win, cache_v_win, state_ssm_re, state_ssm_im, c_prompt, c_sample,
           w_ada, b_ada, g_norm1, w_in, ssm_a_re, ssm_a_im, ssm_log_dt, ssm_b_re, ssm_b_im, ssm_c_re, ssm_c_im,
           ssm_d, w_glu, b_glu, g_out_attn, g_out_ssm, w_out, g_norm2, w_router, b_router, w_gate_up, b_gate_up,
           w_down, b_down, g_final):
    depth = w_ada.shape[0]
    assert depth == 1 and x_prompt.shape[0] == 1
    bp, t, d = x_prompt.shape
    bs, ts, _ = x_sample.shape
    ns = bs * ts
    l = 0
    a = ATTN_WIDTH

    n_c = bp + bs
    c_pad = -(-n_c // V7X_SUBLANES) * V7X_SUBLANES
    c_rows = jnp.concatenate([c_prompt, c_sample, jnp.zeros((c_pad - n_c, d), F32)], axis=0)
    mod = _ada_modulation(c_rows, w_ada[l], b_ada[l])
    mod_p = [mod[0:1, i * d:(i + 1) * d] for i in range(N_MOD)]
    mod_s = [jnp.repeat(mod[bp:bp + bs, i * d:(i + 1) * d], ts, axis=0) for i in range(N_MOD)]

    steps_p = TOKEN_TILE // SSM_ROWS_PROMPT
    perm_p = _chunk_perm(SSM_ROWS_PROMPT, steps_p)
    perm_s = _chunk_perm(bs, ts)
    tabs = _ssm_tables(ssm_a_re[l], ssm_a_im[l], ssm_log_dt[l], ssm_b_re[l], ssm_b_im[l],
                       ssm_c_re[l], ssm_c_im[l], max(steps_p, ts))
    wr_pad = jnp.zeros((d, V7X_LANES), F32).at[:, :N_EXPERTS].set(w_router[l]).astype(BF16)
    br_pad = jnp.full((1, V7X_LANES), MASKED, F32).at[0, :N_EXPERTS].set(b_router[l])

    xp = x_prompt.reshape(t, d)
    dils = tuple(dil for _, dil in DILATED_BRANCHES)
    wide = tuple(dil for dil in dils if dil > 1)
    proj_p = _inproj(xp, mod_p[0], mod_p[1], g_norm1[l], w_in[l], jnp.asarray(perm_p, BF16), TOKEN_TILE, wide)
    kpf, vpf, up = proj_p[3:6]
    views = {1: proj_p[0:3]}
    for n_d, dil in enumerate(wide):
        views[dil] = proj_p[6 + 3 * n_d:9 + 3 * n_d]
    outs = [_attn_branch(*views[dil], dil) for dil in dils]
    ap = _attn_combine([o for o, _ in outs], [s for _, s in outs], g_out_attn[l], dils)
    zeros_h = jnp.zeros((SSM_ROWS_PROMPT, 2 * N_STATE), F32)
    zp, hp = _ssm(up, zeros_h, tabs, ssm_d[l], w_glu[l], b_glu[l], g_out_ssm[l],
                  jnp.asarray(perm_p.T, BF16), SSM_ROWS_PROMPT, steps_p, True)
    n_blocks_p = t // TOKEN_TILE
    n_blocks = n_blocks_p + 1
    x1p, xs_all, slot_p, gate_p, meta_p = _outproj(xp, ap, zp, mod_p[2], mod_p[3], mod_p[4], g_norm2[l], w_out[l],
                                                   wr_pad, br_pad, TOKEN_TILE, n_blocks, 0)

    xs = x_sample.reshape(ns, d)
    qs, ks, vs, ksf, vsf, us = _inproj(xs, mod_s[0], mod_s[1], g_norm1[l], w_in[l], jnp.asarray(perm_s, BF16), ns)
    split = lambda z: z.reshape(bs, ts, N_HEADS, HEAD_DIM)
    new_t = lambda z: jnp.pad(split(z).transpose(0, 2, 3, 1), ((0, 0), (0, 0), (0, 0), (0, NEW_KEY_PAD - ts)))
    as_ = _attn_sample(split(qs).transpose(0, 2, 1, 3), new_t(ksf), new_t(vsf),
                       cache_k_win.transpose(0, 1, 3, 4, 2), cache_v_win.transpose(0, 1, 3, 4, 2), l, g_out_attn[l])
    as_ = as_.transpose(0, 2, 1, 3)
    h0s = jnp.concatenate([state_ssm_re[l].reshape(bs, N_STATE), state_ssm_im[l].reshape(bs, N_STATE)], axis=1)
    zs, hs = _ssm(us, h0s, tabs, ssm_d[l], w_glu[l], b_glu[l], g_out_ssm[l],
                  jnp.asarray(perm_s.T, BF16), bs, ts, False)
    x1s, xs_all, slot_s, gate_s, meta_s = _outproj(xs, as_.reshape(ns, a), zs, mod_s[2], mod_s[3], mod_s[4],
                                                   g_norm2[l], w_out[l], wr_pad, br_pad, ns, n_blocks, n_blocks_p,
                                                   xs_prev=xs_all)

    meta = jnp.concatenate([meta_p, meta_s], axis=0)
    plan = _moe_plan(meta[:, 0, :N_EXPERTS], meta[:, 1, :N_EXPERTS])
    ys_all = _moe(plan, xs_all, w_gate_up[l], b_gate_up[l], w_down[l], b_down[l])

    y_prompt = _final(x1p, ys_all, slot_p, gate_p, mod_p[5], g_final, TOKEN_TILE, 0).reshape(bp, t, d)
    y_sample = _final(x1s, ys_all, slot_s, gate_s, mod_s[5], g_final, ns, n_blocks_p).reshape(bs, ts, d)

    keep = min(MAX_WINDOW, t)
    shp = (1, bp, keep, N_HEADS, HEAD_DIM)
    k_win = kpf[t - keep:].reshape(shp)
    v_win = vpf[t - keep:].reshape(shp)
    st = (1, bp, SSM_GROUPS, SSM_STATE)
    hp_last = hp[SSM_ROWS_PROMPT - 1]
    ss = (1, bs, SSM_GROUPS, SSM_STATE)
    return (y_prompt, y_sample, k_win, v_win,
            hp_last[:N_STATE].reshape(st), hp_last[N_STATE:].reshape(st),
            ksf.reshape(1, bs, ts, N_HEADS, HEAD_DIM), vsf.reshape(1, bs, ts, N_HEADS, HEAD_DIM),
            hs[:, :N_STATE].reshape(ss), hs[:, N_STATE:].reshape(ss))
```

```python
import functools

import numpy as np
import jax
import jax.numpy as jnp
from jax import lax
from jax.experimental import pallas as pl
from jax.experimental.pallas import tpu as pltpu

F32 = jnp.float32
BF16 = jnp.bfloat16

D_MODEL = 1024
N_HEADS = 8
HEAD_DIM = 64
ATTN_WIDTH = N_HEADS * HEAD_DIM
DILATED_BRANCHES = ((128, 1), (512, 4), (2048, 16))
KEYS_PER_BRANCH = 129
MAX_WINDOW = 2048
SSM_WIDTH = D_MODEL - ATTN_WIDTH
SSM_CH = 16
SSM_GROUPS = SSM_WIDTH // SSM_CH
SSM_STATE = 64
N_STATE = SSM_GROUPS * SSM_STATE
N_EXPERTS = 32
TOP_K = 4
D_EXPERT = D_MODEL
SWIGLU_LIMIT = 7.0
SWIGLU_ALPHA = 1.702
N_MOD = 6
EPS = 1e-6
MASKED = -1e30

V7X_LANES = 128
V7X_SUBLANES = 8
V7X_VMEM_LIMIT_BYTES = 56 * 1024 * 1024

TOKEN_TILE = 512
Q_TILE = 128
Q_BLOCKS_PER_STEP = 8
SSM_ROWS_PROMPT = 8
MOE_TILE = 512
ROW_ALIGN = 16
GROUP_CAP = -(-(TOKEN_TILE * TOP_K + N_EXPERTS * (ROW_ALIGN - 1)) // (2 * V7X_LANES)) * (2 * V7X_LANES)


def _cparams(n_axes=1):
    return pltpu.CompilerParams(
        dimension_semantics=("arbitrary",) * n_axes,
        vmem_limit_bytes=V7X_VMEM_LIMIT_BYTES,
    )


def _full(shape):
    n = len(shape)
    return pl.BlockSpec(shape, lambda *_: (0,) * n)


def _rms(x, g):
    return x * lax.rsqrt(jnp.mean(x * x, axis=-1, keepdims=True) + EPS) * g


def _sigmoid(x):
    return 1.0 / (1.0 + jnp.exp(-x))


def _ada_kernel(c_ref, w_ref, b_ref, o_ref):
    c = c_ref[...]
    s = (c * _sigmoid(c)).astype(BF16)
    o_ref[...] = jnp.dot(s, w_ref[...].astype(BF16), preferred_element_type=F32) + b_ref[...]


def _ada_modulation(c_rows, w_ada, b_ada):
    m, d = c_rows.shape
    n = w_ada.shape[1]
    tn = n // 4
    return pl.pallas_call(
        _ada_kernel,
        out_shape=jax.ShapeDtypeStruct((m, n), F32),
        grid=(n // tn,),
        in_specs=[_full((m, d)),
                  pl.BlockSpec((d, tn), lambda j: (0, j)),
                  pl.BlockSpec((1, tn), lambda j: (0, j))],
        out_specs=pl.BlockSpec((m, tn), lambda j: (0, j)),
        compiler_params=_cparams(),
        name="ada_modulation",
    )(c_rows, w_ada, b_ada.reshape(1, n))


def _inproj_kernel(*refs, dils):
    x_ref, sh_ref, sc_ref, g_ref, w_ref, perm_ref = refs[:6]
    dperm_refs = refs[6:6 + len(dils)]
    q_ref, k_ref, v_ref, kf_ref, vf_ref, u_ref = refs[6 + len(dils):12 + len(dils)]
    dil_refs = refs[12 + len(dils):-1]
    wbf_ref = refs[-1]

    @pl.when(pl.program_id(0) == 0)
    def _():
        wbf_ref[...] = w_ref[...].astype(BF16)

    h = _rms(x_ref[...], g_ref[...]) * (1.0 + sc_ref[...]) + sh_ref[...]
    hb = h.astype(BF16)
    a = ATTN_WIDTH
    proj = jnp.dot(hb, wbf_ref[:, :3 * a], preferred_element_type=F32)
    k = proj[:, a:2 * a]
    v = proj[:, 2 * a:]
    qkv = jnp.concatenate([(proj[:, :a] * (HEAD_DIM ** -0.5)).astype(BF16), k.astype(BF16), v.astype(BF16)],
                          axis=1)
    q_ref[...] = qkv[:, :a]
    k_ref[...] = qkv[:, a:2 * a]
    v_ref[...] = qkv[:, 2 * a:]
    kf_ref[...] = k
    vf_ref[...] = v
    hp = jnp.dot(perm_ref[...], hb, preferred_element_type=F32).astype(BF16)
    u_ref[...] = jnp.dot(hp, wbf_ref[:, 3 * a:], preferred_element_type=F32)
    tm = qkv.shape[0]
    for n_d, dil in enumerate(dils):
        by_residue = jnp.dot(dperm_refs[n_d][...], qkv, preferred_element_type=F32).astype(BF16)
        per = tm // dil
        for r in range(dil):
            rows = by_residue[r * per:(r + 1) * per]
            for j in range(3):
                dil_refs[3 * n_d + j][:, r * a:(r + 1) * a] = rows[:, j * a:(j + 1) * a]


def _inproj(x, sh, sc, g, w_in, perm, tm, dils=()):
    n, d = x.shape
    a = ATTN_WIDTH
    mod_rows = sh.shape[0]
    mod_spec = (pl.BlockSpec((1, d), lambda i: (0, 0)) if mod_rows == 1
                else pl.BlockSpec((tm, d), lambda i: (i, 0)))
    row = lambda w: pl.BlockSpec((tm, w), lambda i: (i, 0))
    dperms = [jnp.asarray(_chunk_perm(tm // dil, dil), BF16) for dil in dils]
    view_shapes = tuple(jax.ShapeDtypeStruct((n // dil, dil * a), BF16) for dil in dils for _ in range(3))
    view_specs = tuple(pl.BlockSpec((tm // dil, dil * a), lambda i: (i, 0)) for dil in dils for _ in range(3))
    return pl.pallas_call(
        functools.partial(_inproj_kernel, dils=tuple(dils)),
        out_shape=(jax.ShapeDtypeStruct((n, a), BF16),) * 3
        + (jax.ShapeDtypeStruct((n, a), F32),) * 2
        + (jax.ShapeDtypeStruct((n, SSM_WIDTH), F32),) + view_shapes,
        grid=(n // tm,),
        in_specs=[row(d), mod_spec, mod_spec, _full((1, d)), _full(w_in.shape), _full((tm, tm))]
        + [_full((tm, tm))] * len(dils),
        out_specs=(row(a),) * 5 + (row(SSM_WIDTH),) + view_specs,
        scratch_shapes=[pltpu.VMEM(w_in.shape, BF16)],
        compiler_params=_cparams(),
        name="inproj",
    )(x, sh, sc, g.reshape(1, d), w_in, perm, *dperms)


def _chunk_perm(rows, steps):
    n = rows * steps
    p = np.zeros((n, n), np.float32)
    c, t = np.meshgrid(np.arange(rows), np.arange(steps), indexing="ij")
    p[(t * rows + c).ravel(), (c * steps + t).ravel()] = 1.0
    return p


def _alibi_slopes():
    return np.exp2(-8.0 * np.arange(1, N_HEADS + 1, dtype=np.float64) / N_HEADS).astype(np.float32)


def _branch_bias(dil):
    qi = np.arange(Q_TILE)[:, None]
    col = np.arange(2 * Q_TILE)[None, :]
    j = Q_TILE + qi - col
    valid = (j >= 0) & (j <= Q_TILE)
    dist = (j * dil).astype(np.float32)
    tabs = []
    for first in (True, False):
        ok = valid & (col >= Q_TILE) if first else valid
        per_head = [np.where(ok, -s * dist, np.float32(MASKED)) for s in _alibi_slopes()]
        tabs.append(np.concatenate(per_head, axis=0))
    return np.stack(tabs).astype(np.float32)


def _attn_branch_kernel(q_ref, kp_ref, kc_ref, vp_ref, vc_ref, bias_ref, o_ref, lse_ref):
    first_step = pl.program_id(0) == 0
    lane = lax.broadcasted_iota(jnp.int32, (Q_TILE, V7X_LANES), 1)
    lo = lane < HEAD_DIM
    for j in range(Q_BLOCKS_PER_STEP):
        rows = slice(j * Q_TILE, (j + 1) * Q_TILE)
        before = slice((j - 1) * Q_TILE, j * Q_TILE)
        sel = jnp.where(first_step, 0, 1) if j == 0 else 1
        lse_acc = jnp.zeros((Q_TILE, V7X_LANES), F32)
        for p in range(N_HEADS // 2):
            cs = slice(V7X_LANES * p, V7X_LANES * (p + 1))
            q2 = q_ref[rows, cs]
            zero = jnp.zeros_like(q2)
            qq = jnp.concatenate([jnp.where(lo, q2, zero), jnp.where(lo, zero, q2)], axis=0)
            k_before = kp_ref[:, cs] if j == 0 else kc_ref[before, cs]
            v_before = vp_ref[:, cs] if j == 0 else vc_ref[before, cs]
            kk = jnp.concatenate([k_before, kc_ref[rows, cs]], axis=0)
            vv = jnp.concatenate([v_before, vc_ref[rows, cs]], axis=0)
            s = lax.dot_general(qq, kk, (((1,), (1,)), ((), ())), preferred_element_type=F32)
            s = s + bias_ref[sel, 2 * Q_TILE * p:2 * Q_TILE * (p + 1), :]
            m = jnp.max(s, axis=1, keepdims=True)
            e = jnp.exp(s - m)
            l = jnp.sum(e, axis=1, keepdims=True)
            eb = e.astype(BF16)
            o0 = jnp.dot(eb[:Q_TILE], vv, preferred_element_type=F32) * (1.0 / l[:Q_TILE])
            o1 = jnp.dot(eb[Q_TILE:], vv, preferred_element_type=F32) * (1.0 / l[Q_TILE:])
            o_ref[rows, cs] = jnp.where(lo, o0, o1).astype(o_ref.dtype)
            lse = m + jnp.log(l)
            lse_acc = jnp.where(lane == 2 * p, lse[:Q_TILE], lse_acc)
            lse_acc = jnp.where(lane == 2 * p + 1, lse[Q_TILE:], lse_acc)
        lse_ref[rows, :] = lse_acc[:, :N_HEADS]


def _attn_branch(qv, kv, vv, dil):
    a = ATTN_WIDTH
    rows = qv.shape[0]
    t = rows * dil
    step = Q_BLOCKS_PER_STEP * Q_TILE
    cur = pl.BlockSpec((step, a), lambda i, r: (i, r))
    prev = pl.BlockSpec((Q_TILE, a), lambda i, r: (jnp.maximum(i * Q_BLOCKS_PER_STEP - 1, 0), r))
    bias = jnp.asarray(_branch_bias(dil))
    o, lse = pl.pallas_call(
        _attn_branch_kernel,
        out_shape=(jax.ShapeDtypeStruct((rows, dil * a), BF16),
                   jax.ShapeDtypeStruct((dil, rows, N_HEADS), F32)),
        grid=(rows // step, dil),
        in_specs=[cur, prev, cur, prev, cur, _full(bias.shape)],
        out_specs=(cur, pl.BlockSpec((None, step, N_HEADS), lambda i, r: (r, i, 0))),
        compiler_params=_cparams(2),
        name=f"attn_branch_d{dil}",
    )(qv, kv, kv, vv, vv, bias)
    return o, lse.transpose(1, 0, 2).reshape(t, N_HEADS)


def _attn_combine_kernel(*refs, dils):
    nb = len(dils)
    o_refs = refs[:nb]
    l_refs = refs[nb:2 * nb]
    g_ref = refs[2 * nb]
    unperm_refs = refs[2 * nb + 1:-1]
    a_ref = refs[-1]
    tq, a = a_ref.shape
    outs = []
    n_u = 0
    for o_ref, dil in zip(o_refs, dils):
        if dil == 1:
            outs.append(o_ref[...].astype(F32))
            continue
        by_residue = jnp.concatenate([o_ref[:, r * a:(r + 1) * a] for r in range(dil)], axis=0)
        outs.append(jnp.dot(unperm_refs[n_u][...], by_residue, preferred_element_type=F32))
        n_u += 1
    ls = [l_ref[...] for l_ref in l_refs]
    top = functools.reduce(jnp.maximum, ls)
    ws = [jnp.exp(l - top) for l in ls]
    inv = 1.0 / functools.reduce(jnp.add, ws)
    cs = [w * inv for w in ws]
    lane = lax.broadcasted_iota(jnp.int32, (tq, V7X_LANES), 1)
    lo = lane < HEAD_DIM
    cols = []
    for p in range(N_HEADS // 2):
        sl = slice(V7X_LANES * p, V7X_LANES * (p + 1))
        acc = jnp.zeros((tq, V7X_LANES), F32)
        for c, o in zip(cs, outs):
            cexp = jnp.where(lo,
                             jnp.broadcast_to(c[:, 2 * p:2 * p + 1], (tq, V7X_LANES)),
                             jnp.broadcast_to(c[:, 2 * p + 1:2 * p + 2], (tq, V7X_LANES)))
            acc = acc + cexp * o[:, sl]
        cols.append(acc)
    o = jnp.concatenate(cols, axis=1)
    a_ref[...] = _rms(o, g_ref[...]).astype(a_ref.dtype)


def _attn_combine(os_, lses, g, dils):
    t = lses[0].shape[0]
    a = ATTN_WIDTH
    tq = TOKEN_TILE
    unperms = [jnp.asarray(_chunk_perm(tq // dil, dil).T, BF16) for dil in dils if dil > 1]
    return pl.pallas_call(
        functools.partial(_attn_combine_kernel, dils=tuple(dils)),
        out_shape=jax.ShapeDtypeStruct((t, a), BF16),
        grid=(t // tq,),
        in_specs=[pl.BlockSpec((tq // dil, dil * a), lambda i: (i, 0)) for dil in dils]
        + [pl.BlockSpec((tq, N_HEADS), lambda i: (i, 0))] * len(dils) + [_full((1, a))]
        + [_full((tq, tq))] * len(unperms),
        out_specs=pl.BlockSpec((tq, a), lambda i: (i, 0)),
        compiler_params=_cparams(),
        name="attn_combine",
    )(*os_, *lses, g.reshape(1, a), *unperms)


NEW_KEY_PAD = V7X_LANES


def _sample_bias(win, steps):
    slopes = _alibi_slopes()[None, :, None, None]
    t = np.arange(steps)[None, None, :, None]

    def table(dist, live):
        tabs = []
        for window, dil in DILATED_BRANCHES:
            ok = live & (dist >= 0) & (dist <= window) & (dist % dil == 0)
            tabs.append(np.where(ok, -slopes * dist.astype(np.float32), np.float32(MASKED))[0])
        return np.stack(tabs).astype(np.float32)

    pos = np.arange(win)[None, None, None, :]
    col = np.arange(NEW_KEY_PAD)[None, None, None, :]
    return table(win + t - pos, np.bool_(True)), table(t - col, col < steps)


def _attn_sample_kernel(q_ref, kn_ref, vn_ref, kt_ref, vt_ref, bo_ref, bn_ref, g_ref, o_ref):
    nb = bo_ref.shape[0]
    steps = q_ref.shape[1]
    nt = lambda p, v: lax.dot_general(p, v, (((1,), (1,)), ((), ())), preferred_element_type=F32)
    res = []
    sq = jnp.zeros((steps, 1), F32)
    for h in range(N_HEADS):
        q = q_ref[h]
        vt = vt_ref[h].astype(BF16)
        vn = vn_ref[h].astype(BF16)
        s_old = jnp.dot(q, kt_ref[h].astype(BF16), preferred_element_type=F32)
        s_new = jnp.dot(q, kn_ref[h].astype(BF16), preferred_element_type=F32)
        ms, ls, e_old, e_new = [], [], [], []
        for b in range(nb):
            so = s_old + bo_ref[b, h]
            sn = s_new + bn_ref[b, h]
            m = jnp.maximum(jnp.max(so, axis=1, keepdims=True), jnp.max(sn, axis=1, keepdims=True))
            eo = jnp.exp(so - m)
            en = jnp.exp(sn - m)
            ms.append(m)
            ls.append(jnp.sum(eo, axis=1, keepdims=True) + jnp.sum(en, axis=1, keepdims=True))
            e_old.append(eo)
            e_new.append(en)
        o_all = (nt(jnp.concatenate(e_old, axis=0).astype(BF16), vt)
                 + nt(jnp.concatenate(e_new, axis=0).astype(BF16), vn))
        top = functools.reduce(jnp.maximum, ms)
        num = jnp.zeros((steps, HEAD_DIM), F32)
        den = jnp.zeros((steps, 1), F32)
        for b in range(nb):
            w = jnp.exp(ms[b] - top)
            num = num + o_all[b * steps:(b + 1) * steps] * w
            den = den + ls[b] * w
        r = num * (1.0 / den)
        res.append(r)
        sq = sq + jnp.sum(r * r, axis=1, keepdims=True)
    inv = lax.rsqrt(sq * (1.0 / ATTN_WIDTH) + EPS)
    for h in range(N_HEADS):
        o_ref[h] = res[h] * inv * g_ref[h]


def _attn_sample(q, kn_t, vn_t, cache_kt, cache_vt, layer, g):
    b, heads, steps, dh = q.shape
    win = cache_kt.shape[-1]
    b_old, b_new = (jnp.asarray(z) for z in _sample_bias(win, steps))
    per_b = lambda *tail: pl.BlockSpec((None,) + tail, lambda i: (i,) + (0,) * len(tail))
    cache = pl.BlockSpec((None, None, heads, dh, win), lambda i: (layer, i, 0, 0, 0))
    return pl.pallas_call(
        _attn_sample_kernel,
        out_shape=jax.ShapeDtypeStruct((b, heads, steps, dh), F32),
        grid=(b,),
        in_specs=[per_b(heads, steps, dh), per_b(heads, dh, NEW_KEY_PAD), per_b(heads, dh, NEW_KEY_PAD),
                  cache, cache, _full(b_old.shape), _full(b_new.shape), _full((heads, 1, dh))],
        out_specs=per_b(heads, steps, dh),
        compiler_params=_cparams(),
        name="attn_sample",
    )(q, kn_t, vn_t, cache_kt, cache_vt, b_old, b_new, g.reshape(heads, 1, dh))


def _gelu_tanh(x):
    return 0.5 * x * (1.0 + jnp.tanh(np.sqrt(2.0 / np.pi).astype(np.float32) * (x + 0.044715 * (x * x * x))))


def _ssm_kernel(u_ref, bb_ref, lam_ref, pow_ref, cm_ref, dsk_ref, wglu_ref, bglu_ref, g_ref, pt_ref, h0_ref,
                z_ref, ht_ref, h_s, hin_s, carry_s, *, rows, steps, chain):
    ns = N_STATE
    cw = (4 * V7X_SUBLANES * V7X_LANES) // rows
    last = (steps - 1) * rows

    @pl.when(pl.program_id(0) == 0)
    def _():
        carry_s[...] = h0_ref[0:1, :]

    ub = u_ref[...].astype(BF16)
    n_slabs = SSM_WIDTH // V7X_LANES
    sw = ns // n_slabs
    for s in range(n_slabs):
        part = jnp.dot(ub[:, s * V7X_LANES:(s + 1) * V7X_LANES], bb_ref[s], preferred_element_type=F32)
        h_s[:, s * sw:(s + 1) * sw] = part[:, :sw]
        h_s[:, ns + s * sw:ns + (s + 1) * sw] = part[:, sw:]

    for cc in range(0, ns // cw, 2):
        crs = [slice(c * cw, (c + 1) * cw) for c in (cc, cc + 1)]
        cis = [slice(ns + c * cw, ns + (c + 1) * cw) for c in (cc, cc + 1)]
        lrs = [jnp.broadcast_to(lam_ref[0:1, cr], (rows, cw)) for cr in crs]
        lis = [jnp.broadcast_to(lam_ref[1:2, cr], (rows, cw)) for cr in crs]

        def scan_body(t, carry, crs=crs, cis=cis, lrs=lrs, lis=lis):
            rs = pl.ds(pl.multiple_of(t * rows, rows), rows)
            out = []
            for j in range(2):
                hr, hi = carry[2 * j], carry[2 * j + 1]
                nr = lrs[j] * hr - lis[j] * hi + h_s[rs, crs[j]]
                ni = lrs[j] * hi + lis[j] * hr + h_s[rs, cis[j]]
                h_s[rs, crs[j]] = nr
                h_s[rs, cis[j]] = ni
                out += [nr, ni]
            return tuple(out)

        zero = jnp.zeros((rows, cw), F32)
        lax.fori_loop(0, steps, scan_body, (zero,) * 4)

    if chain:
        ptr = pow_ref[steps - 1, 0:1, :ns]
        pti = pow_ref[steps - 1, 0:1, ns:]
        carry = carry_s[...]
        for c in range(rows):
            hin_s[c:c + 1, :] = carry
            cr_, ci_ = carry[:, :ns], carry[:, ns:]
            e = h_s[last + c:last + c + 1, :]
            carry = jnp.concatenate([ptr * cr_ - pti * ci_ + e[:, :ns],
                                     ptr * ci_ + pti * cr_ + e[:, ns:]], axis=1)
        carry_s[...] = carry
    else:
        hin_s[...] = h0_ref[...]

    for cc in range(ns // cw):
        cr = slice(cc * cw, (cc + 1) * cw)
        ci = slice(ns + cc * cw, ns + (cc + 1) * cw)
        hr0 = hin_s[:, cr]
        hi0 = hin_s[:, ci]

        def fix_body(t, _, cr=cr, ci=ci, hr0=hr0, hi0=hi0):
            rs = pl.ds(pl.multiple_of(t * rows, rows), rows)
            pr = pow_ref[t, :, cr]
            pi_ = pow_ref[t, :, ci]
            h_s[rs, cr] = h_s[rs, cr] + (pr * hr0 - pi_ * hi0)
            h_s[rs, ci] = h_s[rs, ci] + (pr * hi0 + pi_ * hr0)
            return 0

        lax.fori_loop(0, steps, fix_body, 0, unroll=8)

    ht_ref[...] = h_s[last:last + rows, :]
    ys = []
    for s in range(n_slabs):
        hs = jnp.concatenate([h_s[:, s * sw:(s + 1) * sw], h_s[:, ns + s * sw:ns + (s + 1) * sw]], axis=1)
        ys.append(jnp.dot(hs.astype(BF16), cm_ref[s], preferred_element_type=F32))
    y = jnp.concatenate(ys, axis=1) + dsk_ref[...] * u_ref[...]
    y = _gelu_tanh(y)
    gl = jnp.dot(y.astype(BF16), wglu_ref[...], preferred_element_type=F32) + bglu_ref[...]
    z = _rms(y * _sigmoid(gl), g_ref[...]).astype(BF16)
    z_ref[...] = jnp.dot(pt_ref[...], z, preferred_element_type=F32).astype(z_ref.dtype)


def _ssm_tables(ssm_a_re, ssm_a_im, ssm_log_dt, ssm_b_re, ssm_b_im, ssm_c_re, ssm_c_im, max_steps):
    g, n, ch = SSM_GROUPS, SSM_STATE, SSM_CH
    a_re = ssm_a_re.astype(F32)
    a_im = ssm_a_im.astype(F32)
    dt = jnp.exp(ssm_log_dt.astype(F32))[:, None]
    mag = jnp.exp(dt * a_re)
    lam_re = mag * jnp.cos(dt * a_im)
    lam_im = mag * jnp.sin(dt * a_im)
    nr = lam_re - 1.0
    ni = lam_im
    inv = 1.0 / (a_re * a_re + a_im * a_im)
    coef_re = (nr * a_re + ni * a_im) * inv
    coef_im = (ni * a_re - nr * a_im) * inv
    br = ssm_b_re.astype(F32)
    bi = ssm_b_im.astype(F32)
    bb_re = coef_re[..., None] * br - coef_im[..., None] * bi
    bb_im = coef_re[..., None] * bi + coef_im[..., None] * br
    gs = V7X_LANES // ch
    ns_ = g // gs
    eye = jnp.eye(gs, dtype=F32)
    bmat = lambda b: jnp.einsum("sgnc,gh->sgchn", b.reshape(ns_, gs, n, ch), eye).reshape(ns_, gs * ch, gs * n)
    cmat = lambda c: jnp.einsum("sgcn,gh->sgnhc", c.astype(F32).reshape(ns_, gs, ch, n), eye).reshape(
        ns_, gs * n, gs * ch)
    bb = jnp.concatenate([bmat(bb_re), bmat(bb_im)], axis=2).astype(BF16)
    cm = jnp.concatenate([cmat(ssm_c_re), -cmat(ssm_c_im)], axis=1).astype(BF16)
    lam = jnp.stack([lam_re.reshape(-1), lam_im.reshape(-1)])

    k = jnp.arange(1, max_steps + 1, dtype=F32)[:, None]
    kdt = k * dt.reshape(1, -1).repeat(n, axis=1)
    pmag = jnp.exp(kdt * a_re.reshape(1, -1))
    parg = kdt * a_im.reshape(1, -1)
    pows = jnp.concatenate([pmag * jnp.cos(parg), pmag * jnp.sin(parg)], axis=1)
    return bb, cm, lam, pows


def _ssm(u_perm, h0, tabs, dsk, w_glu, b_glu, g, perm_t, rows, steps, chain):
    n, w = u_perm.shape
    blk = rows * steps
    bb, cm, lam, pows = tabs
    row = pl.BlockSpec((blk, w), lambda i: (i, 0))
    kern = functools.partial(_ssm_kernel, rows=rows, steps=steps, chain=chain)
    return pl.pallas_call(
        kern,
        out_shape=(jax.ShapeDtypeStruct((n, w), BF16), jax.ShapeDtypeStruct((rows, 2 * N_STATE), F32)),
        grid=(n // blk,),
        in_specs=[row, _full(bb.shape), _full(lam.shape), _full((steps, rows, 2 * N_STATE)), _full(cm.shape),
                  _full((1, w)), _full((w, w)), _full((1, w)), _full((1, w)), _full((blk, blk)),
                  _full((rows, 2 * N_STATE))],
        out_specs=(row, _full((rows, 2 * N_STATE))),
        scratch_shapes=[pltpu.VMEM((blk, 2 * N_STATE), F32),
                        pltpu.VMEM((rows, 2 * N_STATE), F32),
                        pltpu.VMEM((1, 2 * N_STATE), F32)],
        compiler_params=_cparams(),
        name=f"ssm_r{rows}",
    )(u_perm, bb, lam, jnp.broadcast_to(pows[:steps, None, :], (steps, rows, 2 * N_STATE)), cm,
      dsk.reshape(1, w), w_glu.astype(BF16), b_glu.reshape(1, w),
      g.reshape(1, w), perm_t, h0)


def _outproj_kernel(*refs, n_real, aliased):
    ins, outs = refs[:12], refs[12 + aliased:]
    step = pl.program_id(0)

    @pl.when(step < n_real)
    def _():
        _outproj_tile(*ins, *outs)

    @pl.when(step >= n_real)
    def _():
        xs_ref = outs[1]
        xs_ref[...] = jnp.zeros(xs_ref.shape, xs_ref.dtype)


def _outproj_tile(x_ref, a_ref, z_ref, gt_ref, sh_ref, sc_ref, g_ref, wo_ref, wr_ref, br_ref, ltri_ref, utri_ref,
                  x1_ref, xs_ref, slot_ref, gates_ref, meta_ref, wbf_ref):
    @pl.when(pl.program_id(0) == 0)
    def _():
        wbf_ref[...] = wo_ref[...].astype(BF16)

    a = ATTN_WIDTH
    mixed = (jnp.dot(a_ref[...].astype(BF16), wbf_ref[:a, :], preferred_element_type=F32)
             + jnp.dot(z_ref[...], wbf_ref[a:, :], preferred_element_type=F32))
    x1 = x_ref[...] + gt_ref[...] * mixed
    x1_ref[...] = x1
    h2 = (_rms(x1, g_ref[...]) * (1.0 + sc_ref[...]) + sh_ref[...]).astype(BF16)
    lg = jnp.dot(h2, wr_ref[...], preferred_element_type=F32) + br_ref[...]
    tm = lg.shape[0]
    lane = lax.broadcasted_iota(jnp.int32, (tm, V7X_LANES), 1).astype(F32)
    vals, hots = [], []
    for _ in range(TOP_K):
        m = jnp.max(lg, axis=1, keepdims=True)
        idx = jnp.min(jnp.where(lg == m, lane, float(V7X_LANES)), axis=1, keepdims=True)
        hot = lane == idx
        vals.append(m)
        hots.append(jnp.where(hot, 1.0, 0.0))
        lg = jnp.where(hot, MASKED * 2, lg)
    es = [jnp.exp(v - vals[0]) for v in vals]
    inv = 1.0 / (es[0] + es[1] + es[2] + es[3])

    member = hots[0] + hots[1] + hots[2] + hots[3]
    before = jnp.dot(ltri_ref[...], member.astype(BF16), preferred_element_type=F32)
    count = jnp.sum(member, axis=0, keepdims=True)
    padded = jnp.floor((count + (ROW_ALIGN - 1.0)) * (1.0 / ROW_ALIGN)) * ROW_ALIGN
    padded8 = jnp.broadcast_to(padded, (V7X_SUBLANES, V7X_LANES))
    start = jnp.dot(padded8.astype(BF16), utri_ref[...], preferred_element_type=F32)[0:1]
    where_to = start + before
    slots = jnp.zeros((tm, V7X_LANES), F32)
    gates = jnp.zeros((tm, V7X_LANES), F32)
    for k in range(TOP_K):
        slot_k = jnp.sum(hots[k] * where_to, axis=1, keepdims=True)
        slots = jnp.where(lane == float(k), slot_k, slots)
        gates = jnp.where(lane == float(k), es[k] * inv, gates)
    slot_ref[...] = slots
    gates_ref[...] = gates
    row = lax.broadcasted_iota(jnp.int32, (V7X_SUBLANES, V7X_LANES), 0)
    meta_ref[...] = jnp.where(row == 0, padded8, jnp.where(row == 1, jnp.broadcast_to(start, padded8.shape), 0.0))

    cap = xs_ref.shape[0]
    slots_t = jnp.transpose(slots)
    srow = lax.broadcasted_iota(jnp.int32, (cap, tm), 0).astype(F32)
    place = jnp.zeros((cap, tm), F32)
    for k in range(TOP_K):
        place = jnp.where(srow == slots_t[k:k + 1, :], 1.0, place)
    xs_ref[...] = jnp.dot(place.astype(BF16), h2, preferred_element_type=F32).astype(BF16)


def _outproj(x, a, z, gt, sh, sc, g, w_out, wr_pad, br_pad, tm, n_blocks, block0, xs_prev=None):
    n, d = x.shape
    aw = ATTN_WIDTH
    n_real = n // tm
    aliased = xs_prev is not None
    n_steps = n_real if aliased else n_blocks
    tile = lambda i: jnp.minimum(i, n_real - 1)
    mod_spec = (pl.BlockSpec((1, d), lambda i: (0, 0)) if gt.shape[0] == 1
                else pl.BlockSpec((tm, d), lambda i: (tile(i), 0)))
    row = lambda w: pl.BlockSpec((tm, w), lambda i: (tile(i), 0))
    ltri = jnp.asarray(np.tril(np.ones((tm, tm), np.float32), -1), BF16)
    utri = jnp.asarray(np.triu(np.ones((V7X_LANES, V7X_LANES), np.float32), 1), BF16)
    in_specs = [row(d), row(aw), row(SSM_WIDTH), mod_spec, mod_spec, mod_spec, _full((1, d)),
                _full(w_out.shape), _full(wr_pad.shape), _full(br_pad.shape), _full(ltri.shape), _full(utri.shape)]
    args = [x, a, z, gt, sh, sc, g.reshape(1, d), w_out, wr_pad, br_pad, ltri, utri]
    if aliased:
        in_specs.append(pl.BlockSpec(memory_space=pl.ANY))
        args.append(xs_prev)
    return pl.pallas_call(
        functools.partial(_outproj_kernel, n_real=n_real, aliased=int(aliased)),
        out_shape=(jax.ShapeDtypeStruct((n, d), F32),
                   jax.ShapeDtypeStruct((n_blocks, GROUP_CAP, d), BF16),
                   jax.ShapeDtypeStruct((n, V7X_LANES), F32), jax.ShapeDtypeStruct((n, V7X_LANES), F32),
                   jax.ShapeDtypeStruct((n_real, V7X_SUBLANES, V7X_LANES), F32)),
        grid=(n_steps,),
        in_specs=in_specs,
        out_specs=(row(d), pl.BlockSpec((None, GROUP_CAP, d), lambda i: (block0 + i, 0, 0)),
                   row(V7X_LANES), row(V7X_LANES),
                   pl.BlockSpec((None, V7X_SUBLANES, V7X_LANES), lambda i: (tile(i), 0, 0))),
        scratch_shapes=[pltpu.VMEM(w_out.shape, BF16)],
        input_output_aliases={len(args) - 1: 1} if aliased else {},
        compiler_params=_cparams(),
        name="outproj_router",
    )(*args)


_PIECE_SIZES = tuple(MOE_TILE >> s for s in range(6))
_CHUNK_SHIFT = 6
_CHUNK = 1 << _CHUNK_SHIFT
_TAIL_SIZES = tuple(sz for sz in _PIECE_SIZES if sz < _CHUNK)


def _moe_kernel(te_ref, tl_ref, lo_ref, hi_ref, nu_ref, gstart_ref, gsize_ref, gbase_ref, rows_ref, used_ref,
                nxt_ref, xs_hbm, wgu_hbm, bgu_ref, wd_hbm, bd_ref, ys_hbm,
                xbuf, ybuf, zbuf, wgu_f32, wd_f32, wgu_bf, wd_bf, xsem, ysem, zsem, wsem):
    t = pl.program_id(0)
    n_used = nu_ref[0]
    n_blocks, cap = xs_hbm.shape[0], xs_hbm.shape[1]

    def weight_copies(e):
        return (pltpu.make_async_copy(wgu_hbm.at[e], wgu_f32, wsem.at[0]),
                pltpu.make_async_copy(wd_hbm.at[e], wd_f32, wsem.at[1]))

    def x_copy(i, br, tr, sz, slot):
        pltpu.make_async_copy(xs_hbm.at[i, pl.ds(br, sz)], xbuf.at[slot, pl.ds(tr, sz)], xsem.at[slot]).start()

    def y_copy(i, br, tr, sz, slot):
        pltpu.make_async_copy(ybuf.at[slot, pl.ds(tr, sz)], ys_hbm.at[i, pl.ds(br, sz)], ysem.at[slot]).start()

    def z_copy(i, row, sz, start):
        cp = pltpu.make_async_copy(zbuf.at[pl.ds(0, sz)], ys_hbm.at[i, pl.ds(row, sz)], zsem)
        cp.start() if start else cp.wait()

    def pieces(tt, fn):
        e = te_ref[tt]
        lo = tl_ref[tt] * MOE_TILE

        def per_block(i, c):
            g = e * n_blocks + i
            s0 = gbase_ref[g]
            a = jnp.maximum(s0, lo)
            b = jnp.minimum(s0 + gsize_ref[g], lo + MOE_TILE)
            length = jnp.maximum(b - a, 0)
            src = gstart_ref[g] + (a - s0)
            dst = a - lo
            whole = lax.shift_right_logical(length, _CHUNK_SHIFT)

            def chunk(j, cc):
                off = j * _CHUNK
                fn(i, pl.multiple_of(src + off, ROW_ALIGN), pl.multiple_of(dst + off, ROW_ALIGN), _CHUNK)
                return cc

            lax.fori_loop(0, whole, chunk, 0)
            done = whole * _CHUNK
            for sz in _TAIL_SIZES:
                hit = (length & sz) != 0

                @pl.when(hit)
                def _(sz=sz, done=done):
                    fn(i, pl.multiple_of(src + done, ROW_ALIGN), pl.multiple_of(dst + done, ROW_ALIGN), sz)

                done = done + jnp.where(hit, sz, 0)
            return c

        lax.fori_loop(lo_ref[tt], hi_ref[tt], per_block, 0)

    def tile_rows(tt):
        return jnp.minimum(rows_ref[te_ref[tt]] - tl_ref[tt] * MOE_TILE, MOE_TILE)

    def wait_rows(n, sem, buf):
        def chunk(j, cc):
            pltpu.make_async_copy(buf.at[pl.ds(0, _CHUNK)], buf.at[pl.ds(0, _CHUNK)], sem).wait()
            return cc

        lax.fori_loop(0, lax.shift_right_logical(n, _CHUNK_SHIFT), chunk, 0)
        for sz in _TAIL_SIZES:
            @pl.when((n & sz) != 0)
            def _(sz=sz):
                pltpu.make_async_copy(buf.at[pl.ds(0, sz)], buf.at[pl.ds(0, sz)], sem).wait()

    def fetch(tt, slot):
        pieces(tt, lambda i, br, tr, sz: x_copy(i, br, tr, sz, slot))

    def writeback(tt, slot):
        pieces(tt, lambda i, br, tr, sz: y_copy(i, br, tr, sz, slot))

    def zero_tail(i, start):
        u = used_ref[i]
        rem = cap - u
        nz = zbuf.shape[0]
        whole = lax.shift_right_logical(rem, nz.bit_length() - 1)

        def chunk(j, c):
            z_copy(i, pl.multiple_of(u + j * nz, ROW_ALIGN), nz, start)
            return c

        lax.fori_loop(0, whole, chunk, 0)
        base = u + whole * nz
        done = jnp.int32(0)
        for sz in _PIECE_SIZES:
            if sz >= nz:
                continue
            hit = (rem & sz) != 0

            @pl.when(hit)
            def _(sz=sz, done=done):
                z_copy(i, pl.multiple_of(base + done, ROW_ALIGN), sz, start)

            done = done + jnp.where(hit, sz, 0)

    @pl.when(t == 0)
    def _():
        xbuf[...] = jnp.zeros(xbuf.shape, xbuf.dtype)
        zbuf[...] = jnp.zeros(zbuf.shape, zbuf.dtype)
        lax.fori_loop(0, n_blocks, lambda i, c: (zero_tail(i, True), c)[1], 0)
        lax.fori_loop(0, n_blocks, lambda i, c: (zero_tail(i, False), c)[1], 0)
        fetch(0, 0)
        for cp in weight_copies(te_ref[0]):
            cp.start()

    @pl.when(t < n_used)
    def _():
        slot = t % 2

        @pl.when(t + 1 < n_used)
        def _():
            fetch(t + 1, 1 - slot)

        @pl.when(tl_ref[t] == 0)
        def _():
            for cp in weight_copies(te_ref[t]):
                cp.wait()
            wgu_bf[...] = wgu_f32[...].astype(BF16)
            wd_bf[...] = wd_f32[...].astype(BF16)
            nxt = nxt_ref[te_ref[t]]

            @pl.when(nxt >= 0)
            def _():
                for cp in weight_copies(nxt):
                    cp.start()

        wait_rows(tile_rows(t), xsem.at[slot], xbuf.at[slot])

        @pl.when(t >= 2)
        def _():
            wait_rows(tile_rows(t - 2), ysem.at[slot], ybuf.at[slot])

        def expert_mlp(m):
            f = D_EXPERT
            gu = jnp.dot(xbuf[slot, pl.ds(0, m)], wgu_bf[...], preferred_element_type=F32) + bgu_ref[...]
            gate = jnp.minimum(gu[:, :f], SWIGLU_LIMIT)
            up = jnp.clip(gu[:, f:], -SWIGLU_LIMIT, SWIGLU_LIMIT)
            act = (up + 1.0) * gate * _sigmoid(SWIGLU_ALPHA * gate)
            y = jnp.dot(act.astype(BF16), wd_bf[...], preferred_element_type=F32) + bd_ref[...]
            ybuf[slot, pl.ds(0, m)] = y.astype(BF16)

        quarter = MOE_TILE // 4
        n_quarters = lax.shift_right_logical(tile_rows(t) + (quarter - 1), quarter.bit_length() - 1)
        for nq in range(1, 5):
            @pl.when(n_quarters == nq)
            def _(nq=nq):
                expert_mlp(nq * quarter)

        writeback(t, slot)

        @pl.when(t == n_used - 1)
        def _():
            wait_rows(tile_rows(t), ysem.at[slot], ybuf.at[slot])

            @pl.when(t >= 1)
            def _():
                wait_rows(tile_rows(t - 1), ysem.at[1 - slot], ybuf.at[1 - slot])


def _moe(plan, xs, w_gate_up, b_gate_up, w_down, b_down):
    _, cap, d = xs.shape
    e, _, f2 = w_gate_up.shape
    nt = plan[0].shape[0]
    wmap = lambda t, te, *_: (te[t], 0, 0)
    anyspec = pl.BlockSpec(memory_space=pl.ANY)
    grid_spec = pltpu.PrefetchScalarGridSpec(
        num_scalar_prefetch=len(plan),
        grid=(nt,),
        in_specs=[anyspec,
                  anyspec,
                  pl.BlockSpec((None, 1, f2), wmap),
                  anyspec,
                  pl.BlockSpec((None, 1, d), wmap)],
        out_specs=anyspec,
        scratch_shapes=[pltpu.VMEM((2, MOE_TILE, d), BF16), pltpu.VMEM((2, MOE_TILE, d), BF16),
                        pltpu.VMEM((MOE_TILE // 2, d), BF16),
                        pltpu.VMEM((d, f2), F32), pltpu.VMEM((f2 // 2, d), F32),
                        pltpu.VMEM((d, f2), BF16), pltpu.VMEM((f2 // 2, d), BF16),
                        pltpu.SemaphoreType.DMA((2,)), pltpu.SemaphoreType.DMA((2,)), pltpu.SemaphoreType.DMA(()),
                        pltpu.SemaphoreType.DMA((2,))],
    )
    return pl.pallas_call(
        _moe_kernel,
        out_shape=jax.ShapeDtypeStruct(xs.shape, BF16),
        grid_spec=grid_spec,
        compiler_params=_cparams(),
        name="moe_experts",
    )(*plan, xs, w_gate_up, b_gate_up.reshape(e, 1, f2), w_down, b_down.reshape(e, 1, d))


def _moe_plan(group_size, group_start):
    n_blocks = group_size.shape[0]
    gsize = group_size.T.astype(jnp.int32)
    gstart = group_start.T.astype(jnp.int32)
    gbase = jnp.cumsum(gsize, axis=1) - gsize
    rows = jnp.sum(gsize, axis=1)
    tiles = (rows + MOE_TILE - 1) // MOE_TILE
    tile_end = jnp.cumsum(tiles)
    n_used = tile_end[-1:]
    nt = (n_blocks * GROUP_CAP) // MOE_TILE + N_EXPERTS
    t = jnp.arange(nt, dtype=jnp.int32)
    te = jnp.sum((tile_end[None, :] <= t[:, None]).astype(jnp.int32), axis=1)
    last = jnp.max(jnp.where(tiles > 0, jnp.arange(N_EXPERTS, dtype=jnp.int32), 0))
    te = jnp.where(t < n_used[0], jnp.minimum(te, N_EXPERTS - 1), last)
    hot = (te[:, None] == jnp.arange(N_EXPERTS, dtype=jnp.int32)[None, :]).astype(jnp.int32)
    tl = jnp.where(t < n_used[0], t - hot @ (tile_end - tiles), 0)
    lo_row = tl * MOE_TILE
    base_t = hot @ gbase
    size_t = hot @ gsize
    first = jnp.sum((base_t + size_t <= lo_row[:, None]).astype(jnp.int32), axis=1)
    stop = jnp.sum((base_t < lo_row[:, None] + MOE_TILE).astype(jnp.int32), axis=1)
    used = jnp.sum(gsize, axis=0)
    ids = jnp.arange(N_EXPERTS, dtype=jnp.int32)
    later = (ids[None, :] > ids[:, None]) & (tiles[None, :] > 0)
    nxt = jnp.min(jnp.where(later, ids[None, :], N_EXPERTS), axis=1)
    nxt = jnp.where(nxt < N_EXPERTS, nxt, -1)
    i32 = lambda z: z.astype(jnp.int32)
    return (i32(te), i32(tl), i32(first), i32(stop), i32(n_used), i32(gstart.reshape(-1)), i32(gsize.reshape(-1)),
            i32(gbase.reshape(-1)), i32(rows), i32(used), i32(nxt))


def _final_kernel(x_ref, ys_ref, slot_ref, gates_ref, gt_ref, g_ref, y_ref):
    tm = x_ref.shape[0]
    cap = ys_ref.shape[0]
    col = lax.broadcasted_iota(jnp.int32, (tm, cap), 1).astype(F32)
    slots = slot_ref[...]
    gates = gates_ref[...]
    mix = jnp.zeros((tm, cap), F32)
    for k in range(TOP_K):
        mix = jnp.where(col == slots[:, k:k + 1], gates[:, k:k + 1], mix)
    ff = jnp.dot(mix.astype(BF16), ys_ref[...], preferred_element_type=F32)
    y_ref[...] = _rms(x_ref[...] + gt_ref[...] * ff, g_ref[...])


def _final(x1, ys, slots, gates, gt, g, tm, block0):
    n, d = x1.shape
    cap = ys.shape[1]
    mod_spec = (pl.BlockSpec((1, d), lambda i: (0, 0)) if gt.shape[0] == 1
                else pl.BlockSpec((tm, d), lambda i: (i, 0)))
    row = lambda w: pl.BlockSpec((tm, w), lambda i: (i, 0))
    return pl.pallas_call(
        _final_kernel,
        out_shape=jax.ShapeDtypeStruct((n, d), F32),
        grid=(n // tm,),
        in_specs=[row(d), pl.BlockSpec((None, cap, d), lambda i: (block0 + i, 0, 0)),
                  row(V7X_LANES), row(V7X_LANES), mod_spec, _full((1, d))],
        out_specs=row(d),
        compiler_params=_cparams(),
        name="final_norm",
    )(x1, ys, slots, gates, gt, g.reshape(1, d))


def kernel(x_prompt, x_sample, cache_k_win, cache_v_win, state_ssm_re, state_ssm_im, c_prompt, c_sample,
           w_ada, b_ada, g_norm1, w_in, ssm_a_re, ssm_a_im, ssm_log_dt, ssm_b_re, ssm_b_im, ssm_c_re, ssm_c_im,
           ssm_d, w_glu, b_glu, g_out_attn, g_out_ssm, w_out, g_norm2, w_router, b_router, w_gate_up, b_gate_up,
           w_down, b_down, g_final):
    depth = w_ada.shape[0]
    assert depth == 1 and x_prompt.shape[0] == 1
    bp, t, d = x_prompt.shape
    bs, ts, _ = x_sample.shape
    ns = bs * ts
    l = 0
    a = ATTN_WIDTH

    n_c = bp + bs
    c_pad = -(-n_c // V7X_SUBLANES) * V7X_SUBLANES
    c_rows = jnp.concatenate([c_prompt, c_sample, jnp.zeros((c_pad - n_c, d), F32)], axis=0)
    mod = _ada_modulation(c_rows, w_ada[l], b_ada[l])
    mod_p = [mod[0:1, i * d:(i + 1) * d] for i in range(N_MOD)]
    mod_s = [jnp.repeat(mod[bp:bp + bs, i * d:(i + 1) * d], ts, axis=0) for i in range(N_MOD)]

    steps_p = TOKEN_TILE // SSM_ROWS_PROMPT
    perm_p = _chunk_perm(SSM_ROWS_PROMPT, steps_p)
    perm_s = _chunk_perm(bs, ts)
    tabs = _ssm_tables(ssm_a_re[l], ssm_a_im[l], ssm_log_dt[l], ssm_b_re[l], ssm_b_im[l],
                       ssm_c_re[l], ssm_c_im[l], max(steps_p, ts))
    wr_pad = jnp.zeros((d, V7X_LANES), F32).at[:, :N_EXPERTS].set(w_router[l]).astype(BF16)
    br_pad = jnp.full((1, V7X_LANES), MASKED, F32).at[0, :N_EXPERTS].set(b_router[l])

    xp = x_prompt.reshape(t, d)
    dils = tuple(dil for _, dil in DILATED_BRANCHES)
    wide = tuple(dil for dil in dils if dil > 1)
    proj_p = _inproj(xp, mod_p[0], mod_p[1], g_norm1[l], w_in[l], jnp.asarray(perm_p, BF16), TOKEN_TILE, wide)
    kpf, vpf, up = proj_p[3:6]
    views = {1: proj_p[0:3]}
    for n_d, dil in enumerate(wide):
        views[dil] = proj_p[6 + 3 * n_d:9 + 3 * n_d]
    outs = [_attn_branch(*views[dil], dil) for dil in dils]
    ap = _attn_combine([o for o, _ in outs], [s for _, s in outs], g_out_attn[l], dils)
    zeros_h = jnp.zeros((SSM_ROWS_PROMPT, 2 * N_STATE), F32)
    zp, hp = _ssm(up, zeros_h, tabs, ssm_d[l], w_glu[l], b_glu[l], g_out_ssm[l],
                  jnp.asarray(perm_p.T, BF16), SSM_ROWS_PROMPT, steps_p, True)
    n_blocks_p = t // TOKEN_TILE
    n_blocks = n_blocks_p + 1
    x1p, xs_all, slot_p, gate_p, meta_p = _outproj(xp, ap, zp, mod_p[2], mod_p[3], mod_p[4], g_norm2[l], w_out[l],
                                                   wr_pad, br_pad, TOKEN_TILE, n_blocks, 0)

    xs = x_sample.reshape(ns, d)
    qs, ks, vs, ksf, vsf, us = _inproj(xs, mod_s[0], mod_s[1], g_norm1[l], w_in[l], jnp.asarray(perm_s, BF16), ns)
    split = lambda z: z.reshape(bs, ts, N_HEADS, HEAD_DIM)
    new_t = lambda z: jnp.pad(split(z).transpose(0, 2, 3, 1), ((0, 0), (0, 0), (0, 0), (0, NEW_KEY_PAD - ts)))
    as_ = _attn_sample(split(qs).transpose(0, 2, 1, 3), new_t(ksf), new_t(vsf),
                       cache_k_win.transpose(0, 1, 3, 4, 2), cache_v_win.transpose(0, 1, 3, 4, 2), l, g_out_attn[l])
    as_ = as_.transpose(0, 2, 1, 3)
    h0s = jnp.concatenate([state_ssm_re[l].reshape(bs, N_STATE), state_ssm_im[l].reshape(bs, N_STATE)], axis=1)
    zs, hs = _ssm(us, h0s, tabs, ssm_d[l], w_glu[l], b_glu[l], g_out_ssm[l],
                  jnp.asarray(perm_s.T, BF16), bs, ts, False)
    x1s, xs_all, slot_s, gate_s, meta_s = _outproj(xs, as_.reshape(ns, a), zs, mod_s[2], mod_s[3], mod_s[4],
                                                   g_norm2[l], w_out[l], wr_pad, br_pad, ns, n_blocks, n_blocks_p,
                                                   xs_prev=xs_all)

    meta = jnp.concatenate([meta_p, meta_s], axis=0)
    plan = _moe_plan(meta[:, 0, :N_EXPERTS], meta[:, 1, :N_EXPERTS])
    ys_all = _moe(plan, xs_all, w_gate_up[l], b_gate_up[l], w_down[l], b_down[l])

    y_prompt = _final(x1p, ys_all, slot_p, gate_p, mod_p[5], g_final, TOKEN_TILE, 0).reshape(bp, t, d)
    y_sample = _final(x1s, ys_all, slot_s, gate_s, mod_s[5], g_final, ns, n_blocks_p).reshape(bs, ts, d)

    keep = min(MAX_WINDOW, t)
    shp = (1, bp, keep, N_HEADS, HEAD_DIM)
    k_win = kpf[t - keep:].reshape(shp)
    v_win = vpf[t - keep:].reshape(shp)
    st = (1, bp, SSM_GROUPS, SSM_STATE)
    hp_last = hp[SSM_ROWS_PROMPT - 1]
    ss = (1, bs, SSM_GROUPS, SSM_STATE)
    return (y_prompt, y_sample, k_win, v_win,
            hp_last[:N_STATE].reshape(st), hp_last[N_STATE:].reshape(st),
            ksf.reshape(1, bs, ts, N_HEADS, HEAD_DIM), vsf.reshape(1, bs, ts, N_HEADS, HEAD_DIM),
            hs[:, :N_STATE].reshape(ss), hs[:, N_STATE:].reshape(ss))
```

```python
import functools

import numpy as np
import jax
import jax.numpy as jnp
from jax import lax
from jax.experimental import pallas as pl
from jax.experimental.pallas import tpu as pltpu

F32 = jnp.float32
BF16 = jnp.bfloat16

D_MODEL = 1024
N_HEADS = 8
HEAD_DIM = 64
ATTN_WIDTH = N_HEADS * HEAD_DIM
DILATED_BRANCHES = ((128, 1), (512, 4), (2048, 16))
KEYS_PER_BRANCH = 129
MAX_WINDOW = 2048
SSM_WIDTH = D_MODEL - ATTN_WIDTH
SSM_CH = 16
SSM_GROUPS = SSM_WIDTH // SSM_CH
SSM_STATE = 64
N_STATE = SSM_GROUPS * SSM_STATE
N_EXPERTS = 32
TOP_K = 4
D_EXPERT = D_MODEL
SWIGLU_LIMIT = 7.0
SWIGLU_ALPHA = 1.702
N_MOD = 6
EPS = 1e-6
MASKED = -1e30

V7X_LANES = 128
V7X_SUBLANES = 8
V7X_VMEM_LIMIT_BYTES = 56 * 1024 * 1024

TOKEN_TILE = 512
Q_TILE = 128
Q_BLOCKS_PER_STEP = 8
SSM_ROWS_PROMPT = 8
MOE_TILE = 512
ROW_ALIGN = 16
GROUP_CAP = -(-(TOKEN_TILE * TOP_K + N_EXPERTS * (ROW_ALIGN - 1)) // (2 * V7X_LANES)) * (2 * V7X_LANES)


def _cparams(n_axes=1):
    return pltpu.CompilerParams(
        dimension_semantics=("arbitrary",) * n_axes,
        vmem_limit_bytes=V7X_VMEM_LIMIT_BYTES,
    )


def _full(shape):
    n = len(shape)
    return pl.BlockSpec(shape, lambda *_: (0,) * n)


def _rms(x, g):
    return x * lax.rsqrt(jnp.mean(x * x, axis=-1, keepdims=True) + EPS) * g


def _sigmoid(x):
    return 1.0 / (1.0 + jnp.exp(-x))


def _ada_kernel(c_ref, w_ref, b_ref, o_ref):
    c = c_ref[...]
    s = (c * _sigmoid(c)).astype(BF16)
    o_ref[...] = jnp.dot(s, w_ref[...].astype(BF16), preferred_element_type=F32) + b_ref[...]


def _ada_modulation(c_rows, w_ada, b_ada):
    m, d = c_rows.shape
    n = w_ada.shape[1]
    tn = n // 4
    return pl.pallas_call(
        _ada_kernel,
        out_shape=jax.ShapeDtypeStruct((m, n), F32),
        grid=(n // tn,),
        in_specs=[_full((m, d)),
                  pl.BlockSpec((d, tn), lambda j: (0, j)),
                  pl.BlockSpec((1, tn), lambda j: (0, j))],
        out_specs=pl.BlockSpec((m, tn), lambda j: (0, j)),
        compiler_params=_cparams(),
        name="ada_modulation",
    )(c_rows, w_ada, b_ada.reshape(1, n))


def _inproj_kernel(*refs, dils):
    x_ref, sh_ref, sc_ref, g_ref, w_ref, perm_ref = refs[:6]
    dperm_refs = refs[6:6 + len(dils)]
    q_ref, k_ref, v_ref, kf_ref, vf_ref, u_ref = refs[6 + len(dils):12 + len(dils)]
    dil_refs = refs[12 + len(dils):-1]
    wbf_ref = refs[-1]

    @pl.when(pl.program_id(0) == 0)
    def _():
        wbf_ref[...] = w_ref[...].astype(BF16)

    h = _rms(x_ref[...], g_ref[...]) * (1.0 + sc_ref[...]) + sh_ref[...]
    hb = h.astype(BF16)
    a = ATTN_WIDTH
    proj = jnp.dot(hb, wbf_ref[:, :3 * a], preferred_element_type=F32)
    k = proj[:, a:2 * a]
    v = proj[:, 2 * a:]
    qkv = jnp.concatenate([(proj[:, :a] * (HEAD_DIM ** -0.5)).astype(BF16), k.astype(BF16), v.astype(BF16)],
                          axis=1)
    q_ref[...] = qkv[:, :a]
    k_ref[...] = qkv[:, a:2 * a]
    v_ref[...] = qkv[:, 2 * a:]
    kf_ref[...] = k
    vf_ref[...] = v
    hp = jnp.dot(perm_ref[...], hb, preferred_element_type=F32).astype(BF16)
    u_ref[...] = jnp.dot(hp, wbf_ref[:, 3 * a:], preferred_element_type=F32)
    tm = qkv.shape[0]
    for n_d, dil in enumerate(dils):
        by_residue = jnp.dot(dperm_refs[n_d][...], qkv, preferred_element_type=F32).astype(BF16)
        per = tm // dil
        for r in range(dil):
            rows = by_residue[r * per:(r + 1) * per]
            for j in range(3):
                dil_refs[3 * n_d + j][:, r * a:(r + 1) * a] = rows[:, j * a:(j + 1) * a]


def _inproj(x, sh, sc, g, w_in, perm, tm, dils=()):
    n, d = x.shape
    a = ATTN_WIDTH
    mod_rows = sh.shape[0]
    mod_spec = (pl.BlockSpec((1, d), lambda i: (0, 0)) if mod_rows == 1
                else pl.BlockSpec((tm, d), lambda i: (i, 0)))
    row = lambda w: pl.BlockSpec((tm, w), lambda i: (i, 0))
    dperms = [jnp.asarray(_chunk_perm(tm // dil, dil), BF16) for dil in dils]
    view_shapes = tuple(jax.ShapeDtypeStruct((n // dil, dil * a), BF16) for dil in dils for _ in range(3))
    view_specs = tuple(pl.BlockSpec((tm // dil, dil * a), lambda i: (i, 0)) for dil in dils for _ in range(3))
    return pl.pallas_call(
        functools.partial(_inproj_kernel, dils=tuple(dils)),
        out_shape=(jax.ShapeDtypeStruct((n, a), BF16),) * 3
        + (jax.ShapeDtypeStruct((n, a), F32),) * 2
        + (jax.ShapeDtypeStruct((n, SSM_WIDTH), F32),) + view_shapes,
        grid=(n // tm,),
        in_specs=[row(d), mod_spec, mod_spec, _full((1, d)), _full(w_in.shape), _full((tm, tm))]
        + [_full((tm, tm))] * len(dils),
        out_specs=(row(a),) * 5 + (row(SSM_WIDTH),) + view_specs,
        scratch_shapes=[pltpu.VMEM(w_in.shape, BF16)],
        compiler_params=_cparams(),
        name="inproj",
    )(x, sh, sc, g.reshape(1, d), w_in, perm, *dperms)


def _chunk_perm(rows, steps):
    n = rows * steps
    p = np.zeros((n, n), np.float32)
    c, t = np.meshgrid(np.arange(rows), np.arange(steps), indexing="ij")
    p[(t * rows + c).ravel(), (c * steps + t).ravel()] = 1.0
    return p


def _alibi_slopes():
    return np.exp2(-8.0 * np.arange(1, N_HEADS + 1, dtype=np.float64) / N_HEADS).astype(np.float32)


def _branch_bias(dil):
    qi = np.arange(Q_TILE)[:, None]
    col = np.arange(2 * Q_TILE)[None, :]
    j = Q_TILE + qi - col
    valid = (j >= 0) & (j <= Q_TILE)
    dist = (j * dil).astype(np.float32)
    tabs = []
    for first in (True, False):
        ok = valid & (col >= Q_TILE) if first else valid
        per_head = [np.where(ok, -s * dist, np.float32(MASKED)) for s in _alibi_slopes()]
        tabs.append(np.concatenate(per_head, axis=0))
    return np.stack(tabs).astype(np.float32)


def _attn_branch_kernel(q_ref, kp_ref, kc_ref, vp_ref, vc_ref, bias_ref, o_ref, lse_ref):
    first_step = pl.program_id(0) == 0
    lane = lax.broadcasted_iota(jnp.int32, (Q_TILE, V7X_LANES), 1)
    lo = lane < HEAD_DIM
    for j in range(Q_BLOCKS_PER_STEP):
        rows = slice(j * Q_TILE, (j + 1) * Q_TILE)
        before = slice((j - 1) * Q_TILE, j * Q_TILE)
        sel = jnp.where(first_step, 0, 1) if j == 0 else 1
        lse_acc = jnp.zeros((Q_TILE, V7X_LANES), F32)
        for p in range(N_HEADS // 2):
            cs = slice(V7X_LANES * p, V7X_LANES * (p + 1))
            q2 = q_ref[rows, cs]
            zero = jnp.zeros_like(q2)
            qq = jnp.concatenate([jnp.where(lo, q2, zero), jnp.where(lo, zero, q2)], axis=0)
            k_before = kp_ref[:, cs] if j == 0 else kc_ref[before, cs]
            v_before = vp_ref[:, cs] if j == 0 else vc_ref[before, cs]
            kk = jnp.concatenate([k_before, kc_ref[rows, cs]], axis=0)
            vv = jnp.concatenate([v_before, vc_ref[rows, cs]], axis=0)
            s = lax.dot_general(qq, kk, (((1,), (1,)), ((), ())), preferred_element_type=F32)
            s = s + bias_ref[sel, 2 * Q_TILE * p:2 * Q_TILE * (p + 1), :]
            m = jnp.max(s, axis=1, keepdims=True)
            e = jnp.exp(s - m)
            l = jnp.sum(e, axis=1, keepdims=True)
            eb = e.astype(BF16)
            o0 = jnp.dot(eb[:Q_TILE], vv, preferred_element_type=F32) * (1.0 / l[:Q_TILE])
            o1 = jnp.dot(eb[Q_TILE:], vv, preferred_element_type=F32) * (1.0 / l[Q_TILE:])
            o_ref[rows, cs] = jnp.where(lo, o0, o1).astype(o_ref.dtype)
            lse = m + jnp.log(l)
            lse_acc = jnp.where(lane == 2 * p, lse[:Q_TILE], lse_acc)
            lse_acc = jnp.where(lane == 2 * p + 1, lse[Q_TILE:], lse_acc)
        lse_ref[rows, :] = lse_acc[:, :N_HEADS]


def _attn_branch(qv, kv, vv, dil):
    a = ATTN_WIDTH
    rows = qv.shape[0]
    t = rows * dil
    step = Q_BLOCKS_PER_STEP * Q_TILE
    cur = pl.BlockSpec((step, a), lambda i, r: (i, r))
    prev = pl.BlockSpec((Q_TILE, a), lambda i, r: (jnp.maximum(i * Q_BLOCKS_PER_STEP - 1, 0), r))
    bias = jnp.asarray(_branch_bias(dil))
    o, lse = pl.pallas_call(
        _attn_branch_kernel,
        out_shape=(jax.ShapeDtypeStruct((rows, dil * a), BF16),
                   jax.ShapeDtypeStruct((dil, rows, N_HEADS), F32)),
        grid=(rows // step, dil),
        in_specs=[cur, prev, cur, prev, cur, _full(bias.shape)],
        out_specs=(cur, pl.BlockSpec((None, step, N_HEADS), lambda i, r: (r, i, 0))),
        compiler_params=_cparams(2),
        name=f"attn_branch_d{dil}",
    )(qv, kv, kv, vv, vv, bias)
    return o, lse.transpose(1, 0, 2).reshape(t, N_HEADS)


def _attn_combine_kernel(*refs, dils):
    nb = len(dils)
    o_refs = refs[:nb]
    l_refs = refs[nb:2 * nb]
    g_ref = refs[2 * nb]
    unperm_refs = refs[2 * nb + 1:-1]
    a_ref = refs[-1]
    tq, a = a_ref.shape
    outs = []
    n_u = 0
    for o_ref, dil in zip(o_refs, dils):
        if dil == 1:
            outs.append(o_ref[...].astype(F32))
            continue
        by_residue = jnp.concatenate([o_ref[:, r * a:(r + 1) * a] for r in range(dil)], axis=0)
        outs.append(jnp.dot(unperm_refs[n_u][...], by_residue, preferred_element_type=F32))
        n_u += 1
    ls = [l_ref[...] for l_ref in l_refs]
    top = functools.reduce(jnp.maximum, ls)
    ws = [jnp.exp(l - top) for l in ls]
    inv = 1.0 / functools.reduce(jnp.add, ws)
    cs = [w * inv for w in ws]
    lane = lax.broadcasted_iota(jnp.int32, (tq, V7X_LANES), 1)
    lo = lane < HEAD_DIM
    cols = []
    for p in range(N_HEADS // 2):
        sl = slice(V7X_LANES * p, V7X_LANES * (p + 1))
        acc = jnp.zeros((tq, V7X_LANES), F32)
        for c, o in zip(cs, outs):
            cexp = jnp.where(lo,
                             jnp.broadcast_to(c[:, 2 * p:2 * p + 1], (tq, V7X_LANES)),
                             jnp.broadcast_to(c[:, 2 * p + 1:2 * p + 2], (tq, V7X_LANES)))
            acc = acc + cexp * o[:, sl]
        cols.append(acc)
    o = jnp.concatenate(cols, axis=1)
    a_ref[...] = _rms(o, g_ref[...]).astype(a_ref.dtype)


def _attn_combine(os_, lses, g, dils):
    t = lses[0].shape[0]
    a = ATTN_WIDTH
    tq = TOKEN_TILE
    unperms = [jnp.asarray(_chunk_perm(tq // dil, dil).T, BF16) for dil in dils if dil > 1]
    return pl.pallas_call(
        functools.partial(_attn_combine_kernel, dils=tuple(dils)),
        out_shape=jax.ShapeDtypeStruct((t, a), BF16),
        grid=(t // tq,),
        in_specs=[pl.BlockSpec((tq // dil, dil * a), lambda i: (i, 0)) for dil in dils]
        + [pl.BlockSpec((tq, N_HEADS), lambda i: (i, 0))] * len(dils) + [_full((1, a))]
        + [_full((tq, tq))] * len(unperms),
        out_specs=pl.BlockSpec((tq, a), lambda i: (i, 0)),
        compiler_params=_cparams(),
        name="attn_combine",
    )(*os_, *lses, g.reshape(1, a), *unperms)


NEW_KEY_PAD = V7X_LANES


def _sample_bias(win, steps):
    slopes = _alibi_slopes()[None, :, None, None]
    t = np.arange(steps)[None, None, :, None]

    def table(dist, live):
        tabs = []
        for window, dil in DILATED_BRANCHES:
            ok = live & (dist >= 0) & (dist <= window) & (dist % dil == 0)
            tabs.append(np.where(ok, -slopes * dist.astype(np.float32), np.float32(MASKED))[0])
        return np.stack(tabs).astype(np.float32)

    pos = np.arange(win)[None, None, None, :]
    col = np.arange(NEW_KEY_PAD)[None, None, None, :]
    return table(win + t - pos, np.bool_(True)), table(t - col, col < steps)


def _attn_sample_kernel(q_ref, kn_ref, vn_ref, kt_ref, vt_ref, bo_ref, bn_ref, g_ref, o_ref):
    nb = bo_ref.shape[0]
    steps = q_ref.shape[1]
    nt = lambda p, v: lax.dot_general(p, v, (((1,), (1,)), ((), ())), preferred_element_type=F32)
    res = []
    sq = jnp.zeros((steps, 1), F32)
    for h in range(N_HEADS):
        q = q_ref[h]
        vt = vt_ref[h].astype(BF16)
        vn = vn_ref[h].astype(BF16)
        s_old = jnp.dot(q, kt_ref[h].astype(BF16), preferred_element_type=F32)
        s_new = jnp.dot(q, kn_ref[h].astype(BF16), preferred_element_type=F32)
        ms, ls, e_old, e_new = [], [], [], []
        for b in range(nb):
            so = s_old + bo_ref[b, h]
            sn = s_new + bn_ref[b, h]
            m = jnp.maximum(jnp.max(so, axis=1, keepdims=True), jnp.max(sn, axis=1, keepdims=True))
            eo = jnp.exp(so - m)
            en = jnp.exp(sn - m)
            ms.append(m)
            ls.append(jnp.sum(eo, axis=1, keepdims=True) + jnp.sum(en, axis=1, keepdims=True))
            e_old.append(eo)
            e_new.append(en)
        o_all = (nt(jnp.concatenate(e_old, axis=0).astype(BF16), vt)
                 + nt(jnp.concatenate(e_new, axis=0).astype(BF16), vn))
        top = functools.reduce(jnp.maximum, ms)
        num = jnp.zeros((steps, HEAD_DIM), F32)
        den = jnp.zeros((steps, 1), F32)
        for b in range(nb):
            w = jnp.exp(ms[b] - top)
            num = num + o_all[b * steps:(b + 1) * steps] * w
            den = den + ls[b] * w
        r = num * (1.0 / den)
        res.append(r)
        sq = sq + jnp.sum(r * r, axis=1, keepdims=True)
    inv = lax.rsqrt(sq * (1.0 / ATTN_WIDTH) + EPS)
    for h in range(N_HEADS):
        o_ref[h] = res[h] * inv * g_ref[h]


def _attn_sample(q, kn_t, vn_t, cache_kt, cache_vt, layer, g):
    b, heads, steps, dh = q.shape
    win = cache_kt.shape[-1]
    b_old, b_new = (jnp.asarray(z) for z in _sample_bias(win, steps))
    per_b = lambda *tail: pl.BlockSpec((None,) + tail, lambda i: (i,) + (0,) * len(tail))
    cache = pl.BlockSpec((None, None, heads, dh, win), lambda i: (layer, i, 0, 0, 0))
    return pl.pallas_call(
        _attn_sample_kernel,
        out_shape=jax.ShapeDtypeStruct((b, heads, steps, dh), F32),
        grid=(b,),
        in_specs=[per_b(heads, steps, dh), per_b(heads, dh, NEW_KEY_PAD), per_b(heads, dh, NEW_KEY_PAD),
                  cache, cache, _full(b_old.shape), _full(b_new.shape), _full((heads, 1, dh))],
        out_specs=per_b(heads, steps, dh),
        compiler_params=_cparams(),
        name="attn_sample",
    )(q, kn_t, vn_t, cache_kt, cache_vt, b_old, b_new, g.reshape(heads, 1, dh))


def _gelu_tanh(x):
    return 0.5 * x * (1.0 + jnp.tanh(np.sqrt(2.0 / np.pi).astype(np.float32) * (x + 0.044715 * (x * x * x))))


def _ssm_kernel(u_ref, bb_ref, lam_ref, pow_ref, cm_ref, dsk_ref, wglu_ref, bglu_ref, g_ref, pt_ref, h0_ref,
                z_ref, ht_ref, h_s, hin_s, carry_s, *, rows, steps, chain):
    ns = N_STATE
    cw = (4 * V7X_SUBLANES * V7X_LANES) // rows
    last = (steps - 1) * rows

    @pl.when(pl.program_id(0) == 0)
    def _():
        carry_s[...] = h0_ref[0:1, :]

    ub = u_ref[...].astype(BF16)
    n_slabs = SSM_WIDTH // V7X_LANES
    sw = ns // n_slabs
    for s in range(n_slabs):
        part = jnp.dot(ub[:, s * V7X_LANES:(s + 1) * V7X_LANES], bb_ref[s], preferred_element_type=F32)
        h_s[:, s * sw:(s + 1) * sw] = part[:, :sw]
        h_s[:, ns + s * sw:ns + (s + 1) * sw] = part[:, sw:]

    for cc in range(0, ns // cw, 2):
        crs = [slice(c * cw, (c + 1) * cw) for c in (cc, cc + 1)]
        cis = [slice(ns + c * cw, ns + (c + 1) * cw) for c in (cc, cc + 1)]
        lrs = [jnp.broadcast_to(lam_ref[0:1, cr], (rows, cw)) for cr in crs]
        lis = [jnp.broadcast_to(lam_ref[1:2, cr], (rows, cw)) for cr in crs]

        def scan_body(t, carry, crs=crs, cis=cis, lrs=lrs, lis=lis):
            rs = pl.ds(pl.multiple_of(t * rows, rows), rows)
            out = []
            for j in range(2):
                hr, hi = carry[2 * j], carry[2 * j + 1]
                nr = lrs[j] * hr - lis[j] * hi + h_s[rs, crs[j]]
                ni = lrs[j] * hi + lis[j] * hr + h_s[rs, cis[j]]
                h_s[rs, crs[j]] = nr
                h_s[rs, cis[j]] = ni
                out += [nr, ni]
            return tuple(out)

        zero = jnp.zeros((rows, cw), F32)
        lax.fori_loop(0, steps, scan_body, (zero,) * 4)

    if chain:
        ptr = pow_ref[steps - 1, 0:1, :ns]
        pti = pow_ref[steps - 1, 0:1, ns:]
        carry = carry_s[...]
        for c in range(rows):
            hin_s[c:c + 1, :] = carry
            cr_, ci_ = carry[:, :ns], carry[:, ns:]
            e = h_s[last + c:last + c + 1, :]
            carry = jnp.concatenate([ptr * cr_ - pti * ci_ + e[:, :ns],
                                     ptr * ci_ + pti * cr_ + e[:, ns:]], axis=1)
        carry_s[...] = carry
    else:
        hin_s[...] = h0_ref[...]

    for cc in range(ns // cw):
        cr = slice(cc * cw, (cc + 1) * cw)
        ci = slice(ns + cc * cw, ns + (cc + 1) * cw)
        hr0 = hin_s[:, cr]
        hi0 = hin_s[:, ci]

        def fix_body(t, _, cr=cr, ci=ci, hr0=hr0, hi0=hi0):
            rs = pl.ds(pl.multiple_of(t * rows, rows), rows)
            pr = pow_ref[t, :, cr]
            pi_ = pow_ref[t, :, ci]
            h_s[rs, cr] = h_s[rs, cr] + (pr * hr0 - pi_ * hi0)
            h_s[rs, ci] = h_s[rs, ci] + (pr * hi0 + pi_ * hr0)
            return 0

        lax.fori_loop(0, steps, fix_body, 0, unroll=8)

    ht_ref[...] = h_s[last:last + rows, :]
    ys = []
    for s in range(n_slabs):
        hs = jnp.concatenate([h_s[:, s * sw:(s + 1) * sw], h_s[:, ns + s * sw:ns + (s + 1) * sw]], axis=1)
        ys.append(jnp.dot(hs.astype(BF16), cm_ref[s], preferred_element_type=F32))
    y = jnp.concatenate(ys, axis=1) + dsk_ref[...] * u_ref[...]
    y = _gelu_tanh(y)
    gl = jnp.dot(y.astype(BF16), wglu_ref[...], preferred_element_type=F32) + bglu_ref[...]
    z = _rms(y * _sigmoid(gl), g_ref[...]).astype(BF16)
    z_ref[...] = jnp.dot(pt_ref[...], z, preferred_element_type=F32).astype(z_ref.dtype)


def _ssm_tables(ssm_a_re, ssm_a_im, ssm_log_dt, ssm_b_re, ssm_b_im, ssm_c_re, ssm_c_im, max_steps):
    g, n, ch = SSM_GROUPS, SSM_STATE, SSM_CH
    a_re = ssm_a_re.astype(F32)
    a_im = ssm_a_im.astype(F32)
    dt = jnp.exp(ssm_log_dt.astype(F32))[:, None]
    mag = jnp.exp(dt * a_re)
    lam_re = mag * jnp.cos(dt * a_im)
    lam_im = mag * jnp.sin(dt * a_im)
    nr = lam_re - 1.0
    ni = lam_im
    inv = 1.0 / (a_re * a_re + a_im * a_im)
    coef_re = (nr * a_re + ni * a_im) * inv
    coef_im = (ni * a_re - nr * a_im) * inv
    br = ssm_b_re.astype(F32)
    bi = ssm_b_im.astype(F32)
    bb_re = coef_re[..., None] * br - coef_im[..., None] * bi
    bb_im = coef_re[..., None] * bi + coef_im[..., None] * br
    gs = V7X_LANES // ch
    ns_ = g // gs
    eye = jnp.eye(gs, dtype=F32)
    bmat = lambda b: jnp.einsum("sgnc,gh->sgchn", b.reshape(ns_, gs, n, ch), eye).reshape(ns_, gs * ch, gs * n)
    cmat = lambda c: jnp.einsum("sgcn,gh->sgnhc", c.astype(F32).reshape(ns_, gs, ch, n), eye).reshape(
        ns_, gs * n, gs * ch)
    bb = jnp.concatenate([bmat(bb_re), bmat(bb_im)], axis=2).astype(BF16)
    cm = jnp.concatenate([cmat(ssm_c_re), -cmat(ssm_c_im)], axis=1).astype(BF16)
    lam = jnp.stack([lam_re.reshape(-1), lam_im.reshape(-1)])

    k = jnp.arange(1, max_steps + 1, dtype=F32)[:, None]
    kdt = k * dt.reshape(1, -1).repeat(n, axis=1)
    pmag = jnp.exp(kdt * a_re.reshape(1, -1))
    parg = kdt * a_im.reshape(1, -1)
    pows = jnp.concatenate([pmag * jnp.cos(parg), pmag * jnp.sin(parg)], axis=1)
    return bb, cm, lam, pows


def _ssm(u_perm, h0, tabs, dsk, w_glu, b_glu, g, perm_t, rows, steps, chain):
    n, w = u_perm.shape
    blk = rows * steps
    bb, cm, lam, pows = tabs
    row = pl.BlockSpec((blk, w), lambda i: (i, 0))
    kern = functools.partial(_ssm_kernel, rows=rows, steps=steps, chain=chain)
    return pl.pallas_call(
        kern,
        out_shape=(jax.ShapeDtypeStruct((n, w), BF16), jax.ShapeDtypeStruct((rows, 2 * N_STATE), F32)),
        grid=(n // blk,),
        in_specs=[row, _full(bb.shape), _full(lam.shape), _full((steps, rows, 2 * N_STATE)), _full(cm.shape),
                  _full((1, w)), _full((w, w)), _full((1, w)), _full((1, w)), _full((blk, blk)),
                  _full((rows, 2 * N_STATE))],
        out_specs=(row, _full((rows, 2 * N_STATE))),
        scratch_shapes=[pltpu.VMEM((blk, 2 * N_STATE), F32),
                        pltpu.VMEM((rows, 2 * N_STATE), F32),
                        pltpu.VMEM((1, 2 * N_STATE), F32)],
        compiler_params=_cparams(),
        name=f"ssm_r{rows}",
    )(u_perm, bb, lam, jnp.broadcast_to(pows[:steps, None, :], (steps, rows, 2 * N_STATE)), cm,
      dsk.reshape(1, w), w_glu.astype(BF16), b_glu.reshape(1, w),
      g.reshape(1, w), perm_t, h0)


def _outproj_kernel(*refs, n_real, aliased):
    ins, outs = refs[:12], refs[12 + aliased:]
    step = pl.program_id(0)

    @pl.when(step < n_real)
    def _():
        _outproj_tile(*ins, *outs)

    @pl.when(step >= n_real)
    def _():
        xs_ref = outs[1]
        xs_ref[...] = jnp.zeros(xs_ref.shape, xs_ref.dtype)


def _outproj_tile(x_ref, a_ref, z_ref, gt_ref, sh_ref, sc_ref, g_ref, wo_ref, wr_ref, br_ref, ltri_ref, utri_ref,
                  x1_ref, xs_ref, slot_ref, gates_ref, meta_ref, wbf_ref):
    @pl.when(pl.program_id(0) == 0)
    def _():
        wbf_ref[...] = wo_ref[...].astype(BF16)

    a = ATTN_WIDTH
    mixed = (jnp.dot(a_ref[...].astype(BF16), wbf_ref[:a, :], preferred_element_type=F32)
             + jnp.dot(z_ref[...], wbf_ref[a:, :], preferred_element_type=F32))
    x1 = x_ref[...] + gt_ref[...] * mixed
    x1_ref[...] = x1
    h2 = (_rms(x1, g_ref[...]) * (1.0 + sc_ref[...]) + sh_ref[...]).astype(BF16)
    lg = jnp.dot(h2, wr_ref[...], preferred_element_type=F32) + br_ref[...]
    tm = lg.shape[0]
    lane = lax.broadcasted_iota(jnp.int32, (tm, V7X_LANES), 1).astype(F32)
    vals, hots = [], []
    for _ in range(TOP_K):
        m = jnp.max(lg, axis=1, keepdims=True)
        idx = jnp.min(jnp.where(lg == m, lane, float(V7X_LANES)), axis=1, keepdims=True)
        hot = lane == idx
        vals.append(m)
        hots.append(jnp.where(hot, 1.0, 0.0))
        lg = jnp.where(hot, MASKED * 2, lg)
    es = [jnp.exp(v - vals[0]) for v in vals]
    inv = 1.0 / (es[0] + es[1] + es[2] + es[3])

    member = hots[0] + hots[1] + hots[2] + hots[3]
    before = jnp.dot(ltri_ref[...], member.astype(BF16), preferred_element_type=F32)
    count = jnp.sum(member, axis=0, keepdims=True)
    padded = jnp.floor((count + (ROW_ALIGN - 1.0)) * (1.0 / ROW_ALIGN)) * ROW_ALIGN
    padded8 = jnp.broadcast_to(padded, (V7X_SUBLANES, V7X_LANES))
    start = jnp.dot(padded8.astype(BF16), utri_ref[...], preferred_element_type=F32)[0:1]
    where_to = start + before
    slots = jnp.zeros((tm, V7X_LANES), F32)
    gates = jnp.zeros((tm, V7X_LANES), F32)
    for k in range(TOP_K):
        slot_k = jnp.sum(hots[k] * where_to, axis=1, keepdims=True)
        slots = jnp.where(lane == float(k), slot_k, slots)
        gates = jnp.where(lane == float(k), es[k] * inv, gates)
    slot_ref[...] = slots
    gates_ref[...] = gates
    row = lax.broadcasted_iota(jnp.int32, (V7X_SUBLANES, V7X_LANES), 0)
    meta_ref[...] = jnp.where(row == 0, padded8, jnp.where(row == 1, jnp.broadcast_to(start, padded8.shape), 0.0))

    cap = xs_ref.shape[0]
    slots_t = jnp.transpose(slots).astype(jnp.int32).astype(jnp.int16)
    srow = lax.broadcasted_iota(jnp.int16, (cap, tm), 0)
    place = jnp.zeros((cap, tm), BF16)
    for k in range(TOP_K):
        place = jnp.where(srow == slots_t[k:k + 1, :], jnp.ones((), BF16), place)
    xs_ref[...] = jnp.dot(place, h2, preferred_element_type=F32).astype(BF16)


def _outproj(x, a, z, gt, sh, sc, g, w_out, wr_pad, br_pad, tm, n_blocks, block0, xs_prev=None):
    n, d = x.shape
    aw = ATTN_WIDTH
    n_real = n // tm
    aliased = xs_prev is not None
    n_steps = n_real if aliased else n_blocks
    tile = lambda i: jnp.minimum(i, n_real - 1)
    mod_spec = (pl.BlockSpec((1, d), lambda i: (0, 0)) if gt.shape[0] == 1
                else pl.BlockSpec((tm, d), lambda i: (tile(i), 0)))
    row = lambda w: pl.BlockSpec((tm, w), lambda i: (tile(i), 0))
    ltri = jnp.asarray(np.tril(np.ones((tm, tm), np.float32), -1), BF16)
    utri = jnp.asarray(np.triu(np.ones((V7X_LANES, V7X_LANES), np.float32), 1), BF16)
    in_specs = [row(d), row(aw), row(SSM_WIDTH), mod_spec, mod_spec, mod_spec, _full((1, d)),
                _full(w_out.shape), _full(wr_pad.shape), _full(br_pad.shape), _full(ltri.shape), _full(utri.shape)]
    args = [x, a, z, gt, sh, sc, g.reshape(1, d), w_out, wr_pad, br_pad, ltri, utri]
    if aliased:
        in_specs.append(pl.BlockSpec(memory_space=pl.ANY))
        args.append(xs_prev)
    return pl.pallas_call(
        functools.partial(_outproj_kernel, n_real=n_real, aliased=int(aliased)),
        out_shape=(jax.ShapeDtypeStruct((n, d), F32),
                   jax.ShapeDtypeStruct((n_blocks, GROUP_CAP, d), BF16),
                   jax.ShapeDtypeStruct((n, V7X_LANES), F32), jax.ShapeDtypeStruct((n, V7X_LANES), F32),
                   jax.ShapeDtypeStruct((n_real, V7X_SUBLANES, V7X_LANES), F32)),
        grid=(n_steps,),
        in_specs=in_specs,
        out_specs=(row(d), pl.BlockSpec((None, GROUP_CAP, d), lambda i: (block0 + i, 0, 0)),
                   row(V7X_LANES), row(V7X_LANES),
                   pl.BlockSpec((None, V7X_SUBLANES, V7X_LANES), lambda i: (tile(i), 0, 0))),
        scratch_shapes=[pltpu.VMEM(w_out.shape, BF16)],
        input_output_aliases={len(args) - 1: 1} if aliased else {},
        compiler_params=_cparams(),
        name="outproj_router",
    )(*args)


_PIECE_SIZES = tuple(MOE_TILE >> s for s in range(6))
_CHUNK_SHIFT = 6
_CHUNK = 1 << _CHUNK_SHIFT
_TAIL_SIZES = tuple(sz for sz in _PIECE_SIZES if sz < _CHUNK)


def _moe_kernel(te_ref, tl_ref, lo_ref, hi_ref, nu_ref, gstart_ref, gsize_ref, gbase_ref, rows_ref, used_ref,
                nxt_ref, xs_hbm, wgu_hbm, bgu_ref, wd_hbm, bd_ref, ys_hbm,
                xbuf, ybuf, zbuf, wgu_f32, wd_f32, wgu_bf, wd_bf, xsem, ysem, zsem, wsem):
    t = pl.program_id(0)
    n_used = nu_ref[0]
    n_blocks, cap = xs_hbm.shape[0], xs_hbm.shape[1]

    def weight_copies(e):
        return (pltpu.make_async_copy(wgu_hbm.at[e], wgu_f32, wsem.at[0]),
                pltpu.make_async_copy(wd_hbm.at[e], wd_f32, wsem.at[1]))

    def x_copy(i, br, tr, sz, slot):
        pltpu.make_async_copy(xs_hbm.at[i, pl.ds(br, sz)], xbuf.at[slot, pl.ds(tr, sz)], xsem.at[slot]).start()

    def y_copy(i, br, tr, sz, slot):
        pltpu.make_async_copy(ybuf.at[slot, pl.ds(tr, sz)], ys_hbm.at[i, pl.ds(br, sz)], ysem.at[slot]).start()

    def z_copy(i, row, sz, start):
        cp = pltpu.make_async_copy(zbuf.at[pl.ds(0, sz)], ys_hbm.at[i, pl.ds(row, sz)], zsem)
        cp.start() if start else cp.wait()

    def pieces(tt, fn):
        e = te_ref[tt]
        lo = tl_ref[tt] * MOE_TILE

        def per_block(i, c):
            g = e * n_blocks + i
            s0 = gbase_ref[g]
            a = jnp.maximum(s0, lo)
            b = jnp.minimum(s0 + gsize_ref[g], lo + MOE_TILE)
            length = jnp.maximum(b - a, 0)
            src = gstart_ref[g] + (a - s0)
            dst = a - lo
            whole = lax.shift_right_logical(length, _CHUNK_SHIFT)

            def chunk(j, cc):
                off = j * _CHUNK
                fn(i, pl.multiple_of(src + off, ROW_ALIGN), pl.multiple_of(dst + off, ROW_ALIGN), _CHUNK)
                return cc

            lax.fori_loop(0, whole, chunk, 0)
            done = whole * _CHUNK
            for sz in _TAIL_SIZES:
                hit = (length & sz) != 0

                @pl.when(hit)
                def _(sz=sz, done=done):
                    fn(i, pl.multiple_of(src + done, ROW_ALIGN), pl.multiple_of(dst + done, ROW_ALIGN), sz)

                done = done + jnp.where(hit, sz, 0)
            return c

        lax.fori_loop(lo_ref[tt], hi_ref[tt], per_block, 0)

    def tile_rows(tt):
        return jnp.minimum(rows_ref[te_ref[tt]] - tl_ref[tt] * MOE_TILE, MOE_TILE)

    def wait_rows(n, sem, buf):
        def chunk(j, cc):
            pltpu.make_async_copy(buf.at[pl.ds(0, _CHUNK)], buf.at[pl.ds(0, _CHUNK)], sem).wait()
            return cc

        lax.fori_loop(0, lax.shift_right_logical(n, _CHUNK_SHIFT), chunk, 0)
        for sz in _TAIL_SIZES:
            @pl.when((n & sz) != 0)
            def _(sz=sz):
                pltpu.make_async_copy(buf.at[pl.ds(0, sz)], buf.at[pl.ds(0, sz)], sem).wait()

    def fetch(tt, slot):
        pieces(tt, lambda i, br, tr, sz: x_copy(i, br, tr, sz, slot))

    def writeback(tt, slot):
        pieces(tt, lambda i, br, tr, sz: y_copy(i, br, tr, sz, slot))

    def zero_tail(i, start):
        u = used_ref[i]
        rem = cap - u
        nz = zbuf.shape[0]
        whole = lax.shift_right_logical(rem, nz.bit_length() - 1)

        def chunk(j, c):
            z_copy(i, pl.multiple_of(u + j * nz, ROW_ALIGN), nz, start)
            return c

        lax.fori_loop(0, whole, chunk, 0)
        base = u + whole * nz
        done = jnp.int32(0)
        for sz in _PIECE_SIZES:
            if sz >= nz:
                continue
            hit = (rem & sz) != 0

            @pl.when(hit)
            def _(sz=sz, done=done):
                z_copy(i, pl.multiple_of(base + done, ROW_ALIGN), sz, start)

            done = done + jnp.where(hit, sz, 0)

    @pl.when(t == 0)
    def _():
        xbuf[...] = jnp.zeros(xbuf.shape, xbuf.dtype)
        zbuf[...] = jnp.zeros(zbuf.shape, zbuf.dtype)
        lax.fori_loop(0, n_blocks, lambda i, c: (zero_tail(i, True), c)[1], 0)
        lax.fori_loop(0, n_blocks, lambda i, c: (zero_tail(i, False), c)[1], 0)
        fetch(0, 0)
        for cp in weight_copies(te_ref[0]):
            cp.start()

    @pl.when(t < n_used)
    def _():
        slot = t % 2

        @pl.when(t + 1 < n_used)
        def _():
            fetch(t + 1, 1 - slot)

        @pl.when(tl_ref[t] == 0)
        def _():
            for cp in weight_copies(te_ref[t]):
                cp.wait()
            wgu_bf[...] = wgu_f32[...].astype(BF16)
            wd_bf[...] = wd_f32[...].astype(BF16)
            nxt = nxt_ref[te_ref[t]]

            @pl.when(nxt >= 0)
            def _():
                for cp in weight_copies(nxt):
                    cp.start()

        wait_rows(tile_rows(t), xsem.at[slot], xbuf.at[slot])

        @pl.when(t >= 2)
        def _():
            wait_rows(tile_rows(t - 2), ysem.at[slot], ybuf.at[slot])

        def expert_mlp(m):
            f = D_EXPERT
            gu = jnp.dot(xbuf[slot, pl.ds(0, m)], wgu_bf[...], preferred_element_type=F32) + bgu_ref[...]
            gate = jnp.minimum(gu[:, :f], SWIGLU_LIMIT)
            up = jnp.clip(gu[:, f:], -SWIGLU_LIMIT, SWIGLU_LIMIT)
            act = (up + 1.0) * gate * _sigmoid(SWIGLU_ALPHA * gate)
            y = jnp.dot(act.astype(BF16), wd_bf[...], preferred_element_type=F32) + bd_ref[...]
            ybuf[slot, pl.ds(0, m)] = y.astype(BF16)

        quarter = MOE_TILE // 4
        n_quarters = lax.shift_right_logical(tile_rows(t) + (quarter - 1), quarter.bit_length() - 1)
        for nq in range(1, 5):
            @pl.when(n_quarters == nq)
            def _(nq=nq):
                expert_mlp(nq * quarter)

        writeback(t, slot)

        @pl.when(t == n_used - 1)
        def _():
            wait_rows(tile_rows(t), ysem.at[slot], ybuf.at[slot])

            @pl.when(t >= 1)
            def _():
                wait_rows(tile_rows(t - 1), ysem.at[1 - slot], ybuf.at[1 - slot])


def _moe(plan, xs, w_gate_up, b_gate_up, w_down, b_down):
    _, cap, d = xs.shape
    e, _, f2 = w_gate_up.shape
    nt = plan[0].shape[0]
    wmap = lambda t, te, *_: (te[t], 0, 0)
    anyspec = pl.BlockSpec(memory_space=pl.ANY)
    grid_spec = pltpu.PrefetchScalarGridSpec(
        num_scalar_prefetch=len(plan),
        grid=(nt,),
        in_specs=[anyspec,
                  anyspec,
                  pl.BlockSpec((None, 1, f2), wmap),
                  anyspec,
                  pl.BlockSpec((None, 1, d), wmap)],
        out_specs=anyspec,
        scratch_shapes=[pltpu.VMEM((2, MOE_TILE, d), BF16), pltpu.VMEM((2, MOE_TILE, d), BF16),
                        pltpu.VMEM((MOE_TILE // 2, d), BF16),
                        pltpu.VMEM((d, f2), F32), pltpu.VMEM((f2 // 2, d), F32),
                        pltpu.VMEM((d, f2), BF16), pltpu.VMEM((f2 // 2, d), BF16),
                        pltpu.SemaphoreType.DMA((2,)), pltpu.SemaphoreType.DMA((2,)), pltpu.SemaphoreType.DMA(()),
                        pltpu.SemaphoreType.DMA((2,))],
    )
    return pl.pallas_call(
        _moe_kernel,
        out_shape=jax.ShapeDtypeStruct(xs.shape, BF16),
        grid_spec=grid_spec,
        compiler_params=_cparams(),
        name="moe_experts",
    )(*plan, xs, w_gate_up, b_gate_up.reshape(e, 1, f2), w_down, b_down.reshape(e, 1, d))


def _moe_plan(group_size, group_start):
    n_blocks = group_size.shape[0]
    gsize = group_size.T.astype(jnp.int32)
    gstart = group_start.T.astype(jnp.int32)
    gbase = jnp.cumsum(gsize, axis=1) - gsize
    rows = jnp.sum(gsize, axis=1)
    tiles = (rows + MOE_TILE - 1) // MOE_TILE
    tile_end = jnp.cumsum(tiles)
    n_used = tile_end[-1:]
    nt = (n_blocks * GROUP_CAP) // MOE_TILE + N_EXPERTS
    t = jnp.arange(nt, dtype=jnp.int32)
    te = jnp.sum((tile_end[None, :] <= t[:, None]).astype(jnp.int32), axis=1)
    last = jnp.max(jnp.where(tiles > 0, jnp.arange(N_EXPERTS, dtype=jnp.int32), 0))
    te = jnp.where(t < n_used[0], jnp.minimum(te, N_EXPERTS - 1), last)
    hot = (te[:, None] == jnp.arange(N_EXPERTS, dtype=jnp.int32)[None, :]).astype(jnp.int32)
    tl = jnp.where(t < n_used[0], t - hot @ (tile_end - tiles), 0)
    lo_row = tl * MOE_TILE
    base_t = hot @ gbase
    size_t = hot @ gsize
    first = jnp.sum((base_t + size_t <= lo_row[:, None]).astype(jnp.int32), axis=1)
    stop = jnp.sum((base_t < lo_row[:, None] + MOE_TILE).astype(jnp.int32), axis=1)
    used = jnp.sum(gsize, axis=0)
    ids = jnp.arange(N_EXPERTS, dtype=jnp.int32)
    later = (ids[None, :] > ids[:, None]) & (tiles[None, :] > 0)
    nxt = jnp.min(jnp.where(later, ids[None, :], N_EXPERTS), axis=1)
    nxt = jnp.where(nxt < N_EXPERTS, nxt, -1)
    i32 = lambda z: z.astype(jnp.int32)
    return (i32(te), i32(tl), i32(first), i32(stop), i32(n_used), i32(gstart.reshape(-1)), i32(gsize.reshape(-1)),
            i32(gbase.reshape(-1)), i32(rows), i32(used), i32(nxt))


def _final_kernel(x_ref, ys_ref, slot_ref, gates_ref, gt_ref, g_ref, y_ref):
    tm = x_ref.shape[0]
    cap = ys_ref.shape[0]
    col = lax.broadcasted_iota(jnp.int16, (tm, cap), 1)
    slots = slot_ref[...].astype(jnp.int32).astype(jnp.int16)
    gates = gates_ref[...].astype(BF16)
    mix = jnp.zeros((tm, cap), BF16)
    for k in range(TOP_K):
        mix = jnp.where(col == slots[:, k:k + 1], gates[:, k:k + 1], mix)
    ff = jnp.dot(mix, ys_ref[...], preferred_element_type=F32)
    y_ref[...] = _rms(x_ref[...] + gt_ref[...] * ff, g_ref[...])


def _final(x1, ys, slots, gates, gt, g, tm, block0):
    n, d = x1.shape
    cap = ys.shape[1]
    mod_spec = (pl.BlockSpec((1, d), lambda i: (0, 0)) if gt.shape[0] == 1
                else pl.BlockSpec((tm, d), lambda i: (i, 0)))
    row = lambda w: pl.BlockSpec((tm, w), lambda i: (i, 0))
    return pl.pallas_call(
        _final_kernel,
        out_shape=jax.ShapeDtypeStruct((n, d), F32),
        grid=(n // tm,),
        in_specs=[row(d), pl.BlockSpec((None, cap, d), lambda i: (block0 + i, 0, 0)),
                  row(V7X_LANES), row(V7X_LANES), mod_spec, _full((1, d))],
        out_specs=row(d),
        compiler_params=_cparams(),
        name="final_norm",
    )(x1, ys, slots, gates, gt, g.reshape(1, d))


def kernel(x_prompt, x_sample, cache_k_win, cache_v_win, state_ssm_re, state_ssm_im, c_prompt, c_sample,
           w_ada, b_ada, g_norm1, w_in, ssm_a_re, ssm_a_im, ssm_log_dt, ssm_b_re, ssm_b_im, ssm_c_re, ssm_c_im,
           ssm_d, w_glu, b_glu, g_out_attn, g_out_ssm, w_out, g_norm2, w_router, b_router, w_gate_up, b_gate_up,
           w_down, b_down, g_final):
    depth = w_ada.shape[0]
    assert depth == 1 and x_prompt.shape[0] == 1
    bp, t, d = x_prompt.shape
    bs, ts, _ = x_sample.shape
    ns = bs * ts
    l = 0
    a = ATTN_WIDTH

    n_c = bp + bs
    c_pad = -(-n_c // V7X_SUBLANES) * V7X_SUBLANES
    c_rows = jnp.concatenate([c_prompt, c_sample, jnp.zeros((c_pad - n_c, d), F32)], axis=0)
    mod = _ada_modulation(c_rows, w_ada[l], b_ada[l])
    mod_p = [mod[0:1, i * d:(i + 1) * d] for i in range(N_MOD)]
    mod_s = [jnp.repeat(mod[bp:bp + bs, i * d:(i + 1) * d], ts, axis=0) for i in range(N_MOD)]

    steps_p = TOKEN_TILE // SSM_ROWS_PROMPT
    perm_p = _chunk_perm(SSM_ROWS_PROMPT, steps_p)
    perm_s = _chunk_perm(bs, ts)
    tabs = _ssm_tables(ssm_a_re[l], ssm_a_im[l], ssm_log_dt[l], ssm_b_re[l], ssm_b_im[l],
                       ssm_c_re[l], ssm_c_im[l], max(steps_p, ts))
    wr_pad = jnp.zeros((d, V7X_LANES), F32).at[:, :N_EXPERTS].set(w_router[l]).astype(BF16)
    br_pad = jnp.full((1, V7X_LANES), MASKED, F32).at[0, :N_EXPERTS].set(b_router[l])

    xp = x_prompt.reshape(t, d)
    dils = tuple(dil for _, dil in DILATED_BRANCHES)
    wide = tuple(dil for dil in dils if dil > 1)
    proj_p = _inproj(xp, mod_p[0], mod_p[1], g_norm1[l], w_in[l], jnp.asarray(perm_p, BF16), TOKEN_TILE, wide)
    kpf, vpf, up = proj_p[3:6]
    views = {1: proj_p[0:3]}
    for n_d, dil in enumerate(wide):
        views[dil] = proj_p[6 + 3 * n_d:9 + 3 * n_d]
    outs = [_attn_branch(*views[dil], dil) for dil in dils]
    ap = _attn_combine([o for o, _ in outs], [s for _, s in outs], g_out_attn[l], dils)
    zeros_h = jnp.zeros((SSM_ROWS_PROMPT, 2 * N_STATE), F32)
    zp, hp = _ssm(up, zeros_h, tabs, ssm_d[l], w_glu[l], b_glu[l], g_out_ssm[l],
                  jnp.asarray(perm_p.T, BF16), SSM_ROWS_PROMPT, steps_p, True)
    n_blocks_p = t // TOKEN_TILE
    n_blocks = n_blocks_p + 1
    x1p, xs_all, slot_p, gate_p, meta_p = _outproj(xp, ap, zp, mod_p[2], mod_p[3], mod_p[4], g_norm2[l], w_out[l],
                                                   wr_pad, br_pad, TOKEN_TILE, n_blocks, 0)

    xs = x_sample.reshape(ns, d)
    qs, ks, vs, ksf, vsf, us = _inproj(xs, mod_s[0], mod_s[1], g_norm1[l], w_in[l], jnp.asarray(perm_s, BF16), ns)
    split = lambda z: z.reshape(bs, ts, N_HEADS, HEAD_DIM)
    new_t = lambda z: jnp.pad(split(z).transpose(0, 2, 3, 1), ((0, 0), (0, 0), (0, 0), (0, NEW_KEY_PAD - ts)))
    as_ = _attn_sample(split(qs).transpose(0, 2, 1, 3), new_t(ksf), new_t(vsf),
                       cache_k_win.transpose(0, 1, 3, 4, 2), cache_v_win.transpose(0, 1, 3, 4, 2), l, g_out_attn[l])
    as_ = as_.transpose(0, 2, 1, 3)
    h0s = jnp.concatenate([state_ssm_re[l].reshape(bs, N_STATE), state_ssm_im[l].reshape(bs, N_STATE)], axis=1)
    zs, hs = _ssm(us, h0s, tabs, ssm_d[l], w_glu[l], b_glu[l], g_out_ssm[l],
                  jnp.asarray(perm_s.T, BF16), bs, ts, False)
    x1s, xs_all, slot_s, gate_s, meta_s = _outproj(xs, as_.reshape(ns, a), zs, mod_s[2], mod_s[3], mod_s[4],
                                                   g_norm2[l], w_out[l], wr_pad, br_pad, ns, n_blocks, n_blocks_p,
                                                   xs_prev=xs_all)

    meta = jnp.concatenate([meta_p, meta_s], axis=0)
    plan = _moe_plan(meta[:, 0, :N_EXPERTS], meta[:, 1, :N_EXPERTS])
    ys_all = _moe(plan, xs_all, w_gate_up[l], b_gate_up[l], w_down[l], b_down[l])

    y_prompt = _final(x1p, ys_all, slot_p, gate_p, mod_p[5], g_final, TOKEN_TILE, 0).reshape(bp, t, d)
    y_sample = _final(x1s, ys_all, slot_s, gate_s, mod_s[5], g_final, ns, n_blocks_p).reshape(bs, ts, d)

    keep = min(MAX_WINDOW, t)
    shp = (1, bp, keep, N_HEADS, HEAD_DIM)
    k_win = kpf[t - keep:].reshape(shp)
    v_win = vpf[t - keep:].reshape(shp)
    st = (1, bp, SSM_GROUPS, SSM_STATE)
    hp_last = hp[SSM_ROWS_PROMPT - 1]
    ss = (1, bs, SSM_GROUPS, SSM_STATE)
    return (y_prompt, y_sample, k_win, v_win,
            hp_last[:N_STATE].reshape(st), hp_last[N_STATE:].reshape(st),
            ksf.reshape(1, bs, ts, N_HEADS, HEAD_DIM), vsf.reshape(1, bs, ts, N_HEADS, HEAD_DIM),
            hs[:, :N_STATE].reshape(ss), hs[:, N_STATE:].reshape(ss))
```

```python
import functools

import numpy as np
import jax
import jax.numpy as jnp
from jax import lax
from jax.experimental import pallas as pl
from jax.experimental.pallas import tpu as pltpu

F32 = jnp.float32
BF16 = jnp.bfloat16

D_MODEL = 1024
N_HEADS = 8
HEAD_DIM = 64
ATTN_WIDTH = N_HEADS * HEAD_DIM
DILATED_BRANCHES = ((128, 1), (512, 4), (2048, 16))
KEYS_PER_BRANCH = 129
MAX_WINDOW = 2048
SSM_WIDTH = D_MODEL - ATTN_WIDTH
SSM_CH = 16
SSM_GROUPS = SSM_WIDTH // SSM_CH
SSM_STATE = 64
N_STATE = SSM_GROUPS * SSM_STATE
N_EXPERTS = 32
TOP_K = 4
D_EXPERT = D_MODEL
SWIGLU_LIMIT = 7.0
SWIGLU_ALPHA = 1.702
N_MOD = 6
EPS = 1e-6
MASKED = -1e30

V7X_LANES = 128
V7X_SUBLANES = 8
V7X_VMEM_LIMIT_BYTES = 56 * 1024 * 1024

TOKEN_TILE = 512
Q_TILE = 128
Q_BLOCKS_PER_STEP = 8
SSM_ROWS_PROMPT = 8
MOE_TILE = 1024
MOE_ROW_STEP = 128
ROW_ALIGN = 16
GROUP_CAP = -(-(TOKEN_TILE * TOP_K + N_EXPERTS * (ROW_ALIGN - 1)) // (2 * V7X_LANES)) * (2 * V7X_LANES)


def _cparams(n_axes=1):
    return pltpu.CompilerParams(
        dimension_semantics=("arbitrary",) * n_axes,
        vmem_limit_bytes=V7X_VMEM_LIMIT_BYTES,
    )


def _full(shape):
    n = len(shape)
    return pl.BlockSpec(shape, lambda *_: (0,) * n)


def _rms(x, g):
    return x * lax.rsqrt(jnp.mean(x * x, axis=-1, keepdims=True) + EPS) * g


def _sigmoid(x):
    return 1.0 / (1.0 + jnp.exp(-x))


def _ada_kernel(c_ref, w_ref, b_ref, o_ref):
    c = c_ref[...]
    s = (c * _sigmoid(c)).astype(BF16)
    o_ref[...] = jnp.dot(s, w_ref[...].astype(BF16), preferred_element_type=F32) + b_ref[...]


def _ada_modulation(c_rows, w_ada, b_ada):
    m, d = c_rows.shape
    n = w_ada.shape[1]
    tn = n // 4
    return pl.pallas_call(
        _ada_kernel,
        out_shape=jax.ShapeDtypeStruct((m, n), F32),
        grid=(n // tn,),
        in_specs=[_full((m, d)),
                  pl.BlockSpec((d, tn), lambda j: (0, j)),
                  pl.BlockSpec((1, tn), lambda j: (0, j))],
        out_specs=pl.BlockSpec((m, tn), lambda j: (0, j)),
        compiler_params=_cparams(),
        name="ada_modulation",
    )(c_rows, w_ada, b_ada.reshape(1, n))


def _inproj_kernel(*refs, dils):
    x_ref, sh_ref, sc_ref, g_ref, w_ref, perm_ref = refs[:6]
    dperm_refs = refs[6:6 + len(dils)]
    q_ref, k_ref, v_ref, kf_ref, vf_ref, u_ref = refs[6 + len(dils):12 + len(dils)]
    dil_refs = refs[12 + len(dils):-1]
    wbf_ref = refs[-1]

    @pl.when(pl.program_id(0) == 0)
    def _():
        wbf_ref[...] = w_ref[...].astype(BF16)

    h = _rms(x_ref[...], g_ref[...]) * (1.0 + sc_ref[...]) + sh_ref[...]
    hb = h.astype(BF16)
    a = ATTN_WIDTH
    proj = jnp.dot(hb, wbf_ref[:, :3 * a], preferred_element_type=F32)
    k = proj[:, a:2 * a]
    v = proj[:, 2 * a:]
    qkv = jnp.concatenate([(proj[:, :a] * (HEAD_DIM ** -0.5)).astype(BF16), k.astype(BF16), v.astype(BF16)],
                          axis=1)
    q_ref[...] = qkv[:, :a]
    k_ref[...] = qkv[:, a:2 * a]
    v_ref[...] = qkv[:, 2 * a:]
    kf_ref[...] = k
    vf_ref[...] = v
    hp = jnp.dot(perm_ref[...], hb, preferred_element_type=F32).astype(BF16)
    u_ref[...] = jnp.dot(hp, wbf_ref[:, 3 * a:], preferred_element_type=F32)
    tm = qkv.shape[0]
    for n_d, dil in enumerate(dils):
        by_residue = jnp.dot(dperm_refs[n_d][...], qkv, preferred_element_type=F32).astype(BF16)
        per = tm // dil
        for r in range(dil):
            rows = by_residue[r * per:(r + 1) * per]
            for j in range(3):
                dil_refs[3 * n_d + j][:, r * a:(r + 1) * a] = rows[:, j * a:(j + 1) * a]


def _inproj(x, sh, sc, g, w_in, perm, tm, dils=()):
    n, d = x.shape
    a = ATTN_WIDTH
    mod_rows = sh.shape[0]
    mod_spec = (pl.BlockSpec((1, d), lambda i: (0, 0)) if mod_rows == 1
                else pl.BlockSpec((tm, d), lambda i: (i, 0)))
    row = lambda w: pl.BlockSpec((tm, w), lambda i: (i, 0))
    dperms = [jnp.asarray(_chunk_perm(tm // dil, dil), BF16) for dil in dils]
    view_shapes = tuple(jax.ShapeDtypeStruct((n // dil, dil * a), BF16) for dil in dils for _ in range(3))
    view_specs = tuple(pl.BlockSpec((tm // dil, dil * a), lambda i: (i, 0)) for dil in dils for _ in range(3))
    return pl.pallas_call(
        functools.partial(_inproj_kernel, dils=tuple(dils)),
        out_shape=(jax.ShapeDtypeStruct((n, a), BF16),) * 3
        + (jax.ShapeDtypeStruct((n, a), F32),) * 2
        + (jax.ShapeDtypeStruct((n, SSM_WIDTH), F32),) + view_shapes,
        grid=(n // tm,),
        in_specs=[row(d), mod_spec, mod_spec, _full((1, d)), _full(w_in.shape), _full((tm, tm))]
        + [_full((tm, tm))] * len(dils),
        out_specs=(row(a),) * 5 + (row(SSM_WIDTH),) + view_specs,
        scratch_shapes=[pltpu.VMEM(w_in.shape, BF16)],
        compiler_params=_cparams(),
        name="inproj",
    )(x, sh, sc, g.reshape(1, d), w_in, perm, *dperms)


def _chunk_perm(rows, steps):
    n = rows * steps
    p = np.zeros((n, n), np.float32)
    c, t = np.meshgrid(np.arange(rows), np.arange(steps), indexing="ij")
    p[(t * rows + c).ravel(), (c * steps + t).ravel()] = 1.0
    return p


def _alibi_slopes():
    return np.exp2(-8.0 * np.arange(1, N_HEADS + 1, dtype=np.float64) / N_HEADS).astype(np.float32)


def _branch_bias(dil):
    qi = np.arange(Q_TILE)[:, None]
    col = np.arange(2 * Q_TILE)[None, :]
    j = Q_TILE + qi - col
    valid = (j >= 0) & (j <= Q_TILE)
    dist = (j * dil).astype(np.float32)
    tabs = []
    for first in (True, False):
        ok = valid & (col >= Q_TILE) if first else valid
        per_head = [np.where(ok, -s * dist, np.float32(MASKED)) for s in _alibi_slopes()]
        tabs.append(np.concatenate(per_head, axis=0))
    return np.stack(tabs).astype(np.float32)


def _attn_branch_kernel(q_ref, kp_ref, kc_ref, vp_ref, vc_ref, bias_ref, o_ref, lse_ref):
    first_step = pl.program_id(0) == 0
    lane = lax.broadcasted_iota(jnp.int32, (Q_TILE, V7X_LANES), 1)
    lo = lane < HEAD_DIM
    for j in range(Q_BLOCKS_PER_STEP):
        rows = slice(j * Q_TILE, (j + 1) * Q_TILE)
        before = slice((j - 1) * Q_TILE, j * Q_TILE)
        sel = jnp.where(first_step, 0, 1) if j == 0 else 1
        lse_acc = jnp.zeros((Q_TILE, V7X_LANES), F32)
        for p in range(N_HEADS // 2):
            cs = slice(V7X_LANES * p, V7X_LANES * (p + 1))
            q2 = q_ref[rows, cs]
            zero = jnp.zeros_like(q2)
            qq = jnp.concatenate([jnp.where(lo, q2, zero), jnp.where(lo, zero, q2)], axis=0)
            k_before = kp_ref[:, cs] if j == 0 else kc_ref[before, cs]
            v_before = vp_ref[:, cs] if j == 0 else vc_ref[before, cs]
            kk = jnp.concatenate([k_before, kc_ref[rows, cs]], axis=0)
            vv = jnp.concatenate([v_before, vc_ref[rows, cs]], axis=0)
            s = lax.dot_general(qq, kk, (((1,), (1,)), ((), ())), preferred_element_type=F32)
            s = s + bias_ref[sel, 2 * Q_TILE * p:2 * Q_TILE * (p + 1), :]
            m = jnp.max(s, axis=1, keepdims=True)
            e = jnp.exp(s - m)
            l = jnp.sum(e, axis=1, keepdims=True)
            eb = e.astype(BF16)
            o0 = jnp.dot(eb[:Q_TILE], vv, preferred_element_type=F32) * (1.0 / l[:Q_TILE])
            o1 = jnp.dot(eb[Q_TILE:], vv, preferred_element_type=F32) * (1.0 / l[Q_TILE:])
            o_ref[rows, cs] = jnp.where(lo, o0, o1).astype(o_ref.dtype)
            lse = m + jnp.log(l)
            lse_acc = jnp.where(lane == 2 * p, lse[:Q_TILE], lse_acc)
            lse_acc = jnp.where(lane == 2 * p + 1, lse[Q_TILE:], lse_acc)
        lse_ref[rows, :] = lse_acc[:, :N_HEADS]


def _attn_branch(qv, kv, vv, dil):
    a = ATTN_WIDTH
    rows = qv.shape[0]
    t = rows * dil
    step = Q_BLOCKS_PER_STEP * Q_TILE
    cur = pl.BlockSpec((step, a), lambda i, r: (i, r))
    prev = pl.BlockSpec((Q_TILE, a), lambda i, r: (jnp.maximum(i * Q_BLOCKS_PER_STEP - 1, 0), r))
    bias = jnp.asarray(_branch_bias(dil))
    o, lse = pl.pallas_call(
        _attn_branch_kernel,
        out_shape=(jax.ShapeDtypeStruct((rows, dil * a), BF16),
                   jax.ShapeDtypeStruct((dil, rows, N_HEADS), F32)),
        grid=(rows // step, dil),
        in_specs=[cur, prev, cur, prev, cur, _full(bias.shape)],
        out_specs=(cur, pl.BlockSpec((None, step, N_HEADS), lambda i, r: (r, i, 0))),
        compiler_params=_cparams(2),
        name=f"attn_branch_d{dil}",
    )(qv, kv, kv, vv, vv, bias)
    return o, lse.transpose(1, 0, 2).reshape(t, N_HEADS)


def _attn_combine_kernel(*refs, dils):
    nb = len(dils)
    o_refs = refs[:nb]
    l_refs = refs[nb:2 * nb]
    g_ref = refs[2 * nb]
    unperm_refs = refs[2 * nb + 1:-1]
    a_ref = refs[-1]
    tq, a = a_ref.shape
    outs = []
    n_u = 0
    for o_ref, dil in zip(o_refs, dils):
        if dil == 1:
            outs.append(o_ref[...].astype(F32))
            continue
        by_residue = jnp.concatenate([o_ref[:, r * a:(r + 1) * a] for r in range(dil)], axis=0)
        outs.append(jnp.dot(unperm_refs[n_u][...], by_residue, preferred_element_type=F32))
        n_u += 1
    ls = [l_ref[...] for l_ref in l_refs]
    top = functools.reduce(jnp.maximum, ls)
    ws = [jnp.exp(l - top) for l in ls]
    inv = 1.0 / functools.reduce(jnp.add, ws)
    cs = [w * inv for w in ws]
    lane = lax.broadcasted_iota(jnp.int32, (tq, V7X_LANES), 1)
    lo = lane < HEAD_DIM
    cols = []
    for p in range(N_HEADS // 2):
        sl = slice(V7X_LANES * p, V7X_LANES * (p + 1))
        acc = jnp.zeros((tq, V7X_LANES), F32)
        for c, o in zip(cs, outs):
            cexp = jnp.where(lo,
                             jnp.broadcast_to(c[:, 2 * p:2 * p + 1], (tq, V7X_LANES)),
                             jnp.broadcast_to(c[:, 2 * p + 1:2 * p + 2], (tq, V7X_LANES)))
            acc = acc + cexp * o[:, sl]
        cols.append(acc)
    o = jnp.concatenate(cols, axis=1)
    a_ref[...] = _rms(o, g_ref[...]).astype(a_ref.dtype)


def _attn_combine(os_, lses, g, dils):
    t = lses[0].shape[0]
    a = ATTN_WIDTH
    tq = TOKEN_TILE
    unperms = [jnp.asarray(_chunk_perm(tq // dil, dil).T, BF16) for dil in dils if dil > 1]
    return pl.pallas_call(
        functools.partial(_attn_combine_kernel, dils=tuple(dils)),
        out_shape=jax.ShapeDtypeStruct((t, a), BF16),
        grid=(t // tq,),
        in_specs=[pl.BlockSpec((tq // dil, dil * a), lambda i: (i, 0)) for dil in dils]
        + [pl.BlockSpec((tq, N_HEADS), lambda i: (i, 0))] * len(dils) + [_full((1, a))]
        + [_full((tq, tq))] * len(unperms),
        out_specs=pl.BlockSpec((tq, a), lambda i: (i, 0)),
        compiler_params=_cparams(),
        name="attn_combine",
    )(*os_, *lses, g.reshape(1, a), *unperms)


NEW_KEY_PAD = V7X_LANES


def _sample_bias(win, steps):
    slopes = _alibi_slopes()[None, :, None, None]
    t = np.arange(steps)[None, None, :, None]

    def table(dist, live):
        tabs = []
        for window, dil in DILATED_BRANCHES:
            ok = live & (dist >= 0) & (dist <= window) & (dist % dil == 0)
            tabs.append(np.where(ok, -slopes * dist.astype(np.float32), np.float32(MASKED))[0])
        return np.stack(tabs).astype(np.float32)

    pos = np.arange(win)[None, None, None, :]
    col = np.arange(NEW_KEY_PAD)[None, None, None, :]
    return table(win + t - pos, np.bool_(True)), table(t - col, col < steps)


def _attn_sample_kernel(q_ref, kn_ref, vn_ref, kt_ref, vt_ref, bo_ref, bn_ref, g_ref, o_ref):
    nb = bo_ref.shape[0]
    steps = q_ref.shape[1]
    nt = lambda p, v: lax.dot_general(p, v, (((1,), (1,)), ((), ())), preferred_element_type=F32)
    res = []
    sq = jnp.zeros((steps, 1), F32)
    for h in range(N_HEADS):
        q = q_ref[h]
        vt = vt_ref[h].astype(BF16)
        vn = vn_ref[h].astype(BF16)
        s_old = jnp.dot(q, kt_ref[h].astype(BF16), preferred_element_type=F32)
        s_new = jnp.dot(q, kn_ref[h].astype(BF16), preferred_element_type=F32)
        ms, ls, e_old, e_new = [], [], [], []
        for b in range(nb):
            so = s_old + bo_ref[b, h]
            sn = s_new + bn_ref[b, h]
            m = jnp.maximum(jnp.max(so, axis=1, keepdims=True), jnp.max(sn, axis=1, keepdims=True))
            eo = jnp.exp(so - m)
            en = jnp.exp(sn - m)
            ms.append(m)
            ls.append(jnp.sum(eo, axis=1, keepdims=True) + jnp.sum(en, axis=1, keepdims=True))
            e_old.append(eo)
            e_new.append(en)
        o_all = (nt(jnp.concatenate(e_old, axis=0).astype(BF16), vt)
                 + nt(jnp.concatenate(e_new, axis=0).astype(BF16), vn))
        top = functools.reduce(jnp.maximum, ms)
        num = jnp.zeros((steps, HEAD_DIM), F32)
        den = jnp.zeros((steps, 1), F32)
        for b in range(nb):
            w = jnp.exp(ms[b] - top)
            num = num + o_all[b * steps:(b + 1) * steps] * w
            den = den + ls[b] * w
        r = num * (1.0 / den)
        res.append(r)
        sq = sq + jnp.sum(r * r, axis=1, keepdims=True)
    inv = lax.rsqrt(sq * (1.0 / ATTN_WIDTH) + EPS)
    for h in range(N_HEADS):
        o_ref[h] = res[h] * inv * g_ref[h]


def _attn_sample(q, kn_t, vn_t, cache_kt, cache_vt, layer, g):
    b, heads, steps, dh = q.shape
    win = cache_kt.shape[-1]
    b_old, b_new = (jnp.asarray(z) for z in _sample_bias(win, steps))
    per_b = lambda *tail: pl.BlockSpec((None,) + tail, lambda i: (i,) + (0,) * len(tail))
    cache = pl.BlockSpec((None, None, heads, dh, win), lambda i: (layer, i, 0, 0, 0))
    return pl.pallas_call(
        _attn_sample_kernel,
        out_shape=jax.ShapeDtypeStruct((b, heads, steps, dh), F32),
        grid=(b,),
        in_specs=[per_b(heads, steps, dh), per_b(heads, dh, NEW_KEY_PAD), per_b(heads, dh, NEW_KEY_PAD),
                  cache, cache, _full(b_old.shape), _full(b_new.shape), _full((heads, 1, dh))],
        out_specs=per_b(heads, steps, dh),
        compiler_params=_cparams(),
        name="attn_sample",
    )(q, kn_t, vn_t, cache_kt, cache_vt, b_old, b_new, g.reshape(heads, 1, dh))


def _gelu_tanh(x):
    return 0.5 * x * (1.0 + jnp.tanh(np.sqrt(2.0 / np.pi).astype(np.float32) * (x + 0.044715 * (x * x * x))))


def _ssm_kernel(u_ref, bb_ref, lam_ref, pow_ref, cm_ref, dsk_ref, wglu_ref, bglu_ref, g_ref, pt_ref, h0_ref,
                z_ref, ht_ref, h_s, hin_s, carry_s, *, rows, steps, chain):
    ns = N_STATE
    cw = (4 * V7X_SUBLANES * V7X_LANES) // rows
    last = (steps - 1) * rows

    @pl.when(pl.program_id(0) == 0)
    def _():
        carry_s[...] = h0_ref[0:1, :]

    ub = u_ref[...].astype(BF16)
    n_slabs = SSM_WIDTH // V7X_LANES
    sw = ns // n_slabs
    for s in range(n_slabs):
        part = jnp.dot(ub[:, s * V7X_LANES:(s + 1) * V7X_LANES], bb_ref[s], preferred_element_type=F32)
        h_s[:, s * sw:(s + 1) * sw] = part[:, :sw]
        h_s[:, ns + s * sw:ns + (s + 1) * sw] = part[:, sw:]

    for cc in range(0, ns // cw, 2):
        crs = [slice(c * cw, (c + 1) * cw) for c in (cc, cc + 1)]
        cis = [slice(ns + c * cw, ns + (c + 1) * cw) for c in (cc, cc + 1)]
        lrs = [jnp.broadcast_to(lam_ref[0:1, cr], (rows, cw)) for cr in crs]
        lis = [jnp.broadcast_to(lam_ref[1:2, cr], (rows, cw)) for cr in crs]

        def scan_body(t, carry, crs=crs, cis=cis, lrs=lrs, lis=lis):
            rs = pl.ds(pl.multiple_of(t * rows, rows), rows)
            out = []
            for j in range(2):
                hr, hi = carry[2 * j], carry[2 * j + 1]
                nr = lrs[j] * hr - lis[j] * hi + h_s[rs, crs[j]]
                ni = lrs[j] * hi + lis[j] * hr + h_s[rs, cis[j]]
                h_s[rs, crs[j]] = nr
                h_s[rs, cis[j]] = ni
                out += [nr, ni]
            return tuple(out)

        zero = jnp.zeros((rows, cw), F32)
        lax.fori_loop(0, steps, scan_body, (zero,) * 4)

    if chain:
        ptr = pow_ref[steps - 1, 0:1, :ns]
        pti = pow_ref[steps - 1, 0:1, ns:]
        carry = carry_s[...]
        for c in range(rows):
            hin_s[c:c + 1, :] = carry
            cr_, ci_ = carry[:, :ns], carry[:, ns:]
            e = h_s[last + c:last + c + 1, :]
            carry = jnp.concatenate([ptr * cr_ - pti * ci_ + e[:, :ns],
                                     ptr * ci_ + pti * cr_ + e[:, ns:]], axis=1)
        carry_s[...] = carry
    else:
        hin_s[...] = h0_ref[...]

    for cc in range(ns // cw):
        cr = slice(cc * cw, (cc + 1) * cw)
        ci = slice(ns + cc * cw, ns + (cc + 1) * cw)
        hr0 = hin_s[:, cr]
        hi0 = hin_s[:, ci]

        def fix_body(t, _, cr=cr, ci=ci, hr0=hr0, hi0=hi0):
            rs = pl.ds(pl.multiple_of(t * rows, rows), rows)
            pr = pow_ref[t, :, cr]
            pi_ = pow_ref[t, :, ci]
            h_s[rs, cr] = h_s[rs, cr] + (pr * hr0 - pi_ * hi0)
            h_s[rs, ci] = h_s[rs, ci] + (pr * hi0 + pi_ * hr0)
            return 0

        lax.fori_loop(0, steps, fix_body, 0, unroll=8)

    ht_ref[...] = h_s[last:last + rows, :]
    ys = []
    for s in range(n_slabs):
        hs = jnp.concatenate([h_s[:, s * sw:(s + 1) * sw], h_s[:, ns + s * sw:ns + (s + 1) * sw]], axis=1)
        ys.append(jnp.dot(hs.astype(BF16), cm_ref[s], preferred_element_type=F32))
    y = jnp.concatenate(ys, axis=1) + dsk_ref[...] * u_ref[...]
    y = _gelu_tanh(y)
    gl = jnp.dot(y.astype(BF16), wglu_ref[...], preferred_element_type=F32) + bglu_ref[...]
    z = _rms(y * _sigmoid(gl), g_ref[...]).astype(BF16)
    z_ref[...] = jnp.dot(pt_ref[...], z, preferred_element_type=F32).astype(z_ref.dtype)


def _ssm_tables(ssm_a_re, ssm_a_im, ssm_log_dt, ssm_b_re, ssm_b_im, ssm_c_re, ssm_c_im, max_steps):
    g, n, ch = SSM_GROUPS, SSM_STATE, SSM_CH
    a_re = ssm_a_re.astype(F32)
    a_im = ssm_a_im.astype(F32)
    dt = jnp.exp(ssm_log_dt.astype(F32))[:, None]
    mag = jnp.exp(dt * a_re)
    lam_re = mag * jnp.cos(dt * a_im)
    lam_im = mag * jnp.sin(dt * a_im)
    nr = lam_re - 1.0
    ni = lam_im
    inv = 1.0 / (a_re * a_re + a_im * a_im)
    coef_re = (nr * a_re + ni * a_im) * inv
    coef_im = (ni * a_re - nr * a_im) * inv
    br = ssm_b_re.astype(F32)
    bi = ssm_b_im.astype(F32)
    bb_re = coef_re[..., None] * br - coef_im[..., None] * bi
    bb_im = coef_re[..., None] * bi + coef_im[..., None] * br
    gs = V7X_LANES // ch
    ns_ = g // gs
    eye = jnp.eye(gs, dtype=F32)
    bmat = lambda b: jnp.einsum("sgnc,gh->sgchn", b.reshape(ns_, gs, n, ch), eye).reshape(ns_, gs * ch, gs * n)
    cmat = lambda c: jnp.einsum("sgcn,gh->sgnhc", c.astype(F32).reshape(ns_, gs, ch, n), eye).reshape(
        ns_, gs * n, gs * ch)
    bb = jnp.concatenate([bmat(bb_re), bmat(bb_im)], axis=2).astype(BF16)
    cm = jnp.concatenate([cmat(ssm_c_re), -cmat(ssm_c_im)], axis=1).astype(BF16)
    lam = jnp.stack([lam_re.reshape(-1), lam_im.reshape(-1)])

    k = jnp.arange(1, max_steps + 1, dtype=F32)[:, None]
    kdt = k * dt.reshape(1, -1).repeat(n, axis=1)
    pmag = jnp.exp(kdt * a_re.reshape(1, -1))
    parg = kdt * a_im.reshape(1, -1)
    pows = jnp.concatenate([pmag * jnp.cos(parg), pmag * jnp.sin(parg)], axis=1)
    return bb, cm, lam, pows


def _ssm(u_perm, h0, tabs, dsk, w_glu, b_glu, g, perm_t, rows, steps, chain):
    n, w = u_perm.shape
    blk = rows * steps
    bb, cm, lam, pows = tabs
    row = pl.BlockSpec((blk, w), lambda i: (i, 0))
    kern = functools.partial(_ssm_kernel, rows=rows, steps=steps, chain=chain)
    return pl.pallas_call(
        kern,
        out_shape=(jax.ShapeDtypeStruct((n, w), BF16), jax.ShapeDtypeStruct((rows, 2 * N_STATE), F32)),
        grid=(n // blk,),
        in_specs=[row, _full(bb.shape), _full(lam.shape), _full((steps, rows, 2 * N_STATE)), _full(cm.shape),
                  _full((1, w)), _full((w, w)), _full((1, w)), _full((1, w)), _full((blk, blk)),
                  _full((rows, 2 * N_STATE))],
        out_specs=(row, _full((rows, 2 * N_STATE))),
        scratch_shapes=[pltpu.VMEM((blk, 2 * N_STATE), F32),
                        pltpu.VMEM((rows, 2 * N_STATE), F32),
                        pltpu.VMEM((1, 2 * N_STATE), F32)],
        compiler_params=_cparams(),
        name=f"ssm_r{rows}",
    )(u_perm, bb, lam, jnp.broadcast_to(pows[:steps, None, :], (steps, rows, 2 * N_STATE)), cm,
      dsk.reshape(1, w), w_glu.astype(BF16), b_glu.reshape(1, w),
      g.reshape(1, w), perm_t, h0)


def _outproj_kernel(*refs, n_real, aliased):
    ins, outs = refs[:12], refs[12 + aliased:]
    step = pl.program_id(0)

    @pl.when(step < n_real)
    def _():
        _outproj_tile(*ins, *outs)

    @pl.when(step >= n_real)
    def _():
        xs_ref = outs[1]
        xs_ref[...] = jnp.zeros(xs_ref.shape, xs_ref.dtype)


def _outproj_tile(x_ref, a_ref, z_ref, gt_ref, sh_ref, sc_ref, g_ref, wo_ref, wr_ref, br_ref, ltri_ref, utri_ref,
                  x1_ref, xs_ref, slot_ref, gates_ref, meta_ref, wbf_ref):
    @pl.when(pl.program_id(0) == 0)
    def _():
        wbf_ref[...] = wo_ref[...].astype(BF16)

    a = ATTN_WIDTH
    mixed = (jnp.dot(a_ref[...].astype(BF16), wbf_ref[:a, :], preferred_element_type=F32)
             + jnp.dot(z_ref[...], wbf_ref[a:, :], preferred_element_type=F32))
    x1 = x_ref[...] + gt_ref[...] * mixed
    x1_ref[...] = x1
    h2 = (_rms(x1, g_ref[...]) * (1.0 + sc_ref[...]) + sh_ref[...]).astype(BF16)
    lg = jnp.dot(h2, wr_ref[...], preferred_element_type=F32) + br_ref[...]
    tm = lg.shape[0]
    lane = lax.broadcasted_iota(jnp.int32, (tm, V7X_LANES), 1).astype(F32)
    vals, hots = [], []
    for _ in range(TOP_K):
        m = jnp.max(lg, axis=1, keepdims=True)
        idx = jnp.min(jnp.where(lg == m, lane, float(V7X_LANES)), axis=1, keepdims=True)
        hot = lane == idx
        vals.append(m)
        hots.append(jnp.where(hot, 1.0, 0.0))
        lg = jnp.where(hot, MASKED * 2, lg)
    es = [jnp.exp(v - vals[0]) for v in vals]
    inv = 1.0 / (es[0] + es[1] + es[2] + es[3])

    member = hots[0] + hots[1] + hots[2] + hots[3]
    before = jnp.dot(ltri_ref[...], member.astype(BF16), preferred_element_type=F32)
    count = jnp.sum(member, axis=0, keepdims=True)
    padded = jnp.floor((count + (ROW_ALIGN - 1.0)) * (1.0 / ROW_ALIGN)) * ROW_ALIGN
    padded8 = jnp.broadcast_to(padded, (V7X_SUBLANES, V7X_LANES))
    start = jnp.dot(padded8.astype(BF16), utri_ref[...], preferred_element_type=F32)[0:1]
    where_to = start + before
    slots = jnp.zeros((tm, V7X_LANES), F32)
    gates = jnp.zeros((tm, V7X_LANES), F32)
    for k in range(TOP_K):
        slot_k = jnp.sum(hots[k] * where_to, axis=1, keepdims=True)
        slots = jnp.where(lane == float(k), slot_k, slots)
        gates = jnp.where(lane == float(k), es[k] * inv, gates)
    slot_ref[...] = slots
    gates_ref[...] = gates
    row = lax.broadcasted_iota(jnp.int32, (V7X_SUBLANES, V7X_LANES), 0)
    meta_ref[...] = jnp.where(row == 0, padded8, jnp.where(row == 1, jnp.broadcast_to(start, padded8.shape), 0.0))

    cap = xs_ref.shape[0]
    slots_t = jnp.transpose(slots).astype(jnp.int32).astype(jnp.int16)
    srow = lax.broadcasted_iota(jnp.int16, (cap, tm), 0)
    place = jnp.zeros((cap, tm), BF16)
    for k in range(TOP_K):
        place = jnp.where(srow == slots_t[k:k + 1, :], jnp.ones((), BF16), place)
    xs_ref[...] = jnp.dot(place, h2, preferred_element_type=F32).astype(BF16)


def _outproj(x, a, z, gt, sh, sc, g, w_out, wr_pad, br_pad, tm, n_blocks, block0, xs_prev=None):
    n, d = x.shape
    aw = ATTN_WIDTH
    n_real = n // tm
    aliased = xs_prev is not None
    n_steps = n_real if aliased else n_blocks
    tile = lambda i: jnp.minimum(i, n_real - 1)
    mod_spec = (pl.BlockSpec((1, d), lambda i: (0, 0)) if gt.shape[0] == 1
                else pl.BlockSpec((tm, d), lambda i: (tile(i), 0)))
    row = lambda w: pl.BlockSpec((tm, w), lambda i: (tile(i), 0))
    ltri = jnp.asarray(np.tril(np.ones((tm, tm), np.float32), -1), BF16)
    utri = jnp.asarray(np.triu(np.ones((V7X_LANES, V7X_LANES), np.float32), 1), BF16)
    in_specs = [row(d), row(aw), row(SSM_WIDTH), mod_spec, mod_spec, mod_spec, _full((1, d)),
                _full(w_out.shape), _full(wr_pad.shape), _full(br_pad.shape), _full(ltri.shape), _full(utri.shape)]
    args = [x, a, z, gt, sh, sc, g.reshape(1, d), w_out, wr_pad, br_pad, ltri, utri]
    if aliased:
        in_specs.append(pl.BlockSpec(memory_space=pl.ANY))
        args.append(xs_prev)
    return pl.pallas_call(
        functools.partial(_outproj_kernel, n_real=n_real, aliased=int(aliased)),
        out_shape=(jax.ShapeDtypeStruct((n, d), F32),
                   jax.ShapeDtypeStruct((n_blocks, GROUP_CAP, d), BF16),
                   jax.ShapeDtypeStruct((n, V7X_LANES), F32), jax.ShapeDtypeStruct((n, V7X_LANES), F32),
                   jax.ShapeDtypeStruct((n_real, V7X_SUBLANES, V7X_LANES), F32)),
        grid=(n_steps,),
        in_specs=in_specs,
        out_specs=(row(d), pl.BlockSpec((None, GROUP_CAP, d), lambda i: (block0 + i, 0, 0)),
                   row(V7X_LANES), row(V7X_LANES),
                   pl.BlockSpec((None, V7X_SUBLANES, V7X_LANES), lambda i: (tile(i), 0, 0))),
        scratch_shapes=[pltpu.VMEM(w_out.shape, BF16)],
        input_output_aliases={len(args) - 1: 1} if aliased else {},
        compiler_params=_cparams(),
        name="outproj_router",
    )(*args)


_PIECE_SIZES = tuple(1 << s for s in range(MOE_TILE.bit_length() - 1, ROW_ALIGN.bit_length() - 2, -1))
_CHUNK_SHIFT = 6
_CHUNK = 1 << _CHUNK_SHIFT
_TAIL_SIZES = tuple(sz for sz in _PIECE_SIZES if sz < _CHUNK)


def _moe_kernel(te_ref, tl_ref, lo_ref, hi_ref, nu_ref, gstart_ref, gsize_ref, gbase_ref, rows_ref, used_ref,
                nxt_ref, xs_hbm, wgu_hbm, bgu_ref, wd_hbm, bd_ref, ys_hbm,
                xbuf, ybuf, zbuf, wgu_f32, wd_f32, wgu_bf, wd_bf, xsem, ysem, zsem, wsem):
    t = pl.program_id(0)
    n_used = nu_ref[0]
    n_blocks, cap = xs_hbm.shape[0], xs_hbm.shape[1]

    def weight_copies(e):
        return (pltpu.make_async_copy(wgu_hbm.at[e], wgu_f32, wsem.at[0]),
                pltpu.make_async_copy(wd_hbm.at[e], wd_f32, wsem.at[1]))

    def x_copy(i, br, tr, sz, slot):
        pltpu.make_async_copy(xs_hbm.at[i, pl.ds(br, sz)], xbuf.at[slot, pl.ds(tr, sz)], xsem.at[slot]).start()

    def y_copy(i, br, tr, sz, slot):
        pltpu.make_async_copy(ybuf.at[slot, pl.ds(tr, sz)], ys_hbm.at[i, pl.ds(br, sz)], ysem.at[slot]).start()

    def z_copy(i, row, sz, start):
        cp = pltpu.make_async_copy(zbuf.at[pl.ds(0, sz)], ys_hbm.at[i, pl.ds(row, sz)], zsem)
        cp.start() if start else cp.wait()

    def pieces(tt, fn):
        e = te_ref[tt]
        lo = tl_ref[tt] * MOE_TILE

        def per_block(i, c):
            g = e * n_blocks + i
            s0 = gbase_ref[g]
            a = jnp.maximum(s0, lo)
            b = jnp.minimum(s0 + gsize_ref[g], lo + MOE_TILE)
            length = jnp.maximum(b - a, 0)
            src = gstart_ref[g] + (a - s0)
            dst = a - lo
            whole = lax.shift_right_logical(length, _CHUNK_SHIFT)

            def chunk(j, cc):
                off = j * _CHUNK
                fn(i, pl.multiple_of(src + off, ROW_ALIGN), pl.multiple_of(dst + off, ROW_ALIGN), _CHUNK)
                return cc

            lax.fori_loop(0, whole, chunk, 0)
            done = whole * _CHUNK
            for sz in _TAIL_SIZES:
                hit = (length & sz) != 0

                @pl.when(hit)
                def _(sz=sz, done=done):
                    fn(i, pl.multiple_of(src + done, ROW_ALIGN), pl.multiple_of(dst + done, ROW_ALIGN), sz)

                done = done + jnp.where(hit, sz, 0)
            return c

        lax.fori_loop(lo_ref[tt], hi_ref[tt], per_block, 0)

    def tile_rows(tt):
        return jnp.minimum(rows_ref[te_ref[tt]] - tl_ref[tt] * MOE_TILE, MOE_TILE)

    def wait_rows(n, sem, buf):
        def chunk(j, cc):
            pltpu.make_async_copy(buf.at[pl.ds(0, _CHUNK)], buf.at[pl.ds(0, _CHUNK)], sem).wait()
            return cc

        lax.fori_loop(0, lax.shift_right_logical(n, _CHUNK_SHIFT), chunk, 0)
        for sz in _TAIL_SIZES:
            @pl.when((n & sz) != 0)
            def _(sz=sz):
                pltpu.make_async_copy(buf.at[pl.ds(0, sz)], buf.at[pl.ds(0, sz)], sem).wait()

    def fetch(tt, slot):
        pieces(tt, lambda i, br, tr, sz: x_copy(i, br, tr, sz, slot))

    def writeback(tt, slot):
        pieces(tt, lambda i, br, tr, sz: y_copy(i, br, tr, sz, slot))

    def zero_tail(i, start):
        u = used_ref[i]
        rem = cap - u
        nz = zbuf.shape[0]
        whole = lax.shift_right_logical(rem, nz.bit_length() - 1)

        def chunk(j, c):
            z_copy(i, pl.multiple_of(u + j * nz, ROW_ALIGN), nz, start)
            return c

        lax.fori_loop(0, whole, chunk, 0)
        base = u + whole * nz
        done = jnp.int32(0)
        for sz in _PIECE_SIZES:
            if sz >= nz:
                continue
            hit = (rem & sz) != 0

            @pl.when(hit)
            def _(sz=sz, done=done):
                z_copy(i, pl.multiple_of(base + done, ROW_ALIGN), sz, start)

            done = done + jnp.where(hit, sz, 0)

    @pl.when(t == 0)
    def _():
        xbuf[...] = jnp.zeros(xbuf.shape, xbuf.dtype)
        zbuf[...] = jnp.zeros(zbuf.shape, zbuf.dtype)
        lax.fori_loop(0, n_blocks, lambda i, c: (zero_tail(i, True), c)[1], 0)
        lax.fori_loop(0, n_blocks, lambda i, c: (zero_tail(i, False), c)[1], 0)
        fetch(0, 0)
        for cp in weight_copies(te_ref[0]):
            cp.start()

    @pl.when(t < n_used)
    def _():
        slot = t % 2

        @pl.when(t + 1 < n_used)
        def _():
            fetch(t + 1, 1 - slot)

        @pl.when(tl_ref[t] == 0)
        def _():
            for cp in weight_copies(te_ref[t]):
                cp.wait()
            wgu_bf[...] = wgu_f32[...].astype(BF16)
            wd_bf[...] = wd_f32[...].astype(BF16)
            nxt = nxt_ref[te_ref[t]]

            @pl.when(nxt >= 0)
            def _():
                for cp in weight_copies(nxt):
                    cp.start()

        wait_rows(tile_rows(t), xsem.at[slot], xbuf.at[slot])

        @pl.when(t >= 2)
        def _():
            wait_rows(tile_rows(t - 2), ysem.at[slot], ybuf.at[slot])

        def expert_mlp(m):
            f = D_EXPERT
            gu = jnp.dot(xbuf[slot, pl.ds(0, m)], wgu_bf[...], preferred_element_type=F32) + bgu_ref[...]
            gate = jnp.minimum(gu[:, :f], SWIGLU_LIMIT)
            up = jnp.clip(gu[:, f:], -SWIGLU_LIMIT, SWIGLU_LIMIT)
            act = (up + 1.0) * gate * _sigmoid(SWIGLU_ALPHA * gate)
            y = jnp.dot(act.astype(BF16), wd_bf[...], preferred_element_type=F32) + bd_ref[...]
            ybuf[slot, pl.ds(0, m)] = y.astype(BF16)

        n_steps = lax.shift_right_logical(tile_rows(t) + (MOE_ROW_STEP - 1), MOE_ROW_STEP.bit_length() - 1)
        for ns in range(1, MOE_TILE // MOE_ROW_STEP + 1):
            @pl.when(n_steps == ns)
            def _(ns=ns):
                expert_mlp(ns * MOE_ROW_STEP)

        writeback(t, slot)

        @pl.when(t == n_used - 1)
        def _():
            wait_rows(tile_rows(t), ysem.at[slot], ybuf.at[slot])

            @pl.when(t >= 1)
            def _():
                wait_rows(tile_rows(t - 1), ysem.at[1 - slot], ybuf.at[1 - slot])


def _moe(plan, xs, w_gate_up, b_gate_up, w_down, b_down):
    _, cap, d = xs.shape
    e, _, f2 = w_gate_up.shape
    nt = plan[0].shape[0]
    wmap = lambda t, te, *_: (te[t], 0, 0)
    anyspec = pl.BlockSpec(memory_space=pl.ANY)
    grid_spec = pltpu.PrefetchScalarGridSpec(
        num_scalar_prefetch=len(plan),
        grid=(nt,),
        in_specs=[anyspec,
                  anyspec,
                  pl.BlockSpec((None, 1, f2), wmap),
                  anyspec,
                  pl.BlockSpec((None, 1, d), wmap)],
        out_specs=anyspec,
        scratch_shapes=[pltpu.VMEM((2, MOE_TILE, d), BF16), pltpu.VMEM((2, MOE_TILE, d), BF16),
                        pltpu.VMEM((MOE_TILE // 2, d), BF16),
                        pltpu.VMEM((d, f2), F32), pltpu.VMEM((f2 // 2, d), F32),
                        pltpu.VMEM((d, f2), BF16), pltpu.VMEM((f2 // 2, d), BF16),
                        pltpu.SemaphoreType.DMA((2,)), pltpu.SemaphoreType.DMA((2,)), pltpu.SemaphoreType.DMA(()),
                        pltpu.SemaphoreType.DMA((2,))],
    )
    return pl.pallas_call(
        _moe_kernel,
        out_shape=jax.ShapeDtypeStruct(xs.shape, BF16),
        grid_spec=grid_spec,
        compiler_params=_cparams(),
        name="moe_experts",
    )(*plan, xs, w_gate_up, b_gate_up.reshape(e, 1, f2), w_down, b_down.reshape(e, 1, d))


def _moe_plan(group_size, group_start):
    n_blocks = group_size.shape[0]
    gsize = group_size.T.astype(jnp.int32)
    gstart = group_start.T.astype(jnp.int32)
    gbase = jnp.cumsum(gsize, axis=1) - gsize
    rows = jnp.sum(gsize, axis=1)
    tiles = (rows + MOE_TILE - 1) // MOE_TILE
    tile_end = jnp.cumsum(tiles)
    n_used = tile_end[-1:]
    nt = (n_blocks * GROUP_CAP) // MOE_TILE + N_EXPERTS
    t = jnp.arange(nt, dtype=jnp.int32)
    te = jnp.sum((tile_end[None, :] <= t[:, None]).astype(jnp.int32), axis=1)
    last = jnp.max(jnp.where(tiles > 0, jnp.arange(N_EXPERTS, dtype=jnp.int32), 0))
    te = jnp.where(t < n_used[0], jnp.minimum(te, N_EXPERTS - 1), last)
    hot = (te[:, None] == jnp.arange(N_EXPERTS, dtype=jnp.int32)[None, :]).astype(jnp.int32)
    tl = jnp.where(t < n_used[0], t - hot @ (tile_end - tiles), 0)
    lo_row = tl * MOE_TILE
    base_t = hot @ gbase
    size_t = hot @ gsize
    first = jnp.sum((base_t + size_t <= lo_row[:, None]).astype(jnp.int32), axis=1)
    stop = jnp.sum((base_t < lo_row[:, None] + MOE_TILE).astype(jnp.int32), axis=1)
    used = jnp.sum(gsize, axis=0)
    ids = jnp.arange(N_EXPERTS, dtype=jnp.int32)
    later = (ids[None, :] > ids[:, None]) & (tiles[None, :] > 0)
    nxt = jnp.min(jnp.where(later, ids[None, :], N_EXPERTS), axis=1)
    nxt = jnp.where(nxt < N_EXPERTS, nxt, -1)
    i32 = lambda z: z.astype(jnp.int32)
    return (i32(te), i32(tl), i32(first), i32(stop), i32(n_used), i32(gstart.reshape(-1)), i32(gsize.reshape(-1)),
            i32(gbase.reshape(-1)), i32(rows), i32(used), i32(nxt))


def _final_kernel(x_ref, ys_ref, slot_ref, gates_ref, gt_ref, g_ref, y_ref):
    tm = x_ref.shape[0]
    cap = ys_ref.shape[0]
    col = lax.broadcasted_iota(jnp.int16, (tm, cap), 1)
    slots = slot_ref[...].astype(jnp.int32).astype(jnp.int16)
    gates = gates_ref[...].astype(BF16)
    mix = jnp.zeros((tm, cap), BF16)
    for k in range(TOP_K):
        mix = jnp.where(col == slots[:, k:k + 1], gates[:, k:k + 1], mix)
    ff = jnp.dot(mix, ys_ref[...], preferred_element_type=F32)
    y_ref[...] = _rms(x_ref[...] + gt_ref[...] * ff, g_ref[...])


def _final(x1, ys, slots, gates, gt, g, tm, block0):
    n, d = x1.shape
    cap = ys.shape[1]
    mod_spec = (pl.BlockSpec((1, d), lambda i: (0, 0)) if gt.shape[0] == 1
                else pl.BlockSpec((tm, d), lambda i: (i, 0)))
    row = lambda w: pl.BlockSpec((tm, w), lambda i: (i, 0))
    return pl.pallas_call(
        _final_kernel,
        out_shape=jax.ShapeDtypeStruct((n, d), F32),
        grid=(n // tm,),
        in_specs=[row(d), pl.BlockSpec((None, cap, d), lambda i: (block0 + i, 0, 0)),
                  row(V7X_LANES), row(V7X_LANES), mod_spec, _full((1, d))],
        out_specs=row(d),
        compiler_params=_cparams(),
        name="final_norm",
    )(x1, ys, slots, gates, gt, g.reshape(1, d))


def kernel(x_prompt, x_sample, cache_k_win, cache_v_win, state_ssm_re, state_ssm_im, c_prompt, c_sample,
           w_ada, b_ada, g_norm1, w_in, ssm_a_re, ssm_a_im, ssm_log_dt, ssm_b_re, ssm_b_im, ssm_c_re, ssm_c_im,
           ssm_d, w_glu, b_glu, g_out_attn, g_out_ssm, w_out, g_norm2, w_router, b_router, w_gate_up, b_gate_up,
           w_down, b_down, g_final):
    depth = w_ada.shape[0]
    assert depth == 1 and x_prompt.shape[0] == 1
    bp, t, d = x_prompt.shape
    bs, ts, _ = x_sample.shape
    ns = bs * ts
    l = 0
    a = ATTN_WIDTH

    n_c = bp + bs
    c_pad = -(-n_c // V7X_SUBLANES) * V7X_SUBLANES
    c_rows = jnp.concatenate([c_prompt, c_sample, jnp.zeros((c_pad - n_c, d), F32)], axis=0)
    mod = _ada_modulation(c_rows, w_ada[l], b_ada[l])
    mod_p = [mod[0:1, i * d:(i + 1) * d] for i in range(N_MOD)]
    mod_s = [jnp.repeat(mod[bp:bp + bs, i * d:(i + 1) * d], ts, axis=0) for i in range(N_MOD)]

    steps_p = TOKEN_TILE // SSM_ROWS_PROMPT
    perm_p = _chunk_perm(SSM_ROWS_PROMPT, steps_p)
    perm_s = _chunk_perm(bs, ts)
    tabs = _ssm_tables(ssm_a_re[l], ssm_a_im[l], ssm_log_dt[l], ssm_b_re[l], ssm_b_im[l],
                       ssm_c_re[l], ssm_c_im[l], max(steps_p, ts))
    wr_pad = jnp.zeros((d, V7X_LANES), F32).at[:, :N_EXPERTS].set(w_router[l]).astype(BF16)
    br_pad = jnp.full((1, V7X_LANES), MASKED, F32).at[0, :N_EXPERTS].set(b_router[l])

    xp = x_prompt.reshape(t, d)
    dils = tuple(dil for _, dil in DILATED_BRANCHES)
    wide = tuple(dil for dil in dils if dil > 1)
    proj_p = _inproj(xp, mod_p[0], mod_p[1], g_norm1[l], w_in[l], jnp.asarray(perm_p, BF16), TOKEN_TILE, wide)
    kpf, vpf, up = proj_p[3:6]
    views = {1: proj_p[0:3]}
    for n_d, dil in enumerate(wide):
        views[dil] = proj_p[6 + 3 * n_d:9 + 3 * n_d]
    outs = [_attn_branch(*views[dil], dil) for dil in dils]
    ap = _attn_combine([o for o, _ in outs], [s for _, s in outs], g_out_attn[l], dils)
    zeros_h = jnp.zeros((SSM_ROWS_PROMPT, 2 * N_STATE), F32)
    zp, hp = _ssm(up, zeros_h, tabs, ssm_d[l], w_glu[l], b_glu[l], g_out_ssm[l],
                  jnp.asarray(perm_p.T, BF16), SSM_ROWS_PROMPT, steps_p, True)
    n_blocks_p = t // TOKEN_TILE
    n_blocks = n_blocks_p + 1
    x1p, xs_all, slot_p, gate_p, meta_p = _outproj(xp, ap, zp, mod_p[2], mod_p[3], mod_p[4], g_norm2[l], w_out[l],
                                                   wr_pad, br_pad, TOKEN_TILE, n_blocks, 0)

    xs = x_sample.reshape(ns, d)
    qs, ks, vs, ksf, vsf, us = _inproj(xs, mod_s[0], mod_s[1], g_norm1[l], w_in[l], jnp.asarray(perm_s, BF16), ns)
    split = lambda z: z.reshape(bs, ts, N_HEADS, HEAD_DIM)
    new_t = lambda z: jnp.pad(split(z).transpose(0, 2, 3, 1), ((0, 0), (0, 0), (0, 0), (0, NEW_KEY_PAD - ts)))
    as_ = _attn_sample(split(qs).transpose(0, 2, 1, 3), new_t(ksf), new_t(vsf),
                       cache_k_win.transpose(0, 1, 3, 4, 2), cache_v_win.transpose(0, 1, 3, 4, 2), l, g_out_attn[l])
    as_ = as_.transpose(0, 2, 1, 3)
    h0s = jnp.concatenate([state_ssm_re[l].reshape(bs, N_STATE), state_ssm_im[l].reshape(bs, N_STATE)], axis=1)
    zs, hs = _ssm(us, h0s, tabs, ssm_d[l], w_glu[l], b_glu[l], g_out_ssm[l],
                  jnp.asarray(perm_s.T, BF16), bs, ts, False)
    x1s, xs_all, slot_s, gate_s, meta_s = _outproj(xs, as_.reshape(ns, a), zs, mod_s[2], mod_s[3], mod_s[4],
                                                   g_norm2[l], w_out[l], wr_pad, br_pad, ns, n_blocks, n_blocks_p,
                                                   xs_prev=xs_all)

    meta = jnp.concatenate([meta_p, meta_s], axis=0)
    plan = _moe_plan(meta[:, 0, :N_EXPERTS], meta[:, 1, :N_EXPERTS])
    ys_all = _moe(plan, xs_all, w_gate_up[l], b_gate_up[l], w_down[l], b_down[l])

    y_prompt = _final(x1p, ys_all, slot_p, gate_p, mod_p[5], g_final, TOKEN_TILE, 0).reshape(bp, t, d)
    y_sample = _final(x1s, ys_all, slot_s, gate_s, mod_s[5], g_final, ns, n_blocks_p).reshape(bs, ts, d)

    keep = min(MAX_WINDOW, t)
    shp = (1, bp, keep, N_HEADS, HEAD_DIM)
    k_win = kpf[t - keep:].reshape(shp)
    v_win = vpf[t - keep:].reshape(shp)
    st = (1, bp, SSM_GROUPS, SSM_STATE)
    hp_last = hp[SSM_ROWS_PROMPT - 1]
    ss = (1, bs, SSM_GROUPS, SSM_STATE)
    return (y_prompt, y_sample, k_win, v_win,
            hp_last[:N_STATE].reshape(st), hp_last[N_STATE:].reshape(st),
            ksf.reshape(1, bs, ts, N_HEADS, HEAD_DIM), vsf.reshape(1, bs, ts, N_HEADS, HEAD_DIM),
            hs[:, :N_STATE].reshape(ss), hs[:, N_STATE:].reshape(ss))
```

```python
import functools

import numpy as np
import jax
import jax.numpy as jnp
from jax import lax
from jax.experimental import pallas as pl
from jax.experimental.pallas import tpu as pltpu

F32 = jnp.float32
BF16 = jnp.bfloat16

D_MODEL = 1024
N_HEADS = 8
HEAD_DIM = 64
ATTN_WIDTH = N_HEADS * HEAD_DIM
DILATED_BRANCHES = ((128, 1), (512, 4), (2048, 16))
MAX_WINDOW = 2048
SSM_WIDTH = D_MODEL - ATTN_WIDTH
SSM_CH = 16
SSM_GROUPS = SSM_WIDTH // SSM_CH
SSM_STATE = 64
N_STATE = SSM_GROUPS * SSM_STATE
N_EXPERTS = 32
TOP_K = 4
D_EXPERT = D_MODEL
SWIGLU_LIMIT = 7.0
SWIGLU_ALPHA = 1.702
N_MOD = 6
EPS = 1e-6
MASKED = -1e30

V7X_LANES = 128
V7X_SUBLANES = 8
V7X_VMEM_LIMIT_BYTES = 56 * 1024 * 1024

TOKEN_TILE = 512
Q_TILE = 128
Q_BLOCKS_PER_STEP = 8
SSM_ROWS_PROMPT = 8
MOE_TILE = 1024
MOE_ROW_STEP = 128
ROW_ALIGN = 16
GROUP_CAP = -(-(TOKEN_TILE * TOP_K + N_EXPERTS * (ROW_ALIGN - 1)) // (2 * V7X_LANES)) * (2 * V7X_LANES)


def _cparams(n_axes=1):
    return pltpu.CompilerParams(
        dimension_semantics=("arbitrary",) * n_axes,
        vmem_limit_bytes=V7X_VMEM_LIMIT_BYTES,
    )


def _full(shape):
    n = len(shape)
    return pl.BlockSpec(shape, lambda *_: (0,) * n)


def _rms(x, g):
    return x * lax.rsqrt(jnp.mean(x * x, axis=-1, keepdims=True) + EPS) * g


def _sigmoid(x):
    return 1.0 / (1.0 + jnp.exp(-x))


def _ada_kernel(c_ref, w_ref, b_ref, o_ref):
    c = c_ref[...]
    s = (c * _sigmoid(c)).astype(BF16)
    o_ref[...] = jnp.dot(s, w_ref[...].astype(BF16), preferred_element_type=F32) + b_ref[...]


def _ada_modulation(c_rows, w_ada, b_ada):
    m, d = c_rows.shape
    n = w_ada.shape[1]
    tn = n // 4
    return pl.pallas_call(
        _ada_kernel,
        out_shape=jax.ShapeDtypeStruct((m, n), F32),
        grid=(n // tn,),
        in_specs=[_full((m, d)),
                  pl.BlockSpec((d, tn), lambda j: (0, j)),
                  pl.BlockSpec((1, tn), lambda j: (0, j))],
        out_specs=pl.BlockSpec((m, tn), lambda j: (0, j)),
        compiler_params=_cparams(),
        name="ada_modulation",
    )(c_rows, w_ada, b_ada.reshape(1, n))


def _inproj_kernel(*refs, dils):
    x_ref, sh_ref, sc_ref, g_ref, w_ref, perm_ref = refs[:6]
    dperm_refs = refs[6:6 + len(dils)]
    q_ref, k_ref, v_ref, kf_ref, vf_ref, u_ref = refs[6 + len(dils):12 + len(dils)]
    dil_refs = refs[12 + len(dils):-1]
    wbf_ref = refs[-1]

    @pl.when(pl.program_id(0) == 0)
    def _():
        wbf_ref[...] = w_ref[...].astype(BF16)

    h = _rms(x_ref[...], g_ref[...]) * (1.0 + sc_ref[...]) + sh_ref[...]
    hb = h.astype(BF16)
    a = ATTN_WIDTH
    proj = jnp.dot(hb, wbf_ref[:, :3 * a], preferred_element_type=F32)
    k = proj[:, a:2 * a]
    v = proj[:, 2 * a:]
    qkv = jnp.concatenate([(proj[:, :a] * (HEAD_DIM ** -0.5)).astype(BF16), k.astype(BF16), v.astype(BF16)],
                          axis=1)
    q_ref[...] = qkv[:, :a]
    k_ref[...] = qkv[:, a:2 * a]
    v_ref[...] = qkv[:, 2 * a:]
    kf_ref[...] = k
    vf_ref[...] = v
    hp = jnp.dot(perm_ref[...], hb, preferred_element_type=F32).astype(BF16)
    u_ref[...] = jnp.dot(hp, wbf_ref[:, 3 * a:], preferred_element_type=F32)
    tm = qkv.shape[0]
    for n_d, dil in enumerate(dils):
        by_residue = jnp.dot(dperm_refs[n_d][...], qkv, preferred_element_type=F32).astype(BF16)
        per = tm // dil
        for r in range(dil):
            rows = by_residue[r * per:(r + 1) * per]
            for j in range(3):
                dil_refs[3 * n_d + j][:, r * a:(r + 1) * a] = rows[:, j * a:(j + 1) * a]


def _inproj(x, sh, sc, g, w_in, perm, tm, dils=()):
    n, d = x.shape
    a = ATTN_WIDTH
    mod_rows = sh.shape[0]
    mod_spec = (pl.BlockSpec((1, d), lambda i: (0, 0)) if mod_rows == 1
                else pl.BlockSpec((tm, d), lambda i: (i, 0)))
    row = lambda w: pl.BlockSpec((tm, w), lambda i: (i, 0))
    dperms = [jnp.asarray(_chunk_perm(tm // dil, dil), BF16) for dil in dils]
    view_shapes = tuple(jax.ShapeDtypeStruct((n // dil, dil * a), BF16) for dil in dils for _ in range(3))
    view_specs = tuple(pl.BlockSpec((tm // dil, dil * a), lambda i: (i, 0)) for dil in dils for _ in range(3))
    return pl.pallas_call(
        functools.partial(_inproj_kernel, dils=tuple(dils)),
        out_shape=(jax.ShapeDtypeStruct((n, a), BF16),) * 3
        + (jax.ShapeDtypeStruct((n, a), F32),) * 2
        + (jax.ShapeDtypeStruct((n, SSM_WIDTH), F32),) + view_shapes,
        grid=(n // tm,),
        in_specs=[row(d), mod_spec, mod_spec, _full((1, d)), _full(w_in.shape), _full((tm, tm))]
        + [_full((tm, tm))] * len(dils),
        out_specs=(row(a),) * 5 + (row(SSM_WIDTH),) + view_specs,
        scratch_shapes=[pltpu.VMEM(w_in.shape, BF16)],
        compiler_params=_cparams(),
        name="inproj",
    )(x, sh, sc, g.reshape(1, d), w_in, perm, *dperms)


def _chunk_perm(rows, steps):
    n = rows * steps
    p = np.zeros((n, n), np.float32)
    c, t = np.meshgrid(np.arange(rows), np.arange(steps), indexing="ij")
    p[(t * rows + c).ravel(), (c * steps + t).ravel()] = 1.0
    return p


def _alibi_slopes():
    return np.exp2(-8.0 * np.arange(1, N_HEADS + 1, dtype=np.float64) / N_HEADS).astype(np.float32)


def _branch_bias(dil):
    qi = np.arange(Q_TILE)[:, None]
    col = np.arange(2 * Q_TILE)[None, :]
    j = Q_TILE + qi - col
    valid = (j >= 0) & (j <= Q_TILE)
    dist = (j * dil).astype(np.float32)
    tabs = []
    for first in (True, False):
        ok = valid & (col >= Q_TILE) if first else valid
        per_head = [np.where(ok, -s * dist, np.float32(MASKED)) for s in _alibi_slopes()]
        tabs.append(np.concatenate(per_head, axis=0))
    return np.stack(tabs).astype(np.float32)


def _attn_branch_kernel(q_ref, kp_ref, kc_ref, vp_ref, vc_ref, bias_ref, o_ref, lse_ref):
    first_step = pl.program_id(0) == 0
    lane = lax.broadcasted_iota(jnp.int32, (Q_TILE, V7X_LANES), 1)
    lo = lane < HEAD_DIM
    for j in range(Q_BLOCKS_PER_STEP):
        rows = slice(j * Q_TILE, (j + 1) * Q_TILE)
        before = slice((j - 1) * Q_TILE, j * Q_TILE)
        sel = jnp.where(first_step, 0, 1) if j == 0 else 1
        lse_acc = jnp.zeros((Q_TILE, V7X_LANES), F32)
        for p in range(N_HEADS // 2):
            cs = slice(V7X_LANES * p, V7X_LANES * (p + 1))
            q2 = q_ref[rows, cs]
            zero = jnp.zeros_like(q2)
            qq = jnp.concatenate([jnp.where(lo, q2, zero), jnp.where(lo, zero, q2)], axis=0)
            k_before = kp_ref[:, cs] if j == 0 else kc_ref[before, cs]
            v_before = vp_ref[:, cs] if j == 0 else vc_ref[before, cs]
            kk = jnp.concatenate([k_before, kc_ref[rows, cs]], axis=0)
            vv = jnp.concatenate([v_before, vc_ref[rows, cs]], axis=0)
            s = lax.dot_general(qq, kk, (((1,), (1,)), ((), ())), preferred_element_type=F32)
            s = s + bias_ref[sel, 2 * Q_TILE * p:2 * Q_TILE * (p + 1), :]
            m = jnp.max(s, axis=1, keepdims=True)
            e = jnp.exp(s - m)
            l = jnp.sum(e, axis=1, keepdims=True)
            eb = e.astype(BF16)
            o0 = jnp.dot(eb[:Q_TILE], vv, preferred_element_type=F32) * (1.0 / l[:Q_TILE])
            o1 = jnp.dot(eb[Q_TILE:], vv, preferred_element_type=F32) * (1.0 / l[Q_TILE:])
            o_ref[rows, cs] = jnp.where(lo, o0, o1).astype(o_ref.dtype)
            lse = m + jnp.log(l)
            lse_acc = jnp.where(lane == 2 * p, lse[:Q_TILE], lse_acc)
            lse_acc = jnp.where(lane == 2 * p + 1, lse[Q_TILE:], lse_acc)
        lse_ref[rows, :] = lse_acc[:, :N_HEADS]


def _attn_branch(qv, kv, vv, dil):
    a = ATTN_WIDTH
    rows = qv.shape[0]
    t = rows * dil
    step = Q_BLOCKS_PER_STEP * Q_TILE
    cur = pl.BlockSpec((step, a), lambda i, r: (i, r))
    prev = pl.BlockSpec((Q_TILE, a), lambda i, r: (jnp.maximum(i * Q_BLOCKS_PER_STEP - 1, 0), r))
    bias = jnp.asarray(_branch_bias(dil))
    o, lse = pl.pallas_call(
        _attn_branch_kernel,
        out_shape=(jax.ShapeDtypeStruct((rows, dil * a), BF16),
                   jax.ShapeDtypeStruct((dil, rows, N_HEADS), F32)),
        grid=(rows // step, dil),
        in_specs=[cur, prev, cur, prev, cur, _full(bias.shape)],
        out_specs=(cur, pl.BlockSpec((None, step, N_HEADS), lambda i, r: (r, i, 0))),
        compiler_params=_cparams(2),
        name=f"attn_branch_d{dil}",
    )(qv, kv, kv, vv, vv, bias)
    return o, lse.transpose(1, 0, 2).reshape(t, N_HEADS)


def _attn_combine_kernel(*refs, dils):
    nb = len(dils)
    o_refs = refs[:nb]
    l_refs = refs[nb:2 * nb]
    g_ref = refs[2 * nb]
    unperm_refs = refs[2 * nb + 1:-1]
    a_ref = refs[-1]
    tq, a = a_ref.shape
    outs = []
    n_u = 0
    for o_ref, dil in zip(o_refs, dils):
        if dil == 1:
            outs.append(o_ref[...].astype(F32))
            continue
        by_residue = jnp.concatenate([o_ref[:, r * a:(r + 1) * a] for r in range(dil)], axis=0)
        outs.append(jnp.dot(unperm_refs[n_u][...], by_residue, preferred_element_type=F32))
        n_u += 1
    ls = [l_ref[...] for l_ref in l_refs]
    top = functools.reduce(jnp.maximum, ls)
    ws = [jnp.exp(l - top) for l in ls]
    inv = 1.0 / functools.reduce(jnp.add, ws)
    cs = [w * inv for w in ws]
    lane = lax.broadcasted_iota(jnp.int32, (tq, V7X_LANES), 1)
    lo = lane < HEAD_DIM
    cols = []
    for p in range(N_HEADS // 2):
        sl = slice(V7X_LANES * p, V7X_LANES * (p + 1))
        acc = jnp.zeros((tq, V7X_LANES), F32)
        for c, o in zip(cs, outs):
            cexp = jnp.where(lo,
                             jnp.broadcast_to(c[:, 2 * p:2 * p + 1], (tq, V7X_LANES)),
                             jnp.broadcast_to(c[:, 2 * p + 1:2 * p + 2], (tq, V7X_LANES)))
            acc = acc + cexp * o[:, sl]
        cols.append(acc)
    o = jnp.concatenate(cols, axis=1)
    a_ref[...] = _rms(o, g_ref[...]).astype(a_ref.dtype)


def _attn_combine(os_, lses, g, dils):
    t = lses[0].shape[0]
    a = ATTN_WIDTH
    tq = TOKEN_TILE
    unperms = [jnp.asarray(_chunk_perm(tq // dil, dil).T, BF16) for dil in dils if dil > 1]
    return pl.pallas_call(
        functools.partial(_attn_combine_kernel, dils=tuple(dils)),
        out_shape=jax.ShapeDtypeStruct((t, a), BF16),
        grid=(t // tq,),
        in_specs=[pl.BlockSpec((tq // dil, dil * a), lambda i: (i, 0)) for dil in dils]
        + [pl.BlockSpec((tq, N_HEADS), lambda i: (i, 0))] * len(dils) + [_full((1, a))]
        + [_full((tq, tq))] * len(unperms),
        out_specs=pl.BlockSpec((tq, a), lambda i: (i, 0)),
        compiler_params=_cparams(),
        name="attn_combine",
    )(*os_, *lses, g.reshape(1, a), *unperms)


NEW_KEY_PAD = V7X_LANES


def _sample_bias(win, steps):
    slopes = _alibi_slopes()[None, :, None, None]
    t = np.arange(steps)[None, None, :, None]

    def table(dist, live):
        tabs = []
        for window, dil in DILATED_BRANCHES:
            ok = live & (dist >= 0) & (dist <= window) & (dist % dil == 0)
            tabs.append(np.where(ok, -slopes * dist.astype(np.float32), np.float32(MASKED))[0])
        return np.stack(tabs).astype(np.float32)

    pos = np.arange(win)[None, None, None, :]
    col = np.arange(NEW_KEY_PAD)[None, None, None, :]
    return table(win + t - pos, np.bool_(True)), table(t - col, col < steps)


def _attn_sample_kernel(q_ref, kn_ref, vn_ref, kt_ref, vt_ref, bo_ref, bn_ref, g_ref, o_ref):
    nb = bo_ref.shape[0]
    steps = q_ref.shape[1]
    nt = lambda p, v: lax.dot_general(p, v, (((1,), (1,)), ((), ())), preferred_element_type=F32)
    res = []
    sq = jnp.zeros((steps, 1), F32)
    for h in range(N_HEADS):
        q = q_ref[h]
        vt = vt_ref[h].astype(BF16)
        vn = vn_ref[h].astype(BF16)
        s_old = jnp.dot(q, kt_ref[h].astype(BF16), preferred_element_type=F32)
        s_new = jnp.dot(q, kn_ref[h].astype(BF16), preferred_element_type=F32)
        ms, ls, e_old, e_new = [], [], [], []
        for b in range(nb):
            so = s_old + bo_ref[b, h]
            sn = s_new + bn_ref[b, h]
            m = jnp.maximum(jnp.max(so, axis=1, keepdims=True), jnp.max(sn, axis=1, keepdims=True))
            eo = jnp.exp(so - m)
            en = jnp.exp(sn - m)
            ms.append(m)
            ls.append(jnp.sum(eo, axis=1, keepdims=True) + jnp.sum(en, axis=1, keepdims=True))
            e_old.append(eo)
            e_new.append(en)
        o_all = (nt(jnp.concatenate(e_old, axis=0).astype(BF16), vt)
                 + nt(jnp.concatenate(e_new, axis=0).astype(BF16), vn))
        top = functools.reduce(jnp.maximum, ms)
        num = jnp.zeros((steps, HEAD_DIM), F32)
        den = jnp.zeros((steps, 1), F32)
        for b in range(nb):
            w = jnp.exp(ms[b] - top)
            num = num + o_all[b * steps:(b + 1) * steps] * w
            den = den + ls[b] * w
        r = num * (1.0 / den)
        res.append(r)
        sq = sq + jnp.sum(r * r, axis=1, keepdims=True)
    inv = lax.rsqrt(sq * (1.0 / ATTN_WIDTH) + EPS)
    for h in range(N_HEADS):
        o_ref[h] = res[h] * inv * g_ref[h]


def _attn_sample(q, kn_t, vn_t, cache_kt, cache_vt, layer, g):
    b, heads, steps, dh = q.shape
    win = cache_kt.shape[-1]
    b_old, b_new = (jnp.asarray(z) for z in _sample_bias(win, steps))
    per_b = lambda *tail: pl.BlockSpec((None,) + tail, lambda i: (i,) + (0,) * len(tail))
    cache = pl.BlockSpec((None, None, heads, dh, win), lambda i: (layer, i, 0, 0, 0))
    return pl.pallas_call(
        _attn_sample_kernel,
        out_shape=jax.ShapeDtypeStruct((b, heads, steps, dh), F32),
        grid=(b,),
        in_specs=[per_b(heads, steps, dh), per_b(heads, dh, NEW_KEY_PAD), per_b(heads, dh, NEW_KEY_PAD),
                  cache, cache, _full(b_old.shape), _full(b_new.shape), _full((heads, 1, dh))],
        out_specs=per_b(heads, steps, dh),
        compiler_params=_cparams(),
        name="attn_sample",
    )(q, kn_t, vn_t, cache_kt, cache_vt, b_old, b_new, g.reshape(heads, 1, dh))


def _gelu_tanh(x):
    return 0.5 * x * (1.0 + jnp.tanh(np.sqrt(2.0 / np.pi).astype(np.float32) * (x + 0.044715 * (x * x * x))))


def _ssm_kernel(u_ref, bb_ref, lam_ref, pow_ref, cm_ref, dsk_ref, wglu_ref, bglu_ref, g_ref, pt_ref, h0_ref,
                z_ref, ht_ref, h_s, hin_s, carry_s, *, rows, steps, chain):
    ns = N_STATE
    cw = (4 * V7X_SUBLANES * V7X_LANES) // rows
    last = (steps - 1) * rows

    @pl.when(pl.program_id(0) == 0)
    def _():
        carry_s[...] = h0_ref[0:1, :]

    ub = u_ref[...].astype(BF16)
    n_slabs = SSM_WIDTH // V7X_LANES
    sw = ns // n_slabs
    for s in range(n_slabs):
        part = jnp.dot(ub[:, s * V7X_LANES:(s + 1) * V7X_LANES], bb_ref[s], preferred_element_type=F32)
        h_s[:, s * sw:(s + 1) * sw] = part[:, :sw]
        h_s[:, ns + s * sw:ns + (s + 1) * sw] = part[:, sw:]

    for cc in range(0, ns // cw, 2):
        crs = [slice(c * cw, (c + 1) * cw) for c in (cc, cc + 1)]
        cis = [slice(ns + c * cw, ns + (c + 1) * cw) for c in (cc, cc + 1)]
        lrs = [jnp.broadcast_to(lam_ref[0:1, cr], (rows, cw)) for cr in crs]
        lis = [jnp.broadcast_to(lam_ref[1:2, cr], (rows, cw)) for cr in crs]

        def scan_body(t, carry, crs=crs, cis=cis, lrs=lrs, lis=lis):
            rs = pl.ds(pl.multiple_of(t * rows, rows), rows)
            out = []
            for j in range(2):
                hr, hi = carry[2 * j], carry[2 * j + 1]
                nr = lrs[j] * hr - lis[j] * hi + h_s[rs, crs[j]]
                ni = lrs[j] * hi + lis[j] * hr + h_s[rs, cis[j]]
                h_s[rs, crs[j]] = nr
                h_s[rs, cis[j]] = ni
                out += [nr, ni]
            return tuple(out)

        zero = jnp.zeros((rows, cw), F32)
        lax.fori_loop(0, steps, scan_body, (zero,) * 4)

    if chain:
        ptr = pow_ref[steps - 1, 0:1, :ns]
        pti = pow_ref[steps - 1, 0:1, ns:]
        carry = carry_s[...]
        for c in range(rows):
            hin_s[c:c + 1, :] = carry
            cr_, ci_ = carry[:, :ns], carry[:, ns:]
            e = h_s[last + c:last + c + 1, :]
            carry = jnp.concatenate([ptr * cr_ - pti * ci_ + e[:, :ns],
                                     ptr * ci_ + pti * cr_ + e[:, ns:]], axis=1)
        carry_s[...] = carry
    else:
        hin_s[...] = h0_ref[...]

    for cc in range(ns // cw):
        cr = slice(cc * cw, (cc + 1) * cw)
        ci = slice(ns + cc * cw, ns + (cc + 1) * cw)
        hr0 = hin_s[:, cr]
        hi0 = hin_s[:, ci]

        def fix_body(t, _, cr=cr, ci=ci, hr0=hr0, hi0=hi0):
            rs = pl.ds(pl.multiple_of(t * rows, rows), rows)
            pr = pow_ref[t, :, cr]
            pi_ = pow_ref[t, :, ci]
            h_s[rs, cr] = h_s[rs, cr] + (pr * hr0 - pi_ * hi0)
            h_s[rs, ci] = h_s[rs, ci] + (pr * hi0 + pi_ * hr0)
            return 0

        lax.fori_loop(0, steps, fix_body, 0, unroll=8)

    ht_ref[...] = h_s[last:last + rows, :]
    ys = []
    for s in range(n_slabs):
        hs = jnp.concatenate([h_s[:, s * sw:(s + 1) * sw], h_s[:, ns + s * sw:ns + (s + 1) * sw]], axis=1)
        ys.append(jnp.dot(hs.astype(BF16), cm_ref[s], preferred_element_type=F32))
    y = jnp.concatenate(ys, axis=1) + dsk_ref[...] * u_ref[...]
    y = _gelu_tanh(y)
    gl = jnp.dot(y.astype(BF16), wglu_ref[...], preferred_element_type=F32) + bglu_ref[...]
    z = _rms(y * _sigmoid(gl), g_ref[...]).astype(BF16)
    z_ref[...] = jnp.dot(pt_ref[...], z, preferred_element_type=F32).astype(z_ref.dtype)


def _ssm_tables(ssm_a_re, ssm_a_im, ssm_log_dt, ssm_b_re, ssm_b_im, ssm_c_re, ssm_c_im, max_steps):
    g, n, ch = SSM_GROUPS, SSM_STATE, SSM_CH
    a_re = ssm_a_re.astype(F32)
    a_im = ssm_a_im.astype(F32)
    dt = jnp.exp(ssm_log_dt.astype(F32))[:, None]
    mag = jnp.exp(dt * a_re)
    lam_re = mag * jnp.cos(dt * a_im)
    lam_im = mag * jnp.sin(dt * a_im)
    nr = lam_re - 1.0
    ni = lam_im
    inv = 1.0 / (a_re * a_re + a_im * a_im)
    coef_re = (nr * a_re + ni * a_im) * inv
    coef_im = (ni * a_re - nr * a_im) * inv
    br = ssm_b_re.astype(F32)
    bi = ssm_b_im.astype(F32)
    bb_re = coef_re[..., None] * br - coef_im[..., None] * bi
    bb_im = coef_re[..., None] * bi + coef_im[..., None] * br
    gs = V7X_LANES // ch
    ns_ = g // gs
    eye = jnp.eye(gs, dtype=F32)
    bmat = lambda b: jnp.einsum("sgnc,gh->sgchn", b.reshape(ns_, gs, n, ch), eye).reshape(ns_, gs * ch, gs * n)
    cmat = lambda c: jnp.einsum("sgcn,gh->sgnhc", c.astype(F32).reshape(ns_, gs, ch, n), eye).reshape(
        ns_, gs * n, gs * ch)
    bb = jnp.concatenate([bmat(bb_re), bmat(bb_im)], axis=2).astype(BF16)
    cm = jnp.concatenate([cmat(ssm_c_re), -cmat(ssm_c_im)], axis=1).astype(BF16)
    lam = jnp.stack([lam_re.reshape(-1), lam_im.reshape(-1)])

    k = jnp.arange(1, max_steps + 1, dtype=F32)[:, None]
    kdt = k * dt.reshape(1, -1).repeat(n, axis=1)
    pmag = jnp.exp(kdt * a_re.reshape(1, -1))
    parg = kdt * a_im.reshape(1, -1)
    pows = jnp.concatenate([pmag * jnp.cos(parg), pmag * jnp.sin(parg)], axis=1)
    return bb, cm, lam, pows


def _ssm(u_perm, h0, tabs, dsk, w_glu, b_glu, g, perm_t, rows, steps, chain):
    n, w = u_perm.shape
    blk = rows * steps
    bb, cm, lam, pows = tabs
    row = pl.BlockSpec((blk, w), lambda i: (i, 0))
    kern = functools.partial(_ssm_kernel, rows=rows, steps=steps, chain=chain)
    return pl.pallas_call(
        kern,
        out_shape=(jax.ShapeDtypeStruct((n, w), BF16), jax.ShapeDtypeStruct((rows, 2 * N_STATE), F32)),
        grid=(n // blk,),
        in_specs=[row, _full(bb.shape), _full(lam.shape), _full((steps, rows, 2 * N_STATE)), _full(cm.shape),
                  _full((1, w)), _full((w, w)), _full((1, w)), _full((1, w)), _full((blk, blk)),
                  _full((rows, 2 * N_STATE))],
        out_specs=(row, _full((rows, 2 * N_STATE))),
        scratch_shapes=[pltpu.VMEM((blk, 2 * N_STATE), F32),
                        pltpu.VMEM((rows, 2 * N_STATE), F32),
                        pltpu.VMEM((1, 2 * N_STATE), F32)],
        compiler_params=_cparams(),
        name=f"ssm_r{rows}",
    )(u_perm, bb, lam, jnp.broadcast_to(pows[:steps, None, :], (steps, rows, 2 * N_STATE)), cm,
      dsk.reshape(1, w), w_glu.astype(BF16), b_glu.reshape(1, w),
      g.reshape(1, w), perm_t, h0)


def _outproj_kernel(*refs, n_real, aliased):
    ins, outs = refs[:12], refs[12 + aliased:]
    step = pl.program_id(0)

    @pl.when(step < n_real)
    def _():
        _outproj_tile(*ins, *outs)

    @pl.when(step >= n_real)
    def _():
        xs_ref = outs[1]
        xs_ref[...] = jnp.zeros(xs_ref.shape, xs_ref.dtype)


def _outproj_tile(x_ref, a_ref, z_ref, gt_ref, sh_ref, sc_ref, g_ref, wo_ref, wr_ref, br_ref, ltri_ref, utri_ref,
                  x1_ref, xs_ref, slot_ref, gates_ref, meta_ref, wbf_ref):
    @pl.when(pl.program_id(0) == 0)
    def _():
        wbf_ref[...] = wo_ref[...].astype(BF16)

    a = ATTN_WIDTH
    mixed = (jnp.dot(a_ref[...].astype(BF16), wbf_ref[:a, :], preferred_element_type=F32)
             + jnp.dot(z_ref[...], wbf_ref[a:, :], preferred_element_type=F32))
    x1 = x_ref[...] + gt_ref[...] * mixed
    x1_ref[...] = x1
    h2 = (_rms(x1, g_ref[...]) * (1.0 + sc_ref[...]) + sh_ref[...]).astype(BF16)
    lg = jnp.dot(h2, wr_ref[...], preferred_element_type=F32) + br_ref[...]
    tm = lg.shape[0]
    lane = lax.broadcasted_iota(jnp.int32, (tm, V7X_LANES), 1).astype(F32)
    vals, hots = [], []
    for _ in range(TOP_K):
        m = jnp.max(lg, axis=1, keepdims=True)
        idx = jnp.min(jnp.where(lg == m, lane, float(V7X_LANES)), axis=1, keepdims=True)
        hot = lane == idx
        vals.append(m)
        hots.append(jnp.where(hot, 1.0, 0.0))
        lg = jnp.where(hot, MASKED * 2, lg)
    es = [jnp.exp(v - vals[0]) for v in vals]
    inv = 1.0 / (es[0] + es[1] + es[2] + es[3])

    member = hots[0] + hots[1] + hots[2] + hots[3]
    before = jnp.dot(ltri_ref[...], member.astype(BF16), preferred_element_type=F32)
    count = jnp.sum(member, axis=0, keepdims=True)
    padded = jnp.floor((count + (ROW_ALIGN - 1.0)) * (1.0 / ROW_ALIGN)) * ROW_ALIGN
    padded8 = jnp.broadcast_to(padded, (V7X_SUBLANES, V7X_LANES))
    start = jnp.dot(padded8.astype(BF16), utri_ref[...], preferred_element_type=F32)[0:1]
    where_to = start + before
    slots = jnp.zeros((tm, V7X_LANES), F32)
    gates = jnp.zeros((tm, V7X_LANES), F32)
    for k in range(TOP_K):
        slot_k = jnp.sum(hots[k] * where_to, axis=1, keepdims=True)
        slots = jnp.where(lane == float(k), slot_k, slots)
        gates = jnp.where(lane == float(k), es[k] * inv, gates)
    slot_ref[...] = slots
    gates_ref[...] = gates
    row = lax.broadcasted_iota(jnp.int32, (V7X_SUBLANES, V7X_LANES), 0)
    meta_ref[...] = jnp.where(row == 0, padded8, jnp.where(row == 1, jnp.broadcast_to(start, padded8.shape), 0.0))

    cap = xs_ref.shape[0]
    slots_t = jnp.transpose(slots).astype(jnp.int32).astype(jnp.int16)
    srow = lax.broadcasted_iota(jnp.int16, (cap, tm), 0)
    place = jnp.zeros((cap, tm), BF16)
    for k in range(TOP_K):
        place = jnp.where(srow == slots_t[k:k + 1, :], jnp.ones((), BF16), place)
    xs_ref[...] = jnp.dot(place, h2, preferred_element_type=F32).astype(BF16)


def _outproj(x, a, z, gt, sh, sc, g, w_out, wr_pad, br_pad, tm, n_blocks, block0, xs_prev=None):
    n, d = x.shape
    aw = ATTN_WIDTH
    n_real = n // tm
    aliased = xs_prev is not None
    n_steps = n_real if aliased else n_blocks
    tile = lambda i: jnp.minimum(i, n_real - 1)
    mod_spec = (pl.BlockSpec((1, d), lambda i: (0, 0)) if gt.shape[0] == 1
                else pl.BlockSpec((tm, d), lambda i: (tile(i), 0)))
    row = lambda w: pl.BlockSpec((tm, w), lambda i: (tile(i), 0))
    ltri = jnp.asarray(np.tril(np.ones((tm, tm), np.float32), -1), BF16)
    utri = jnp.asarray(np.triu(np.ones((V7X_LANES, V7X_LANES), np.float32), 1), BF16)
    in_specs = [row(d), row(aw), row(SSM_WIDTH), mod_spec, mod_spec, mod_spec, _full((1, d)),
                _full(w_out.shape), _full(wr_pad.shape), _full(br_pad.shape), _full(ltri.shape), _full(utri.shape)]
    args = [x, a, z, gt, sh, sc, g.reshape(1, d), w_out, wr_pad, br_pad, ltri, utri]
    if aliased:
        in_specs.append(pl.BlockSpec(memory_space=pl.ANY))
        args.append(xs_prev)
    return pl.pallas_call(
        functools.partial(_outproj_kernel, n_real=n_real, aliased=int(aliased)),
        out_shape=(jax.ShapeDtypeStruct((n, d), F32),
                   jax.ShapeDtypeStruct((n_blocks, GROUP_CAP, d), BF16),
                   jax.ShapeDtypeStruct((n, V7X_LANES), F32), jax.ShapeDtypeStruct((n, V7X_LANES), F32),
                   jax.ShapeDtypeStruct((n_real, V7X_SUBLANES, V7X_LANES), F32)),
        grid=(n_steps,),
        in_specs=in_specs,
        out_specs=(row(d), pl.BlockSpec((None, GROUP_CAP, d), lambda i: (block0 + i, 0, 0)),
                   row(V7X_LANES), row(V7X_LANES),
                   pl.BlockSpec((None, V7X_SUBLANES, V7X_LANES), lambda i: (tile(i), 0, 0))),
        scratch_shapes=[pltpu.VMEM(w_out.shape, BF16)],
        input_output_aliases={len(args) - 1: 1} if aliased else {},
        compiler_params=_cparams(),
        name="outproj_router",
    )(*args)


_PIECE_SIZES = tuple(1 << s for s in range(MOE_TILE.bit_length() - 1, ROW_ALIGN.bit_length() - 2, -1))
_CHUNK_SHIFT = 6
_CHUNK = 1 << _CHUNK_SHIFT
_TAIL_SIZES = tuple(sz for sz in _PIECE_SIZES if sz < _CHUNK)


def _moe_kernel(te_ref, tl_ref, lo_ref, hi_ref, nu_ref, gstart_ref, gsize_ref, gbase_ref, rows_ref, used_ref,
                nxt_ref, xs_hbm, wgu_hbm, bgu_ref, wd_hbm, bd_ref, ys_hbm,
                xbuf, ybuf, zbuf, wgu_f32, wd_f32, wgu_bf, wd_bf, xsem, ysem, zsem, wsem):
    t = pl.program_id(0)
    n_used = nu_ref[0]
    n_blocks, cap = xs_hbm.shape[0], xs_hbm.shape[1]

    def weight_copies(e):
        return (pltpu.make_async_copy(wgu_hbm.at[e], wgu_f32, wsem.at[0]),
                pltpu.make_async_copy(wd_hbm.at[e], wd_f32, wsem.at[1]))

    def x_copy(i, br, tr, sz, slot):
        pltpu.make_async_copy(xs_hbm.at[i, pl.ds(br, sz)], xbuf.at[slot, pl.ds(tr, sz)], xsem.at[slot]).start()

    def y_copy(i, br, tr, sz, slot):
        pltpu.make_async_copy(ybuf.at[slot, pl.ds(tr, sz)], ys_hbm.at[i, pl.ds(br, sz)], ysem.at[slot]).start()

    def z_copy(i, row, sz, start):
        cp = pltpu.make_async_copy(zbuf.at[pl.ds(0, sz)], ys_hbm.at[i, pl.ds(row, sz)], zsem)
        cp.start() if start else cp.wait()

    def pieces(tt, fn):
        e = te_ref[tt]
        lo = tl_ref[tt] * MOE_TILE

        def per_block(i, c):
            g = e * n_blocks + i
            s0 = gbase_ref[g]
            a = jnp.maximum(s0, lo)
            b = jnp.minimum(s0 + gsize_ref[g], lo + MOE_TILE)
            length = jnp.maximum(b - a, 0)
            src = gstart_ref[g] + (a - s0)
            dst = a - lo
            whole = lax.shift_right_logical(length, _CHUNK_SHIFT)

            def chunk(j, cc):
                off = j * _CHUNK
                fn(i, pl.multiple_of(src + off, ROW_ALIGN), pl.multiple_of(dst + off, ROW_ALIGN), _CHUNK)
                return cc

            lax.fori_loop(0, whole, chunk, 0)
            done = whole * _CHUNK
            for sz in _TAIL_SIZES:
                hit = (length & sz) != 0

                @pl.when(hit)
                def _(sz=sz, done=done):
                    fn(i, pl.multiple_of(src + done, ROW_ALIGN), pl.multiple_of(dst + done, ROW_ALIGN), sz)

                done = done + jnp.where(hit, sz, 0)
            return c

        lax.fori_loop(lo_ref[tt], hi_ref[tt], per_block, 0)

    def tile_rows(tt):
        return jnp.minimum(rows_ref[te_ref[tt]] - tl_ref[tt] * MOE_TILE, MOE_TILE)

    def wait_rows(n, sem, buf):
        def chunk(j, cc):
            pltpu.make_async_copy(buf.at[pl.ds(0, _CHUNK)], buf.at[pl.ds(0, _CHUNK)], sem).wait()
            return cc

        lax.fori_loop(0, lax.shift_right_logical(n, _CHUNK_SHIFT), chunk, 0)
        for sz in _TAIL_SIZES:
            @pl.when((n & sz) != 0)
            def _(sz=sz):
                pltpu.make_async_copy(buf.at[pl.ds(0, sz)], buf.at[pl.ds(0, sz)], sem).wait()

    def fetch(tt, slot):
        pieces(tt, lambda i, br, tr, sz: x_copy(i, br, tr, sz, slot))

    def writeback(tt, slot):
        pieces(tt, lambda i, br, tr, sz: y_copy(i, br, tr, sz, slot))

    def zero_tail(i, start):
        u = used_ref[i]
        rem = cap - u
        nz = zbuf.shape[0]
        whole = lax.shift_right_logical(rem, nz.bit_length() - 1)

        def chunk(j, c):
            z_copy(i, pl.multiple_of(u + j * nz, ROW_ALIGN), nz, start)
            return c

        lax.fori_loop(0, whole, chunk, 0)
        base = u + whole * nz
        done = jnp.int32(0)
        for sz in _PIECE_SIZES:
            if sz >= nz:
                continue
            hit = (rem & sz) != 0

            @pl.when(hit)
            def _(sz=sz, done=done):
                z_copy(i, pl.multiple_of(base + done, ROW_ALIGN), sz, start)

            done = done + jnp.where(hit, sz, 0)

    @pl.when(t == 0)
    def _():
        xbuf[...] = jnp.zeros(xbuf.shape, xbuf.dtype)
        zbuf[...] = jnp.zeros(zbuf.shape, zbuf.dtype)
        lax.fori_loop(0, n_blocks, lambda i, c: (zero_tail(i, True), c)[1], 0)
        lax.fori_loop(0, n_blocks, lambda i, c: (zero_tail(i, False), c)[1], 0)
        fetch(0, 0)
        for cp in weight_copies(te_ref[0]):
            cp.start()

    @pl.when(t < n_used)
    def _():
        slot = t % 2

        @pl.when(t + 1 < n_used)
        def _():
            fetch(t + 1, 1 - slot)

        @pl.when(tl_ref[t] == 0)
        def _():
            for cp in weight_copies(te_ref[t]):
                cp.wait()
            wgu_bf[...] = wgu_f32[...].astype(BF16)
            wd_bf[...] = wd_f32[...].astype(BF16)
            nxt = nxt_ref[te_ref[t]]

            @pl.when(nxt >= 0)
            def _():
                for cp in weight_copies(nxt):
                    cp.start()

        wait_rows(tile_rows(t), xsem.at[slot], xbuf.at[slot])

        @pl.when(t >= 2)
        def _():
            wait_rows(tile_rows(t - 2), ysem.at[slot], ybuf.at[slot])

        def expert_mlp(m):
            f = D_EXPERT
            gu = jnp.dot(xbuf[slot, pl.ds(0, m)], wgu_bf[...], preferred_element_type=F32) + bgu_ref[...]
            gate = jnp.minimum(gu[:, :f], SWIGLU_LIMIT)
            up = jnp.clip(gu[:, f:], -SWIGLU_LIMIT, SWIGLU_LIMIT)
            act = (up + 1.0) * gate * _sigmoid(SWIGLU_ALPHA * gate)
            y = jnp.dot(act.astype(BF16), wd_bf[...], preferred_element_type=F32) + bd_ref[...]
            ybuf[slot, pl.ds(0, m)] = y.astype(BF16)

        n_steps = lax.shift_right_logical(tile_rows(t) + (MOE_ROW_STEP - 1), MOE_ROW_STEP.bit_length() - 1)
        for ns in range(1, MOE_TILE // MOE_ROW_STEP + 1):
            @pl.when(n_steps == ns)
            def _(ns=ns):
                expert_mlp(ns * MOE_ROW_STEP)

        writeback(t, slot)

        @pl.when(t == n_used - 1)
        def _():
            wait_rows(tile_rows(t), ysem.at[slot], ybuf.at[slot])

            @pl.when(t >= 1)
            def _():
                wait_rows(tile_rows(t - 1), ysem.at[1 - slot], ybuf.at[1 - slot])


def _moe(plan, xs, w_gate_up, b_gate_up, w_down, b_down):
    _, cap, d = xs.shape
    e, _, f2 = w_gate_up.shape
    nt = plan[0].shape[0]
    wmap = lambda t, te, *_: (te[t], 0, 0)
    anyspec = pl.BlockSpec(memory_space=pl.ANY)
    grid_spec = pltpu.PrefetchScalarGridSpec(
        num_scalar_prefetch=len(plan),
        grid=(nt,),
        in_specs=[anyspec,
                  anyspec,
                  pl.BlockSpec((None, 1, f2), wmap),
                  anyspec,
                  pl.BlockSpec((None, 1, d), wmap)],
        out_specs=anyspec,
        scratch_shapes=[pltpu.VMEM((2, MOE_TILE, d), BF16), pltpu.VMEM((2, MOE_TILE, d), BF16),
                        pltpu.VMEM((MOE_TILE // 2, d), BF16),
                        pltpu.VMEM((d, f2), F32), pltpu.VMEM((f2 // 2, d), F32),
                        pltpu.VMEM((d, f2), BF16), pltpu.VMEM((f2 // 2, d), BF16),
                        pltpu.SemaphoreType.DMA((2,)), pltpu.SemaphoreType.DMA((2,)), pltpu.SemaphoreType.DMA(()),
                        pltpu.SemaphoreType.DMA((2,))],
    )
    return pl.pallas_call(
        _moe_kernel,
        out_shape=jax.ShapeDtypeStruct(xs.shape, BF16),
        grid_spec=grid_spec,
        compiler_params=_cparams(),
        name="moe_experts",
    )(*plan, xs, w_gate_up, b_gate_up.reshape(e, 1, f2), w_down, b_down.reshape(e, 1, d))


def _moe_plan(group_size, group_start):
    n_blocks = group_size.shape[0]
    gsize = group_size.T.astype(jnp.int32)
    gstart = group_start.T.astype(jnp.int32)
    gbase = jnp.cumsum(gsize, axis=1) - gsize
    rows = jnp.sum(gsize, axis=1)
    tiles = (rows + MOE_TILE - 1) // MOE_TILE
    tile_end = jnp.cumsum(tiles)
    n_used = tile_end[-1:]
    nt = (n_blocks * GROUP_CAP) // MOE_TILE + N_EXPERTS
    t = jnp.arange(nt, dtype=jnp.int32)
    te = jnp.sum((tile_end[None, :] <= t[:, None]).astype(jnp.int32), axis=1)
    last = jnp.max(jnp.where(tiles > 0, jnp.arange(N_EXPERTS, dtype=jnp.int32), 0))
    te = jnp.where(t < n_used[0], jnp.minimum(te, N_EXPERTS - 1), last)
    hot = (te[:, None] == jnp.arange(N_EXPERTS, dtype=jnp.int32)[None, :]).astype(jnp.int32)
    tl = jnp.where(t < n_used[0], t - hot @ (tile_end - tiles), 0)
    lo_row = tl * MOE_TILE
    base_t = hot @ gbase
    size_t = hot @ gsize
    first = jnp.sum((base_t + size_t <= lo_row[:, None]).astype(jnp.int32), axis=1)
    stop = jnp.sum((base_t < lo_row[:, None] + MOE_TILE).astype(jnp.int32), axis=1)
    used = jnp.sum(gsize, axis=0)
    ids = jnp.arange(N_EXPERTS, dtype=jnp.int32)
    later = (ids[None, :] > ids[:, None]) & (tiles[None, :] > 0)
    nxt = jnp.min(jnp.where(later, ids[None, :], N_EXPERTS), axis=1)
    nxt = jnp.where(nxt < N_EXPERTS, nxt, -1)
    i32 = lambda z: z.astype(jnp.int32)
    return (i32(te), i32(tl), i32(first), i32(stop), i32(n_used), i32(gstart.reshape(-1)), i32(gsize.reshape(-1)),
            i32(gbase.reshape(-1)), i32(rows), i32(used), i32(nxt))


def _final_kernel(x_ref, ys_ref, slot_ref, gates_ref, gt_ref, g_ref, y_ref):
    tm = x_ref.shape[0]
    cap = ys_ref.shape[0]
    col = lax.broadcasted_iota(jnp.int16, (tm, cap), 1)
    slots = slot_ref[...].astype(jnp.int32).astype(jnp.int16)
    gates = gates_ref[...].astype(BF16)
    mix = jnp.zeros((tm, cap), BF16)
    for k in range(TOP_K):
        mix = jnp.where(col == slots[:, k:k + 1], gates[:, k:k + 1], mix)
    ff = jnp.dot(mix, ys_ref[...], preferred_element_type=F32)
    y_ref[...] = _rms(x_ref[...] + gt_ref[...] * ff, g_ref[...])


def _final(x1, ys, slots, gates, gt, g, tm, block0):
    n, d = x1.shape
    cap = ys.shape[1]
    mod_spec = (pl.BlockSpec((1, d), lambda i: (0, 0)) if gt.shape[0] == 1
                else pl.BlockSpec((tm, d), lambda i: (i, 0)))
    row = lambda w: pl.BlockSpec((tm, w), lambda i: (i, 0))
    return pl.pallas_call(
        _final_kernel,
        out_shape=jax.ShapeDtypeStruct((n, d), F32),
        grid=(n // tm,),
        in_specs=[row(d), pl.BlockSpec((None, cap, d), lambda i: (block0 + i, 0, 0)),
                  row(V7X_LANES), row(V7X_LANES), mod_spec, _full((1, d))],
        out_specs=row(d),
        compiler_params=_cparams(),
        name="final_norm",
    )(x1, ys, slots, gates, gt, g.reshape(1, d))


def kernel(x_prompt, x_sample, cache_k_win, cache_v_win, state_ssm_re, state_ssm_im, c_prompt, c_sample,
           w_ada, b_ada, g_norm1, w_in, ssm_a_re, ssm_a_im, ssm_log_dt, ssm_b_re, ssm_b_im, ssm_c_re, ssm_c_im,
           ssm_d, w_glu, b_glu, g_out_attn, g_out_ssm, w_out, g_norm2, w_router, b_router, w_gate_up, b_gate_up,
           w_down, b_down, g_final):
    depth = w_ada.shape[0]
    assert depth == 1 and x_prompt.shape[0] == 1
    bp, t, d = x_prompt.shape
    bs, ts, _ = x_sample.shape
    ns = bs * ts
    l = 0
    a = ATTN_WIDTH

    n_c = bp + bs
    c_pad = -(-n_c // V7X_SUBLANES) * V7X_SUBLANES
    c_rows = jnp.concatenate([c_prompt, c_sample, jnp.zeros((c_pad - n_c, d), F32)], axis=0)
    mod = _ada_modulation(c_rows, w_ada[l], b_ada[l])
    mod_p = [mod[0:1, i * d:(i + 1) * d] for i in range(N_MOD)]
    mod_s = [jnp.repeat(mod[bp:bp + bs, i * d:(i + 1) * d], ts, axis=0) for i in range(N_MOD)]

    steps_p = TOKEN_TILE // SSM_ROWS_PROMPT
    perm_p = _chunk_perm(SSM_ROWS_PROMPT, steps_p)
    perm_s = _chunk_perm(bs, ts)
    tabs = _ssm_tables(ssm_a_re[l], ssm_a_im[l], ssm_log_dt[l], ssm_b_re[l], ssm_b_im[l],
                       ssm_c_re[l], ssm_c_im[l], max(steps_p, ts))
    wr_pad = jnp.zeros((d, V7X_LANES), F32).at[:, :N_EXPERTS].set(w_router[l]).astype(BF16)
    br_pad = jnp.full((1, V7X_LANES), MASKED, F32).at[0, :N_EXPERTS].set(b_router[l])

    xp = x_prompt.reshape(t, d)
    dils = tuple(dil for _, dil in DILATED_BRANCHES)
    wide = tuple(dil for dil in dils if dil > 1)
    proj_p = _inproj(xp, mod_p[0], mod_p[1], g_norm1[l], w_in[l], jnp.asarray(perm_p, BF16), TOKEN_TILE, wide)
    kpf, vpf, up = proj_p[3:6]
    views = {1: proj_p[0:3]}
    for n_d, dil in enumerate(wide):
        views[dil] = proj_p[6 + 3 * n_d:9 + 3 * n_d]
    outs = [_attn_branch(*views[dil], dil) for dil in dils]
    ap = _attn_combine([o for o, _ in outs], [s for _, s in outs], g_out_attn[l], dils)
    zeros_h = jnp.zeros((SSM_ROWS_PROMPT, 2 * N_STATE), F32)
    zp, hp = _ssm(up, zeros_h, tabs, ssm_d[l], w_glu[l], b_glu[l], g_out_ssm[l],
                  jnp.asarray(perm_p.T, BF16), SSM_ROWS_PROMPT, steps_p, True)
    n_blocks_p = t // TOKEN_TILE
    n_blocks = n_blocks_p + 1
    x1p, xs_all, slot_p, gate_p, meta_p = _outproj(xp, ap, zp, mod_p[2], mod_p[3], mod_p[4], g_norm2[l], w_out[l],
                                                   wr_pad, br_pad, TOKEN_TILE, n_blocks, 0)

    xs = x_sample.reshape(ns, d)
    qs, ks, vs, ksf, vsf, us = _inproj(xs, mod_s[0], mod_s[1], g_norm1[l], w_in[l], jnp.asarray(perm_s, BF16), ns)
    split = lambda z: z.reshape(bs, ts, N_HEADS, HEAD_DIM)
    new_t = lambda z: jnp.pad(split(z).transpose(0, 2, 3, 1), ((0, 0), (0, 0), (0, 0), (0, NEW_KEY_PAD - ts)))
    as_ = _attn_sample(split(qs).transpose(0, 2, 1, 3), new_t(ks), new_t(vs),
                       cache_k_win.transpose(0, 1, 3, 4, 2), cache_v_win.transpose(0, 1, 3, 4, 2), l, g_out_attn[l])
    as_ = as_.transpose(0, 2, 1, 3)
    h0s = jnp.concatenate([state_ssm_re[l].reshape(bs, N_STATE), state_ssm_im[l].reshape(bs, N_STATE)], axis=1)
    zs, hs = _ssm(us, h0s, tabs, ssm_d[l], w_glu[l], b_glu[l], g_out_ssm[l],
                  jnp.asarray(perm_s.T, BF16), bs, ts, False)
    x1s, xs_all, slot_s, gate_s, meta_s = _outproj(xs, as_.reshape(ns, a), zs, mod_s[2], mod_s[3], mod_s[4],
                                                   g_norm2[l], w_out[l], wr_pad, br_pad, ns, n_blocks, n_blocks_p,
                                                   xs_prev=xs_all)

    meta = jnp.concatenate([meta_p, meta_s], axis=0)
    plan = _moe_plan(meta[:, 0, :N_EXPERTS], meta[:, 1, :N_EXPERTS])
    ys_all = _moe(plan, xs_all, w_gate_up[l], b_gate_up[l], w_down[l], b_down[l])

    y_prompt = _final(x1p, ys_all, slot_p, gate_p, mod_p[5], g_final, TOKEN_TILE, 0).reshape(bp, t, d)
    y_sample = _final(x1s, ys_all, slot_s, gate_s, mod_s[5], g_final, ns, n_blocks_p).reshape(bs, ts, d)

    keep = min(MAX_WINDOW, t)
    shp = (1, bp, keep, N_HEADS, HEAD_DIM)
    k_win = kpf[t - keep:].reshape(shp)
    v_win = vpf[t - keep:].reshape(shp)
    st = (1, bp, SSM_GROUPS, SSM_STATE)
    hp_last = hp[SSM_ROWS_PROMPT - 1]
    ss = (1, bs, SSM_GROUPS, SSM_STATE)
    return (y_prompt, y_sample, k_win, v_win,
            hp_last[:N_STATE].reshape(st), hp_last[N_STATE:].reshape(st),
            ksf.reshape(1, bs, ts, N_HEADS, HEAD_DIM), vsf.reshape(1, bs, ts, N_HEADS, HEAD_DIM),
            hs[:, :N_STATE].reshape(ss), hs[:, N_STATE:].reshape(ss))
```

```python
import functools

import numpy as np
import jax
import jax.numpy as jnp
from jax import lax
from jax.experimental import pallas as pl
from jax.experimental.pallas import tpu as pltpu

F32 = jnp.float32
BF16 = jnp.bfloat16

D_MODEL = 1024
N_HEADS = 8
HEAD_DIM = 64
ATTN_WIDTH = N_HEADS * HEAD_DIM
DILATED_BRANCHES = ((128, 1), (512, 4), (2048, 16))
MAX_WINDOW = 2048
SSM_WIDTH = D_MODEL - ATTN_WIDTH
SSM_CH = 16
SSM_GROUPS = SSM_WIDTH // SSM_CH
SSM_STATE = 64
N_STATE = SSM_GROUPS * SSM_STATE
N_EXPERTS = 32
TOP_K = 4
D_EXPERT = D_MODEL
SWIGLU_LIMIT = 7.0
SWIGLU_ALPHA = 1.702
N_MOD = 6
EPS = 1e-6
MASKED = -1e30

V7X_LANES = 128
V7X_SUBLANES = 8
V7X_VMEM_LIMIT_BYTES = 56 * 1024 * 1024

TOKEN_TILE = 512
Q_TILE = 128
Q_BLOCKS_PER_STEP = 16
SSM_ROWS_PROMPT = 8
MOE_TILE = 1024
MOE_ROW_STEP = 128
ROW_ALIGN = 16
GROUP_CAP = -(-(TOKEN_TILE * TOP_K + N_EXPERTS * (ROW_ALIGN - 1)) // (2 * V7X_LANES)) * (2 * V7X_LANES)


def _cparams(n_axes=1):
    return pltpu.CompilerParams(
        dimension_semantics=("arbitrary",) * n_axes,
        vmem_limit_bytes=V7X_VMEM_LIMIT_BYTES,
    )


def _full(shape):
    n = len(shape)
    return pl.BlockSpec(shape, lambda *_: (0,) * n)


def _rms(x, g):
    return x * lax.rsqrt(jnp.mean(x * x, axis=-1, keepdims=True) + EPS) * g


def _sigmoid(x):
    return 1.0 / (1.0 + jnp.exp(-x))


def _ada_kernel(c_ref, w_ref, b_ref, o_ref):
    c = c_ref[...]
    s = (c * _sigmoid(c)).astype(BF16)
    o_ref[...] = jnp.dot(s, w_ref[...].astype(BF16), preferred_element_type=F32) + b_ref[...]


def _ada_modulation(c_rows, w_ada, b_ada):
    m, d = c_rows.shape
    n = w_ada.shape[1]
    tn = n // 4
    return pl.pallas_call(
        _ada_kernel,
        out_shape=jax.ShapeDtypeStruct((m, n), F32),
        grid=(n // tn,),
        in_specs=[_full((m, d)),
                  pl.BlockSpec((d, tn), lambda j: (0, j)),
                  pl.BlockSpec((1, tn), lambda j: (0, j))],
        out_specs=pl.BlockSpec((m, tn), lambda j: (0, j)),
        compiler_params=_cparams(),
        name="ada_modulation",
    )(c_rows, w_ada, b_ada.reshape(1, n))


def _inproj_kernel(*refs, dils):
    x_ref, sh_ref, sc_ref, g_ref, w_ref, perm_ref = refs[:6]
    dperm_refs = refs[6:6 + len(dils)]
    q_ref, k_ref, v_ref, kf_ref, vf_ref, u_ref = refs[6 + len(dils):12 + len(dils)]
    dil_refs = refs[12 + len(dils):-1]
    wbf_ref = refs[-1]

    @pl.when(pl.program_id(0) == 0)
    def _():
        wbf_ref[...] = w_ref[...].astype(BF16)

    h = _rms(x_ref[...], g_ref[...]) * (1.0 + sc_ref[...]) + sh_ref[...]
    hb = h.astype(BF16)
    a = ATTN_WIDTH
    proj = jnp.dot(hb, wbf_ref[:, :3 * a], preferred_element_type=F32)
    k = proj[:, a:2 * a]
    v = proj[:, 2 * a:]
    qkv = jnp.concatenate([(proj[:, :a] * (HEAD_DIM ** -0.5)).astype(BF16), k.astype(BF16), v.astype(BF16)],
                          axis=1)
    q_ref[...] = qkv[:, :a]
    k_ref[...] = qkv[:, a:2 * a]
    v_ref[...] = qkv[:, 2 * a:]
    kf_ref[...] = k
    vf_ref[...] = v
    hp = jnp.dot(perm_ref[...], hb, preferred_element_type=F32).astype(BF16)
    u_ref[...] = jnp.dot(hp, wbf_ref[:, 3 * a:], preferred_element_type=F32)
    tm = qkv.shape[0]
    for n_d, dil in enumerate(dils):
        by_residue = jnp.dot(dperm_refs[n_d][...], qkv, preferred_element_type=F32).astype(BF16)
        per = tm // dil
        for r in range(dil):
            rows = by_residue[r * per:(r + 1) * per]
            for j in range(3):
                dil_refs[3 * n_d + j][:, r * a:(r + 1) * a] = rows[:, j * a:(j + 1) * a]


def _inproj(x, sh, sc, g, w_in, perm, tm, dils=()):
    n, d = x.shape
    a = ATTN_WIDTH
    mod_rows = sh.shape[0]
    mod_spec = (pl.BlockSpec((1, d), lambda i: (0, 0)) if mod_rows == 1
                else pl.BlockSpec((tm, d), lambda i: (i, 0)))
    row = lambda w: pl.BlockSpec((tm, w), lambda i: (i, 0))
    dperms = [jnp.asarray(_chunk_perm(tm // dil, dil), BF16) for dil in dils]
    view_shapes = tuple(jax.ShapeDtypeStruct((n // dil, dil * a), BF16) for dil in dils for _ in range(3))
    view_specs = tuple(pl.BlockSpec((tm // dil, dil * a), lambda i: (i, 0)) for dil in dils for _ in range(3))
    return pl.pallas_call(
        functools.partial(_inproj_kernel, dils=tuple(dils)),
        out_shape=(jax.ShapeDtypeStruct((n, a), BF16),) * 3
        + (jax.ShapeDtypeStruct((n, a), F32),) * 2
        + (jax.ShapeDtypeStruct((n, SSM_WIDTH), F32),) + view_shapes,
        grid=(n // tm,),
        in_specs=[row(d), mod_spec, mod_spec, _full((1, d)), _full(w_in.shape), _full((tm, tm))]
        + [_full((tm, tm))] * len(dils),
        out_specs=(row(a),) * 5 + (row(SSM_WIDTH),) + view_specs,
        scratch_shapes=[pltpu.VMEM(w_in.shape, BF16)],
        compiler_params=_cparams(),
        name="inproj",
    )(x, sh, sc, g.reshape(1, d), w_in, perm, *dperms)


def _chunk_perm(rows, steps):
    n = rows * steps
    p = np.zeros((n, n), np.float32)
    c, t = np.meshgrid(np.arange(rows), np.arange(steps), indexing="ij")
    p[(t * rows + c).ravel(), (c * steps + t).ravel()] = 1.0
    return p


def _alibi_slopes():
    return np.exp2(-8.0 * np.arange(1, N_HEADS + 1, dtype=np.float64) / N_HEADS).astype(np.float32)


def _branch_bias(dil):
    qi = np.arange(Q_TILE)[:, None]
    col = np.arange(2 * Q_TILE)[None, :]
    j = Q_TILE + qi - col
    valid = (j >= 0) & (j <= Q_TILE)
    dist = (j * dil).astype(np.float32)
    tabs = []
    for first in (True, False):
        ok = valid & (col >= Q_TILE) if first else valid
        per_head = [np.where(ok, -s * dist, np.float32(MASKED)) for s in _alibi_slopes()]
        tabs.append(np.concatenate(per_head, axis=0))
    return np.stack(tabs).astype(np.float32)


def _attn_branch_kernel(q_ref, kp_ref, kc_ref, vp_ref, vc_ref, bias_ref, o_ref, lse_ref):
    first_step = pl.program_id(0) == 0
    lane = lax.broadcasted_iota(jnp.int32, (Q_TILE, V7X_LANES), 1)
    lo = lane < HEAD_DIM
    for j in range(q_ref.shape[0] // Q_TILE):
        rows = slice(j * Q_TILE, (j + 1) * Q_TILE)
        before = slice((j - 1) * Q_TILE, j * Q_TILE)
        sel = jnp.where(first_step, 0, 1) if j == 0 else 1
        lse_acc = jnp.zeros((Q_TILE, V7X_LANES), F32)
        for p in range(N_HEADS // 2):
            cs = slice(V7X_LANES * p, V7X_LANES * (p + 1))
            q2 = q_ref[rows, cs]
            zero = jnp.zeros_like(q2)
            qq = jnp.concatenate([jnp.where(lo, q2, zero), jnp.where(lo, zero, q2)], axis=0)
            k_before = kp_ref[:, cs] if j == 0 else kc_ref[before, cs]
            v_before = vp_ref[:, cs] if j == 0 else vc_ref[before, cs]
            kk = jnp.concatenate([k_before, kc_ref[rows, cs]], axis=0)
            vv = jnp.concatenate([v_before, vc_ref[rows, cs]], axis=0)
            s = lax.dot_general(qq, kk, (((1,), (1,)), ((), ())), preferred_element_type=F32)
            s = s + bias_ref[sel, 2 * Q_TILE * p:2 * Q_TILE * (p + 1), :]
            m = jnp.max(s, axis=1, keepdims=True)
            e = jnp.exp(s - m)
            l = jnp.sum(e, axis=1, keepdims=True)
            eb = e.astype(BF16)
            o0 = jnp.dot(eb[:Q_TILE], vv, preferred_element_type=F32) * (1.0 / l[:Q_TILE])
            o1 = jnp.dot(eb[Q_TILE:], vv, preferred_element_type=F32) * (1.0 / l[Q_TILE:])
            o_ref[rows, cs] = jnp.where(lo, o0, o1).astype(o_ref.dtype)
            lse = m + jnp.log(l)
            lse_acc = jnp.where(lane == 2 * p, lse[:Q_TILE], lse_acc)
            lse_acc = jnp.where(lane == 2 * p + 1, lse[Q_TILE:], lse_acc)
        lse_ref[rows, :] = lse_acc[:, :N_HEADS]


def _attn_branch(qv, kv, vv, dil):
    a = ATTN_WIDTH
    rows = qv.shape[0]
    t = rows * dil
    blocks = min(Q_BLOCKS_PER_STEP, rows // Q_TILE)
    step = blocks * Q_TILE
    cur = pl.BlockSpec((step, a), lambda i, r: (i, r))
    prev = pl.BlockSpec((Q_TILE, a), lambda i, r: (jnp.maximum(i * blocks - 1, 0), r))
    bias = jnp.asarray(_branch_bias(dil))
    o, lse = pl.pallas_call(
        _attn_branch_kernel,
        out_shape=(jax.ShapeDtypeStruct((rows, dil * a), BF16),
                   jax.ShapeDtypeStruct((dil, rows, N_HEADS), F32)),
        grid=(rows // step, dil),
        in_specs=[cur, prev, cur, prev, cur, _full(bias.shape)],
        out_specs=(cur, pl.BlockSpec((None, step, N_HEADS), lambda i, r: (r, i, 0))),
        compiler_params=_cparams(2),
        name=f"attn_branch_d{dil}",
    )(qv, kv, kv, vv, vv, bias)
    return o, lse.transpose(1, 0, 2).reshape(t, N_HEADS)


def _attn_combine_kernel(*refs, dils):
    nb = len(dils)
    o_refs = refs[:nb]
    l_refs = refs[nb:2 * nb]
    g_ref = refs[2 * nb]
    unperm_refs = refs[2 * nb + 1:-1]
    a_ref = refs[-1]
    tq, a = a_ref.shape
    outs = []
    n_u = 0
    for o_ref, dil in zip(o_refs, dils):
        if dil == 1:
            outs.append(o_ref[...].astype(F32))
            continue
        by_residue = jnp.concatenate([o_ref[:, r * a:(r + 1) * a] for r in range(dil)], axis=0)
        outs.append(jnp.dot(unperm_refs[n_u][...], by_residue, preferred_element_type=F32))
        n_u += 1
    ls = [l_ref[...] for l_ref in l_refs]
    top = functools.reduce(jnp.maximum, ls)
    ws = [jnp.exp(l - top) for l in ls]
    inv = 1.0 / functools.reduce(jnp.add, ws)
    cs = [w * inv for w in ws]
    lane = lax.broadcasted_iota(jnp.int32, (tq, V7X_LANES), 1)
    lo = lane < HEAD_DIM
    cols = []
    for p in range(N_HEADS // 2):
        sl = slice(V7X_LANES * p, V7X_LANES * (p + 1))
        acc = jnp.zeros((tq, V7X_LANES), F32)
        for c, o in zip(cs, outs):
            cexp = jnp.where(lo,
                             jnp.broadcast_to(c[:, 2 * p:2 * p + 1], (tq, V7X_LANES)),
                             jnp.broadcast_to(c[:, 2 * p + 1:2 * p + 2], (tq, V7X_LANES)))
            acc = acc + cexp * o[:, sl]
        cols.append(acc)
    o = jnp.concatenate(cols, axis=1)
    a_ref[...] = _rms(o, g_ref[...]).astype(a_ref.dtype)


def _attn_combine(os_, lses, g, dils):
    t = lses[0].shape[0]
    a = ATTN_WIDTH
    tq = TOKEN_TILE
    unperms = [jnp.asarray(_chunk_perm(tq // dil, dil).T, BF16) for dil in dils if dil > 1]
    return pl.pallas_call(
        functools.partial(_attn_combine_kernel, dils=tuple(dils)),
        out_shape=jax.ShapeDtypeStruct((t, a), BF16),
        grid=(t // tq,),
        in_specs=[pl.BlockSpec((tq // dil, dil * a), lambda i: (i, 0)) for dil in dils]
        + [pl.BlockSpec((tq, N_HEADS), lambda i: (i, 0))] * len(dils) + [_full((1, a))]
        + [_full((tq, tq))] * len(unperms),
        out_specs=pl.BlockSpec((tq, a), lambda i: (i, 0)),
        compiler_params=_cparams(),
        name="attn_combine",
    )(*os_, *lses, g.reshape(1, a), *unperms)


NEW_KEY_PAD = V7X_LANES


def _sample_bias(win, steps):
    slopes = _alibi_slopes()[None, :, None, None]
    t = np.arange(steps)[None, None, :, None]

    def table(dist, live):
        tabs = []
        for window, dil in DILATED_BRANCHES:
            ok = live & (dist >= 0) & (dist <= window) & (dist % dil == 0)
            tabs.append(np.where(ok, -slopes * dist.astype(np.float32), np.float32(MASKED))[0])
        return np.stack(tabs).astype(np.float32)

    pos = np.arange(win)[None, None, None, :]
    col = np.arange(NEW_KEY_PAD)[None, None, None, :]
    return table(win + t - pos, np.bool_(True)), table(t - col, col < steps)


def _attn_sample_kernel(q_ref, kn_ref, vn_ref, kt_ref, vt_ref, bo_ref, bn_ref, g_ref, o_ref):
    nb = bo_ref.shape[0]
    steps = q_ref.shape[1]
    nt = lambda p, v: lax.dot_general(p, v, (((1,), (1,)), ((), ())), preferred_element_type=F32)
    res = []
    sq = jnp.zeros((steps, 1), F32)
    for h in range(N_HEADS):
        q = q_ref[h]
        vt = vt_ref[h].astype(BF16)
        vn = vn_ref[h].astype(BF16)
        s_old = jnp.dot(q, kt_ref[h].astype(BF16), preferred_element_type=F32)
        s_new = jnp.dot(q, kn_ref[h].astype(BF16), preferred_element_type=F32)
        ms, ls, e_old, e_new = [], [], [], []
        for b in range(nb):
            so = s_old + bo_ref[b, h]
            sn = s_new + bn_ref[b, h]
            m = jnp.maximum(jnp.max(so, axis=1, keepdims=True), jnp.max(sn, axis=1, keepdims=True))
            eo = jnp.exp(so - m)
            en = jnp.exp(sn - m)
            ms.append(m)
            ls.append(jnp.sum(eo, axis=1, keepdims=True) + jnp.sum(en, axis=1, keepdims=True))
            e_old.append(eo)
            e_new.append(en)
        o_all = (nt(jnp.concatenate(e_old, axis=0).astype(BF16), vt)
                 + nt(jnp.concatenate(e_new, axis=0).astype(BF16), vn))
        top = functools.reduce(jnp.maximum, ms)
        num = jnp.zeros((steps, HEAD_DIM), F32)
        den = jnp.zeros((steps, 1), F32)
        for b in range(nb):
            w = jnp.exp(ms[b] - top)
            num = num + o_all[b * steps:(b + 1) * steps] * w
            den = den + ls[b] * w
        r = num * (1.0 / den)
        res.append(r)
        sq = sq + jnp.sum(r * r, axis=1, keepdims=True)
    inv = lax.rsqrt(sq * (1.0 / ATTN_WIDTH) + EPS)
    for h in range(N_HEADS):
        o_ref[h] = res[h] * inv * g_ref[h]


def _attn_sample(q, kn_t, vn_t, cache_kt, cache_vt, layer, g):
    b, heads, steps, dh = q.shape
    win = cache_kt.shape[-1]
    b_old, b_new = (jnp.asarray(z) for z in _sample_bias(win, steps))
    per_b = lambda *tail: pl.BlockSpec((None,) + tail, lambda i: (i,) + (0,) * len(tail))
    cache = pl.BlockSpec((None, None, heads, dh, win), lambda i: (layer, i, 0, 0, 0))
    return pl.pallas_call(
        _attn_sample_kernel,
        out_shape=jax.ShapeDtypeStruct((b, heads, steps, dh), F32),
        grid=(b,),
        in_specs=[per_b(heads, steps, dh), per_b(heads, dh, NEW_KEY_PAD), per_b(heads, dh, NEW_KEY_PAD),
                  cache, cache, _full(b_old.shape), _full(b_new.shape), _full((heads, 1, dh))],
        out_specs=per_b(heads, steps, dh),
        compiler_params=_cparams(),
        name="attn_sample",
    )(q, kn_t, vn_t, cache_kt, cache_vt, b_old, b_new, g.reshape(heads, 1, dh))


def _gelu_tanh(x):
    return 0.5 * x * (1.0 + jnp.tanh(np.sqrt(2.0 / np.pi).astype(np.float32) * (x + 0.044715 * (x * x * x))))


def _ssm_kernel(u_ref, bb_ref, lam_ref, pow_ref, cm_ref, dsk_ref, wglu_ref, bglu_ref, g_ref, pt_ref, h0_ref,
                z_ref, ht_ref, h_s, hin_s, carry_s, *, rows, steps, chain):
    ns = N_STATE
    cw = (4 * V7X_SUBLANES * V7X_LANES) // rows
    last = (steps - 1) * rows

    @pl.when(pl.program_id(0) == 0)
    def _():
        carry_s[...] = h0_ref[0:1, :]

    ub = u_ref[...].astype(BF16)
    n_slabs = SSM_WIDTH // V7X_LANES
    sw = ns // n_slabs
    for s in range(n_slabs):
        part = jnp.dot(ub[:, s * V7X_LANES:(s + 1) * V7X_LANES], bb_ref[s], preferred_element_type=F32)
        h_s[:, s * sw:(s + 1) * sw] = part[:, :sw]
        h_s[:, ns + s * sw:ns + (s + 1) * sw] = part[:, sw:]

    for cc in range(0, ns // cw, 2):
        crs = [slice(c * cw, (c + 1) * cw) for c in (cc, cc + 1)]
        cis = [slice(ns + c * cw, ns + (c + 1) * cw) for c in (cc, cc + 1)]
        lrs = [jnp.broadcast_to(lam_ref[0:1, cr], (rows, cw)) for cr in crs]
        lis = [jnp.broadcast_to(lam_ref[1:2, cr], (rows, cw)) for cr in crs]

        def scan_body(t, carry, crs=crs, cis=cis, lrs=lrs, lis=lis):
            rs = pl.ds(pl.multiple_of(t * rows, rows), rows)
            out = []
            for j in range(2):
                hr, hi = carry[2 * j], carry[2 * j + 1]
                nr = lrs[j] * hr - lis[j] * hi + h_s[rs, crs[j]]
                ni = lrs[j] * hi + lis[j] * hr + h_s[rs, cis[j]]
                h_s[rs, crs[j]] = nr
                h_s[rs, cis[j]] = ni
                out += [nr, ni]
            return tuple(out)

        zero = jnp.zeros((rows, cw), F32)
        lax.fori_loop(0, steps, scan_body, (zero,) * 4)

    if chain:
        ptr = pow_ref[steps - 1, 0:1, :ns]
        pti = pow_ref[steps - 1, 0:1, ns:]
        carry = carry_s[...]
        for c in range(rows):
            hin_s[c:c + 1, :] = carry
            cr_, ci_ = carry[:, :ns], carry[:, ns:]
            e = h_s[last + c:last + c + 1, :]
            carry = jnp.concatenate([ptr * cr_ - pti * ci_ + e[:, :ns],
                                     ptr * ci_ + pti * cr_ + e[:, ns:]], axis=1)
        carry_s[...] = carry
    else:
        hin_s[...] = h0_ref[...]

    for cc in range(ns // cw):
        cr = slice(cc * cw, (cc + 1) * cw)
        ci = slice(ns + cc * cw, ns + (cc + 1) * cw)
        hr0 = hin_s[:, cr]
        hi0 = hin_s[:, ci]

        def fix_body(t, _, cr=cr, ci=ci, hr0=hr0, hi0=hi0):
            rs = pl.ds(pl.multiple_of(t * rows, rows), rows)
            pr = pow_ref[t, :, cr]
            pi_ = pow_ref[t, :, ci]
            h_s[rs, cr] = h_s[rs, cr] + (pr * hr0 - pi_ * hi0)
            h_s[rs, ci] = h_s[rs, ci] + (pr * hi0 + pi_ * hr0)
            return 0

        lax.fori_loop(0, steps, fix_body, 0, unroll=8)

    ht_ref[...] = h_s[last:last + rows, :]
    ys = []
    for s in range(n_slabs):
        hs = jnp.concatenate([h_s[:, s * sw:(s + 1) * sw], h_s[:, ns + s * sw:ns + (s + 1) * sw]], axis=1)
        ys.append(jnp.dot(hs.astype(BF16), cm_ref[s], preferred_element_type=F32))
    y = jnp.concatenate(ys, axis=1) + dsk_ref[...] * u_ref[...]
    y = _gelu_tanh(y)
    gl = jnp.dot(y.astype(BF16), wglu_ref[...], preferred_element_type=F32) + bglu_ref[...]
    z = _rms(y * _sigmoid(gl), g_ref[...]).astype(BF16)
    z_ref[...] = jnp.dot(pt_ref[...], z, preferred_element_type=F32).astype(z_ref.dtype)


def _ssm_tables(ssm_a_re, ssm_a_im, ssm_log_dt, ssm_b_re, ssm_b_im, ssm_c_re, ssm_c_im, max_steps):
    g, n, ch = SSM_GROUPS, SSM_STATE, SSM_CH
    a_re = ssm_a_re.astype(F32)
    a_im = ssm_a_im.astype(F32)
    dt = jnp.exp(ssm_log_dt.astype(F32))[:, None]
    mag = jnp.exp(dt * a_re)
    lam_re = mag * jnp.cos(dt * a_im)
    lam_im = mag * jnp.sin(dt * a_im)
    nr = lam_re - 1.0
    ni = lam_im
    inv = 1.0 / (a_re * a_re + a_im * a_im)
    coef_re = (nr * a_re + ni * a_im) * inv
    coef_im = (ni * a_re - nr * a_im) * inv
    br = ssm_b_re.astype(F32)
    bi = ssm_b_im.astype(F32)
    bb_re = coef_re[..., None] * br - coef_im[..., None] * bi
    bb_im = coef_re[..., None] * bi + coef_im[..., None] * br
    gs = V7X_LANES // ch
    ns_ = g // gs
    eye = jnp.eye(gs, dtype=F32)
    bmat = lambda b: jnp.einsum("sgnc,gh->sgchn", b.reshape(ns_, gs, n, ch), eye).reshape(ns_, gs * ch, gs * n)
    cmat = lambda c: jnp.einsum("sgcn,gh->sgnhc", c.astype(F32).reshape(ns_, gs, ch, n), eye).reshape(
        ns_, gs * n, gs * ch)
    bb = jnp.concatenate([bmat(bb_re), bmat(bb_im)], axis=2).astype(BF16)
    cm = jnp.concatenate([cmat(ssm_c_re), -cmat(ssm_c_im)], axis=1).astype(BF16)
    lam = jnp.stack([lam_re.reshape(-1), lam_im.reshape(-1)])

    k = jnp.arange(1, max_steps + 1, dtype=F32)[:, None]
    kdt = k * dt.reshape(1, -1).repeat(n, axis=1)
    pmag = jnp.exp(kdt * a_re.reshape(1, -1))
    parg = kdt * a_im.reshape(1, -1)
    pows = jnp.concatenate([pmag * jnp.cos(parg), pmag * jnp.sin(parg)], axis=1)
    return bb, cm, lam, pows


def _ssm(u_perm, h0, tabs, dsk, w_glu, b_glu, g, perm_t, rows, steps, chain):
    n, w = u_perm.shape
    blk = rows * steps
    bb, cm, lam, pows = tabs
    row = pl.BlockSpec((blk, w), lambda i: (i, 0))
    kern = functools.partial(_ssm_kernel, rows=rows, steps=steps, chain=chain)
    return pl.pallas_call(
        kern,
        out_shape=(jax.ShapeDtypeStruct((n, w), BF16), jax.ShapeDtypeStruct((rows, 2 * N_STATE), F32)),
        grid=(n // blk,),
        in_specs=[row, _full(bb.shape), _full(lam.shape), _full((steps, rows, 2 * N_STATE)), _full(cm.shape),
                  _full((1, w)), _full((w, w)), _full((1, w)), _full((1, w)), _full((blk, blk)),
                  _full((rows, 2 * N_STATE))],
        out_specs=(row, _full((rows, 2 * N_STATE))),
        scratch_shapes=[pltpu.VMEM((blk, 2 * N_STATE), F32),
                        pltpu.VMEM((rows, 2 * N_STATE), F32),
                        pltpu.VMEM((1, 2 * N_STATE), F32)],
        compiler_params=_cparams(),
        name=f"ssm_r{rows}",
    )(u_perm, bb, lam, jnp.broadcast_to(pows[:steps, None, :], (steps, rows, 2 * N_STATE)), cm,
      dsk.reshape(1, w), w_glu.astype(BF16), b_glu.reshape(1, w),
      g.reshape(1, w), perm_t, h0)


def _outproj_kernel(*refs, n_real, aliased):
    ins, outs = refs[:12], refs[12 + aliased:]
    step = pl.program_id(0)

    @pl.when(step < n_real)
    def _():
        _outproj_tile(*ins, *outs)

    @pl.when(step >= n_real)
    def _():
        xs_ref = outs[1]
        xs_ref[...] = jnp.zeros(xs_ref.shape, xs_ref.dtype)


def _outproj_tile(x_ref, a_ref, z_ref, gt_ref, sh_ref, sc_ref, g_ref, wo_ref, wr_ref, br_ref, ltri_ref, utri_ref,
                  x1_ref, xs_ref, slot_ref, gates_ref, meta_ref, wbf_ref):
    @pl.when(pl.program_id(0) == 0)
    def _():
        wbf_ref[...] = wo_ref[...].astype(BF16)

    a = ATTN_WIDTH
    mixed = (jnp.dot(a_ref[...].astype(BF16), wbf_ref[:a, :], preferred_element_type=F32)
             + jnp.dot(z_ref[...], wbf_ref[a:, :], preferred_element_type=F32))
    x1 = x_ref[...] + gt_ref[...] * mixed
    x1_ref[...] = x1
    h2 = (_rms(x1, g_ref[...]) * (1.0 + sc_ref[...]) + sh_ref[...]).astype(BF16)
    lg = jnp.dot(h2, wr_ref[...], preferred_element_type=F32) + br_ref[...]
    tm = lg.shape[0]
    lane = lax.broadcasted_iota(jnp.int32, (tm, V7X_LANES), 1).astype(F32)
    vals, hots = [], []
    for _ in range(TOP_K):
        m = jnp.max(lg, axis=1, keepdims=True)
        idx = jnp.min(jnp.where(lg == m, lane, float(V7X_LANES)), axis=1, keepdims=True)
        hot = lane == idx
        vals.append(m)
        hots.append(jnp.where(hot, 1.0, 0.0))
        lg = jnp.where(hot, MASKED * 2, lg)
    es = [jnp.exp(v - vals[0]) for v in vals]
    inv = 1.0 / (es[0] + es[1] + es[2] + es[3])

    member = hots[0] + hots[1] + hots[2] + hots[3]
    before = jnp.dot(ltri_ref[...], member.astype(BF16), preferred_element_type=F32)
    count = jnp.sum(member, axis=0, keepdims=True)
    padded = jnp.floor((count + (ROW_ALIGN - 1.0)) * (1.0 / ROW_ALIGN)) * ROW_ALIGN
    padded8 = jnp.broadcast_to(padded, (V7X_SUBLANES, V7X_LANES))
    start = jnp.dot(padded8.astype(BF16), utri_ref[...], preferred_element_type=F32)[0:1]
    where_to = start + before
    slots = jnp.zeros((tm, V7X_LANES), F32)
    gates = jnp.zeros((tm, V7X_LANES), F32)
    for k in range(TOP_K):
        slot_k = jnp.sum(hots[k] * where_to, axis=1, keepdims=True)
        slots = jnp.where(lane == float(k), slot_k, slots)
        gates = jnp.where(lane == float(k), es[k] * inv, gates)
    slot_ref[...] = slots
    gates_ref[...] = gates
    row = lax.broadcasted_iota(jnp.int32, (V7X_SUBLANES, V7X_LANES), 0)
    meta_ref[...] = jnp.where(row == 0, padded8, jnp.where(row == 1, jnp.broadcast_to(start, padded8.shape), 0.0))

    cap = xs_ref.shape[0]
    slots_t = jnp.transpose(slots).astype(jnp.int32).astype(jnp.int16)
    srow = lax.broadcasted_iota(jnp.int16, (cap, tm), 0)
    place = jnp.zeros((cap, tm), BF16)
    for k in range(TOP_K):
        place = jnp.where(srow == slots_t[k:k + 1, :], jnp.ones((), BF16), place)
    xs_ref[...] = jnp.dot(place, h2, preferred_element_type=F32).astype(BF16)


def _outproj(x, a, z, gt, sh, sc, g, w_out, wr_pad, br_pad, tm, n_blocks, block0, xs_prev=None):
    n, d = x.shape
    aw = ATTN_WIDTH
    n_real = n // tm
    aliased = xs_prev is not None
    n_steps = n_real if aliased else n_blocks
    tile = lambda i: jnp.minimum(i, n_real - 1)
    mod_spec = (pl.BlockSpec((1, d), lambda i: (0, 0)) if gt.shape[0] == 1
                else pl.BlockSpec((tm, d), lambda i: (tile(i), 0)))
    row = lambda w: pl.BlockSpec((tm, w), lambda i: (tile(i), 0))
    ltri = jnp.asarray(np.tril(np.ones((tm, tm), np.float32), -1), BF16)
    utri = jnp.asarray(np.triu(np.ones((V7X_LANES, V7X_LANES), np.float32), 1), BF16)
    in_specs = [row(d), row(aw), row(SSM_WIDTH), mod_spec, mod_spec, mod_spec, _full((1, d)),
                _full(w_out.shape), _full(wr_pad.shape), _full(br_pad.shape), _full(ltri.shape), _full(utri.shape)]
    args = [x, a, z, gt, sh, sc, g.reshape(1, d), w_out, wr_pad, br_pad, ltri, utri]
    if aliased:
        in_specs.append(pl.BlockSpec(memory_space=pl.ANY))
        args.append(xs_prev)
    return pl.pallas_call(
        functools.partial(_outproj_kernel, n_real=n_real, aliased=int(aliased)),
        out_shape=(jax.ShapeDtypeStruct((n, d), F32),
                   jax.ShapeDtypeStruct((n_blocks, GROUP_CAP, d), BF16),
                   jax.ShapeDtypeStruct((n, V7X_LANES), F32), jax.ShapeDtypeStruct((n, V7X_LANES), F32),
                   jax.ShapeDtypeStruct((n_real, V7X_SUBLANES, V7X_LANES), F32)),
        grid=(n_steps,),
        in_specs=in_specs,
        out_specs=(row(d), pl.BlockSpec((None, GROUP_CAP, d), lambda i: (block0 + i, 0, 0)),
                   row(V7X_LANES), row(V7X_LANES),
                   pl.BlockSpec((None, V7X_SUBLANES, V7X_LANES), lambda i: (tile(i), 0, 0))),
        scratch_shapes=[pltpu.VMEM(w_out.shape, BF16)],
        input_output_aliases={len(args) - 1: 1} if aliased else {},
        compiler_params=_cparams(),
        name="outproj_router",
    )(*args)


_PIECE_SIZES = tuple(1 << s for s in range(MOE_TILE.bit_length() - 1, ROW_ALIGN.bit_length() - 2, -1))
_CHUNK_SHIFT = 6
_CHUNK = 1 << _CHUNK_SHIFT
_TAIL_SIZES = tuple(sz for sz in _PIECE_SIZES if sz < _CHUNK)


def _moe_kernel(te_ref, tl_ref, lo_ref, hi_ref, nu_ref, gstart_ref, gsize_ref, gbase_ref, rows_ref, used_ref,
                nxt_ref, xs_hbm, wgu_hbm, bgu_ref, wd_hbm, bd_ref, ys_hbm,
                xbuf, ybuf, zbuf, wgu_f32, wd_f32, wgu_bf, wd_bf, xsem, ysem, zsem, wsem):
    t = pl.program_id(0)
    n_used = nu_ref[0]
    n_blocks, cap = xs_hbm.shape[0], xs_hbm.shape[1]

    def weight_copies(e):
        return (pltpu.make_async_copy(wgu_hbm.at[e], wgu_f32, wsem.at[0]),
                pltpu.make_async_copy(wd_hbm.at[e], wd_f32, wsem.at[1]))

    def x_copy(i, br, tr, sz, slot):
        pltpu.make_async_copy(xs_hbm.at[i, pl.ds(br, sz)], xbuf.at[slot, pl.ds(tr, sz)], xsem.at[slot]).start()

    def y_copy(i, br, tr, sz, slot):
        pltpu.make_async_copy(ybuf.at[slot, pl.ds(tr, sz)], ys_hbm.at[i, pl.ds(br, sz)], ysem.at[slot]).start()

    def z_copy(i, row, sz, start):
        cp = pltpu.make_async_copy(zbuf.at[pl.ds(0, sz)], ys_hbm.at[i, pl.ds(row, sz)], zsem)
        cp.start() if start else cp.wait()

    def pieces(tt, fn):
        e = te_ref[tt]
        lo = tl_ref[tt] * MOE_TILE

        def per_block(i, c):
            g = e * n_blocks + i
            s0 = gbase_ref[g]
            a = jnp.maximum(s0, lo)
            b = jnp.minimum(s0 + gsize_ref[g], lo + MOE_TILE)
            length = jnp.maximum(b - a, 0)
            src = gstart_ref[g] + (a - s0)
            dst = a - lo
            whole = lax.shift_right_logical(length, _CHUNK_SHIFT)

            def chunk(j, cc):
                off = j * _CHUNK
                fn(i, pl.multiple_of(src + off, ROW_ALIGN), pl.multiple_of(dst + off, ROW_ALIGN), _CHUNK)
                return cc

            lax.fori_loop(0, whole, chunk, 0)
            done = whole * _CHUNK
            for sz in _TAIL_SIZES:
                hit = (length & sz) != 0

                @pl.when(hit)
                def _(sz=sz, done=done):
                    fn(i, pl.multiple_of(src + done, ROW_ALIGN), pl.multiple_of(dst + done, ROW_ALIGN), sz)

                done = done + jnp.where(hit, sz, 0)
            return c

        lax.fori_loop(lo_ref[tt], hi_ref[tt], per_block, 0)

    def tile_rows(tt):
        return jnp.minimum(rows_ref[te_ref[tt]] - tl_ref[tt] * MOE_TILE, MOE_TILE)

    def wait_rows(n, sem, buf):
        def chunk(j, cc):
            pltpu.make_async_copy(buf.at[pl.ds(0, _CHUNK)], buf.at[pl.ds(0, _CHUNK)], sem).wait()
            return cc

        lax.fori_loop(0, lax.shift_right_logical(n, _CHUNK_SHIFT), chunk, 0)
        for sz in _TAIL_SIZES:
            @pl.when((n & sz) != 0)
            def _(sz=sz):
                pltpu.make_async_copy(buf.at[pl.ds(0, sz)], buf.at[pl.ds(0, sz)], sem).wait()

    def fetch(tt, slot):
        pieces(tt, lambda i, br, tr, sz: x_copy(i, br, tr, sz, slot))

    def writeback(tt, slot):
        pieces(tt, lambda i, br, tr, sz: y_copy(i, br, tr, sz, slot))

    def zero_tail(i, start):
        u = used_ref[i]
        rem = cap - u
        nz = zbuf.shape[0]
        whole = lax.shift_right_logical(rem, nz.bit_length() - 1)

        def chunk(j, c):
            z_copy(i, pl.multiple_of(u + j * nz, ROW_ALIGN), nz, start)
            return c

        lax.fori_loop(0, whole, chunk, 0)
        base = u + whole * nz
        done = jnp.int32(0)
        for sz in _PIECE_SIZES:
            if sz >= nz:
                continue
            hit = (rem & sz) != 0

            @pl.when(hit)
            def _(sz=sz, done=done):
                z_copy(i, pl.multiple_of(base + done, ROW_ALIGN), sz, start)

            done = done + jnp.where(hit, sz, 0)

    @pl.when(t == 0)
    def _():
        xbuf[...] = jnp.zeros(xbuf.shape, xbuf.dtype)
        zbuf[...] = jnp.zeros(zbuf.shape, zbuf.dtype)
        lax.fori_loop(0, n_blocks, lambda i, c: (zero_tail(i, True), c)[1], 0)
        lax.fori_loop(0, n_blocks, lambda i, c: (zero_tail(i, False), c)[1], 0)
        fetch(0, 0)
        for cp in weight_copies(te_ref[0]):
            cp.start()

    @pl.when(t < n_used)
    def _():
        slot = t % 2

        @pl.when(t + 1 < n_used)
        def _():
            fetch(t + 1, 1 - slot)

        @pl.when(tl_ref[t] == 0)
        def _():
            for cp in weight_copies(te_ref[t]):
                cp.wait()
            wgu_bf[...] = wgu_f32[...].astype(BF16)
            wd_bf[...] = wd_f32[...].astype(BF16)
            nxt = nxt_ref[te_ref[t]]

            @pl.when(nxt >= 0)
            def _():
                for cp in weight_copies(nxt):
                    cp.start()

        wait_rows(tile_rows(t), xsem.at[slot], xbuf.at[slot])

        @pl.when(t >= 2)
        def _():
            wait_rows(tile_rows(t - 2), ysem.at[slot], ybuf.at[slot])

        def expert_mlp(m):
            f = D_EXPERT
            gu = jnp.dot(xbuf[slot, pl.ds(0, m)], wgu_bf[...], preferred_element_type=F32) + bgu_ref[...]
            gate = jnp.minimum(gu[:, :f], SWIGLU_LIMIT)
            up = jnp.clip(gu[:, f:], -SWIGLU_LIMIT, SWIGLU_LIMIT)
            act = (up + 1.0) * gate * _sigmoid(SWIGLU_ALPHA * gate)
            y = jnp.dot(act.astype(BF16), wd_bf[...], preferred_element_type=F32) + bd_ref[...]
            ybuf[slot, pl.ds(0, m)] = y.astype(BF16)

        n_steps = lax.shift_right_logical(tile_rows(t) + (MOE_ROW_STEP - 1), MOE_ROW_STEP.bit_length() - 1)
        for ns in range(1, MOE_TILE // MOE_ROW_STEP + 1):
            @pl.when(n_steps == ns)
            def _(ns=ns):
                expert_mlp(ns * MOE_ROW_STEP)

        writeback(t, slot)

        @pl.when(t == n_used - 1)
        def _():
            wait_rows(tile_rows(t), ysem.at[slot], ybuf.at[slot])

            @pl.when(t >= 1)
            def _():
                wait_rows(tile_rows(t - 1), ysem.at[1 - slot], ybuf.at[1 - slot])


def _moe(plan, xs, w_gate_up, b_gate_up, w_down, b_down):
    _, cap, d = xs.shape
    e, _, f2 = w_gate_up.shape
    nt = plan[0].shape[0]
    wmap = lambda t, te, *_: (te[t], 0, 0)
    anyspec = pl.BlockSpec(memory_space=pl.ANY)
    grid_spec = pltpu.PrefetchScalarGridSpec(
        num_scalar_prefetch=len(plan),
        grid=(nt,),
        in_specs=[anyspec,
                  anyspec,
                  pl.BlockSpec((None, 1, f2), wmap),
                  anyspec,
                  pl.BlockSpec((None, 1, d), wmap)],
        out_specs=anyspec,
        scratch_shapes=[pltpu.VMEM((2, MOE_TILE, d), BF16), pltpu.VMEM((2, MOE_TILE, d), BF16),
                        pltpu.VMEM((MOE_TILE // 2, d), BF16),
                        pltpu.VMEM((d, f2), F32), pltpu.VMEM((f2 // 2, d), F32),
                        pltpu.VMEM((d, f2), BF16), pltpu.VMEM((f2 // 2, d), BF16),
                        pltpu.SemaphoreType.DMA((2,)), pltpu.SemaphoreType.DMA((2,)), pltpu.SemaphoreType.DMA(()),
                        pltpu.SemaphoreType.DMA((2,))],
    )
    return pl.pallas_call(
        _moe_kernel,
        out_shape=jax.ShapeDtypeStruct(xs.shape, BF16),
        grid_spec=grid_spec,
        compiler_params=_cparams(),
        name="moe_experts",
    )(*plan, xs, w_gate_up, b_gate_up.reshape(e, 1, f2), w_down, b_down.reshape(e, 1, d))


def _moe_plan(group_size, group_start):
    n_blocks = group_size.shape[0]
    gsize = group_size.T.astype(jnp.int32)
    gstart = group_start.T.astype(jnp.int32)
    gbase = jnp.cumsum(gsize, axis=1) - gsize
    rows = jnp.sum(gsize, axis=1)
    tiles = (rows + MOE_TILE - 1) // MOE_TILE
    tile_end = jnp.cumsum(tiles)
    n_used = tile_end[-1:]
    nt = (n_blocks * GROUP_CAP) // MOE_TILE + N_EXPERTS
    t = jnp.arange(nt, dtype=jnp.int32)
    te = jnp.sum((tile_end[None, :] <= t[:, None]).astype(jnp.int32), axis=1)
    last = jnp.max(jnp.where(tiles > 0, jnp.arange(N_EXPERTS, dtype=jnp.int32), 0))
    te = jnp.where(t < n_used[0], jnp.minimum(te, N_EXPERTS - 1), last)
    hot = (te[:, None] == jnp.arange(N_EXPERTS, dtype=jnp.int32)[None, :]).astype(jnp.int32)
    tl = jnp.where(t < n_used[0], t - hot @ (tile_end - tiles), 0)
    lo_row = tl * MOE_TILE
    base_t = hot @ gbase
    size_t = hot @ gsize
    first = jnp.sum((base_t + size_t <= lo_row[:, None]).astype(jnp.int32), axis=1)
    stop = jnp.sum((base_t < lo_row[:, None] + MOE_TILE).astype(jnp.int32), axis=1)
    used = jnp.sum(gsize, axis=0)
    ids = jnp.arange(N_EXPERTS, dtype=jnp.int32)
    later = (ids[None, :] > ids[:, None]) & (tiles[None, :] > 0)
    nxt = jnp.min(jnp.where(later, ids[None, :], N_EXPERTS), axis=1)
    nxt = jnp.where(nxt < N_EXPERTS, nxt, -1)
    i32 = lambda z: z.astype(jnp.int32)
    return (i32(te), i32(tl), i32(first), i32(stop), i32(n_used), i32(gstart.reshape(-1)), i32(gsize.reshape(-1)),
            i32(gbase.reshape(-1)), i32(rows), i32(used), i32(nxt))


def _final_kernel(x_ref, ys_ref, slot_ref, gates_ref, gt_ref, g_ref, y_ref):
    tm = x_ref.shape[0]
    cap = ys_ref.shape[0]
    col = lax.broadcasted_iota(jnp.int16, (tm, cap), 1)
    slots = slot_ref[...].astype(jnp.int32).astype(jnp.int16)
    gates = gates_ref[...].astype(BF16)
    mix = jnp.zeros((tm, cap), BF16)
    for k in range(TOP_K):
        mix = jnp.where(col == slots[:, k:k + 1], gates[:, k:k + 1], mix)
    ff = jnp.dot(mix, ys_ref[...], preferred_element_type=F32)
    y_ref[...] = _rms(x_ref[...] + gt_ref[...] * ff, g_ref[...])


def _final(x1, ys, slots, gates, gt, g, tm, block0):
    n, d = x1.shape
    cap = ys.shape[1]
    mod_spec = (pl.BlockSpec((1, d), lambda i: (0, 0)) if gt.shape[0] == 1
                else pl.BlockSpec((tm, d), lambda i: (i, 0)))
    row = lambda w: pl.BlockSpec((tm, w), lambda i: (i, 0))
    return pl.pallas_call(
        _final_kernel,
        out_shape=jax.ShapeDtypeStruct((n, d), F32),
        grid=(n // tm,),
        in_specs=[row(d), pl.BlockSpec((None, cap, d), lambda i: (block0 + i, 0, 0)),
                  row(V7X_LANES), row(V7X_LANES), mod_spec, _full((1, d))],
        out_specs=row(d),
        compiler_params=_cparams(),
        name="final_norm",
    )(x1, ys, slots, gates, gt, g.reshape(1, d))


def kernel(x_prompt, x_sample, cache_k_win, cache_v_win, state_ssm_re, state_ssm_im, c_prompt, c_sample,
           w_ada, b_ada, g_norm1, w_in, ssm_a_re, ssm_a_im, ssm_log_dt, ssm_b_re, ssm_b_im, ssm_c_re, ssm_c_im,
           ssm_d, w_glu, b_glu, g_out_attn, g_out_ssm, w_out, g_norm2, w_router, b_router, w_gate_up, b_gate_up,
           w_down, b_down, g_final):
    depth = w_ada.shape[0]
    assert depth == 1 and x_prompt.shape[0] == 1
    bp, t, d = x_prompt.shape
    bs, ts, _ = x_sample.shape
    ns = bs * ts
    l = 0
    a = ATTN_WIDTH

    n_c = bp + bs
    c_pad = -(-n_c // V7X_SUBLANES) * V7X_SUBLANES
    c_rows = jnp.concatenate([c_prompt, c_sample, jnp.zeros((c_pad - n_c, d), F32)], axis=0)
    mod = _ada_modulation(c_rows, w_ada[l], b_ada[l])
    mod_p = [mod[0:1, i * d:(i + 1) * d] for i in range(N_MOD)]
    mod_s = [jnp.repeat(mod[bp:bp + bs, i * d:(i + 1) * d], ts, axis=0) for i in range(N_MOD)]

    steps_p = TOKEN_TILE // SSM_ROWS_PROMPT
    perm_p = _chunk_perm(SSM_ROWS_PROMPT, steps_p)
    perm_s = _chunk_perm(bs, ts)
    tabs = _ssm_tables(ssm_a_re[l], ssm_a_im[l], ssm_log_dt[l], ssm_b_re[l], ssm_b_im[l],
                       ssm_c_re[l], ssm_c_im[l], max(steps_p, ts))
    wr_pad = jnp.zeros((d, V7X_LANES), F32).at[:, :N_EXPERTS].set(w_router[l]).astype(BF16)
    br_pad = jnp.full((1, V7X_LANES), MASKED, F32).at[0, :N_EXPERTS].set(b_router[l])

    xp = x_prompt.reshape(t, d)
    dils = tuple(dil for _, dil in DILATED_BRANCHES)
    wide = tuple(dil for dil in dils if dil > 1)
    proj_p = _inproj(xp, mod_p[0], mod_p[1], g_norm1[l], w_in[l], jnp.asarray(perm_p, BF16), TOKEN_TILE, wide)
    kpf, vpf, up = proj_p[3:6]
    views = {1: proj_p[0:3]}
    for n_d, dil in enumerate(wide):
        views[dil] = proj_p[6 + 3 * n_d:9 + 3 * n_d]
    outs = [_attn_branch(*views[dil], dil) for dil in dils]
    ap = _attn_combine([o for o, _ in outs], [s for _, s in outs], g_out_attn[l], dils)
    zeros_h = jnp.zeros((SSM_ROWS_PROMPT, 2 * N_STATE), F32)
    zp, hp = _ssm(up, zeros_h, tabs, ssm_d[l], w_glu[l], b_glu[l], g_out_ssm[l],
                  jnp.asarray(perm_p.T, BF16), SSM_ROWS_PROMPT, steps_p, True)
    n_blocks_p = t // TOKEN_TILE
    n_blocks = n_blocks_p + 1
    x1p, xs_all, slot_p, gate_p, meta_p = _outproj(xp, ap, zp, mod_p[2], mod_p[3], mod_p[4], g_norm2[l], w_out[l],
                                                   wr_pad, br_pad, TOKEN_TILE, n_blocks, 0)

    xs = x_sample.reshape(ns, d)
    qs, ks, vs, ksf, vsf, us = _inproj(xs, mod_s[0], mod_s[1], g_norm1[l], w_in[l], jnp.asarray(perm_s, BF16), ns)
    split = lambda z: z.reshape(bs, ts, N_HEADS, HEAD_DIM)
    new_t = lambda z: jnp.pad(split(z).transpose(0, 2, 3, 1), ((0, 0), (0, 0), (0, 0), (0, NEW_KEY_PAD - ts)))
    as_ = _attn_sample(split(qs).transpose(0, 2, 1, 3), new_t(ks), new_t(vs),
                       cache_k_win.transpose(0, 1, 3, 4, 2), cache_v_win.transpose(0, 1, 3, 4, 2), l, g_out_attn[l])
    as_ = as_.transpose(0, 2, 1, 3)
    h0s = jnp.concatenate([state_ssm_re[l].reshape(bs, N_STATE), state_ssm_im[l].reshape(bs, N_STATE)], axis=1)
    zs, hs = _ssm(us, h0s, tabs, ssm_d[l], w_glu[l], b_glu[l], g_out_ssm[l],
                  jnp.asarray(perm_s.T, BF16), bs, ts, False)
    x1s, xs_all, slot_s, gate_s, meta_s = _outproj(xs, as_.reshape(ns, a), zs, mod_s[2], mod_s[3], mod_s[4],
                                                   g_norm2[l], w_out[l], wr_pad, br_pad, ns, n_blocks, n_blocks_p,
                                                   xs_prev=xs_all)

    meta = jnp.concatenate([meta_p, meta_s], axis=0)
    plan = _moe_plan(meta[:, 0, :N_EXPERTS], meta[:, 1, :N_EXPERTS])
    ys_all = _moe(plan, xs_all, w_gate_up[l], b_gate_up[l], w_down[l], b_down[l])

    y_prompt = _final(x1p, ys_all, slot_p, gate_p, mod_p[5], g_final, TOKEN_TILE, 0).reshape(bp, t, d)
    y_sample = _final(x1s, ys_all, slot_s, gate_s, mod_s[5], g_final, ns, n_blocks_p).reshape(bs, ts, d)

    keep = min(MAX_WINDOW, t)
    shp = (1, bp, keep, N_HEADS, HEAD_DIM)
    k_win = kpf[t - keep:].reshape(shp)
    v_win = vpf[t - keep:].reshape(shp)
    st = (1, bp, SSM_GROUPS, SSM_STATE)
    hp_last = hp[SSM_ROWS_PROMPT - 1]
    ss = (1, bs, SSM_GROUPS, SSM_STATE)
    return (y_prompt, y_sample, k_win, v_win,
            hp_last[:N_STATE].reshape(st), hp_last[N_STATE:].reshape(st),
            ksf.reshape(1, bs, ts, N_HEADS, HEAD_DIM), vsf.reshape(1, bs, ts, N_HEADS, HEAD_DIM),
            hs[:, :N_STATE].reshape(ss), hs[:, N_STATE:].reshape(ss))
```

```python
import functools

import numpy as np
import jax
import jax.numpy as jnp
from jax import lax
from jax.experimental import pallas as pl
from jax.experimental.pallas import tpu as pltpu

F32 = jnp.float32
BF16 = jnp.bfloat16

D_MODEL = 1024
N_HEADS = 8
HEAD_DIM = 64
ATTN_WIDTH = N_HEADS * HEAD_DIM
DILATED_BRANCHES = ((128, 1), (512, 4), (2048, 16))
MAX_WINDOW = 2048
SSM_WIDTH = D_MODEL - ATTN_WIDTH
SSM_CH = 16
SSM_GROUPS = SSM_WIDTH // SSM_CH
SSM_STATE = 64
N_STATE = SSM_GROUPS * SSM_STATE
N_EXPERTS = 32
TOP_K = 4
D_EXPERT = D_MODEL
SWIGLU_LIMIT = 7.0
SWIGLU_ALPHA = 1.702
N_MOD = 6
EPS = 1e-6
MASKED = -1e30

V7X_LANES = 128
V7X_SUBLANES = 8
V7X_VMEM_LIMIT_BYTES = 56 * 1024 * 1024

TOKEN_TILE = 512
Q_TILE = 128
Q_BLOCKS_PER_STEP = 16
SSM_ROWS_PROMPT = 8
MOE_TILE = 1024
MOE_ROW_STEP = 128
ROW_ALIGN = 16
GROUP_CAP = -(-(TOKEN_TILE * TOP_K + N_EXPERTS * (ROW_ALIGN - 1)) // (2 * V7X_LANES)) * (2 * V7X_LANES)


def _cparams(n_axes=1):
    return pltpu.CompilerParams(
        dimension_semantics=("arbitrary",) * n_axes,
        vmem_limit_bytes=V7X_VMEM_LIMIT_BYTES,
    )


def _full(shape):
    n = len(shape)
    return pl.BlockSpec(shape, lambda *_: (0,) * n)


def _rms(x, g):
    return x * lax.rsqrt(jnp.mean(x * x, axis=-1, keepdims=True) + EPS) * g


def _sigmoid(x):
    return 1.0 / (1.0 + jnp.exp(-x))


def _ada_kernel(c_ref, w_ref, b_ref, o_ref):
    c = c_ref[...]
    s = (c * _sigmoid(c)).astype(BF16)
    o_ref[...] = jnp.dot(s, w_ref[...].astype(BF16), preferred_element_type=F32) + b_ref[...]


def _ada_modulation(c_rows, w_ada, b_ada):
    m, d = c_rows.shape
    n = w_ada.shape[1]
    tn = n // 4
    return pl.pallas_call(
        _ada_kernel,
        out_shape=jax.ShapeDtypeStruct((m, n), F32),
        grid=(n // tn,),
        in_specs=[_full((m, d)),
                  pl.BlockSpec((d, tn), lambda j: (0, j)),
                  pl.BlockSpec((1, tn), lambda j: (0, j))],
        out_specs=pl.BlockSpec((m, tn), lambda j: (0, j)),
        compiler_params=_cparams(),
        name="ada_modulation",
    )(c_rows, w_ada, b_ada.reshape(1, n))


def _inproj_kernel(*refs, dils):
    x_ref, sh_ref, sc_ref, g_ref, w_ref, perm_ref = refs[:6]
    dperm_refs = refs[6:6 + len(dils)]
    q_ref, k_ref, v_ref, kf_ref, vf_ref, u_ref = refs[6 + len(dils):12 + len(dils)]
    dil_refs = refs[12 + len(dils):-1]
    wbf_ref = refs[-1]

    @pl.when(pl.program_id(0) == 0)
    def _():
        wbf_ref[...] = w_ref[...].astype(BF16)

    h = _rms(x_ref[...], g_ref[...]) * (1.0 + sc_ref[...]) + sh_ref[...]
    hb = h.astype(BF16)
    a = ATTN_WIDTH
    proj = jnp.dot(hb, wbf_ref[:, :3 * a], preferred_element_type=F32)
    k = proj[:, a:2 * a]
    v = proj[:, 2 * a:]
    qkv = jnp.concatenate([(proj[:, :a] * (HEAD_DIM ** -0.5)).astype(BF16), k.astype(BF16), v.astype(BF16)],
                          axis=1)
    q_ref[...] = qkv[:, :a]
    k_ref[...] = qkv[:, a:2 * a]
    v_ref[...] = qkv[:, 2 * a:]
    kf_ref[...] = k
    vf_ref[...] = v
    hp = jnp.dot(perm_ref[...], hb, preferred_element_type=F32).astype(BF16)
    u_ref[...] = jnp.dot(hp, wbf_ref[:, 3 * a:], preferred_element_type=F32)
    tm = qkv.shape[0]
    for n_d, dil in enumerate(dils):
        by_residue = jnp.dot(dperm_refs[n_d][...], qkv, preferred_element_type=F32).astype(BF16)
        per = tm // dil
        for r in range(dil):
            rows = by_residue[r * per:(r + 1) * per]
            for j in range(3):
                dil_refs[3 * n_d + j][:, r * a:(r + 1) * a] = rows[:, j * a:(j + 1) * a]


def _inproj(x, sh, sc, g, w_in, perm, tm, dils=()):
    n, d = x.shape
    a = ATTN_WIDTH
    mod_rows = sh.shape[0]
    mod_spec = (pl.BlockSpec((1, d), lambda i: (0, 0)) if mod_rows == 1
                else pl.BlockSpec((tm, d), lambda i: (i, 0)))
    row = lambda w: pl.BlockSpec((tm, w), lambda i: (i, 0))
    dperms = [jnp.asarray(_chunk_perm(tm // dil, dil), BF16) for dil in dils]
    view_shapes = tuple(jax.ShapeDtypeStruct((n // dil, dil * a), BF16) for dil in dils for _ in range(3))
    view_specs = tuple(pl.BlockSpec((tm // dil, dil * a), lambda i: (i, 0)) for dil in dils for _ in range(3))
    return pl.pallas_call(
        functools.partial(_inproj_kernel, dils=tuple(dils)),
        out_shape=(jax.ShapeDtypeStruct((n, a), BF16),) * 3
        + (jax.ShapeDtypeStruct((n, a), F32),) * 2
        + (jax.ShapeDtypeStruct((n, SSM_WIDTH), F32),) + view_shapes,
        grid=(n // tm,),
        in_specs=[row(d), mod_spec, mod_spec, _full((1, d)), _full(w_in.shape), _full((tm, tm))]
        + [_full((tm, tm))] * len(dils),
        out_specs=(row(a),) * 5 + (row(SSM_WIDTH),) + view_specs,
        scratch_shapes=[pltpu.VMEM(w_in.shape, BF16)],
        compiler_params=_cparams(),
        name="inproj",
    )(x, sh, sc, g.reshape(1, d), w_in, perm, *dperms)


def _chunk_perm(rows, steps):
    n = rows * steps
    p = np.zeros((n, n), np.float32)
    c, t = np.meshgrid(np.arange(rows), np.arange(steps), indexing="ij")
    p[(t * rows + c).ravel(), (c * steps + t).ravel()] = 1.0
    return p


def _alibi_slopes():
    return np.exp2(-8.0 * np.arange(1, N_HEADS + 1, dtype=np.float64) / N_HEADS).astype(np.float32)


def _branch_bias(dil):
    qi = np.arange(Q_TILE)[:, None]
    col = np.arange(2 * Q_TILE)[None, :]
    j = Q_TILE + qi - col
    valid = (j >= 0) & (j <= Q_TILE)
    dist = (j * dil).astype(np.float32)
    tabs = []
    for first in (True, False):
        ok = valid & (col >= Q_TILE) if first else valid
        per_head = [np.where(ok, -s * dist, np.float32(MASKED)) for s in _alibi_slopes()]
        tabs.append(np.concatenate(per_head, axis=0))
    return np.stack(tabs).astype(np.float32)


def _attn_branch_kernel(q_ref, kp_ref, kc_ref, vp_ref, vc_ref, bias_ref, o_ref, lse_ref):
    first_step = pl.program_id(0) == 0
    lane = lax.broadcasted_iota(jnp.int32, (Q_TILE, V7X_LANES), 1)
    lo = lane < HEAD_DIM
    for j in range(q_ref.shape[0] // Q_TILE):
        rows = slice(j * Q_TILE, (j + 1) * Q_TILE)
        before = slice((j - 1) * Q_TILE, j * Q_TILE)
        sel = jnp.where(first_step, 0, 1) if j == 0 else 1
        lse_acc = jnp.zeros((Q_TILE, V7X_LANES), F32)
        for p in range(N_HEADS // 2):
            cs = slice(V7X_LANES * p, V7X_LANES * (p + 1))
            q2 = q_ref[rows, cs]
            zero = jnp.zeros_like(q2)
            qq = jnp.concatenate([jnp.where(lo, q2, zero), jnp.where(lo, zero, q2)], axis=0)
            k_before = kp_ref[:, cs] if j == 0 else kc_ref[before, cs]
            v_before = vp_ref[:, cs] if j == 0 else vc_ref[before, cs]
            kk = jnp.concatenate([k_before, kc_ref[rows, cs]], axis=0)
            vv = jnp.concatenate([v_before, vc_ref[rows, cs]], axis=0)
            s = lax.dot_general(qq, kk, (((1,), (1,)), ((), ())), preferred_element_type=F32)
            s = s + bias_ref[sel, 2 * Q_TILE * p:2 * Q_TILE * (p + 1), :]
            m = jnp.max(s, axis=1, keepdims=True)
            e = jnp.exp(s - m)
            l = jnp.sum(e, axis=1, keepdims=True)
            eb = e.astype(BF16)
            o0 = jnp.dot(eb[:Q_TILE], vv, preferred_element_type=F32) * (1.0 / l[:Q_TILE])
            o1 = jnp.dot(eb[Q_TILE:], vv, preferred_element_type=F32) * (1.0 / l[Q_TILE:])
            o_ref[rows, cs] = jnp.where(lo, o0, o1).astype(o_ref.dtype)
            lse = m + jnp.log(l)
            lse_acc = jnp.where(lane == 2 * p, lse[:Q_TILE], lse_acc)
            lse_acc = jnp.where(lane == 2 * p + 1, lse[Q_TILE:], lse_acc)
        lse_ref[rows, :] = lse_acc[:, :N_HEADS]


def _attn_branch(qv, kv, vv, dil):
    a = ATTN_WIDTH
    rows = qv.shape[0]
    t = rows * dil
    blocks = min(Q_BLOCKS_PER_STEP, rows // Q_TILE)
    step = blocks * Q_TILE
    cur = pl.BlockSpec((step, a), lambda i, r: (i, r))
    prev = pl.BlockSpec((Q_TILE, a), lambda i, r: (jnp.maximum(i * blocks - 1, 0), r))
    bias = jnp.asarray(_branch_bias(dil))
    o, lse = pl.pallas_call(
        _attn_branch_kernel,
        out_shape=(jax.ShapeDtypeStruct((rows, dil * a), BF16),
                   jax.ShapeDtypeStruct((dil, rows, N_HEADS), F32)),
        grid=(rows // step, dil),
        in_specs=[cur, prev, cur, prev, cur, _full(bias.shape)],
        out_specs=(cur, pl.BlockSpec((None, step, N_HEADS), lambda i, r: (r, i, 0))),
        compiler_params=_cparams(2),
        name=f"attn_branch_d{dil}",
    )(qv, kv, kv, vv, vv, bias)
    return o, lse.transpose(1, 0, 2).reshape(t, N_HEADS)


def _attn_combine_kernel(*refs, dils):
    nb = len(dils)
    o_refs = refs[:nb]
    l_refs = refs[nb:2 * nb]
    g_ref = refs[2 * nb]
    unperm_refs = refs[2 * nb + 1:-1]
    a_ref = refs[-1]
    tq, a = a_ref.shape
    outs = []
    n_u = 0
    for o_ref, dil in zip(o_refs, dils):
        if dil == 1:
            outs.append(o_ref[...].astype(F32))
            continue
        by_residue = jnp.concatenate([o_ref[:, r * a:(r + 1) * a] for r in range(dil)], axis=0)
        outs.append(jnp.dot(unperm_refs[n_u][...], by_residue, preferred_element_type=F32))
        n_u += 1
    ls = [l_ref[...] for l_ref in l_refs]
    top = functools.reduce(jnp.maximum, ls)
    ws = [jnp.exp(l - top) for l in ls]
    inv = 1.0 / functools.reduce(jnp.add, ws)
    cs = [w * inv for w in ws]
    lane = lax.broadcasted_iota(jnp.int32, (tq, V7X_LANES), 1)
    lo = lane < HEAD_DIM
    cols = []
    for p in range(N_HEADS // 2):
        sl = slice(V7X_LANES * p, V7X_LANES * (p + 1))
        acc = jnp.zeros((tq, V7X_LANES), F32)
        for c, o in zip(cs, outs):
            cexp = jnp.where(lo,
                             jnp.broadcast_to(c[:, 2 * p:2 * p + 1], (tq, V7X_LANES)),
                             jnp.broadcast_to(c[:, 2 * p + 1:2 * p + 2], (tq, V7X_LANES)))
            acc = acc + cexp * o[:, sl]
        cols.append(acc)
    o = jnp.concatenate(cols, axis=1)
    a_ref[...] = _rms(o, g_ref[...]).astype(a_ref.dtype)


def _attn_combine(os_, lses, g, dils):
    t = lses[0].shape[0]
    a = ATTN_WIDTH
    tq = TOKEN_TILE
    unperms = [jnp.asarray(_chunk_perm(tq // dil, dil).T, BF16) for dil in dils if dil > 1]
    return pl.pallas_call(
        functools.partial(_attn_combine_kernel, dils=tuple(dils)),
        out_shape=jax.ShapeDtypeStruct((t, a), BF16),
        grid=(t // tq,),
        in_specs=[pl.BlockSpec((tq // dil, dil * a), lambda i: (i, 0)) for dil in dils]
        + [pl.BlockSpec((tq, N_HEADS), lambda i: (i, 0))] * len(dils) + [_full((1, a))]
        + [_full((tq, tq))] * len(unperms),
        out_specs=pl.BlockSpec((tq, a), lambda i: (i, 0)),
        compiler_params=_cparams(),
        name="attn_combine",
    )(*os_, *lses, g.reshape(1, a), *unperms)


NEW_KEY_PAD = V7X_LANES


def _sample_bias(win, steps):
    slopes = _alibi_slopes()[None, :, None, None]
    t = np.arange(steps)[None, None, :, None]

    def table(dist, live):
        tabs = []
        for window, dil in DILATED_BRANCHES:
            ok = live & (dist >= 0) & (dist <= window) & (dist % dil == 0)
            tabs.append(np.where(ok, -slopes * dist.astype(np.float32), np.float32(MASKED))[0])
        return np.stack(tabs).astype(np.float32)

    pos = np.arange(win)[None, None, None, :]
    col = np.arange(NEW_KEY_PAD)[None, None, None, :]
    return table(win + t - pos, np.bool_(True)), table(t - col, col < steps)


def _attn_sample_kernel(q_ref, kn_ref, vn_ref, kt_ref, vt_ref, bo_ref, bn_ref, g_ref, o_ref):
    nb = bo_ref.shape[0]
    steps = q_ref.shape[1]
    nt = lambda p, v: lax.dot_general(p, v, (((1,), (1,)), ((), ())), preferred_element_type=F32)
    res = []
    sq = jnp.zeros((steps, 1), F32)
    for h in range(N_HEADS):
        q = q_ref[h]
        vt = vt_ref[h].astype(BF16)
        vn = vn_ref[h].astype(BF16)
        s_old = jnp.dot(q, kt_ref[h].astype(BF16), preferred_element_type=F32)
        s_new = jnp.dot(q, kn_ref[h].astype(BF16), preferred_element_type=F32)
        ms, ls, e_old, e_new = [], [], [], []
        for b in range(nb):
            so = s_old + bo_ref[b, h]
            sn = s_new + bn_ref[b, h]
            m = jnp.maximum(jnp.max(so, axis=1, keepdims=True), jnp.max(sn, axis=1, keepdims=True))
            eo = jnp.exp(so - m)
            en = jnp.exp(sn - m)
            ms.append(m)
            ls.append(jnp.sum(eo, axis=1, keepdims=True) + jnp.sum(en, axis=1, keepdims=True))
            e_old.append(eo)
            e_new.append(en)
        o_all = (nt(jnp.concatenate(e_old, axis=0).astype(BF16), vt)
                 + nt(jnp.concatenate(e_new, axis=0).astype(BF16), vn))
        top = functools.reduce(jnp.maximum, ms)
        num = jnp.zeros((steps, HEAD_DIM), F32)
        den = jnp.zeros((steps, 1), F32)
        for b in range(nb):
            w = jnp.exp(ms[b] - top)
            num = num + o_all[b * steps:(b + 1) * steps] * w
            den = den + ls[b] * w
        r = num * (1.0 / den)
        res.append(r)
        sq = sq + jnp.sum(r * r, axis=1, keepdims=True)
    inv = lax.rsqrt(sq * (1.0 / ATTN_WIDTH) + EPS)
    for h in range(N_HEADS):
        o_ref[h] = res[h] * inv * g_ref[h]


def _attn_sample(q, kn_t, vn_t, cache_kt, cache_vt, layer, g):
    b, heads, steps, dh = q.shape
    win = cache_kt.shape[-1]
    b_old, b_new = (jnp.asarray(z) for z in _sample_bias(win, steps))
    per_b = lambda *tail: pl.BlockSpec((None,) + tail, lambda i: (i,) + (0,) * len(tail))
    cache = pl.BlockSpec((None, None, heads, dh, win), lambda i: (layer, i, 0, 0, 0))
    return pl.pallas_call(
        _attn_sample_kernel,
        out_shape=jax.ShapeDtypeStruct((b, heads, steps, dh), F32),
        grid=(b,),
        in_specs=[per_b(heads, steps, dh), per_b(heads, dh, NEW_KEY_PAD), per_b(heads, dh, NEW_KEY_PAD),
                  cache, cache, _full(b_old.shape), _full(b_new.shape), _full((heads, 1, dh))],
        out_specs=per_b(heads, steps, dh),
        compiler_params=_cparams(),
        name="attn_sample",
    )(q, kn_t, vn_t, cache_kt, cache_vt, b_old, b_new, g.reshape(heads, 1, dh))


def _gelu_tanh(x):
    return 0.5 * x * (1.0 + jnp.tanh(np.sqrt(2.0 / np.pi).astype(np.float32) * (x + 0.044715 * (x * x * x))))


def _ssm_kernel(u_ref, bb_ref, lam_ref, pow_ref, cm_ref, dsk_ref, wglu_ref, bglu_ref, g_ref, pt_ref, h0_ref,
                z_ref, ht_ref, h_s, hin_s, carry_s, *, rows, steps, chain):
    ns = N_STATE
    cw = (4 * V7X_SUBLANES * V7X_LANES) // rows
    last = (steps - 1) * rows

    @pl.when(pl.program_id(0) == 0)
    def _():
        carry_s[...] = h0_ref[0:1, :]

    ub = u_ref[...].astype(BF16)
    n_slabs = SSM_WIDTH // V7X_LANES
    sw = ns // n_slabs
    for s in range(n_slabs):
        part = jnp.dot(ub[:, s * V7X_LANES:(s + 1) * V7X_LANES], bb_ref[s], preferred_element_type=F32)
        h_s[:, s * sw:(s + 1) * sw] = part[:, :sw]
        h_s[:, ns + s * sw:ns + (s + 1) * sw] = part[:, sw:]

    for cc in range(0, ns // cw, 2):
        crs = [slice(c * cw, (c + 1) * cw) for c in (cc, cc + 1)]
        cis = [slice(ns + c * cw, ns + (c + 1) * cw) for c in (cc, cc + 1)]
        lrs = [jnp.broadcast_to(lam_ref[0:1, cr], (rows, cw)) for cr in crs]
        lis = [jnp.broadcast_to(lam_ref[1:2, cr], (rows, cw)) for cr in crs]

        def scan_body(t, carry, crs=crs, cis=cis, lrs=lrs, lis=lis):
            rs = pl.ds(pl.multiple_of(t * rows, rows), rows)
            out = []
            for j in range(2):
                hr, hi = carry[2 * j], carry[2 * j + 1]
                nr = lrs[j] * hr - lis[j] * hi + h_s[rs, crs[j]]
                ni = lrs[j] * hi + lis[j] * hr + h_s[rs, cis[j]]
                h_s[rs, crs[j]] = nr
                h_s[rs, cis[j]] = ni
                out += [nr, ni]
            return tuple(out)

        zero = jnp.zeros((rows, cw), F32)
        lax.fori_loop(0, steps, scan_body, (zero,) * 4)

    if chain:
        ptr = pow_ref[steps - 1, 0:1, :ns]
        pti = pow_ref[steps - 1, 0:1, ns:]
        carry = carry_s[...]
        for c in range(rows):
            hin_s[c:c + 1, :] = carry
            cr_, ci_ = carry[:, :ns], carry[:, ns:]
            e = h_s[last + c:last + c + 1, :]
            carry = jnp.concatenate([ptr * cr_ - pti * ci_ + e[:, :ns],
                                     ptr * ci_ + pti * cr_ + e[:, ns:]], axis=1)
        carry_s[...] = carry
    else:
        hin_s[...] = h0_ref[...]

    for cc in range(ns // cw):
        cr = slice(cc * cw, (cc + 1) * cw)
        ci = slice(ns + cc * cw, ns + (cc + 1) * cw)
        hr0 = hin_s[:, cr]
        hi0 = hin_s[:, ci]

        def fix_body(t, _, cr=cr, ci=ci, hr0=hr0, hi0=hi0):
            rs = pl.ds(pl.multiple_of(t * rows, rows), rows)
            pr = pow_ref[t, :, cr]
            pi_ = pow_ref[t, :, ci]
            h_s[rs, cr] = h_s[rs, cr] + (pr * hr0 - pi_ * hi0)
            h_s[rs, ci] = h_s[rs, ci] + (pr * hi0 + pi_ * hr0)
            return 0

        lax.fori_loop(0, steps, fix_body, 0, unroll=8)

    ht_ref[...] = h_s[last:last + rows, :]
    ys = []
    for s in range(n_slabs):
        hs = jnp.concatenate([h_s[:, s * sw:(s + 1) * sw], h_s[:, ns + s * sw:ns + (s + 1) * sw]], axis=1)
        ys.append(jnp.dot(hs.astype(BF16), cm_ref[s], preferred_element_type=F32))
    y = jnp.concatenate(ys, axis=1) + dsk_ref[...] * u_ref[...]
    y = _gelu_tanh(y)
    gl = jnp.dot(y.astype(BF16), wglu_ref[...], preferred_element_type=F32) + bglu_ref[...]
    z = _rms(y * _sigmoid(gl), g_ref[...]).astype(BF16)
    z_ref[...] = jnp.dot(pt_ref[...], z, preferred_element_type=F32).astype(z_ref.dtype)


def _ssm_tables(ssm_a_re, ssm_a_im, ssm_log_dt, ssm_b_re, ssm_b_im, ssm_c_re, ssm_c_im, max_steps):
    g, n, ch = SSM_GROUPS, SSM_STATE, SSM_CH
    a_re = ssm_a_re.astype(F32)
    a_im = ssm_a_im.astype(F32)
    dt = jnp.exp(ssm_log_dt.astype(F32))[:, None]
    mag = jnp.exp(dt * a_re)
    lam_re = mag * jnp.cos(dt * a_im)
    lam_im = mag * jnp.sin(dt * a_im)
    nr = lam_re - 1.0
    ni = lam_im
    inv = 1.0 / (a_re * a_re + a_im * a_im)
    coef_re = (nr * a_re + ni * a_im) * inv
    coef_im = (ni * a_re - nr * a_im) * inv
    br = ssm_b_re.astype(F32)
    bi = ssm_b_im.astype(F32)
    bb_re = coef_re[..., None] * br - coef_im[..., None] * bi
    bb_im = coef_re[..., None] * bi + coef_im[..., None] * br
    gs = V7X_LANES // ch
    ns_ = g // gs
    eye = jnp.eye(gs, dtype=F32)
    bmat = lambda b: jnp.einsum("sgnc,gh->sgchn", b.reshape(ns_, gs, n, ch), eye).reshape(ns_, gs * ch, gs * n)
    cmat = lambda c: jnp.einsum("sgcn,gh->sgnhc", c.astype(F32).reshape(ns_, gs, ch, n), eye).reshape(
        ns_, gs * n, gs * ch)
    bb = jnp.concatenate([bmat(bb_re), bmat(bb_im)], axis=2).astype(BF16)
    cm = jnp.concatenate([cmat(ssm_c_re), -cmat(ssm_c_im)], axis=1).astype(BF16)
    lam = jnp.stack([lam_re.reshape(-1), lam_im.reshape(-1)])

    k = jnp.arange(1, max_steps + 1, dtype=F32)[:, None]
    kdt = k * dt.reshape(1, -1).repeat(n, axis=1)
    pmag = jnp.exp(kdt * a_re.reshape(1, -1))
    parg = kdt * a_im.reshape(1, -1)
    pows = jnp.concatenate([pmag * jnp.cos(parg), pmag * jnp.sin(parg)], axis=1)
    return bb, cm, lam, pows


def _ssm(u_perm, h0, tabs, dsk, w_glu, b_glu, g, perm_t, rows, steps, chain):
    n, w = u_perm.shape
    blk = rows * steps
    bb, cm, lam, pows = tabs
    row = pl.BlockSpec((blk, w), lambda i: (i, 0))
    kern = functools.partial(_ssm_kernel, rows=rows, steps=steps, chain=chain)
    return pl.pallas_call(
        kern,
        out_shape=(jax.ShapeDtypeStruct((n, w), BF16), jax.ShapeDtypeStruct((rows, 2 * N_STATE), F32)),
        grid=(n // blk,),
        in_specs=[row, _full(bb.shape), _full(lam.shape), _full((steps, rows, 2 * N_STATE)), _full(cm.shape),
                  _full((1, w)), _full((w, w)), _full((1, w)), _full((1, w)), _full((blk, blk)),
                  _full((rows, 2 * N_STATE))],
        out_specs=(row, _full((rows, 2 * N_STATE))),
        scratch_shapes=[pltpu.VMEM((blk, 2 * N_STATE), F32),
                        pltpu.VMEM((rows, 2 * N_STATE), F32),
                        pltpu.VMEM((1, 2 * N_STATE), F32)],
        compiler_params=_cparams(),
        name=f"ssm_r{rows}",
    )(u_perm, bb, lam, jnp.broadcast_to(pows[:steps, None, :], (steps, rows, 2 * N_STATE)), cm,
      dsk.reshape(1, w), w_glu.astype(BF16), b_glu.reshape(1, w),
      g.reshape(1, w), perm_t, h0)


def _outproj_kernel(*refs, n_real, aliased):
    ins, outs = refs[:12], refs[12 + aliased:]
    step = pl.program_id(0)

    @pl.when(step < n_real)
    def _():
        _outproj_tile(*ins, *outs)

    @pl.when(step >= n_real)
    def _():
        xs_ref = outs[1]
        xs_ref[...] = jnp.zeros(xs_ref.shape, xs_ref.dtype)


def _outproj_tile(x_ref, a_ref, z_ref, gt_ref, sh_ref, sc_ref, g_ref, wo_ref, wr_ref, br_ref, ltri_ref, utri_ref,
                  x1_ref, xs_ref, slot_ref, gates_ref, meta_ref, wbf_ref):
    @pl.when(pl.program_id(0) == 0)
    def _():
        wbf_ref[...] = wo_ref[...].astype(BF16)

    a = ATTN_WIDTH
    mixed = (jnp.dot(a_ref[...].astype(BF16), wbf_ref[:a, :], preferred_element_type=F32)
             + jnp.dot(z_ref[...], wbf_ref[a:, :], preferred_element_type=F32))
    x1 = x_ref[...] + gt_ref[...] * mixed
    x1_ref[...] = x1
    h2 = (_rms(x1, g_ref[...]) * (1.0 + sc_ref[...]) + sh_ref[...]).astype(BF16)
    lg = jnp.dot(h2, wr_ref[...], preferred_element_type=F32) + br_ref[...]
    tm = lg.shape[0]
    lane = lax.broadcasted_iota(jnp.int32, (tm, V7X_LANES), 1).astype(F32)
    vals, hots = [], []
    for _ in range(TOP_K):
        m = jnp.max(lg, axis=1, keepdims=True)
        idx = jnp.min(jnp.where(lg == m, lane, float(V7X_LANES)), axis=1, keepdims=True)
        hot = lane == idx
        vals.append(m)
        hots.append(jnp.where(hot, 1.0, 0.0))
        lg = jnp.where(hot, MASKED * 2, lg)
    es = [jnp.exp(v - vals[0]) for v in vals]
    inv = 1.0 / (es[0] + es[1] + es[2] + es[3])

    member = hots[0] + hots[1] + hots[2] + hots[3]
    before = jnp.dot(ltri_ref[...], member.astype(BF16), preferred_element_type=F32)
    count = jnp.sum(member, axis=0, keepdims=True)
    padded = jnp.floor((count + (ROW_ALIGN - 1.0)) * (1.0 / ROW_ALIGN)) * ROW_ALIGN
    padded8 = jnp.broadcast_to(padded, (V7X_SUBLANES, V7X_LANES))
    start = jnp.dot(padded8.astype(BF16), utri_ref[...], preferred_element_type=F32)[0:1]
    where_to = start + before
    slots = jnp.zeros((tm, V7X_LANES), F32)
    gates = jnp.zeros((tm, V7X_LANES), F32)
    for k in range(TOP_K):
        slot_k = jnp.sum(hots[k] * where_to, axis=1, keepdims=True)
        slots = jnp.where(lane == float(k), slot_k, slots)
        gates = jnp.where(lane == float(k), es[k] * inv, gates)
    slot_ref[...] = slots
    gates_ref[...] = gates
    row = lax.broadcasted_iota(jnp.int32, (V7X_SUBLANES, V7X_LANES), 0)
    meta_ref[...] = jnp.where(row == 0, padded8, jnp.where(row == 1, jnp.broadcast_to(start, padded8.shape), 0.0))

    cap = xs_ref.shape[0]
    slots_t = jnp.transpose(slots).astype(jnp.int32).astype(jnp.int16)
    srow = lax.broadcasted_iota(jnp.int16, (cap, tm), 0)
    place = jnp.zeros((cap, tm), BF16)
    for k in range(TOP_K):
        place = jnp.where(srow == slots_t[k:k + 1, :], jnp.ones((), BF16), place)
    xs_ref[...] = jnp.dot(place, h2, preferred_element_type=F32).astype(BF16)


def _outproj(x, a, z, gt, sh, sc, g, w_out, wr_pad, br_pad, tm, n_blocks, block0, xs_prev=None):
    n, d = x.shape
    aw = ATTN_WIDTH
    n_real = n // tm
    aliased = xs_prev is not None
    n_steps = n_real if aliased else n_blocks
    tile = lambda i: jnp.minimum(i, n_real - 1)
    mod_spec = (pl.BlockSpec((1, d), lambda i: (0, 0)) if gt.shape[0] == 1
                else pl.BlockSpec((tm, d), lambda i: (tile(i), 0)))
    row = lambda w: pl.BlockSpec((tm, w), lambda i: (tile(i), 0))
    ltri = jnp.asarray(np.tril(np.ones((tm, tm), np.float32), -1), BF16)
    utri = jnp.asarray(np.triu(np.ones((V7X_LANES, V7X_LANES), np.float32), 1), BF16)
    in_specs = [row(d), row(aw), row(SSM_WIDTH), mod_spec, mod_spec, mod_spec, _full((1, d)),
                _full(w_out.shape), _full(wr_pad.shape), _full(br_pad.shape), _full(ltri.shape), _full(utri.shape)]
    args = [x, a, z, gt, sh, sc, g.reshape(1, d), w_out, wr_pad, br_pad, ltri, utri]
    if aliased:
        in_specs.append(pl.BlockSpec(memory_space=pl.ANY))
        args.append(xs_prev)
    return pl.pallas_call(
        functools.partial(_outproj_kernel, n_real=n_real, aliased=int(aliased)),
        out_shape=(jax.ShapeDtypeStruct((n, d), F32),
                   jax.ShapeDtypeStruct((n_blocks, GROUP_CAP, d), BF16),
                   jax.ShapeDtypeStruct((n, V7X_LANES), F32), jax.ShapeDtypeStruct((n, V7X_LANES), F32),
                   jax.ShapeDtypeStruct((n_real, V7X_SUBLANES, V7X_LANES), F32)),
        grid=(n_steps,),
        in_specs=in_specs,
        out_specs=(row(d), pl.BlockSpec((None, GROUP_CAP, d), lambda i: (block0 + i, 0, 0)),
                   row(V7X_LANES), row(V7X_LANES),
                   pl.BlockSpec((None, V7X_SUBLANES, V7X_LANES), lambda i: (tile(i), 0, 0))),
        scratch_shapes=[pltpu.VMEM(w_out.shape, BF16)],
        input_output_aliases={len(args) - 1: 1} if aliased else {},
        compiler_params=_cparams(),
        name="outproj_router",
    )(*args)


_PIECE_SIZES = tuple(1 << s for s in range(MOE_TILE.bit_length() - 1, ROW_ALIGN.bit_length() - 2, -1))
_CHUNK_SHIFT = 6
_CHUNK = 1 << _CHUNK_SHIFT
_TAIL_SIZES = tuple(sz for sz in _PIECE_SIZES if sz < _CHUNK)


def _moe_kernel(te_ref, tl_ref, lo_ref, hi_ref, nu_ref, gstart_ref, gsize_ref, gbase_ref, rows_ref, used_ref,
                nxt_ref, xs_hbm, wgu_hbm, bgu_ref, wd_hbm, bd_ref, ys_hbm,
                xbuf, ybuf, zbuf, wgu_f32, wd_f32, wgu_bf, wd_bf, xsem, ysem, zsem, wsem):
    t = pl.program_id(0)
    n_used = nu_ref[0]
    n_blocks, cap = xs_hbm.shape[0], xs_hbm.shape[1]

    def weight_copies(e):
        return (pltpu.make_async_copy(wgu_hbm.at[e], wgu_f32, wsem.at[0]),
                pltpu.make_async_copy(wd_hbm.at[e], wd_f32, wsem.at[1]))

    def x_copy(i, br, tr, sz, slot):
        pltpu.make_async_copy(xs_hbm.at[i, pl.ds(br, sz)], xbuf.at[slot, pl.ds(tr, sz)], xsem.at[slot]).start()

    def y_copy(i, br, tr, sz, slot):
        pltpu.make_async_copy(ybuf.at[slot, pl.ds(tr, sz)], ys_hbm.at[i, pl.ds(br, sz)], ysem.at[slot]).start(
            priority=0 if sz == _CHUNK else 1)

    def z_copy(i, row, sz, start):
        cp = pltpu.make_async_copy(zbuf.at[pl.ds(0, sz)], ys_hbm.at[i, pl.ds(row, sz)], zsem)
        cp.start() if start else cp.wait()

    def pieces(tt, fn):
        e = te_ref[tt]
        lo = tl_ref[tt] * MOE_TILE

        def per_block(i, c):
            g = e * n_blocks + i
            s0 = gbase_ref[g]
            a = jnp.maximum(s0, lo)
            b = jnp.minimum(s0 + gsize_ref[g], lo + MOE_TILE)
            length = jnp.maximum(b - a, 0)
            src = gstart_ref[g] + (a - s0)
            dst = a - lo
            whole = lax.shift_right_logical(length, _CHUNK_SHIFT)

            def chunk(j, cc):
                off = j * _CHUNK
                fn(i, pl.multiple_of(src + off, ROW_ALIGN), pl.multiple_of(dst + off, ROW_ALIGN), _CHUNK)
                return cc

            lax.fori_loop(0, whole, chunk, 0)
            done = whole * _CHUNK
            for sz in _TAIL_SIZES:
                hit = (length & sz) != 0

                @pl.when(hit)
                def _(sz=sz, done=done):
                    fn(i, pl.multiple_of(src + done, ROW_ALIGN), pl.multiple_of(dst + done, ROW_ALIGN), sz)

                done = done + jnp.where(hit, sz, 0)
            return c

        lax.fori_loop(lo_ref[tt], hi_ref[tt], per_block, 0)

    def tile_rows(tt):
        return jnp.minimum(rows_ref[te_ref[tt]] - tl_ref[tt] * MOE_TILE, MOE_TILE)

    def wait_rows(n, sem, buf):
        def chunk(j, cc):
            pltpu.make_async_copy(buf.at[pl.ds(0, _CHUNK)], buf.at[pl.ds(0, _CHUNK)], sem).wait()
            return cc

        lax.fori_loop(0, lax.shift_right_logical(n, _CHUNK_SHIFT), chunk, 0)
        for sz in _TAIL_SIZES:
            @pl.when((n & sz) != 0)
            def _(sz=sz):
                pltpu.make_async_copy(buf.at[pl.ds(0, sz)], buf.at[pl.ds(0, sz)], sem).wait()

    def fetch(tt, slot):
        pieces(tt, lambda i, br, tr, sz: x_copy(i, br, tr, sz, slot))

    def writeback(tt, slot):
        pieces(tt, lambda i, br, tr, sz: y_copy(i, br, tr, sz, slot))

    def zero_tail(i, start):
        u = used_ref[i]
        rem = cap - u
        nz = zbuf.shape[0]
        whole = lax.shift_right_logical(rem, nz.bit_length() - 1)

        def chunk(j, c):
            z_copy(i, pl.multiple_of(u + j * nz, ROW_ALIGN), nz, start)
            return c

        lax.fori_loop(0, whole, chunk, 0)
        base = u + whole * nz
        done = jnp.int32(0)
        for sz in _PIECE_SIZES:
            if sz >= nz:
                continue
            hit = (rem & sz) != 0

            @pl.when(hit)
            def _(sz=sz, done=done):
                z_copy(i, pl.multiple_of(base + done, ROW_ALIGN), sz, start)

            done = done + jnp.where(hit, sz, 0)

    @pl.when(t == 0)
    def _():
        xbuf[...] = jnp.zeros(xbuf.shape, xbuf.dtype)
        zbuf[...] = jnp.zeros(zbuf.shape, zbuf.dtype)
        lax.fori_loop(0, n_blocks, lambda i, c: (zero_tail(i, True), c)[1], 0)
        lax.fori_loop(0, n_blocks, lambda i, c: (zero_tail(i, False), c)[1], 0)
        fetch(0, 0)
        for cp in weight_copies(te_ref[0]):
            cp.start(priority=1)

    @pl.when(t < n_used)
    def _():
        slot = t % 2

        @pl.when(t + 1 < n_used)
        def _():
            fetch(t + 1, 1 - slot)

        @pl.when(tl_ref[t] == 0)
        def _():
            for cp in weight_copies(te_ref[t]):
                cp.wait()
            wgu_bf[...] = wgu_f32[...].astype(BF16)
            wd_bf[...] = wd_f32[...].astype(BF16)
            nxt = nxt_ref[te_ref[t]]

            @pl.when(nxt >= 0)
            def _():
                for cp in weight_copies(nxt):
                    cp.start(priority=1)

        wait_rows(tile_rows(t), xsem.at[slot], xbuf.at[slot])

        @pl.when(t >= 2)
        def _():
            wait_rows(tile_rows(t - 2), ysem.at[slot], ybuf.at[slot])

        def expert_mlp(m):
            f = D_EXPERT
            gu = jnp.dot(xbuf[slot, pl.ds(0, m)], wgu_bf[...], preferred_element_type=F32) + bgu_ref[...]
            gate = jnp.minimum(gu[:, :f], SWIGLU_LIMIT)
            up = jnp.clip(gu[:, f:], -SWIGLU_LIMIT, SWIGLU_LIMIT)
            act = (up + 1.0) * gate * _sigmoid(SWIGLU_ALPHA * gate)
            y = jnp.dot(act.astype(BF16), wd_bf[...], preferred_element_type=F32) + bd_ref[...]
            ybuf[slot, pl.ds(0, m)] = y.astype(BF16)

        n_steps = lax.shift_right_logical(tile_rows(t) + (MOE_ROW_STEP - 1), MOE_ROW_STEP.bit_length() - 1)
        for ns in range(1, MOE_TILE // MOE_ROW_STEP + 1):
            @pl.when(n_steps == ns)
            def _(ns=ns):
                expert_mlp(ns * MOE_ROW_STEP)

        writeback(t, slot)

        @pl.when(t == n_used - 1)
        def _():
            wait_rows(tile_rows(t), ysem.at[slot], ybuf.at[slot])

            @pl.when(t >= 1)
            def _():
                wait_rows(tile_rows(t - 1), ysem.at[1 - slot], ybuf.at[1 - slot])


def _moe(plan, xs, w_gate_up, b_gate_up, w_down, b_down):
    _, cap, d = xs.shape
    e, _, f2 = w_gate_up.shape
    nt = plan[0].shape[0]
    wmap = lambda t, te, *_: (te[t], 0, 0)
    anyspec = pl.BlockSpec(memory_space=pl.ANY)
    grid_spec = pltpu.PrefetchScalarGridSpec(
        num_scalar_prefetch=len(plan),
        grid=(nt,),
        in_specs=[anyspec,
                  anyspec,
                  pl.BlockSpec((None, 1, f2), wmap),
                  anyspec,
                  pl.BlockSpec((None, 1, d), wmap)],
        out_specs=anyspec,
        scratch_shapes=[pltpu.VMEM((2, MOE_TILE, d), BF16), pltpu.VMEM((2, MOE_TILE, d), BF16),
                        pltpu.VMEM((MOE_TILE // 2, d), BF16),
                        pltpu.VMEM((d, f2), F32), pltpu.VMEM((f2 // 2, d), F32),
                        pltpu.VMEM((d, f2), BF16), pltpu.VMEM((f2 // 2, d), BF16),
                        pltpu.SemaphoreType.DMA((2,)), pltpu.SemaphoreType.DMA((2,)), pltpu.SemaphoreType.DMA(()),
                        pltpu.SemaphoreType.DMA((2,))],
    )
    return pl.pallas_call(
        _moe_kernel,
        out_shape=jax.ShapeDtypeStruct(xs.shape, BF16),
        grid_spec=grid_spec,
        compiler_params=_cparams(),
        name="moe_experts",
    )(*plan, xs, w_gate_up, b_gate_up.reshape(e, 1, f2), w_down, b_down.reshape(e, 1, d))


def _moe_plan(group_size, group_start):
    n_blocks = group_size.shape[0]
    gsize = group_size.T.astype(jnp.int32)
    gstart = group_start.T.astype(jnp.int32)
    gbase = jnp.cumsum(gsize, axis=1) - gsize
    rows = jnp.sum(gsize, axis=1)
    tiles = (rows + MOE_TILE - 1) // MOE_TILE
    tile_end = jnp.cumsum(tiles)
    n_used = tile_end[-1:]
    nt = (n_blocks * GROUP_CAP) // MOE_TILE + N_EXPERTS
    t = jnp.arange(nt, dtype=jnp.int32)
    te = jnp.sum((tile_end[None, :] <= t[:, None]).astype(jnp.int32), axis=1)
    last = jnp.max(jnp.where(tiles > 0, jnp.arange(N_EXPERTS, dtype=jnp.int32), 0))
    te = jnp.where(t < n_used[0], jnp.minimum(te, N_EXPERTS - 1), last)
    hot = (te[:, None] == jnp.arange(N_EXPERTS, dtype=jnp.int32)[None, :]).astype(jnp.int32)
    tl = jnp.where(t < n_used[0], t - hot @ (tile_end - tiles), 0)
    lo_row = tl * MOE_TILE
    base_t = hot @ gbase
    size_t = hot @ gsize
    first = jnp.sum((base_t + size_t <= lo_row[:, None]).astype(jnp.int32), axis=1)
    stop = jnp.sum((base_t < lo_row[:, None] + MOE_TILE).astype(jnp.int32), axis=1)
    used = jnp.sum(gsize, axis=0)
    ids = jnp.arange(N_EXPERTS, dtype=jnp.int32)
    later = (ids[None, :] > ids[:, None]) & (tiles[None, :] > 0)
    nxt = jnp.min(jnp.where(later, ids[None, :], N_EXPERTS), axis=1)
    nxt = jnp.where(nxt < N_EXPERTS, nxt, -1)
    i32 = lambda z: z.astype(jnp.int32)
    return (i32(te), i32(tl), i32(first), i32(stop), i32(n_used), i32(gstart.reshape(-1)), i32(gsize.reshape(-1)),
            i32(gbase.reshape(-1)), i32(rows), i32(used), i32(nxt))


def _final_kernel(x_ref, ys_ref, slot_ref, gates_ref, gt_ref, g_ref, y_ref):
    tm = x_ref.shape[0]
    cap = ys_ref.shape[0]
    col = lax.broadcasted_iota(jnp.int16, (tm, cap), 1)
    slots = slot_ref[...].astype(jnp.int32).astype(jnp.int16)
    gates = gates_ref[...].astype(BF16)
    mix = jnp.zeros((tm, cap), BF16)
    for k in range(TOP_K):
        mix = jnp.where(col == slots[:, k:k + 1], gates[:, k:k + 1], mix)
    ff = jnp.dot(mix, ys_ref[...], preferred_element_type=F32)
    y_ref[...] = _rms(x_ref[...] + gt_ref[...] * ff, g_ref[...])


def _final(x1, ys, slots, gates, gt, g, tm, block0):
    n, d = x1.shape
    cap = ys.shape[1]
    mod_spec = (pl.BlockSpec((1, d), lambda i: (0, 0)) if gt.shape[0] == 1
                else pl.BlockSpec((tm, d), lambda i: (i, 0)))
    row = lambda w: pl.BlockSpec((tm, w), lambda i: (i, 0))
    return pl.pallas_call(
        _final_kernel,
        out_shape=jax.ShapeDtypeStruct((n, d), F32),
        grid=(n // tm,),
        in_specs=[row(d), pl.BlockSpec((None, cap, d), lambda i: (block0 + i, 0, 0)),
                  row(V7X_LANES), row(V7X_LANES), mod_spec, _full((1, d))],
        out_specs=row(d),
        compiler_params=_cparams(),
        name="final_norm",
    )(x1, ys, slots, gates, gt, g.reshape(1, d))


def kernel(x_prompt, x_sample, cache_k_win, cache_v_win, state_ssm_re, state_ssm_im, c_prompt, c_sample,
           w_ada, b_ada, g_norm1, w_in, ssm_a_re, ssm_a_im, ssm_log_dt, ssm_b_re, ssm_b_im, ssm_c_re, ssm_c_im,
           ssm_d, w_glu, b_glu, g_out_attn, g_out_ssm, w_out, g_norm2, w_router, b_router, w_gate_up, b_gate_up,
           w_down, b_down, g_final):
    depth = w_ada.shape[0]
    assert depth == 1 and x_prompt.shape[0] == 1
    bp, t, d = x_prompt.shape
    bs, ts, _ = x_sample.shape
    ns = bs * ts
    l = 0
    a = ATTN_WIDTH

    n_c = bp + bs
    c_pad = -(-n_c // V7X_SUBLANES) * V7X_SUBLANES
    c_rows = jnp.concatenate([c_prompt, c_sample, jnp.zeros((c_pad - n_c, d), F32)], axis=0)
    mod = _ada_modulation(c_rows, w_ada[l], b_ada[l])
    mod_p = [mod[0:1, i * d:(i + 1) * d] for i in range(N_MOD)]
    mod_s = [jnp.repeat(mod[bp:bp + bs, i * d:(i + 1) * d], ts, axis=0) for i in range(N_MOD)]

    steps_p = TOKEN_TILE // SSM_ROWS_PROMPT
    perm_p = _chunk_perm(SSM_ROWS_PROMPT, steps_p)
    perm_s = _chunk_perm(bs, ts)
    tabs = _ssm_tables(ssm_a_re[l], ssm_a_im[l], ssm_log_dt[l], ssm_b_re[l], ssm_b_im[l],
                       ssm_c_re[l], ssm_c_im[l], max(steps_p, ts))
    wr_pad = jnp.zeros((d, V7X_LANES), F32).at[:, :N_EXPERTS].set(w_router[l]).astype(BF16)
    br_pad = jnp.full((1, V7X_LANES), MASKED, F32).at[0, :N_EXPERTS].set(b_router[l])

    xp = x_prompt.reshape(t, d)
    dils = tuple(dil for _, dil in DILATED_BRANCHES)
    wide = tuple(dil for dil in dils if dil > 1)
    proj_p = _inproj(xp, mod_p[0], mod_p[1], g_norm1[l], w_in[l], jnp.asarray(perm_p, BF16), TOKEN_TILE, wide)
    kpf, vpf, up = proj_p[3:6]
    views = {1: proj_p[0:3]}
    for n_d, dil in enumerate(wide):
        views[dil] = proj_p[6 + 3 * n_d:9 + 3 * n_d]
    outs = [_attn_branch(*views[dil], dil) for dil in dils]
    ap = _attn_combine([o for o, _ in outs], [s for _, s in outs], g_out_attn[l], dils)
    zeros_h = jnp.zeros((SSM_ROWS_PROMPT, 2 * N_STATE), F32)
    zp, hp = _ssm(up, zeros_h, tabs, ssm_d[l], w_glu[l], b_glu[l], g_out_ssm[l],
                  jnp.asarray(perm_p.T, BF16), SSM_ROWS_PROMPT, steps_p, True)
    n_blocks_p = t // TOKEN_TILE
    n_blocks = n_blocks_p + 1
    x1p, xs_all, slot_p, gate_p, meta_p = _outproj(xp, ap, zp, mod_p[2], mod_p[3], mod_p[4], g_norm2[l], w_out[l],
                                                   wr_pad, br_pad, TOKEN_TILE, n_blocks, 0)

    xs = x_sample.reshape(ns, d)
    qs, ks, vs, ksf, vsf, us = _inproj(xs, mod_s[0], mod_s[1], g_norm1[l], w_in[l], jnp.asarray(perm_s, BF16), ns)
    split = lambda z: z.reshape(bs, ts, N_HEADS, HEAD_DIM)
    new_t = lambda z: jnp.pad(split(z).transpose(0, 2, 3, 1), ((0, 0), (0, 0), (0, 0), (0, NEW_KEY_PAD - ts)))
    as_ = _attn_sample(split(qs).transpose(0, 2, 1, 3), new_t(ks), new_t(vs),
                       cache_k_win.transpose(0, 1, 3, 4, 2), cache_v_win.transpose(0, 1, 3, 4, 2), l, g_out_attn[l])
    as_ = as_.transpose(0, 2, 1, 3)
    h0s = jnp.concatenate([state_ssm_re[l].reshape(bs, N_STATE), state_ssm_im[l].reshape(bs, N_STATE)], axis=1)
    zs, hs = _ssm(us, h0s, tabs, ssm_d[l], w_glu[l], b_glu[l], g_out_ssm[l],
                  jnp.asarray(perm_s.T, BF16), bs, ts, False)
    x1s, xs_all, slot_s, gate_s, meta_s = _outproj(xs, as_.reshape(ns, a), zs, mod_s[2], mod_s[3], mod_s[4],
                                                   g_norm2[l], w_out[l], wr_pad, br_pad, ns, n_blocks, n_blocks_p,
                                                   xs_prev=xs_all)

    meta = jnp.concatenate([meta_p, meta_s], axis=0)
    plan = _moe_plan(meta[:, 0, :N_EXPERTS], meta[:, 1, :N_EXPERTS])
    ys_all = _moe(plan, xs_all, w_gate_up[l], b_gate_up[l], w_down[l], b_down[l])

    y_prompt = _final(x1p, ys_all, slot_p, gate_p, mod_p[5], g_final, TOKEN_TILE, 0).reshape(bp, t, d)
    y_sample = _final(x1s, ys_all, slot_s, gate_s, mod_s[5], g_final, ns, n_blocks_p).reshape(bs, ts, d)

    keep = min(MAX_WINDOW, t)
    shp = (1, bp, keep, N_HEADS, HEAD_DIM)
    k_win = kpf[t - keep:].reshape(shp)
    v_win = vpf[t - keep:].reshape(shp)
    st = (1, bp, SSM_GROUPS, SSM_STATE)
    hp_last = hp[SSM_ROWS_PROMPT - 1]
    ss = (1, bs, SSM_GROUPS, SSM_STATE)
    return (y_prompt, y_sample, k_win, v_win,
            hp_last[:N_STATE].reshape(st), hp_last[N_STATE:].reshape(st),
            ksf.reshape(1, bs, ts, N_HEADS, HEAD_DIM), vsf.reshape(1, bs, ts, N_HEADS, HEAD_DIM),
            hs[:, :N_STATE].reshape(ss), hs[:, N_STATE:].reshape(ss))
```
